```python
import numpy as np
import jax
import jax.numpy as jnp
from jax import lax

D_MODEL = 1024
BATCH = 8
SEQ = 4096
DEPTH = 2

N_EVEN = (DEPTH + 1) // 2
N_ODD = DEPTH // 2
HEAD_DIM = 64
EPS = 1e-6

ML_HEADS = 8
ML_W = ML_HEADS * HEAD_DIM
ML_CHUNK = 64
GATE_SOFTCAP = 15.0

NSA_HEADS = 8
NSA_KV_HEADS = 2
NSA_W = NSA_HEADS * HEAD_DIM
NSA_KV_W = NSA_KV_HEADS * HEAD_DIM
CMP_LEN = 32
CMP_STRIDE = 16
CMP_HID = 2 * HEAD_DIM
SEL_LEN = 64
SEL_TOPN = 16
WINDOW = 512
NSA_QBLOCK = 64

AB_SIZES = (ML_W, ML_W, ML_W, ML_W, 2 * ML_HEADS, NSA_W) + (NSA_KV_W,) * 6 + (3 * NSA_HEADS,)
AB_IN = sum(AB_SIZES)
AB_MIX = ML_W + NSA_W

LRU_HEADS = 8
LRU_W = 512
LRU_BLOCK = LRU_W // LRU_HEADS
LRU_CONV = 4
LRU_C = 8.0
SG_GROUPS = 8
SG_W = 512
SG_CHUNK = 128
CD_SIZES = (LRU_W, LRU_W, SG_W, SG_W)
CD_IN = sum(CD_SIZES)
CD_MIX = LRU_W + SG_W

D_FF = 2816
FFN_CONV = 3

kernel_name = "hybrid_mlstm_nsa_rglru_sgu_block"


def rmsnorm(x, g):
    xf = x.astype(jnp.float32)
    y = xf * lax.rsqrt(jnp.mean(xf * xf, axis=-1, keepdims=True) + EPS)
    return (y * g.astype(jnp.float32)).astype(x.dtype)


def layernorm(x, g, b):
    xf = x.astype(jnp.float32)
    mu = jnp.mean(xf, axis=-1, keepdims=True)
    xc = xf - mu
    y = xc * lax.rsqrt(jnp.mean(xc * xc, axis=-1, keepdims=True) + EPS)
    return (y * g.astype(jnp.float32) + b.astype(jnp.float32)).astype(x.dtype)


def softcap(x):
    return GATE_SOFTCAP * jnp.tanh(x / GATE_SOFTCAP)


def split_cols(z, sizes):
    return jnp.split(z, np.cumsum(sizes)[:-1].tolist(), axis=-1)


def causal_dwconv(x, w, b):
    K = w.shape[0]
    y = lax.conv_general_dilated(x, w[:, None, :], window_strides=(1,), padding=[(K - 1, 0)],
                                 dimension_numbers=('NWC', 'WIO', 'NWC'),
                                 feature_group_count=x.shape[-1])
    return y + b


def alibi_slopes(n_heads):
    return jnp.asarray(np.array([2.0 ** (-8.0 * (h + 1) / n_heads) for h in range(n_heads)], np.float32))


def masked_softmax(s, mask):
    s = jnp.where(mask, s.astype(jnp.float32), -jnp.inf)
    m = jnp.max(s, axis=-1, keepdims=True)
    m = jnp.where(jnp.isfinite(m), m, 0.0)
    e = jnp.where(mask, jnp.exp(s - m), 0.0)
    return e / jnp.maximum(jnp.sum(e, axis=-1, keepdims=True), 1.0)


def mlstm(q, k, v, i_pre, f_pre):
    B, S, H, d = q.shape
    L = ML_CHUNK
    nc = S // L
    f32 = jnp.float32

    def to_chunks(a):
        return a.astype(f32).reshape(B, nc, L, H, d).transpose(1, 0, 3, 2, 4)

    def gate_chunks(a):
        return a.astype(f32).reshape(B, nc, L, H).transpose(1, 0, 3, 2)

    qc = to_chunks(q)
    kc = to_chunks(k) * (d ** -0.5)
    vc = to_chunks(v)
    ic = gate_chunks(i_pre)
    lfc = gate_chunks(jax.nn.log_sigmoid(f_pre))
    causal = jnp.tril(jnp.ones((L, L), bool))

    def step(carry, xs):
        C, n, m = carry
        qj, kj, vj, ij, lf = xs
        b = jnp.cumsum(lf, axis=-1)
        g = b[..., -1]
        dlog = jnp.where(causal, b[..., :, None] - b[..., None, :] + ij[..., None, :], -jnp.inf)
        inter = b + m[..., None]
        m_row = jnp.maximum(inter, jnp.max(dlog, axis=-1))
        s = jnp.einsum('bhjd,bhsd->bhjs', qj, kj) * jnp.exp(dlog - m_row[..., None])
        w_inter = jnp.exp(inter - m_row)
        num = jnp.einsum('bhjs,bhsd->bhjd', s, vj) + w_inter[..., None] * jnp.einsum('bhjd,bhde->bhje', qj, C)
        den = jnp.sum(s, axis=-1) + w_inter * jnp.einsum('bhjd,bhd->bhj', qj, n)
        h = num / jnp.maximum(jnp.abs(den), jnp.exp(-m_row))[..., None]
        wlog = g[..., None] - b + ij
        m_new = jnp.maximum(g + m, jnp.max(wlog, axis=-1))
        w = jnp.exp(wlog - m_new[..., None])
        decay = jnp.exp(g + m - m_new)
        C_new = decay[..., None, None] * C + jnp.einsum('bhs,bhsd,bhse->bhde', w, kj, vj)
        n_new = decay[..., None] * n + jnp.einsum('bhs,bhsd->bhd', w, kj)
        return (C_new, n_new, m_new), h

    init = (jnp.zeros((B, H, d, d), f32), jnp.zeros((B, H, d), f32), jnp.zeros((B, H), f32))
    _, h = lax.scan(step, init, (qc, kc, vc, ic, lfc))
    return h.transpose(1, 0, 3, 2, 4).reshape(B, S, H, d)


def cmp_block_index(S):
    n = (S - CMP_LEN) // CMP_STRIDE + 1
    return (np.arange(n)[:, None] * CMP_STRIDE + np.arange(CMP_LEN)[None, :]).astype(np.int32)


def compress_kv(k, pe, w1, w2):
    idx = cmp_block_index(k.shape[1])
    blocks = k[:, idx] + pe[:, None, :]
    hid = jax.nn.gelu(jnp.einsum('bnlgd,ldh->bngh', blocks, w1))
    return jnp.einsum('bngh,he->bnge', hid, w2)


def nsa(q, kcmp, vcmp, ks, vs, kw, vw, gates):
    B, S, H, d = q.shape
    G = NSA_KV_HEADS
    hpg = H // G
    QB = NSA_QBLOCK
    f32 = jnp.float32
    scale = d ** -0.5
    slopes = alibi_slopes(H).reshape(G, hpg)
    cidx = cmp_block_index(S)
    cmp_end = cidx[:, -1]
    cmp_center = cidx.mean(axis=1).astype(np.float32)
    n_sel = S // SEL_LEN
    top_n = min(SEL_TOPN, n_sel)
    sel_start = np.arange(n_sel)[:, None] * SEL_LEN
    overlap = jnp.asarray(((cidx[None, :, 0] < sel_start + SEL_LEN) &
                           (cidx[None, :, -1] >= sel_start)).astype(np.float32))
    ks_blk = ks.reshape(B, n_sel, SEL_LEN, G, d).transpose(0, 3, 1, 2, 4)
    vs_blk = vs.reshape(B, n_sel, SEL_LEN, G, d).transpose(0, 3, 1, 2, 4)
    kw_pad = jnp.pad(kw, ((0, 0), (WINDOW, 0), (0, 0), (0, 0)))
    vw_pad = jnp.pad(vw, ((0, 0), (WINDOW, 0), (0, 0), (0, 0)))
    bi = jnp.arange(B)[:, None, None, None]
    gi = jnp.arange(G)[None, :, None, None]
    blk = jnp.arange(n_sel)

    def block(q0):
        qb = lax.dynamic_slice_in_dim(q, q0, QB, axis=1).reshape(B, QB, G, hpg, d)
        gb = lax.dynamic_slice_in_dim(gates, q0, QB, axis=1).reshape(B, QB, G, hpg, 3)
        t = q0 + jnp.arange(QB)
        tf = t.astype(f32)
        s = jnp.einsum('bqghd,bngd->bghqn', qb, kcmp).astype(f32) * scale
        s = s - slopes[:, :, None, None] * (tf[:, None] - cmp_center[None, :])
        p_c = masked_softmax(s, cmp_end[None, :] <= t[:, None])
        o_c = jnp.einsum('bghqn,bngd->bqghd', p_c.astype(vcmp.dtype), vcmp)
        imp = jnp.einsum('bghqn,jn->bgqj', p_c, overlap)
        cur = t // SEL_LEN
        valid = blk[None, :] <= cur[:, None]
        forced = (blk[None, :] == 0) | (blk[None, :] == cur[:, None]) | (blk[None, :] == cur[:, None] - 1)
        score = jnp.where(forced, jnp.inf, jnp.where(valid, imp, -jnp.inf))
        _, idx = lax.top_k(score, top_n)
        idx_valid = idx <= cur[:, None]
        ksel = ks_blk[bi, gi, idx]
        vsel = vs_blk[bi, gi, idx]
        pos = idx[..., None] * SEL_LEN + jnp.arange(SEL_LEN)
        mask_s = idx_valid[..., None] & (pos <= t[:, None, None])
        s = jnp.einsum('bqghd,bgqnld->bghqnl', qb, ksel).astype(f32) * scale
        s = s - slopes[None, :, :, None, None, None] * (tf[:, None, None] - pos.astype(f32))[:, :, None]
        p_s = masked_softmax(s.reshape(B, G, hpg, QB, top_n * SEL_LEN),
                             mask_s[:, :, None].reshape(B, G, 1, QB, top_n * SEL_LEN))
        o_s = jnp.einsum('bghqm,bgqmd->bqghd', p_s.astype(vsel.dtype),
                         vsel.reshape(B, G, QB, top_n * SEL_LEN, d))
        kwb = lax.dynamic_slice_in_dim(kw_pad, q0, WINDOW + QB, axis=1)
        vwb = lax.dynamic_slice_in_dim(vw_pad, q0, WINDOW + QB, axis=1)
        kpos = q0 - WINDOW + jnp.arange(WINDOW + QB)
        delta = t[:, None] - kpos[None, :]
        mask_w = (kpos[None, :] >= 0) & (delta >= 0) & (delta < WINDOW)
        s = jnp.einsum('bqghd,bngd->bghqn', qb, kwb).astype(f32) * scale
        s = s - slopes[:, :, None, None] * delta.astype(f32)
        p_w = masked_softmax(s, mask_w)
        o_w = jnp.einsum('bghqn,bngd->bqghd', p_w.astype(vwb.dtype), vwb)
        o = gb[..., 0:1] * o_c + gb[..., 1:2] * o_s + gb[..., 2:3] * o_w
        return o.reshape(B, QB, H * d)

    out = lax.map(block, jnp.arange(S // QB) * QB)
    return out.transpose(1, 0, 2, 3).reshape(B, S, H * d)


def mixer_ab(xn, w_in, w_out, ml_gate_b, ml_norm_g, nsa_gate_b,
             cmp_k_pe, cmp_k_w1, cmp_k_w2, cmp_v_pe, cmp_v_w1, cmp_v_w2):
    B, S, _ = xn.shape
    (mq, mk, mv, mo, mif, nq, kc, vc, ks, vs, kw, vw, ng) = split_cols(xn @ w_in, AB_SIZES)

    def heads(a):
        return a.reshape(B, S, -1, HEAD_DIM)

    gif = softcap(mif + ml_gate_b)
    h_ml = mlstm(heads(mq), heads(mk), heads(mv), gif[..., :ML_HEADS], gif[..., ML_HEADS:])
    h_ml = rmsnorm(h_ml, ml_norm_g.reshape(ML_HEADS, HEAD_DIM)).reshape(B, S, ML_W)
    h_ml = h_ml.astype(xn.dtype) * jax.nn.sigmoid(mo)
    kcmp = compress_kv(heads(kc), cmp_k_pe, cmp_k_w1, cmp_k_w2)
    vcmp = compress_kv(heads(vc), cmp_v_pe, cmp_v_w1, cmp_v_w2)
    branch_gates = jax.nn.sigmoid(ng + nsa_gate_b).reshape(B, S, NSA_HEADS, 3)
    h_nsa = nsa(heads(nq), kcmp, vcmp, heads(ks), heads(vs), heads(kw), heads(vw), branch_gates)
    return jnp.concatenate([h_ml, h_nsa.astype(xn.dtype)], axis=-1) @ w_out


def _linear_combine(left, right):
    a1, b1 = left
    a2, b2 = right
    return a1 * a2, a2 * b1 + b2


def rg_lru(xc, wa, ba, wx, bx, lam):
    B, S, _ = xc.shape
    f32 = jnp.float32
    xb = xc.reshape(B, S, LRU_HEADS, LRU_BLOCK)
    r = jax.nn.sigmoid(jnp.einsum('bshi,hij->bshj', xb, wa).reshape(B, S, LRU_W) + ba).astype(f32)
    i = jax.nn.sigmoid(jnp.einsum('bshi,hij->bshj', xb, wx).reshape(B, S, LRU_W) + bx).astype(f32)
    log_a = -LRU_C * r * jax.nn.softplus(-lam.astype(f32))
    a = jnp.exp(log_a)
    u = jnp.sqrt(-jnp.expm1(2.0 * log_a)) * (i * xc.astype(f32))
    _, h = lax.associative_scan(_linear_combine, (a, u), axis=1)
    return h


def spatial_gating(u, v, ln_g, ln_b, w, b):
    B, S, _ = v.shape
    nc = S // SG_CHUNK
    vn = layernorm(v, ln_g, ln_b).reshape(B, nc, SG_CHUNK, SG_GROUPS, SG_W // SG_GROUPS)
    w_causal = w * jnp.tril(jnp.ones((SG_CHUNK, SG_CHUNK), w.dtype))
    mixed = jnp.einsum('gts,bcsgd->bctgd', w_causal, vn) + b.T[:, :, None]
    return u * mixed.reshape(B, S, SG_W).astype(u.dtype)


def mixer_cd(xn, w_in, w_out, conv_w, conv_b, wa, ba, wx, bx, lam, sg_g, sg_bn, sg_w, sg_b):
    gate, xr, su, sv = split_cols(xn @ w_in, CD_SIZES)
    xc = causal_dwconv(xr, conv_w, conv_b)
    y_lru = jax.nn.gelu(gate) * rg_lru(xc, wa, ba, wx, bx, lam).astype(xn.dtype)
    y_sg = spatial_gating(jax.nn.gelu(su), jax.nn.gelu(sv), sg_g, sg_bn, sg_w, sg_b)
    return jnp.concatenate([y_lru, y_sg], axis=-1) @ w_out


def conv_ffn(xn, w_up, conv_w, conv_b, w_down):
    hcat = causal_dwconv(xn @ w_up, conv_w, conv_b)
    a, b = jnp.split(hcat, 2, axis=-1)
    return (jax.nn.gelu(a) * b) @ w_down


def setup_inputs(seed: int = 0) -> dict:
    key = jax.random.key(seed)
    keys = iter(jax.random.split(key, 48))

    def nrm(shape, scale):
        return jax.random.normal(next(keys), shape, jnp.float32) * scale

    def gain(shape):
        return 1.0 + nrm(shape, 0.02)

    d = HEAD_DIM
    u = jax.random.uniform(next(keys), (N_ODD, LRU_W), jnp.float32, 0.9, 0.999)
    a = u ** (1.0 / LRU_C)
    lru_lambda = jnp.log(a) - jnp.log1p(-a)
    ml_gate_b = jnp.concatenate([nrm((N_EVEN, ML_HEADS), 0.1),
                                 3.0 + nrm((N_EVEN, ML_HEADS), 0.5)], axis=-1)
    return {
        'x': nrm((BATCH, SEQ, D_MODEL), 1.0),
        'pre_mix_g': gain((DEPTH, D_MODEL)),
        'post_mix_g': gain((DEPTH, D_MODEL)),
        'pre_ffn_g': gain((DEPTH, D_MODEL)),
        'post_ffn_g': gain((DEPTH, D_MODEL)),
        'ab_w_in': nrm((N_EVEN, D_MODEL, AB_IN), D_MODEL ** -0.5),
        'ab_w_out': nrm((N_EVEN, AB_MIX, D_MODEL), AB_MIX ** -0.5),
        'ml_gate_b': ml_gate_b,
        'ml_norm_g': gain((N_EVEN, ML_W)),
        'nsa_gate_b': nrm((N_EVEN, 3 * NSA_HEADS), 0.1),
        'cmp_k_pe': nrm((N_EVEN, CMP_LEN, d), 0.02),
        'cmp_k_w1': nrm((N_EVEN, CMP_LEN, d, CMP_HID), (CMP_LEN * d) ** -0.5),
        'cmp_k_w2': nrm((N_EVEN, CMP_HID, d), CMP_HID ** -0.5),
        'cmp_v_pe': nrm((N_EVEN, CMP_LEN, d), 0.02),
        'cmp_v_w1': nrm((N_EVEN, CMP_LEN, d, CMP_HID), (CMP_LEN * d) ** -0.5),
        'cmp_v_w2': nrm((N_EVEN, CMP_HID, d), CMP_HID ** -0.5),
        'cd_w_in': nrm((N_ODD, D_MODEL, CD_IN), D_MODEL ** -0.5),
        'cd_w_out': nrm((N_ODD, CD_MIX, D_MODEL), CD_MIX ** -0.5),
        'lru_conv_w': nrm((N_ODD, LRU_CONV, LRU_W), LRU_CONV ** -0.5),
        'lru_conv_b': nrm((N_ODD, LRU_W), 0.01),
        'lru_wa': nrm((N_ODD, LRU_HEADS, LRU_BLOCK, LRU_BLOCK), LRU_BLOCK ** -0.5),
        'lru_ba': nrm((N_ODD, LRU_W), 0.01),
        'lru_wx': nrm((N_ODD, LRU_HEADS, LRU_BLOCK, LRU_BLOCK), LRU_BLOCK ** -0.5),
        'lru_bx': nrm((N_ODD, LRU_W), 0.01),
        'lru_lambda': lru_lambda,
        'sg_norm_g': gain((N_ODD, SG_W)),
        'sg_norm_b': nrm((N_ODD, SG_W), 0.01),
        'sg_w': nrm((N_ODD, SG_GROUPS, SG_CHUNK, SG_CHUNK), 0.5 * SG_CHUNK ** -0.5),
        'sg_b': 1.0 + nrm((N_ODD, SG_GROUPS, SG_CHUNK), 0.1),
        'ffn_w_up': nrm((DEPTH, D_MODEL, 2 * D_FF), D_MODEL ** -0.5),
        'ffn_conv_w': nrm((DEPTH, FFN_CONV, 2 * D_FF), FFN_CONV ** -0.5),
        'ffn_conv_b': nrm((DEPTH, 2 * D_FF), 0.01),
        'ffn_w_down': nrm((DEPTH, D_FF, D_MODEL), D_FF ** -0.5),
    }


def reference(x, pre_mix_g, post_mix_g, pre_ffn_g, post_ffn_g, ab_w_in, ab_w_out, ml_gate_b,
              ml_norm_g, nsa_gate_b, cmp_k_pe, cmp_k_w1, cmp_k_w2, cmp_v_pe, cmp_v_w1, cmp_v_w2,
              cd_w_in, cd_w_out, lru_conv_w, lru_conv_b, lru_wa, lru_ba, lru_wx, lru_bx, lru_lambda,
              sg_norm_g, sg_norm_b, sg_w, sg_b, ffn_w_up, ffn_conv_w, ffn_conv_b, ffn_w_down):
    h = x
    for layer in range(DEPTH):
        xn = rmsnorm(h, pre_mix_g[layer])
        if layer % 2 == 0:
            e = layer // 2
            y = mixer_ab(xn, ab_w_in[e], ab_w_out[e], ml_gate_b[e], ml_norm_g[e], nsa_gate_b[e],
                         cmp_k_pe[e], cmp_k_w1[e], cmp_k_w2[e], cmp_v_pe[e], cmp_v_w1[e], cmp_v_w2[e])
        else:
            o = layer // 2
            y = mixer_cd(xn, cd_w_in[o], cd_w_out[o], lru_conv_w[o], lru_conv_b[o], lru_wa[o], lru_ba[o],
                         lru_wx[o], lru_bx[o], lru_lambda[o], sg_norm_g[o], sg_norm_b[o], sg_w[o], sg_b[o])
        h = h + rmsnorm(y, post_mix_g[layer])
        y = conv_ffn(rmsnorm(h, pre_ffn_g[layer]), ffn_w_up[layer], ffn_conv_w[layer],
                     ffn_conv_b[layer], ffn_w_down[layer])
        h = h + rmsnorm(y, post_ffn_g[layer])
    return h
```

```python
import functools

import numpy as np
import jax
import jax.numpy as jnp
from jax import lax
from jax.experimental import pallas as pl
from jax.experimental.pallas import tpu as pltpu

F32 = jnp.float32
BF16 = jnp.bfloat16

EPS = 1e-6
HEAD_DIM = 64
ML_HEADS = 8
ML_W = 512
GATE_SOFTCAP = 15.0
NSA_HEADS = 8
NSA_KV_HEADS = 2
NSA_HPG = NSA_HEADS // NSA_KV_HEADS
NSA_W = 512
CMP_LEN = 32
CMP_STRIDE = 16
CMP_HID = 128
SEL_LEN = 64
SEL_TOPN = 16
WINDOW = 512
LRU_W = 512
LRU_C = 8.0
LRU_CONV = 4
SG_GROUPS = 8
SG_W = 512
SG_CHUNK = 128
FFN_CONV = 3

LANES = 128
VMEM_LIMIT = 56 * 1024 * 1024
NEG_BIG = -1e30
HIGHEST = lax.Precision.HIGHEST

ML_CHUNK = 128
NSA_TQ = 128
NSA_TK = 256
ROW_TILE = 512
FFN_CK = 256
FFN_HALO = 16
LRU_T = 256
LRU_HALO = 8


def _cparams(sem):
    return pltpu.CompilerParams(dimension_semantics=sem, vmem_limit_bytes=VMEM_LIMIT)


def _rms(x, g):
    return x * lax.rsqrt(jnp.mean(x * x, axis=-1, keepdims=True) + EPS) * g


def _gelu(x):
    return 0.5 * x * (1.0 + jnp.tanh(0.7978845608028654 * (x + 0.044715 * (x * x * x))))


def _sigmoid(x):
    return 1.0 / (1.0 + jnp.exp(-x))


def _dot(a, b):
    return jnp.dot(a, b, preferred_element_type=F32)


def _dot_nt(a, b, precision=None):
    return lax.dot_general(a, b, (((1,), (1,)), ((), ())), preferred_element_type=F32, precision=precision)


def _dot_tn(a, b):
    return lax.dot_general(a, b, (((0,), (0,)), ((), ())), preferred_element_type=F32)


def _norm_proj_kernel(h_ref, g_ref, *refs, n_out, cn):
    w_refs, o_refs = refs[:n_out], refs[n_out:]
    xn = _rms(h_ref[...], g_ref[...]).astype(BF16)
    for w_ref, o_ref in zip(w_refs, o_refs):
        n = w_ref.shape[1]
        for c in range(0, n, cn):
            ce = min(c + cn, n)
            o_ref[:, c:ce] = _dot(xn, w_ref[:, c:ce]).astype(o_ref.dtype)


def norm_proj(h2, g, ws, dtypes, tm=ROW_TILE):
    M, D = h2.shape
    in_specs = [pl.BlockSpec((tm, D), lambda i: (i, 0)), pl.BlockSpec((1, D), lambda i: (0, 0))]
    in_specs += [pl.BlockSpec(w.shape, lambda i: (0, 0)) for w in ws]
    out_specs = [pl.BlockSpec((tm, w.shape[1]), lambda i: (i, 0)) for w in ws]
    out_shape = [jax.ShapeDtypeStruct((M, w.shape[1]), dt) for w, dt in zip(ws, dtypes)]
    return pl.pallas_call(
        functools.partial(_norm_proj_kernel, n_out=len(ws), cn=512),
        grid=(M // tm,), in_specs=in_specs, out_specs=out_specs, out_shape=out_shape,
        compiler_params=_cparams(("parallel",)), name="norm_proj",
    )(h2, g.reshape(1, D), *ws)


def _mix_out_kernel(a1_ref, a2_ref, w1_ref, w2_ref, g_ref, h_ref, o_ref):
    y = _dot(a1_ref[...], w1_ref[...]) + _dot(a2_ref[...], w2_ref[...])
    o_ref[...] = h_ref[...] + _rms(y, g_ref[...])


def mix_out(a1, a2, w_out, g, h2, tm=ROW_TILE):
    M, D = h2.shape
    K1, K2 = a1.shape[1], a2.shape[1]
    w1 = w_out[:K1].astype(BF16)
    w2 = w_out[K1:].astype(BF16)
    return pl.pallas_call(
        _mix_out_kernel, grid=(M // tm,),
        in_specs=[pl.BlockSpec((tm, K1), lambda i: (i, 0)), pl.BlockSpec((tm, K2), lambda i: (i, 0)),
                  pl.BlockSpec((K1, D), lambda i: (0, 0)), pl.BlockSpec((K2, D), lambda i: (0, 0)),
                  pl.BlockSpec((1, D), lambda i: (0, 0)), pl.BlockSpec((tm, D), lambda i: (i, 0))],
        out_specs=pl.BlockSpec((tm, D), lambda i: (i, 0)),
        out_shape=jax.ShapeDtypeStruct((M, D), F32),
        compiler_params=_cparams(("parallel",)), name="mix_out",
    )(a1, a2, w1, w2, g.reshape(1, D), h2)


def _ffn_kernel(h_ref, halo_ref, gpre_ref, wa_ref, wb_ref, cwa_ref, cwb_ref, cba_ref, cbb_ref, wd_ref, gpost_ref,
                o_ref, xn_s, acc_s, *, tiles_per_seq, n_chunks):
    tm = h_ref.shape[0]
    first = (pl.program_id(0) % tiles_per_seq) == 0
    x = h_ref[...]
    g = gpre_ref[...]
    xn_s[0:FFN_HALO, :] = jnp.where(first, 0.0, _rms(halo_ref[...], g)).astype(BF16)
    xn_s[FFN_HALO:, :] = _rms(x, g).astype(BF16)
    acc_s[...] = jnp.zeros_like(acc_s)

    def conv(u, cw, cb):
        y = cw[2:3, :] * u + cw[1:2, :] * pltpu.roll(u, 1, 0) + cw[0:1, :] * pltpu.roll(u, 2, 0) + cb
        return y[FFN_HALO:, :]

    def body(c, carry):
        xn = xn_s[...]
        a = conv(_dot(xn, wa_ref[c]), cwa_ref[c], cba_ref[c])
        b = conv(_dot(xn, wb_ref[c]), cwb_ref[c], cbb_ref[c])
        act = (_gelu(a) * b).astype(BF16)
        acc_s[...] += _dot(act, wd_ref[c])
        return carry

    lax.fori_loop(0, n_chunks, body, 0)
    o_ref[...] = x + _rms(acc_s[...], gpost_ref[...])


def ffn_block(h2, seq_len, g_pre, w_up, conv_w, conv_b, w_down, g_post, tm=ROW_TILE, ck=FFN_CK):
    M, D = h2.shape
    F = w_down.shape[0]
    nck = F // ck
    assert F % ck == 0 and seq_len % tm == 0 and tm % FFN_HALO == 0

    def chunk_cols(a):
        return a.reshape(a.shape[0], nck, ck).transpose(1, 0, 2)

    wa = chunk_cols(w_up[:, :F]).astype(BF16)
    wb = chunk_cols(w_up[:, F:]).astype(BF16)
    cwa = chunk_cols(conv_w[:, :F])
    cwb = chunk_cols(conv_w[:, F:])
    cba = chunk_cols(conv_b[None, :F])
    cbb = chunk_cols(conv_b[None, F:])
    wd = w_down.reshape(nck, ck, D).astype(BF16)
    hb = tm // FFN_HALO

    def full(a):
        return pl.BlockSpec(a.shape, lambda i: (0,) * a.ndim)

    return pl.pallas_call(
        functools.partial(_ffn_kernel, tiles_per_seq=seq_len // tm, n_chunks=nck),
        grid=(M // tm,),
        in_specs=[pl.BlockSpec((tm, D), lambda i: (i, 0)),
                  pl.BlockSpec((FFN_HALO, D), lambda i: (jnp.maximum(i * hb - 1, 0), 0)),
                  pl.BlockSpec((1, D), lambda i: (0, 0)),
                  full(wa), full(wb), full(cwa), full(cwb), full(cba), full(cbb), full(wd),
                  pl.BlockSpec((1, D), lambda i: (0, 0))],
        out_specs=pl.BlockSpec((tm, D), lambda i: (i, 0)),
        out_shape=jax.ShapeDtypeStruct((M, D), F32),
        scratch_shapes=[pltpu.VMEM((tm + FFN_HALO, D), BF16), pltpu.VMEM((tm, D), F32)],
        compiler_params=_cparams(("parallel",)), name="ffn_block",
    )(h2, h2, g_pre.reshape(1, D), wa, wb, cwa, cwb, cba, cbb, wd, g_post.reshape(1, D))


def _mlstm_kernel(q_ref, k_ref, v_ref, mo_ref, gt_ref, gb_ref, ng_ref, o_ref, c_s, n_s, m_s):
    L = q_ref.shape[0]
    d = HEAD_DIM

    @pl.when(pl.program_id(1) == 0)
    def _():
        c_s[...] = jnp.zeros_like(c_s)
        n_s[...] = jnp.zeros_like(n_s)
        m_s[...] = jnp.zeros_like(m_s)

    gcap = GATE_SOFTCAP * jnp.tanh((gt_ref[...] + gb_ref[...]) * (1.0 / GATE_SOFTCAP))
    lf = jnp.minimum(gcap, 0.0) - jnp.log1p(jnp.exp(-jnp.abs(gcap)))
    row = lax.broadcasted_iota(jnp.int32, (L, L), 0)
    col = lax.broadcasted_iota(jnp.int32, (L, L), 1)
    causal = col <= row
    tri = causal.astype(F32)
    b_col = jnp.dot(tri, lf, preferred_element_type=F32, precision=HIGHEST)
    b_row = _dot_nt(lf.T, tri, precision=HIGHEST)
    i_row = gcap.T

    for h in range(ML_HEADS):
        sl = slice(h * d, (h + 1) * d)
        q = q_ref[:, sl]
        k = k_ref[:, sl] * 0.125
        v = v_ref[:, sl]
        bc = b_col[:, ML_HEADS + h:ML_HEADS + h + 1]
        br = b_row[ML_HEADS + h:ML_HEADS + h + 1, :]
        ir = i_row[h:h + 1, :]
        ic = gcap[:, h:h + 1]
        g = bc[L - 1:L, :]
        m_prev = m_s[h:h + 1, 0:1]
        c_prev = c_s[h]
        n_prev = n_s[h:h + 1, 0:d]

        dlog = jnp.where(causal, bc - br + ir, -jnp.inf)
        inter = bc + m_prev
        m_row = jnp.maximum(inter, jnp.max(dlog, axis=-1, keepdims=True))
        s = _dot_nt(q, k) * jnp.exp(dlog - m_row)
        w_inter = jnp.exp(inter - m_row)
        qf = q.astype(F32)
        num = _dot(s.astype(BF16), v) + w_inter * _dot(q, c_prev.astype(BF16))
        den = jnp.sum(s, axis=-1, keepdims=True) + w_inter * jnp.sum(qf * n_prev, axis=-1, keepdims=True)
        hh = num / jnp.maximum(jnp.abs(den), jnp.exp(-m_row))

        m_new = jnp.maximum(g + m_prev, jnp.max(g - br + ir, axis=-1, keepdims=True))
        w_col = jnp.exp(g - bc + ic - m_new)
        decay = jnp.exp(g + m_prev - m_new)
        kf = k.astype(F32)
        c_s[h] = decay * c_prev + _dot_tn(k, (w_col * v.astype(F32)).astype(BF16))
        n_s[h:h + 1, 0:d] = decay * n_prev + jnp.sum(w_col * kf, axis=0, keepdims=True)
        m_s[h:h + 1, :] = jnp.broadcast_to(m_new, (1, m_s.shape[1]))

        hn = hh * lax.rsqrt(jnp.mean(hh * hh, axis=-1, keepdims=True) + EPS) * ng_ref[:, sl]
        o_ref[:, sl] = (hn * _sigmoid(mo_ref[:, sl])).astype(o_ref.dtype)


def mlstm_block(zb, zf, gate_bias, norm_g, L=ML_CHUNK):
    B, S, _ = zb.shape
    W = ML_W
    return pl.pallas_call(
        _mlstm_kernel, grid=(B, S // L),
        in_specs=[pl.BlockSpec((None, L, W), lambda b, c: (b, c, 0)),
                  pl.BlockSpec((None, L, W), lambda b, c: (b, c, 1)),
                  pl.BlockSpec((None, L, W), lambda b, c: (b, c, 2)),
                  pl.BlockSpec((None, L, W), lambda b, c: (b, c, 0)),
                  pl.BlockSpec((None, L, LANES), lambda b, c: (b, c, 6)),
                  pl.BlockSpec((1, LANES), lambda b, c: (0, 0)),
                  pl.BlockSpec((1, W), lambda b, c: (0, 0))],
        out_specs=pl.BlockSpec((None, L, W), lambda b, c: (b, c, 0)),
        out_shape=jax.ShapeDtypeStruct((B, S, W), BF16),
        scratch_shapes=[pltpu.VMEM((ML_HEADS, HEAD_DIM, HEAD_DIM), F32), pltpu.VMEM((ML_HEADS, LANES), F32),
                        pltpu.VMEM((ML_HEADS, LANES), F32)],
        compiler_params=_cparams(("parallel", "arbitrary")), name="mlstm",
    )(zb, zb, zb, zf, zf, gate_bias, norm_g.reshape(1, W))


def _compress_kernel(xk_ref, xv_ref, pek_ref, pev_ref, w1k_ref, w1v_ref, w2k_ref, w2v_ref, ok_ref, ov_ref):
    def one(x_ref, pe_ref, w1_ref, w2_ref, o_ref):
        x = x_ref[...]
        n = x.shape[0]
        xa = (x + pe_ref[0:1, :]).astype(BF16)
        xb = (pltpu.roll(x, n - 1, 0) + pe_ref[1:2, :]).astype(BF16)
        hid = _gelu(_dot(xa, w1_ref[0]) + _dot(xb, w1_ref[1]))
        o_ref[...] = _dot(hid.astype(BF16), w2_ref[...]).astype(o_ref.dtype)

    one(xk_ref, pek_ref, w1k_ref, w2k_ref, ok_ref)
    one(xv_ref, pev_ref, w1v_ref, w2v_ref, ov_ref)


def compress_block(kc, vc, k_pe, k_w1, k_w2, v_pe, v_w1, v_w2):
    B, S, _ = kc.shape
    G, d = NSA_KV_HEADS, HEAD_DIM
    nh = S // CMP_STRIDE
    half = CMP_STRIDE * d

    def relayout(a):
        return a.reshape(B, nh, CMP_STRIDE, G, d).transpose(0, 3, 1, 2, 4).reshape(B, G, nh, half)

    def prep(pe, w1):
        return pe.reshape(2, half), w1.reshape(2, half, CMP_HID).astype(BF16)

    pek, w1k = prep(k_pe, k_w1)
    pev, w1v = prep(v_pe, v_w1)
    xspec = pl.BlockSpec((None, None, nh, half), lambda b, g: (b, g, 0, 0))
    ospec = pl.BlockSpec((None, None, nh, d), lambda b, g: (b, g, 0, 0))

    def full(a):
        return pl.BlockSpec(a.shape, lambda b, g: (0,) * a.ndim)

    w2k, w2v = k_w2.astype(BF16), v_w2.astype(BF16)
    return pl.pallas_call(
        _compress_kernel, grid=(B, G),
        in_specs=[xspec, xspec, full(pek), full(pev), full(w1k), full(w1v), full(w2k), full(w2v)],
        out_specs=[ospec, ospec],
        out_shape=[jax.ShapeDtypeStruct((B, G, nh, d), BF16)] * 2,
        compiler_params=_cparams(("parallel", "parallel")), name="nsa_compress",
    )(relayout(kc), relayout(vc), pek, pev, w1k, w1v, w2k, w2v)


def _nsa_kernel(q_ref, gt_ref, gb_ref, kc_ref, vc_ref, ks_ref, vs_ref, kw_ref, vw_ref, ovl_ref, exp_ref, o_ref,
                *, n_cmp):
    TQ, TK, d, HPG = NSA_TQ, NSA_TK, HEAD_DIM, NSA_HPG
    g = pl.program_id(1)
    qi = pl.program_id(2)
    q0 = qi * TQ
    kt_d = q0 // TK
    n_cb = kc_ref.shape[0]
    n_sb = ovl_ref.shape[0]

    def slope(hh):
        return jnp.where(g == 0, 2.0 ** (-(hh + 1)), 2.0 ** (-(HPG + hh + 1))).astype(F32)

    qt = q_ref[...] * 0.125
    q4 = jnp.concatenate([qt[:, hh * d:(hh + 1) * d] for hh in range(HPG)], axis=0)

    def stack_heads(base, bias_unit, mask):
        parts = [jnp.where(mask, base[hh * TQ:(hh + 1) * TQ] - slope(hh) * bias_unit, NEG_BIG) for hh in range(HPG)]
        return jnp.concatenate(parts, axis=0)

    t_c = (q0 + lax.broadcasted_iota(jnp.int32, (TQ, n_cb), 0))
    n_c = lax.broadcasted_iota(jnp.int32, (TQ, n_cb), 1)
    mask_c = (n_c * CMP_STRIDE + (CMP_LEN - 1) <= t_c) & (n_c < n_cmp)
    dist_c = t_c.astype(F32) - (n_c.astype(F32) * CMP_STRIDE + (CMP_LEN - 1) / 2.0)
    s_c = stack_heads(_dot_nt(q4, kc_ref[...]), dist_c, mask_c)
    mask_c4 = jnp.concatenate([mask_c] * HPG, axis=0)
    m_c = jnp.max(s_c, axis=-1, keepdims=True)
    e_c = jnp.where(mask_c4, jnp.exp(s_c - m_c), 0.0)
    p_c = e_c / jnp.maximum(jnp.sum(e_c, axis=-1, keepdims=True), 1.0)
    o_c = _dot(p_c.astype(BF16), vc_ref[...])

    p_sum = p_c[0:TQ]
    for hh in range(1, HPG):
        p_sum = p_sum + p_c[hh * TQ:(hh + 1) * TQ]
    imp_t = _dot_nt(ovl_ref[...], p_sum, precision=HIGHEST)
    jb = lax.broadcasted_iota(jnp.int32, (n_sb, TQ), 0)
    cur = (q0 + lax.broadcasted_iota(jnp.int32, (n_sb, TQ), 1)) // SEL_LEN
    valid = jb <= cur
    forced = (jb == 0) | (jb == cur) | (jb == cur - 1)
    score = jnp.where(forced, jnp.inf, jnp.where(valid, imp_t, -jnp.inf))
    rank = jnp.zeros((n_sb, TQ), jnp.int32)
    for j in range(n_sb):
        r = score[j:j + 1, :]
        rank = rank + jnp.where(jb > j, (r >= score).astype(jnp.int32), (r > score).astype(jnp.int32))
    sel_t = jnp.where(valid & (rank < SEL_TOPN), 1.0, 0.0)
    sel = sel_t.T.astype(BF16)

    def tile_update(carry, kt, k_ref, v_ref, mask_fn):
        m, l, acc = carry
        k0 = pl.multiple_of(kt * TK, TK)
        kk = k_ref[pl.ds(k0, TK), :]
        vv = v_ref[pl.ds(k0, TK), :]
        delta = (q0 - k0 + lax.broadcasted_iota(jnp.int32, (TQ, TK), 0)
                 - lax.broadcasted_iota(jnp.int32, (TQ, TK), 1))
        s = stack_heads(_dot_nt(q4, kk), delta.astype(F32), mask_fn(delta, k0))
        m_new = jnp.maximum(m, jnp.max(s, axis=-1, keepdims=True))
        p = jnp.exp(s - m_new)
        alpha = jnp.exp(m - m_new)
        l = alpha * l + jnp.sum(p, axis=-1, keepdims=True)
        acc = alpha * acc + _dot(p.astype(BF16), vv)
        return m_new, l, acc

    def run_branch(k_ref, v_ref, mask_fn, lo):
        init = (jnp.full((HPG * TQ, 1), NEG_BIG, F32), jnp.zeros((HPG * TQ, 1), F32), jnp.zeros((HPG * TQ, d), F32))
        carry = tile_update(init, kt_d, k_ref, v_ref, mask_fn)
        m, l, acc = lax.fori_loop(lo, kt_d, lambda kt, c: tile_update(c, kt, k_ref, v_ref, mask_fn), carry)
        return acc / l

    def mask_sel(delta, k0):
        picked = _dot(sel, exp_ref[:, pl.ds(k0, TK)])
        return (picked > 0.5) & (delta >= 0)

    def mask_win(delta, k0):
        return (delta >= 0) & (delta < WINDOW)

    o_s = run_branch(ks_ref, vs_ref, mask_sel, 0)
    o_w = run_branch(kw_ref, vw_ref, mask_win, jnp.maximum((q0 - (WINDOW - 1)) // TK, 0))

    gates = _sigmoid(gt_ref[...] + gb_ref[...])
    gsel = jnp.where(g == 0, gates[:, 16:16 + 3 * HPG], gates[:, 16 + 3 * HPG:16 + 6 * HPG])
    for hh in range(HPG):
        rows = slice(hh * TQ, (hh + 1) * TQ)
        o = (gsel[:, 3 * hh:3 * hh + 1] * o_c[rows] + gsel[:, 3 * hh + 1:3 * hh + 2] * o_s[rows]
             + gsel[:, 3 * hh + 2:3 * hh + 3] * o_w[rows])
        o_ref[:, hh * d:(hh + 1) * d] = o.astype(o_ref.dtype)


def nsa_block(zb, zf, gate_bias, kcmp, vcmp, q_col_block, kv_col0):
    B, S, _ = zb.shape
    G, d = NSA_KV_HEADS, HEAD_DIM
    TQ = NSA_TQ
    n_cb = kcmp.shape[2]
    n_cmp = (S - CMP_LEN) // CMP_STRIDE + 1
    n_sb = S // SEL_LEN
    assert S % NSA_TK == 0 and NSA_TK % TQ == 0 and TQ % SEL_LEN == 0

    def per_group(j):
        a = lax.slice_in_dim(zb, kv_col0 + j * G * d, kv_col0 + (j + 1) * G * d, axis=2)
        return a.reshape(B, S, G, d).transpose(0, 2, 1, 3)

    ks, vs, kw, vw = (per_group(j) for j in range(4))
    cidx = np.arange(n_cb)[None, :] * CMP_STRIDE
    sstart = np.arange(n_sb)[:, None] * SEL_LEN
    ovl = ((cidx < sstart + SEL_LEN) & (cidx + CMP_LEN - 1 >= sstart) & (np.arange(n_cb)[None, :] < n_cmp))
    ovl = jnp.asarray(ovl.astype(np.float32))
    expand = jnp.asarray((np.arange(S)[None, :] // SEL_LEN == np.arange(n_sb)[:, None]).astype(np.float32), BF16)

    kvspec = pl.BlockSpec((None, None, S, d), lambda b, g, i: (b, g, 0, 0))
    cspec = pl.BlockSpec((None, None, n_cb, d), lambda b, g, i: (b, g, 0, 0))
    return pl.pallas_call(
        functools.partial(_nsa_kernel, n_cmp=n_cmp), grid=(B, G, S // TQ),
        in_specs=[pl.BlockSpec((None, TQ, NSA_HPG * d), lambda b, g, i: (b, i, q_col_block + g)),
                  pl.BlockSpec((None, TQ, LANES), lambda b, g, i: (b, i, 6)),
                  pl.BlockSpec((1, LANES), lambda b, g, i: (0, 0)),
                  cspec, cspec, kvspec, kvspec, kvspec, kvspec,
                  pl.BlockSpec(ovl.shape, lambda b, g, i: (0, 0)),
                  pl.BlockSpec(expand.shape, lambda b, g, i: (0, 0))],
        out_specs=pl.BlockSpec((None, TQ, NSA_HPG * d), lambda b, g, i: (b, i, g)),
        out_shape=jax.ShapeDtypeStruct((B, S, NSA_W), BF16),
        compiler_params=_cparams(("parallel", "parallel", "arbitrary")), name="nsa",
    )(zb, zf, gate_bias, kcmp, vcmp, ks, vs, kw, vw, ovl, expand)


def _rglru_kernel(gate_ref, x_ref, halo_ref, cw_ref, cb_ref, wa_ref, wx_ref, ba_ref, bx_ref, lam_ref, o_ref, h_s):
    T, W = x_ref.shape
    first = pl.program_id(1) == 0

    @pl.when(first)
    def _():
        h_s[...] = jnp.zeros_like(h_s)

    xe = jnp.concatenate([jnp.where(first, 0.0, halo_ref[...]), x_ref[...]], axis=0)
    xc = cb_ref[...] + cw_ref[3:4, :] * xe
    for k in range(1, LRU_CONV):
        xc = xc + cw_ref[3 - k:4 - k, :] * pltpu.roll(xe, k, 0)
    xc = xc[LRU_HALO:, :]
    xcb = xc.astype(BF16)
    half = W // 2

    def blockdiag(w_ref):
        return jnp.concatenate([_dot(xcb[:, :half], w_ref[0]), _dot(xcb[:, half:], w_ref[1])], axis=1)

    r = _sigmoid(blockdiag(wa_ref) + ba_ref[...])
    i = _sigmoid(blockdiag(wx_ref) + bx_ref[...])
    nl = -lam_ref[...]
    softplus = jnp.maximum(nl, 0.0) + jnp.log1p(jnp.exp(-jnp.abs(nl)))
    log_a = -LRU_C * r * softplus
    a = jnp.exp(log_a)
    u = jnp.sqrt(-jnp.tanh(log_a) * (a * a + 1.0)) * (i * xc)
    row = lax.broadcasted_iota(jnp.int32, (T, W), 0)
    sft = 1
    while sft < T:
        keep = row >= sft
        u = a * jnp.where(keep, pltpu.roll(u, sft, 0), 0.0) + u
        a = a * jnp.where(keep, pltpu.roll(a, sft, 0), 1.0)
        sft *= 2
    h = u + a * h_s[0:1, :]
    h_s[...] = jnp.broadcast_to(h[T - 1:T, :], h_s.shape)
    o_ref[...] = (_gelu(gate_ref[...]) * h).astype(o_ref.dtype)


def rglru_block(z, conv_w, conv_b, wa, ba, wx, bx, lam, T=LRU_T):
    B, S, _ = z.shape
    W = LRU_W
    half = W // 2
    hb = T // LRU_HALO

    def bd(w):
        blocks = [jax.scipy.linalg.block_diag(*[w[h] for h in range(4 * j, 4 * j + 4)]) for j in range(2)]
        return jnp.stack(blocks).astype(BF16)

    vec = lambda a: a.reshape(1, W)
    vspec = pl.BlockSpec((1, W), lambda b, t: (0, 0))
    wspec = pl.BlockSpec((2, half, half), lambda b, t: (0, 0, 0))
    return pl.pallas_call(
        _rglru_kernel, grid=(B, S // T),
        in_specs=[pl.BlockSpec((None, T, W), lambda b, t: (b, t, 0)),
                  pl.BlockSpec((None, T, W), lambda b, t: (b, t, 1)),
                  pl.BlockSpec((None, LRU_HALO, W), lambda b, t: (b, jnp.maximum(t * hb - 1, 0), 1)),
                  pl.BlockSpec((LRU_CONV, W), lambda b, t: (0, 0)), vspec, wspec, wspec, vspec, vspec, vspec],
        out_specs=pl.BlockSpec((None, T, W), lambda b, t: (b, t, 0)),
        out_shape=jax.ShapeDtypeStruct((B, S, W), BF16),
        scratch_shapes=[pltpu.VMEM((8, W), F32)],
        compiler_params=_cparams(("parallel", "arbitrary")), name="rglru",
    )(z, z, z, conv_w, vec(conv_b), bd(wa), bd(wx), vec(ba), vec(bx), vec(lam))


def _sgu_kernel(u_ref, v_ref, g_ref, b_ref, w_ref, bias_ref, o_ref):
    C, W = v_ref.shape
    dg = W // SG_GROUPS
    v = _gelu(v_ref[...])
    mu = jnp.mean(v, axis=-1, keepdims=True)
    vc = v - mu
    vn = (vc * lax.rsqrt(jnp.mean(vc * vc, axis=-1, keepdims=True) + EPS) * g_ref[...] + b_ref[...]).astype(BF16)
    row = lax.broadcasted_iota(jnp.int32, (C, C), 0)
    col = lax.broadcasted_iota(jnp.int32, (C, C), 1)
    for gi in range(SG_GROUPS):
        sl = slice(gi * dg, (gi + 1) * dg)
        wc = jnp.where(col <= row, w_ref[gi], 0.0).astype(BF16)
        mixed = _dot(wc, vn[:, sl]) + bias_ref[:, sl]
        o_ref[:, sl] = (_gelu(u_ref[:, sl]) * mixed).astype(o_ref.dtype)


def sgu_block(z, ln_g, ln_b, w, b):
    B, S, _ = z.shape
    W, C = SG_W, SG_CHUNK
    bias = jnp.repeat(b.T, W // SG_GROUPS, axis=1)
    vspec = pl.BlockSpec((1, W), lambda bb, c: (0, 0))
    return pl.pallas_call(
        _sgu_kernel, grid=(B, S // C),
        in_specs=[pl.BlockSpec((None, C, W), lambda bb, c: (bb, c, 2)),
                  pl.BlockSpec((None, C, W), lambda bb, c: (bb, c, 3)),
                  vspec, vspec,
                  pl.BlockSpec((SG_GROUPS, C, C), lambda bb, c: (0, 0, 0)),
                  pl.BlockSpec((C, W), lambda bb, c: (0, 0))],
        out_specs=pl.BlockSpec((None, C, W), lambda bb, c: (bb, c, 0)),
        out_shape=jax.ShapeDtypeStruct((B, S, W), BF16),
        compiler_params=_cparams(("parallel", "parallel")), name="sgu",
    )(z, z, ln_g.reshape(1, W), ln_b.reshape(1, W), w, bias)


def _layer_ab(h2, B, S, pre_g, w_in, w_out, ml_gate_b, ml_norm_g, nsa_gate_b,
              k_pe, k_w1, k_w2, v_pe, v_w1, v_w2, post_g):
    D = h2.shape[1]
    G, d = NSA_KV_HEADS, HEAD_DIM
    offs = np.cumsum([0, ML_W, ML_W, ML_W, ML_W, 2 * ML_HEADS, NSA_W] + [G * d] * 6 + [3 * NSA_HEADS])
    mq, mk, mv, mo, mif, nq, kc, vc, ks, vs, kw, vw, ng = [w_in[:, offs[i]:offs[i + 1]] for i in range(13)]
    w_b = jnp.concatenate([mq, mk, mv, nq, ks, vs, kw, vw], axis=1).astype(BF16)
    gpad = LANES - 2 * ML_HEADS - 3 * NSA_HEADS
    w_f = jnp.concatenate([mo, kc, vc, mif, ng, jnp.zeros((D, gpad), w_in.dtype)], axis=1).astype(BF16)
    gate_bias = jnp.concatenate([ml_gate_b, nsa_gate_b, jnp.zeros((gpad,), F32)]).reshape(1, LANES)
    zb, zf = norm_proj(h2, pre_g, [w_b, w_f], [BF16, F32])
    zb = zb.reshape(B, S, -1)
    zf = zf.reshape(B, S, -1)
    h_ml = mlstm_block(zb, zf, gate_bias, ml_norm_g)
    kcmp, vcmp = compress_block(zf[:, :, ML_W:ML_W + G * d], zf[:, :, ML_W + G * d:ML_W + 2 * G * d],
                                k_pe, k_w1, k_w2, v_pe, v_w1, v_w2)
    h_nsa = nsa_block(zb, zf, gate_bias, kcmp, vcmp, q_col_block=3 * ML_W // (NSA_HPG * d), kv_col0=3 * ML_W + NSA_W)
    return mix_out(h_ml.reshape(B * S, ML_W), h_nsa.reshape(B * S, NSA_W), w_out, post_g, h2)


def _layer_cd(h2, B, S, pre_g, w_in, w_out, conv_w, conv_b, wa, ba, wx, bx, lam, sg_g, sg_bn, sg_w, sg_b, post_g):
    (z,) = norm_proj(h2, pre_g, [w_in.astype(BF16)], [F32])
    z = z.reshape(B, S, -1)
    y_lru = rglru_block(z, conv_w, conv_b, wa, ba, wx, bx, lam)
    y_sg = sgu_block(z, sg_g, sg_bn, sg_w, sg_b)
    return mix_out(y_lru.reshape(B * S, LRU_W), y_sg.reshape(B * S, SG_W), w_out, post_g, h2)


def kernel(x, pre_mix_g, post_mix_g, pre_ffn_g, post_ffn_g, ab_w_in, ab_w_out, ml_gate_b, ml_norm_g, nsa_gate_b, cmp_k_pe, cmp_k_w1, cmp_k_w2, cmp_v_pe, cmp_v_w1, cmp_v_w2, cd_w_in, cd_w_out, lru_conv_w, lru_conv_b, lru_wa, lru_ba, lru_wx, lru_bx, lru_lambda, sg_norm_g, sg_norm_b, sg_w, sg_b, ffn_w_up, ffn_conv_w, ffn_conv_b, ffn_w_down):
    B, S, D = x.shape
    depth = pre_mix_g.shape[0]
    h2 = x.reshape(B * S, D)
    for layer in range(depth):
        if layer % 2 == 0:
            e = layer // 2
            h2 = _layer_ab(h2, B, S, pre_mix_g[layer], ab_w_in[e], ab_w_out[e], ml_gate_b[e], ml_norm_g[e],
                           nsa_gate_b[e], cmp_k_pe[e], cmp_k_w1[e], cmp_k_w2[e], cmp_v_pe[e], cmp_v_w1[e],
                           cmp_v_w2[e], post_mix_g[layer])
        else:
            o = layer // 2
            h2 = _layer_cd(h2, B, S, pre_mix_g[layer], cd_w_in[o], cd_w_out[o], lru_conv_w[o], lru_conv_b[o],
                           lru_wa[o], lru_ba[o], lru_wx[o], lru_bx[o], lru_lambda[o], sg_norm_g[o], sg_norm_b[o],
                           sg_w[o], sg_b[o], post_mix_g[layer])
        h2 = ffn_block(h2, S, pre_ffn_g[layer], ffn_w_up[layer], ffn_conv_w[layer], ffn_conv_b[layer],
                       ffn_w_down[layer], post_ffn_g[layer])
    return h2.reshape(B, S, D)
```

```python
import functools

import numpy as np
import jax
import jax.numpy as jnp
from jax import lax
from jax.experimental import pallas as pl
from jax.experimental.pallas import tpu as pltpu

F32 = jnp.float32
BF16 = jnp.bfloat16

EPS = 1e-6
HEAD_DIM = 64
ML_HEADS = 8
ML_W = 512
GATE_SOFTCAP = 15.0
NSA_HEADS = 8
NSA_KV_HEADS = 2
NSA_HPG = NSA_HEADS // NSA_KV_HEADS
NSA_W = 512
CMP_LEN = 32
CMP_STRIDE = 16
CMP_HID = 128
SEL_LEN = 64
SEL_TOPN = 16
WINDOW = 512
LRU_W = 512
LRU_C = 8.0
LRU_CONV = 4
SG_GROUPS = 8
SG_W = 512
SG_CHUNK = 128
FFN_CONV = 3

LANES = 128
VMEM_LIMIT = 56 * 1024 * 1024
NEG_BIG = -1e30
HIGHEST = lax.Precision.HIGHEST

ML_CHUNK = 128
NSA_TQ = 128
NSA_TK = 256
ROW_TILE = 512
FFN_CK = 256
FFN_HALO = 16
LRU_T = 256
LRU_HALO = 8


def _cparams(sem):
    return pltpu.CompilerParams(dimension_semantics=sem, vmem_limit_bytes=VMEM_LIMIT)


def _rms(x, g):
    return x * lax.rsqrt(jnp.mean(x * x, axis=-1, keepdims=True) + EPS) * g


def _gelu(x):
    return 0.5 * x * (1.0 + jnp.tanh(0.7978845608028654 * (x + 0.044715 * (x * x * x))))


def _sigmoid(x):
    return 1.0 / (1.0 + jnp.exp(-x))


def _dot(a, b):
    return jnp.dot(a, b, preferred_element_type=F32)


def _dot_nt(a, b, precision=None):
    return lax.dot_general(a, b, (((1,), (1,)), ((), ())), preferred_element_type=F32, precision=precision)


def _dot_tn(a, b):
    return lax.dot_general(a, b, (((0,), (0,)), ((), ())), preferred_element_type=F32)


def _norm_proj_kernel(h_ref, g_ref, *refs, n_out, cn):
    w_refs, o_refs = refs[:n_out], refs[n_out:]
    xn = _rms(h_ref[...], g_ref[...]).astype(BF16)
    for w_ref, o_ref in zip(w_refs, o_refs):
        n = w_ref.shape[1]
        for c in range(0, n, cn):
            ce = min(c + cn, n)
            o_ref[:, c:ce] = _dot(xn, w_ref[:, c:ce]).astype(o_ref.dtype)


def norm_proj(h2, g, ws, dtypes, tm=ROW_TILE):
    M, D = h2.shape
    in_specs = [pl.BlockSpec((tm, D), lambda i: (i, 0)), pl.BlockSpec((1, D), lambda i: (0, 0))]
    in_specs += [pl.BlockSpec(w.shape, lambda i: (0, 0)) for w in ws]
    out_specs = [pl.BlockSpec((tm, w.shape[1]), lambda i: (i, 0)) for w in ws]
    out_shape = [jax.ShapeDtypeStruct((M, w.shape[1]), dt) for w, dt in zip(ws, dtypes)]
    return pl.pallas_call(
        functools.partial(_norm_proj_kernel, n_out=len(ws), cn=512),
        grid=(M // tm,), in_specs=in_specs, out_specs=out_specs, out_shape=out_shape,
        compiler_params=_cparams(("parallel",)), name="norm_proj",
    )(h2, g.reshape(1, D), *ws)


def _mix_out_kernel(a1_ref, a2_ref, w1_ref, w2_ref, g_ref, h_ref, o_ref):
    y = _dot(a1_ref[...], w1_ref[...]) + _dot(a2_ref[...], w2_ref[...])
    o_ref[...] = h_ref[...] + _rms(y, g_ref[...])


def mix_out(a1, a2, w_out, g, h2, tm=ROW_TILE):
    M, D = h2.shape
    K1, K2 = a1.shape[1], a2.shape[1]
    w1 = w_out[:K1].astype(BF16)
    w2 = w_out[K1:].astype(BF16)
    return pl.pallas_call(
        _mix_out_kernel, grid=(M // tm,),
        in_specs=[pl.BlockSpec((tm, K1), lambda i: (i, 0)), pl.BlockSpec((tm, K2), lambda i: (i, 0)),
                  pl.BlockSpec((K1, D), lambda i: (0, 0)), pl.BlockSpec((K2, D), lambda i: (0, 0)),
                  pl.BlockSpec((1, D), lambda i: (0, 0)), pl.BlockSpec((tm, D), lambda i: (i, 0))],
        out_specs=pl.BlockSpec((tm, D), lambda i: (i, 0)),
        out_shape=jax.ShapeDtypeStruct((M, D), F32),
        compiler_params=_cparams(("parallel",)), name="mix_out",
    )(a1, a2, w1, w2, g.reshape(1, D), h2)


def _ffn_kernel(h_ref, halo_ref, gpre_ref, wa_ref, wb_ref, cwa_ref, cwb_ref, cba_ref, cbb_ref, wd_ref, gpost_ref,
                o_ref, xn_s, acc_s, *, tiles_per_seq, n_chunks):
    tm = h_ref.shape[0]
    first = (pl.program_id(0) % tiles_per_seq) == 0
    x = h_ref[...]
    g = gpre_ref[...]
    xn_s[0:FFN_HALO, :] = jnp.where(first, 0.0, _rms(halo_ref[...], g)).astype(BF16)
    xn_s[FFN_HALO:, :] = _rms(x, g).astype(BF16)
    acc_s[...] = jnp.zeros_like(acc_s)

    def conv(u, cw, cb):
        y = cw[2:3, :] * u + cw[1:2, :] * pltpu.roll(u, 1, 0) + cw[0:1, :] * pltpu.roll(u, 2, 0) + cb
        return y[FFN_HALO:, :]

    def body(c, carry):
        xn = xn_s[...]
        a = conv(_dot(xn, wa_ref[c]), cwa_ref[c], cba_ref[c])
        b = conv(_dot(xn, wb_ref[c]), cwb_ref[c], cbb_ref[c])
        act = (_gelu(a) * b).astype(BF16)
        acc_s[...] += _dot(act, wd_ref[c])
        return carry

    lax.fori_loop(0, n_chunks, body, 0)
    o_ref[...] = x + _rms(acc_s[...], gpost_ref[...])


def ffn_block(h2, seq_len, g_pre, w_up, conv_w, conv_b, w_down, g_post, tm=ROW_TILE, ck=FFN_CK):
    M, D = h2.shape
    F = w_down.shape[0]
    nck = F // ck
    assert F % ck == 0 and seq_len % tm == 0 and tm % FFN_HALO == 0

    def chunk_cols(a):
        return a.reshape(a.shape[0], nck, ck).transpose(1, 0, 2)

    wa = chunk_cols(w_up[:, :F]).astype(BF16)
    wb = chunk_cols(w_up[:, F:]).astype(BF16)
    cwa = chunk_cols(conv_w[:, :F])
    cwb = chunk_cols(conv_w[:, F:])
    cba = chunk_cols(conv_b[None, :F])
    cbb = chunk_cols(conv_b[None, F:])
    wd = w_down.reshape(nck, ck, D).astype(BF16)
    hb = tm // FFN_HALO

    def full(a):
        return pl.BlockSpec(a.shape, lambda i: (0,) * a.ndim)

    return pl.pallas_call(
        functools.partial(_ffn_kernel, tiles_per_seq=seq_len // tm, n_chunks=nck),
        grid=(M // tm,),
        in_specs=[pl.BlockSpec((tm, D), lambda i: (i, 0)),
                  pl.BlockSpec((FFN_HALO, D), lambda i: (jnp.maximum(i * hb - 1, 0), 0)),
                  pl.BlockSpec((1, D), lambda i: (0, 0)),
                  full(wa), full(wb), full(cwa), full(cwb), full(cba), full(cbb), full(wd),
                  pl.BlockSpec((1, D), lambda i: (0, 0))],
        out_specs=pl.BlockSpec((tm, D), lambda i: (i, 0)),
        out_shape=jax.ShapeDtypeStruct((M, D), F32),
        scratch_shapes=[pltpu.VMEM((tm + FFN_HALO, D), BF16), pltpu.VMEM((tm, D), F32)],
        compiler_params=_cparams(("parallel",)), name="ffn_block",
    )(h2, h2, g_pre.reshape(1, D), wa, wb, cwa, cwb, cba, cbb, wd, g_post.reshape(1, D))


def _mlstm_kernel(q_ref, k_ref, v_ref, mo_ref, gt_ref, gb_ref, ng_ref, o_ref, c_s, n_s, m_s):
    L = q_ref.shape[0]
    d = HEAD_DIM

    @pl.when(pl.program_id(1) == 0)
    def _():
        c_s[...] = jnp.zeros_like(c_s)
        n_s[...] = jnp.zeros_like(n_s)
        m_s[...] = jnp.zeros_like(m_s)

    gcap = GATE_SOFTCAP * jnp.tanh((gt_ref[...] + gb_ref[...]) * (1.0 / GATE_SOFTCAP))
    lf = jnp.minimum(gcap, 0.0) - jnp.log1p(jnp.exp(-jnp.abs(gcap)))
    row = lax.broadcasted_iota(jnp.int32, (L, L), 0)
    col = lax.broadcasted_iota(jnp.int32, (L, L), 1)
    causal = col <= row
    tri = causal.astype(F32)
    b_col = jnp.dot(tri, lf, preferred_element_type=F32, precision=HIGHEST)
    b_row = _dot_nt(lf.T, tri, precision=HIGHEST)
    i_row = gcap.T

    for h in range(ML_HEADS):
        sl = slice(h * d, (h + 1) * d)
        q = q_ref[:, sl]
        k = k_ref[:, sl] * 0.125
        v = v_ref[:, sl]
        bc = b_col[:, ML_HEADS + h:ML_HEADS + h + 1]
        br = b_row[ML_HEADS + h:ML_HEADS + h + 1, :]
        ir = i_row[h:h + 1, :]
        ic = gcap[:, h:h + 1]
        g = bc[L - 1:L, :]
        m_prev = m_s[h:h + 1, 0:1]
        c_prev = c_s[h]
        n_prev = n_s[h:h + 1, 0:d]

        dlog = jnp.where(causal, bc - br + ir, -jnp.inf)
        inter = bc + m_prev
        m_row = jnp.maximum(inter, jnp.max(dlog, axis=-1, keepdims=True))
        s = _dot_nt(q, k) * jnp.exp(dlog - m_row)
        w_inter = jnp.exp(inter - m_row)
        qf = q.astype(F32)
        num = _dot(s.astype(BF16), v) + w_inter * _dot(q, c_prev.astype(BF16))
        den = jnp.sum(s, axis=-1, keepdims=True) + w_inter * jnp.sum(qf * n_prev, axis=-1, keepdims=True)
        hh = num / jnp.maximum(jnp.abs(den), jnp.exp(-m_row))

        m_new = jnp.maximum(g + m_prev, jnp.max(g - br + ir, axis=-1, keepdims=True))
        w_col = jnp.exp(g - bc + ic - m_new)
        decay = jnp.exp(g + m_prev - m_new)
        kf = k.astype(F32)
        c_s[h] = decay * c_prev + _dot_tn(k, (w_col * v.astype(F32)).astype(BF16))
        n_s[h:h + 1, 0:d] = decay * n_prev + jnp.sum(w_col * kf, axis=0, keepdims=True)
        m_s[h:h + 1, :] = jnp.broadcast_to(m_new, (1, m_s.shape[1]))

        hn = hh * lax.rsqrt(jnp.mean(hh * hh, axis=-1, keepdims=True) + EPS) * ng_ref[:, sl]
        o_ref[:, sl] = (hn * _sigmoid(mo_ref[:, sl])).astype(o_ref.dtype)


def mlstm_block(zb, zf, gate_bias, norm_g, L=ML_CHUNK):
    B, S, _ = zb.shape
    W = ML_W
    return pl.pallas_call(
        _mlstm_kernel, grid=(B, S // L),
        in_specs=[pl.BlockSpec((None, L, W), lambda b, c: (b, c, 0)),
                  pl.BlockSpec((None, L, W), lambda b, c: (b, c, 1)),
                  pl.BlockSpec((None, L, W), lambda b, c: (b, c, 2)),
                  pl.BlockSpec((None, L, W), lambda b, c: (b, c, 0)),
                  pl.BlockSpec((None, L, LANES), lambda b, c: (b, c, 6)),
                  pl.BlockSpec((1, LANES), lambda b, c: (0, 0)),
                  pl.BlockSpec((1, W), lambda b, c: (0, 0))],
        out_specs=pl.BlockSpec((None, L, W), lambda b, c: (b, c, 0)),
        out_shape=jax.ShapeDtypeStruct((B, S, W), BF16),
        scratch_shapes=[pltpu.VMEM((ML_HEADS, HEAD_DIM, HEAD_DIM), F32), pltpu.VMEM((ML_HEADS, LANES), F32),
                        pltpu.VMEM((ML_HEADS, LANES), F32)],
        compiler_params=_cparams(("parallel", "arbitrary")), name="mlstm",
    )(zb, zb, zb, zf, zf, gate_bias, norm_g.reshape(1, W))


def _compress_kernel(xk_ref, xv_ref, pek_ref, pev_ref, w1k_ref, w1v_ref, w2k_ref, w2v_ref, ok_ref, ov_ref):
    def one(x_ref, pe_ref, w1_ref, w2_ref):
        x = x_ref[...]
        n = x.shape[0]
        xa = (x + pe_ref[0:1, :]).astype(BF16)
        xb = (pltpu.roll(x, n - 1, 0) + pe_ref[1:2, :]).astype(BF16)
        hid = _gelu(_dot(xa, w1_ref[0]) + _dot(xb, w1_ref[1]))
        return _dot(hid.astype(BF16), w2_ref[...])

    ok_ref[...] = one(xk_ref, pek_ref, w1k_ref, w2k_ref).astype(ok_ref.dtype)
    ov_ref[...] = one(xv_ref, pev_ref, w1v_ref, w2v_ref).T.astype(ov_ref.dtype)


def compress_block(kc, vc, k_pe, k_w1, k_w2, v_pe, v_w1, v_w2):
    B, S, _ = kc.shape
    G, d = NSA_KV_HEADS, HEAD_DIM
    nh = S // CMP_STRIDE
    half = CMP_STRIDE * d

    def relayout(a):
        return a.reshape(B, nh, CMP_STRIDE, G, d).transpose(0, 3, 1, 2, 4).reshape(B, G, nh, half)

    def prep(pe, w1):
        return pe.reshape(2, half), w1.reshape(2, half, CMP_HID).astype(BF16)

    pek, w1k = prep(k_pe, k_w1)
    pev, w1v = prep(v_pe, v_w1)
    xspec = pl.BlockSpec((None, None, nh, half), lambda b, g: (b, g, 0, 0))
    ospec = pl.BlockSpec((None, None, nh, d), lambda b, g: (b, g, 0, 0))

    def full(a):
        return pl.BlockSpec(a.shape, lambda b, g: (0,) * a.ndim)

    w2k, w2v = k_w2.astype(BF16), v_w2.astype(BF16)
    return pl.pallas_call(
        _compress_kernel, grid=(B, G),
        in_specs=[xspec, xspec, full(pek), full(pev), full(w1k), full(w1v), full(w2k), full(w2v)],
        out_specs=[ospec, pl.BlockSpec((None, None, d, nh), lambda b, g: (b, g, 0, 0))],
        out_shape=[jax.ShapeDtypeStruct((B, G, nh, d), BF16), jax.ShapeDtypeStruct((B, G, d, nh), BF16)],
        compiler_params=_cparams(("parallel", "parallel")), name="nsa_compress",
    )(relayout(kc), relayout(vc), pek, pev, w1k, w1v, w2k, w2v)


def _nsa_kernel(q_ref, gt_ref, gb_ref, kc_ref, vct_ref, ks_ref, vst_ref, kw_ref, vwt_ref, ovl_ref, aug_ref, caug_ref,
                o_ref, ksa_s, kwa_s, kca_s, *, n_cmp):
    TQ, TK, d, HPG = NSA_TQ, NSA_TK, HEAD_DIM, NSA_HPG
    R = HPG * TQ
    g = pl.program_id(1)
    qi = pl.program_id(2)
    q0 = qi * TQ
    kt_d = q0 // TK
    n_cb = kc_ref.shape[0]
    n_sb = ovl_ref.shape[0]
    kaug = ksa_s.shape[1]

    @pl.when(qi == 0)
    def _():
        ksa_s[:, 0:d] = ks_ref[...]
        ksa_s[:, d:] = aug_ref[...]
        kwa_s[:, 0:d] = kw_ref[...]
        kwa_s[:, d:] = aug_ref[...]
        kca_s[:, 0:d] = kc_ref[...]
        kca_s[:, d:] = caug_ref[...]

    def slope(hh):
        return jnp.where(g == 0, 2.0 ** (-(hh + 1)), 2.0 ** (-(HPG + hh + 1))).astype(F32)

    def per_head(fn):
        return jnp.concatenate([fn(hh) for hh in range(HPG)], axis=1)

    def tile_heads(x):
        return jnp.concatenate([x] * HPG, axis=1)

    def first_row(n, val):
        return jnp.where(lax.broadcasted_iota(jnp.int32, (n, TQ), 0) == 0, val, 0.0)

    q_t = (q_ref[...].astype(F32) * 0.125).T

    def q_head(hh):
        return q_t[hh * d:(hh + 1) * d]

    qc_t = per_head(lambda hh: jnp.concatenate([q_head(hh), first_row(kca_s.shape[1] - d, slope(hh) * CMP_STRIDE)],
                                               axis=0)).astype(BF16)
    n_r = lax.broadcasted_iota(jnp.int32, (n_cb, TQ), 0)
    t_c = q0 + lax.broadcasted_iota(jnp.int32, (n_cb, TQ), 1)
    ok_c = (n_r * CMP_STRIDE + (CMP_LEN - 1) <= t_c) & (n_r < n_cmp)
    s_c = _dot(kca_s[...], qc_t) + tile_heads(jnp.where(ok_c, 0.0, NEG_BIG))
    e_c = jnp.exp(s_c - jnp.max(s_c, axis=0, keepdims=True)) * tile_heads(jnp.where(ok_c, 1.0, 0.0))
    p_c = e_c * (1.0 / jnp.maximum(jnp.sum(e_c, axis=0, keepdims=True), 1.0))
    o_c = _dot(vct_ref[...], p_c.astype(BF16))

    p_sum = p_c[:, 0:TQ]
    for hh in range(1, HPG):
        p_sum = p_sum + p_c[:, hh * TQ:(hh + 1) * TQ]
    imp = jnp.dot(ovl_ref[...], p_sum, preferred_element_type=F32, precision=HIGHEST)
    jb = lax.broadcasted_iota(jnp.int32, (n_sb, TQ), 0)
    cur = (q0 + lax.broadcasted_iota(jnp.int32, (n_sb, TQ), 1)) // SEL_LEN
    valid = jb <= cur
    forced = (jb == 0) | (jb == cur) | (jb == cur - 1)
    score = jnp.where(forced, jnp.inf, jnp.where(valid, imp, -jnp.inf))
    rank = jnp.zeros((n_sb, TQ), jnp.int32)
    for j in range(n_sb):
        r = score[j:j + 1, :]
        rank = rank + jnp.where(jb > j, (r >= score).astype(jnp.int32), (r > score).astype(jnp.int32))
    picked = valid & (rank < SEL_TOPN)

    blk_pos = (jb * SEL_LEN).astype(F32)
    n_tail = kaug - d - n_sb

    def q_aug(block_rows):
        return per_head(lambda hh: jnp.concatenate(
            [q_head(hh), block_rows(hh), first_row(n_tail, slope(hh))], axis=0)).astype(BF16)

    qs_t = q_aug(lambda hh: jnp.where(picked, slope(hh) * blk_pos, NEG_BIG))
    qw_t = q_aug(lambda hh: slope(hh) * blk_pos)

    def scores(ka_s, qa_t, kt):
        return _dot(ka_s[pl.ds(pl.multiple_of(kt * TK, TK), TK), :], qa_t)

    def update(state, s, vt_ref, kt):
        m, l, acc = state
        m_new = jnp.maximum(m, jnp.max(s, axis=0, keepdims=True))
        p = jnp.exp(s - m_new)
        alpha = jnp.exp(m - m_new)
        l = alpha * l + jnp.sum(p, axis=0, keepdims=True)
        acc = alpha * acc + _dot(vt_ref[:, pl.ds(pl.multiple_of(kt * TK, TK), TK)], p.astype(BF16))
        return m_new, l, acc

    def delta(kt):
        return ((q0 - kt * TK) + lax.broadcasted_iota(jnp.int32, (TK, TQ), 1)
                - lax.broadcasted_iota(jnp.int32, (TK, TQ), 0))

    init = (jnp.full((1, R), NEG_BIG, F32), jnp.zeros((1, R), F32), jnp.zeros((d, R), F32))

    s_diag = scores(ksa_s, qs_t, kt_d) + tile_heads(jnp.where(delta(kt_d) >= 0, 0.0, NEG_BIG))

    def sel_body(j, carry):
        m, l, acc, s_cur, j_cur = carry
        s_nxt = scores(ksa_s, qs_t, j)
        m, l, acc = update((m, l, acc), s_cur, vst_ref, j_cur)
        return m, l, acc, s_nxt, j

    m, l, acc, s_cur, j_cur = lax.fori_loop(0, kt_d, sel_body, (*init, s_diag, kt_d))
    m, l, acc = update((m, l, acc), s_cur, vst_ref, j_cur)
    o_s = acc * (1.0 / l)

    state = init
    for back in range((WINDOW - 1 + TK - 1) // TK + 1):
        kt_raw = kt_d - back
        kt = jnp.maximum(kt_raw, 0)
        dl = delta(kt)
        ok_w = (dl >= 0) & (dl < WINDOW) & (kt_raw >= 0)
        s_w = scores(kwa_s, qw_t, kt) + tile_heads(jnp.where(ok_w, 0.0, NEG_BIG))
        state = update(state, s_w, vwt_ref, kt)
    o_w = state[2] * (1.0 / state[1])

    gates = _sigmoid(gt_ref[...] + gb_ref[...]).T
    gsel = jnp.where(g == 0, gates[16:16 + 3 * HPG], gates[16 + 3 * HPG:16 + 6 * HPG])
    outs = []
    for hh in range(HPG):
        cols = slice(hh * TQ, (hh + 1) * TQ)
        outs.append(gsel[3 * hh:3 * hh + 1] * o_c[:, cols] + gsel[3 * hh + 1:3 * hh + 2] * o_s[:, cols]
                    + gsel[3 * hh + 2:3 * hh + 3] * o_w[:, cols])
    o_ref[...] = jnp.concatenate(outs, axis=0).T.astype(o_ref.dtype)


def nsa_block(zb, zf, gate_bias, kcmp, vcmp_t, q_col_block, kv_col0):
    B, S, _ = zb.shape
    G, d = NSA_KV_HEADS, HEAD_DIM
    TQ = NSA_TQ
    n_cb = kcmp.shape[2]
    n_cmp = (S - CMP_LEN) // CMP_STRIDE + 1
    n_sb = S // SEL_LEN
    kaug = -(-(d + n_sb + 1) // LANES) * LANES
    assert S % NSA_TK == 0 and NSA_TK % TQ == 0 and TQ % SEL_LEN == 0

    def per_group(j, transposed):
        a = lax.slice_in_dim(zb, kv_col0 + j * G * d, kv_col0 + (j + 1) * G * d, axis=2).reshape(B, S, G, d)
        return a.transpose(0, 2, 3, 1) if transposed else a.transpose(0, 2, 1, 3)

    ks, vs_t, kw, vw_t = per_group(0, False), per_group(1, True), per_group(2, False), per_group(3, True)
    cidx = np.arange(n_cb)[None, :] * CMP_STRIDE
    sstart = np.arange(n_sb)[:, None] * SEL_LEN
    ovl = ((cidx < sstart + SEL_LEN) & (cidx + CMP_LEN - 1 >= sstart) & (np.arange(n_cb)[None, :] < n_cmp))
    ovl = jnp.asarray(ovl.astype(np.float32))
    pos = np.arange(S)
    aug = np.zeros((S, kaug - d), np.float32)
    aug[pos, pos // SEL_LEN] = 1.0
    aug[:, n_sb] = pos % SEL_LEN
    caug = np.zeros((n_cb, d), np.float32)
    caug[:, 0] = np.arange(n_cb)
    aug, caug = jnp.asarray(aug, BF16), jnp.asarray(caug, BF16)

    kspec = pl.BlockSpec((None, None, S, d), lambda b, g, i: (b, g, 0, 0))
    vspec = pl.BlockSpec((None, None, d, S), lambda b, g, i: (b, g, 0, 0))

    def const(a):
        return pl.BlockSpec(a.shape, lambda b, g, i: (0,) * a.ndim)

    return pl.pallas_call(
        functools.partial(_nsa_kernel, n_cmp=n_cmp), grid=(B, G, S // TQ),
        in_specs=[pl.BlockSpec((None, TQ, NSA_HPG * d), lambda b, g, i: (b, i, q_col_block + g)),
                  pl.BlockSpec((None, TQ, LANES), lambda b, g, i: (b, i, 6)),
                  const(gate_bias),
                  pl.BlockSpec((None, None, n_cb, d), lambda b, g, i: (b, g, 0, 0)),
                  pl.BlockSpec((None, None, d, n_cb), lambda b, g, i: (b, g, 0, 0)),
                  kspec, vspec, kspec, vspec, const(ovl), const(aug), const(caug)],
        out_specs=pl.BlockSpec((None, TQ, NSA_HPG * d), lambda b, g, i: (b, i, g)),
        out_shape=jax.ShapeDtypeStruct((B, S, NSA_W), BF16),
        scratch_shapes=[pltpu.VMEM((S, kaug), BF16), pltpu.VMEM((S, kaug), BF16), pltpu.VMEM((n_cb, 2 * d), BF16)],
        compiler_params=_cparams(("parallel", "parallel", "arbitrary")), name="nsa",
    )(zb, zf, gate_bias, kcmp, vcmp_t, ks, vs_t, kw, vw_t, ovl, aug, caug)


def _rglru_kernel(gate_ref, x_ref, halo_ref, cw_ref, cb_ref, wa_ref, wx_ref, ba_ref, bx_ref, lam_ref, o_ref, h_s):
    T, W = x_ref.shape
    first = pl.program_id(1) == 0

    @pl.when(first)
    def _():
        h_s[...] = jnp.zeros_like(h_s)

    xe = jnp.concatenate([jnp.where(first, 0.0, halo_ref[...]), x_ref[...]], axis=0)
    xc = cb_ref[...] + cw_ref[3:4, :] * xe
    for k in range(1, LRU_CONV):
        xc = xc + cw_ref[3 - k:4 - k, :] * pltpu.roll(xe, k, 0)
    xc = xc[LRU_HALO:, :]
    xcb = xc.astype(BF16)
    half = W // 2

    def blockdiag(w_ref):
        return jnp.concatenate([_dot(xcb[:, :half], w_ref[0]), _dot(xcb[:, half:], w_ref[1])], axis=1)

    r = _sigmoid(blockdiag(wa_ref) + ba_ref[...])
    i = _sigmoid(blockdiag(wx_ref) + bx_ref[...])
    nl = -lam_ref[...]
    softplus = jnp.maximum(nl, 0.0) + jnp.log1p(jnp.exp(-jnp.abs(nl)))
    log_a = -LRU_C * r * softplus
    a = jnp.exp(log_a)
    u = jnp.sqrt(-jnp.tanh(log_a) * (a * a + 1.0)) * (i * xc)
    row = lax.broadcasted_iota(jnp.int32, (T, W), 0)
    sft = 1
    while sft < T:
        keep = row >= sft
        u = a * jnp.where(keep, pltpu.roll(u, sft, 0), 0.0) + u
        a = a * jnp.where(keep, pltpu.roll(a, sft, 0), 1.0)
        sft *= 2
    h = u + a * h_s[0:1, :]
    h_s[...] = jnp.broadcast_to(h[T - 1:T, :], h_s.shape)
    o_ref[...] = (_gelu(gate_ref[...]) * h).astype(o_ref.dtype)


def rglru_block(z, conv_w, conv_b, wa, ba, wx, bx, lam, T=LRU_T):
    B, S, _ = z.shape
    W = LRU_W
    half = W // 2
    hb = T // LRU_HALO

    def bd(w):
        blocks = [jax.scipy.linalg.block_diag(*[w[h] for h in range(4 * j, 4 * j + 4)]) for j in range(2)]
        return jnp.stack(blocks).astype(BF16)

    vec = lambda a: a.reshape(1, W)
    vspec = pl.BlockSpec((1, W), lambda b, t: (0, 0))
    wspec = pl.BlockSpec((2, half, half), lambda b, t: (0, 0, 0))
    return pl.pallas_call(
        _rglru_kernel, grid=(B, S // T),
        in_specs=[pl.BlockSpec((None, T, W), lambda b, t: (b, t, 0)),
                  pl.BlockSpec((None, T, W), lambda b, t: (b, t, 1)),
                  pl.BlockSpec((None, LRU_HALO, W), lambda b, t: (b, jnp.maximum(t * hb - 1, 0), 1)),
                  pl.BlockSpec((LRU_CONV, W), lambda b, t: (0, 0)), vspec, wspec, wspec, vspec, vspec, vspec],
        out_specs=pl.BlockSpec((None, T, W), lambda b, t: (b, t, 0)),
        out_shape=jax.ShapeDtypeStruct((B, S, W), BF16),
        scratch_shapes=[pltpu.VMEM((8, W), F32)],
        compiler_params=_cparams(("parallel", "arbitrary")), name="rglru",
    )(z, z, z, conv_w, vec(conv_b), bd(wa), bd(wx), vec(ba), vec(bx), vec(lam))


def _sgu_kernel(u_ref, v_ref, g_ref, b_ref, w_ref, bias_ref, o_ref):
    C, W = v_ref.shape
    dg = W // SG_GROUPS
    v = _gelu(v_ref[...])
    mu = jnp.mean(v, axis=-1, keepdims=True)
    vc = v - mu
    vn = (vc * lax.rsqrt(jnp.mean(vc * vc, axis=-1, keepdims=True) + EPS) * g_ref[...] + b_ref[...]).astype(BF16)
    row = lax.broadcasted_iota(jnp.int32, (C, C), 0)
    col = lax.broadcasted_iota(jnp.int32, (C, C), 1)
    for gi in range(SG_GROUPS):
        sl = slice(gi * dg, (gi + 1) * dg)
        wc = jnp.where(col <= row, w_ref[gi], 0.0).astype(BF16)
        mixed = _dot(wc, vn[:, sl]) + bias_ref[:, sl]
        o_ref[:, sl] = (_gelu(u_ref[:, sl]) * mixed).astype(o_ref.dtype)


def sgu_block(z, ln_g, ln_b, w, b):
    B, S, _ = z.shape
    W, C = SG_W, SG_CHUNK
    bias = jnp.repeat(b.T, W // SG_GROUPS, axis=1)
    vspec = pl.BlockSpec((1, W), lambda bb, c: (0, 0))
    return pl.pallas_call(
        _sgu_kernel, grid=(B, S // C),
        in_specs=[pl.BlockSpec((None, C, W), lambda bb, c: (bb, c, 2)),
                  pl.BlockSpec((None, C, W), lambda bb, c: (bb, c, 3)),
                  vspec, vspec,
                  pl.BlockSpec((SG_GROUPS, C, C), lambda bb, c: (0, 0, 0)),
                  pl.BlockSpec((C, W), lambda bb, c: (0, 0))],
        out_specs=pl.BlockSpec((None, C, W), lambda bb, c: (bb, c, 0)),
        out_shape=jax.ShapeDtypeStruct((B, S, W), BF16),
        compiler_params=_cparams(("parallel", "parallel")), name="sgu",
    )(z, z, ln_g.reshape(1, W), ln_b.reshape(1, W), w, bias)


def _layer_ab(h2, B, S, pre_g, w_in, w_out, ml_gate_b, ml_norm_g, nsa_gate_b,
              k_pe, k_w1, k_w2, v_pe, v_w1, v_w2, post_g):
    D = h2.shape[1]
    G, d = NSA_KV_HEADS, HEAD_DIM
    offs = np.cumsum([0, ML_W, ML_W, ML_W, ML_W, 2 * ML_HEADS, NSA_W] + [G * d] * 6 + [3 * NSA_HEADS])
    mq, mk, mv, mo, mif, nq, kc, vc, ks, vs, kw, vw, ng = [w_in[:, offs[i]:offs[i + 1]] for i in range(13)]
    w_b = jnp.concatenate([mq, mk, mv, nq, ks, vs, kw, vw], axis=1).astype(BF16)
    gpad = LANES - 2 * ML_HEADS - 3 * NSA_HEADS
    w_f = jnp.concatenate([mo, kc, vc, mif, ng, jnp.zeros((D, gpad), w_in.dtype)], axis=1).astype(BF16)
    gate_bias = jnp.concatenate([ml_gate_b, nsa_gate_b, jnp.zeros((gpad,), F32)]).reshape(1, LANES)
    zb, zf = norm_proj(h2, pre_g, [w_b, w_f], [BF16, F32])
    zb = zb.reshape(B, S, -1)
    zf = zf.reshape(B, S, -1)
    h_ml = mlstm_block(zb, zf, gate_bias, ml_norm_g)
    kcmp, vcmp = compress_block(zf[:, :, ML_W:ML_W + G * d], zf[:, :, ML_W + G * d:ML_W + 2 * G * d],
                                k_pe, k_w1, k_w2, v_pe, v_w1, v_w2)
    h_nsa = nsa_block(zb, zf, gate_bias, kcmp, vcmp, q_col_block=3 * ML_W // (NSA_HPG * d), kv_col0=3 * ML_W + NSA_W)
    return mix_out(h_ml.reshape(B * S, ML_W), h_nsa.reshape(B * S, NSA_W), w_out, post_g, h2)


def _layer_cd(h2, B, S, pre_g, w_in, w_out, conv_w, conv_b, wa, ba, wx, bx, lam, sg_g, sg_bn, sg_w, sg_b, post_g):
    (z,) = norm_proj(h2, pre_g, [w_in.astype(BF16)], [F32])
    z = z.reshape(B, S, -1)
    y_lru = rglru_block(z, conv_w, conv_b, wa, ba, wx, bx, lam)
    y_sg = sgu_block(z, sg_g, sg_bn, sg_w, sg_b)
    return mix_out(y_lru.reshape(B * S, LRU_W), y_sg.reshape(B * S, SG_W), w_out, post_g, h2)


def kernel(x, pre_mix_g, post_mix_g, pre_ffn_g, post_ffn_g, ab_w_in, ab_w_out, ml_gate_b, ml_norm_g, nsa_gate_b, cmp_k_pe, cmp_k_w1, cmp_k_w2, cmp_v_pe, cmp_v_w1, cmp_v_w2, cd_w_in, cd_w_out, lru_conv_w, lru_conv_b, lru_wa, lru_ba, lru_wx, lru_bx, lru_lambda, sg_norm_g, sg_norm_b, sg_w, sg_b, ffn_w_up, ffn_conv_w, ffn_conv_b, ffn_w_down):
    B, S, D = x.shape
    depth = pre_mix_g.shape[0]
    h2 = x.reshape(B * S, D)
    for layer in range(depth):
        if layer % 2 == 0:
            e = layer // 2
            h2 = _layer_ab(h2, B, S, pre_mix_g[layer], ab_w_in[e], ab_w_out[e], ml_gate_b[e], ml_norm_g[e],
                           nsa_gate_b[e], cmp_k_pe[e], cmp_k_w1[e], cmp_k_w2[e], cmp_v_pe[e], cmp_v_w1[e],
                           cmp_v_w2[e], post_mix_g[layer])
        else:
            o = layer // 2
            h2 = _layer_cd(h2, B, S, pre_mix_g[layer], cd_w_in[o], cd_w_out[o], lru_conv_w[o], lru_conv_b[o],
                           lru_wa[o], lru_ba[o], lru_wx[o], lru_bx[o], lru_lambda[o], sg_norm_g[o], sg_norm_b[o],
                           sg_w[o], sg_b[o], post_mix_g[layer])
        h2 = ffn_block(h2, S, pre_ffn_g[layer], ffn_w_up[layer], ffn_conv_w[layer], ffn_conv_b[layer],
                       ffn_w_down[layer], post_ffn_g[layer])
    return h2.reshape(B, S, D)
```

```python
import functools

import numpy as np
import jax
import jax.numpy as jnp
from jax import lax
from jax.experimental import pallas as pl
from jax.experimental.pallas import tpu as pltpu

F32 = jnp.float32
BF16 = jnp.bfloat16

EPS = 1e-6
HEAD_DIM = 64
ML_HEADS = 8
ML_W = 512
GATE_SOFTCAP = 15.0
NSA_HEADS = 8
NSA_KV_HEADS = 2
NSA_HPG = NSA_HEADS // NSA_KV_HEADS
NSA_W = 512
CMP_LEN = 32
CMP_STRIDE = 16
CMP_HID = 128
SEL_LEN = 64
SEL_TOPN = 16
WINDOW = 512
LRU_W = 512
LRU_C = 8.0
LRU_CONV = 4
SG_GROUPS = 8
SG_W = 512
SG_CHUNK = 128
FFN_CONV = 3

LANES = 128
VMEM_LIMIT = 56 * 1024 * 1024
NEG_BIG = -1e30
HIGHEST = lax.Precision.HIGHEST

ML_CHUNK = 128
NSA_TQ = 128
NSA_TK = 256
ROW_TILE = 512
FFN_TM = 256
FFN_CK = 256
FFN_HALO = 16
LRU_T = 256
LRU_HALO = 8


def _cparams(sem):
    return pltpu.CompilerParams(dimension_semantics=sem, vmem_limit_bytes=VMEM_LIMIT)


def _rms(x, g):
    return x * lax.rsqrt(jnp.mean(x * x, axis=-1, keepdims=True) + EPS) * g


def _gelu(x):
    return 0.5 * x * (1.0 + jnp.tanh(0.7978845608028654 * (x + 0.044715 * (x * x * x))))


def _sigmoid(x):
    return 1.0 / (1.0 + jnp.exp(-x))


def _dot(a, b):
    return jnp.dot(a, b, preferred_element_type=F32)


def _dot_nt(a, b, precision=None):
    return lax.dot_general(a, b, (((1,), (1,)), ((), ())), preferred_element_type=F32, precision=precision)


def _dot_tn(a, b):
    return lax.dot_general(a, b, (((0,), (0,)), ((), ())), preferred_element_type=F32)


def _norm_proj_kernel(h_ref, g_ref, *refs, n_out, cn):
    w_refs, o_refs = refs[:n_out], refs[n_out:]
    xn = _rms(h_ref[...], g_ref[...]).astype(BF16)
    for w_ref, o_ref in zip(w_refs, o_refs):
        n = w_ref.shape[1]
        for c in range(0, n, cn):
            ce = min(c + cn, n)
            o_ref[:, c:ce] = _dot(xn, w_ref[:, c:ce]).astype(o_ref.dtype)


def norm_proj(h2, g, ws, dtypes, tm=ROW_TILE):
    M, D = h2.shape
    in_specs = [pl.BlockSpec((tm, D), lambda i: (i, 0)), pl.BlockSpec((1, D), lambda i: (0, 0))]
    in_specs += [pl.BlockSpec(w.shape, lambda i: (0, 0)) for w in ws]
    out_specs = [pl.BlockSpec((tm, w.shape[1]), lambda i: (i, 0)) for w in ws]
    out_shape = [jax.ShapeDtypeStruct((M, w.shape[1]), dt) for w, dt in zip(ws, dtypes)]
    return pl.pallas_call(
        functools.partial(_norm_proj_kernel, n_out=len(ws), cn=512),
        grid=(M // tm,), in_specs=in_specs, out_specs=out_specs, out_shape=out_shape,
        compiler_params=_cparams(("parallel",)), name="norm_proj",
    )(h2, g.reshape(1, D), *ws)


def _mix_out_kernel(a1_ref, a2_ref, w1_ref, w2_ref, g_ref, h_ref, o_ref):
    y = _dot(a1_ref[...], w1_ref[...]) + _dot(a2_ref[...], w2_ref[...])
    o_ref[...] = h_ref[...] + _rms(y, g_ref[...])


def mix_out(a1, a2, w_out, g, h2, tm=ROW_TILE):
    M, D = h2.shape
    K1, K2 = a1.shape[1], a2.shape[1]
    w1 = w_out[:K1].astype(BF16)
    w2 = w_out[K1:].astype(BF16)
    return pl.pallas_call(
        _mix_out_kernel, grid=(M // tm,),
        in_specs=[pl.BlockSpec((tm, K1), lambda i: (i, 0)), pl.BlockSpec((tm, K2), lambda i: (i, 0)),
                  pl.BlockSpec((K1, D), lambda i: (0, 0)), pl.BlockSpec((K2, D), lambda i: (0, 0)),
                  pl.BlockSpec((1, D), lambda i: (0, 0)), pl.BlockSpec((tm, D), lambda i: (i, 0))],
        out_specs=pl.BlockSpec((tm, D), lambda i: (i, 0)),
        out_shape=jax.ShapeDtypeStruct((M, D), F32),
        compiler_params=_cparams(("parallel",)), name="mix_out",
    )(a1, a2, w1, w2, g.reshape(1, D), h2)


def _ffn_kernel(h_ref, halo_ref, gpre_ref, wa_ref, wb_ref, cwa_ref, cwb_ref, cba_ref, cbb_ref, wd_ref, gpost_ref,
                o_ref, xn_s, acc_s, *, tiles_per_seq, n_chunks):
    tm = h_ref.shape[0]
    first = (pl.program_id(0) % tiles_per_seq) == 0
    x = h_ref[...]
    g = gpre_ref[...]
    xn_s[0:FFN_HALO, :] = jnp.where(first, 0.0, _rms(halo_ref[...], g)).astype(BF16)
    xn_s[FFN_HALO:, :] = _rms(x, g).astype(BF16)
    acc_s[...] = jnp.zeros_like(acc_s)

    def conv(u, cw, cb):
        y = cw[2:3, :] * u + cw[1:2, :] * pltpu.roll(u, 1, 0) + cw[0:1, :] * pltpu.roll(u, 2, 0) + cb
        return y[FFN_HALO:, :]

    def up(c):
        xn = xn_s[...]
        return _dot(xn, wa_ref[c]), _dot(xn, wb_ref[c])

    u = up(0)
    for c in range(n_chunks):
        u_next = up(c + 1) if c + 1 < n_chunks else None
        a = conv(u[0], cwa_ref[c], cba_ref[c])
        b = conv(u[1], cwb_ref[c], cbb_ref[c])
        act = (_gelu(a) * b).astype(BF16)
        acc_s[...] += _dot(act, wd_ref[c])
        u = u_next
    o_ref[...] = x + _rms(acc_s[...], gpost_ref[...])


def ffn_block(h2, seq_len, g_pre, w_up, conv_w, conv_b, w_down, g_post, tm=FFN_TM, ck=FFN_CK):
    M, D = h2.shape
    F = w_down.shape[0]
    nck = F // ck
    assert F % ck == 0 and seq_len % tm == 0 and tm % FFN_HALO == 0

    def chunk_cols(a):
        return a.reshape(a.shape[0], nck, ck).transpose(1, 0, 2)

    wa = chunk_cols(w_up[:, :F]).astype(BF16)
    wb = chunk_cols(w_up[:, F:]).astype(BF16)
    cwa = chunk_cols(conv_w[:, :F])
    cwb = chunk_cols(conv_w[:, F:])
    cba = chunk_cols(conv_b[None, :F])
    cbb = chunk_cols(conv_b[None, F:])
    wd = w_down.reshape(nck, ck, D).astype(BF16)
    hb = tm // FFN_HALO

    def full(a):
        return pl.BlockSpec(a.shape, lambda i: (0,) * a.ndim)

    return pl.pallas_call(
        functools.partial(_ffn_kernel, tiles_per_seq=seq_len // tm, n_chunks=nck),
        grid=(M // tm,),
        in_specs=[pl.BlockSpec((tm, D), lambda i: (i, 0)),
                  pl.BlockSpec((FFN_HALO, D), lambda i: (jnp.maximum(i * hb - 1, 0), 0)),
                  pl.BlockSpec((1, D), lambda i: (0, 0)),
                  full(wa), full(wb), full(cwa), full(cwb), full(cba), full(cbb), full(wd),
                  pl.BlockSpec((1, D), lambda i: (0, 0))],
        out_specs=pl.BlockSpec((tm, D), lambda i: (i, 0)),
        out_shape=jax.ShapeDtypeStruct((M, D), F32),
        scratch_shapes=[pltpu.VMEM((tm + FFN_HALO, D), BF16), pltpu.VMEM((tm, D), F32)],
        compiler_params=_cparams(("parallel",)), name="ffn_block",
    )(h2, h2, g_pre.reshape(1, D), wa, wb, cwa, cwb, cba, cbb, wd, g_post.reshape(1, D))


def _mlstm_kernel(q_ref, k_ref, v_ref, mo_ref, gt_ref, gb_ref, ng_ref, o_ref, c_s, n_s, m_s):
    L = q_ref.shape[0]
    d = HEAD_DIM

    @pl.when(pl.program_id(1) == 0)
    def _():
        c_s[...] = jnp.zeros_like(c_s)
        n_s[...] = jnp.zeros_like(n_s)
        m_s[...] = jnp.zeros_like(m_s)

    gcap = GATE_SOFTCAP * jnp.tanh((gt_ref[...] + gb_ref[...]) * (1.0 / GATE_SOFTCAP))
    lf = jnp.minimum(gcap, 0.0) - jnp.log1p(jnp.exp(-jnp.abs(gcap)))
    row = lax.broadcasted_iota(jnp.int32, (L, L), 0)
    col = lax.broadcasted_iota(jnp.int32, (L, L), 1)
    causal = col <= row
    tri = causal.astype(F32)
    b_col = jnp.dot(tri, lf, preferred_element_type=F32, precision=HIGHEST)
    b_row = _dot_nt(lf.T, tri, precision=HIGHEST)
    i_row = gcap.T

    for h in range(ML_HEADS):
        sl = slice(h * d, (h + 1) * d)
        q = q_ref[:, sl]
        k = k_ref[:, sl] * 0.125
        v = v_ref[:, sl]
        bc = b_col[:, ML_HEADS + h:ML_HEADS + h + 1]
        br = b_row[ML_HEADS + h:ML_HEADS + h + 1, :]
        ir = i_row[h:h + 1, :]
        ic = gcap[:, h:h + 1]
        g = bc[L - 1:L, :]
        m_prev = m_s[h:h + 1, 0:1]
        c_prev = c_s[h]
        n_prev = n_s[h:h + 1, 0:d]

        dlog = jnp.where(causal, bc - br + ir, -jnp.inf)
        inter = bc + m_prev
        m_row = jnp.maximum(inter, jnp.max(dlog, axis=-1, keepdims=True))
        s = _dot_nt(q, k) * jnp.exp(dlog - m_row)
        w_inter = jnp.exp(inter - m_row)
        qf = q.astype(F32)
        num = _dot(s.astype(BF16), v) + w_inter * _dot(q, c_prev.astype(BF16))
        den = jnp.sum(s, axis=-1, keepdims=True) + w_inter * jnp.sum(qf * n_prev, axis=-1, keepdims=True)
        hh = num / jnp.maximum(jnp.abs(den), jnp.exp(-m_row))

        m_new = jnp.maximum(g + m_prev, jnp.max(g - br + ir, axis=-1, keepdims=True))
        w_col = jnp.exp(g - bc + ic - m_new)
        decay = jnp.exp(g + m_prev - m_new)
        kf = k.astype(F32)
        c_s[h] = decay * c_prev + _dot_tn(k, (w_col * v.astype(F32)).astype(BF16))
        n_s[h:h + 1, 0:d] = decay * n_prev + jnp.sum(w_col * kf, axis=0, keepdims=True)
        m_s[h:h + 1, :] = jnp.broadcast_to(m_new, (1, m_s.shape[1]))

        hn = hh * lax.rsqrt(jnp.mean(hh * hh, axis=-1, keepdims=True) + EPS) * ng_ref[:, sl]
        o_ref[:, sl] = (hn * _sigmoid(mo_ref[:, sl])).astype(o_ref.dtype)


def mlstm_block(zb, zf, gate_bias, norm_g, L=ML_CHUNK):
    B, S, _ = zb.shape
    W = ML_W
    return pl.pallas_call(
        _mlstm_kernel, grid=(B, S // L),
        in_specs=[pl.BlockSpec((None, L, W), lambda b, c: (b, c, 0)),
                  pl.BlockSpec((None, L, W), lambda b, c: (b, c, 1)),
                  pl.BlockSpec((None, L, W), lambda b, c: (b, c, 2)),
                  pl.BlockSpec((None, L, W), lambda b, c: (b, c, 0)),
                  pl.BlockSpec((None, L, LANES), lambda b, c: (b, c, 6)),
                  pl.BlockSpec((1, LANES), lambda b, c: (0, 0)),
                  pl.BlockSpec((1, W), lambda b, c: (0, 0))],
        out_specs=pl.BlockSpec((None, L, W), lambda b, c: (b, c, 0)),
        out_shape=jax.ShapeDtypeStruct((B, S, W), BF16),
        scratch_shapes=[pltpu.VMEM((ML_HEADS, HEAD_DIM, HEAD_DIM), F32), pltpu.VMEM((ML_HEADS, LANES), F32),
                        pltpu.VMEM((ML_HEADS, LANES), F32)],
        compiler_params=_cparams(("parallel", "arbitrary")), name="mlstm",
    )(zb, zb, zb, zf, zf, gate_bias, norm_g.reshape(1, W))


def _compress_kernel(xk_ref, xv_ref, pek_ref, pev_ref, w1k_ref, w1v_ref, w2k_ref, w2v_ref, ok_ref, ov_ref):
    def one(x_ref, pe_ref, w1_ref, w2_ref):
        x = x_ref[...]
        n = x.shape[0]
        xa = (x + pe_ref[0:1, :]).astype(BF16)
        xb = (pltpu.roll(x, n - 1, 0) + pe_ref[1:2, :]).astype(BF16)
        hid = _gelu(_dot(xa, w1_ref[0]) + _dot(xb, w1_ref[1]))
        return _dot(hid.astype(BF16), w2_ref[...])

    ok_ref[...] = one(xk_ref, pek_ref, w1k_ref, w2k_ref).astype(ok_ref.dtype)
    ov_ref[...] = one(xv_ref, pev_ref, w1v_ref, w2v_ref).T.astype(ov_ref.dtype)


def compress_block(kc, vc, k_pe, k_w1, k_w2, v_pe, v_w1, v_w2):
    B, S, _ = kc.shape
    G, d = NSA_KV_HEADS, HEAD_DIM
    nh = S // CMP_STRIDE
    half = CMP_STRIDE * d

    def relayout(a):
        return a.reshape(B, nh, CMP_STRIDE, G, d).transpose(0, 3, 1, 2, 4).reshape(B, G, nh, half)

    def prep(pe, w1):
        return pe.reshape(2, half), w1.reshape(2, half, CMP_HID).astype(BF16)

    pek, w1k = prep(k_pe, k_w1)
    pev, w1v = prep(v_pe, v_w1)
    xspec = pl.BlockSpec((None, None, nh, half), lambda b, g: (b, g, 0, 0))
    ospec = pl.BlockSpec((None, None, nh, d), lambda b, g: (b, g, 0, 0))

    def full(a):
        return pl.BlockSpec(a.shape, lambda b, g: (0,) * a.ndim)

    w2k, w2v = k_w2.astype(BF16), v_w2.astype(BF16)
    return pl.pallas_call(
        _compress_kernel, grid=(B, G),
        in_specs=[xspec, xspec, full(pek), full(pev), full(w1k), full(w1v), full(w2k), full(w2v)],
        out_specs=[ospec, pl.BlockSpec((None, None, d, nh), lambda b, g: (b, g, 0, 0))],
        out_shape=[jax.ShapeDtypeStruct((B, G, nh, d), BF16), jax.ShapeDtypeStruct((B, G, d, nh), BF16)],
        compiler_params=_cparams(("parallel", "parallel")), name="nsa_compress",
    )(relayout(kc), relayout(vc), pek, pev, w1k, w1v, w2k, w2v)


def _nsa_kernel(q_ref, gt_ref, gb_ref, kc_ref, vct_ref, ks_ref, vst_ref, kw_ref, vwt_ref, ovl_ref, aug_ref, caug_ref,
                o_ref, ksa_s, kwa_s, kca_s, ss_s, ps_s, sw_s, pw_s, *, n_cmp):
    TQ, TK, d, HPG = NSA_TQ, NSA_TK, HEAD_DIM, NSA_HPG
    R = HPG * TQ
    g = pl.program_id(1)
    qi = pl.program_id(2)
    q0 = qi * TQ
    kt_d = q0 // TK
    n_cb = kc_ref.shape[0]
    n_sb = ovl_ref.shape[0]
    kaug = ksa_s.shape[1]

    @pl.when(qi == 0)
    def _():
        ksa_s[:, 0:d] = ks_ref[...]
        ksa_s[:, d:] = aug_ref[...]
        kwa_s[:, 0:d] = kw_ref[...]
        kwa_s[:, d:] = aug_ref[...]
        kca_s[:, 0:d] = kc_ref[...]
        kca_s[:, d:] = caug_ref[...]

    def slope(hh):
        return jnp.where(g == 0, 2.0 ** (-(hh + 1)), 2.0 ** (-(HPG + hh + 1))).astype(F32)

    def per_head(fn):
        return jnp.concatenate([fn(hh) for hh in range(HPG)], axis=1)

    def tile_heads(x):
        return jnp.concatenate([x] * HPG, axis=1)

    def first_row(n, val):
        return jnp.where(lax.broadcasted_iota(jnp.int32, (n, TQ), 0) == 0, val, 0.0)

    q_t = (q_ref[...].astype(F32) * 0.125).T

    def q_head(hh):
        return q_t[hh * d:(hh + 1) * d]

    qc_t = per_head(lambda hh: jnp.concatenate([q_head(hh), first_row(kca_s.shape[1] - d, slope(hh) * CMP_STRIDE)],
                                               axis=0)).astype(BF16)
    n_r = lax.broadcasted_iota(jnp.int32, (n_cb, TQ), 0)
    t_c = q0 + lax.broadcasted_iota(jnp.int32, (n_cb, TQ), 1)
    ok_c = (n_r * CMP_STRIDE + (CMP_LEN - 1) <= t_c) & (n_r < n_cmp)
    s_c = _dot(kca_s[...], qc_t) + tile_heads(jnp.where(ok_c, 0.0, NEG_BIG))
    e_c = jnp.exp(s_c - jnp.max(s_c, axis=0, keepdims=True)) * tile_heads(jnp.where(ok_c, 1.0, 0.0))
    p_c = e_c * (1.0 / jnp.maximum(jnp.sum(e_c, axis=0, keepdims=True), 1.0))
    o_c = _dot(vct_ref[...], p_c.astype(BF16))

    p_sum = p_c[:, 0:TQ]
    for hh in range(1, HPG):
        p_sum = p_sum + p_c[:, hh * TQ:(hh + 1) * TQ]
    imp = jnp.dot(ovl_ref[...], p_sum, preferred_element_type=F32, precision=HIGHEST)
    jb = lax.broadcasted_iota(jnp.int32, (n_sb, TQ), 0)
    cur = (q0 + lax.broadcasted_iota(jnp.int32, (n_sb, TQ), 1)) // SEL_LEN
    valid = jb <= cur
    forced = (jb == 0) | (jb == cur) | (jb == cur - 1)
    score = jnp.where(forced, jnp.inf, jnp.where(valid, imp, -jnp.inf))
    rank = jnp.zeros((n_sb, TQ), jnp.int32)
    for j in range(n_sb):
        r = score[j:j + 1, :]
        rank = rank + jnp.where(jb > j, (r >= score).astype(jnp.int32), (r > score).astype(jnp.int32))
    picked = valid & (rank < SEL_TOPN)

    blk_pos = (jb * SEL_LEN).astype(F32)
    n_tail = kaug - d - n_sb

    def q_aug(block_rows):
        return per_head(lambda hh: jnp.concatenate(
            [q_head(hh), block_rows(hh), first_row(n_tail, slope(hh))], axis=0)).astype(BF16)

    qs_t = q_aug(lambda hh: jnp.where(picked, slope(hh) * blk_pos, NEG_BIG))
    qw_t = q_aug(lambda hh: slope(hh) * blk_pos)

    def scores(ka_s, qa_t, kt):
        return _dot(ka_s[pl.ds(pl.multiple_of(kt * TK, TK), TK), :], qa_t)

    def delta(kt):
        return ((q0 - kt * TK) + lax.broadcasted_iota(jnp.int32, (TK, TQ), 1)
                - lax.broadcasted_iota(jnp.int32, (TK, TQ), 0))

    def stage_scores(buf, slot, ka_s, qa_t, kt, mask_add=None):
        s = scores(ka_s, qa_t, kt)
        buf[slot] = s if mask_add is None else s + mask_add

    def stage_probs(sbuf, pbuf, slot, m, l):
        s = sbuf[slot]
        m_new = jnp.maximum(m, jnp.max(s, axis=0, keepdims=True))
        p = jnp.exp(s - m_new)
        alpha = jnp.exp(m - m_new)
        pbuf[slot] = p.astype(BF16)
        return m_new, alpha * l + jnp.sum(p, axis=0, keepdims=True), alpha

    def stage_values(pbuf, slot, vt_ref, kt, alpha, acc):
        return alpha * acc + _dot(vt_ref[:, pl.ds(pl.multiple_of(kt * TK, TK), TK)], pbuf[slot])

    m0, l0, acc0 = jnp.full((1, R), NEG_BIG, F32), jnp.zeros((1, R), F32), jnp.zeros((d, R), F32)

    n_win = (WINDOW - 1 + TK - 1) // TK + 1
    kt_win = []
    for back in range(n_win):
        kt_raw = kt_d - back
        kt = jnp.maximum(kt_raw, 0)
        dl = delta(kt)
        ok_w = (dl >= 0) & (dl < WINDOW) & (kt_raw >= 0)
        stage_scores(sw_s, back, kwa_s, qw_t, kt, tile_heads(jnp.where(ok_w, 0.0, NEG_BIG)))
        kt_win.append(kt)

    filler = 1 - kt_d % 2
    n_seq = kt_d + 1 + filler

    def sel_tile(i):
        return jnp.where(i == 0, kt_d, jnp.maximum(i - 1 - filler, 0))

    stage_scores(ss_s, 0, ksa_s, qs_t, kt_d, tile_heads(jnp.where(delta(kt_d) >= 0, 0.0, NEG_BIG)))
    m, l, alpha = stage_probs(ss_s, ps_s, 0, m0, l0)
    stage_scores(ss_s, 1, ksa_s, qs_t, sel_tile(1), jnp.where(filler == 1, NEG_BIG, 0.0))

    mw, lw, accw = m0, l0, acc0
    for back in range(n_win):
        mw, lw, aw = stage_probs(sw_s, pw_s, back, mw, lw)
        accw = stage_values(pw_s, back, vwt_ref, kt_win[back], aw, accw)
    o_w = accw * (1.0 / lw)

    def sel_body(k, carry):
        m, l, alpha, acc = carry
        i = 2 * k
        stage_scores(ss_s, 0, ksa_s, qs_t, sel_tile(i + 2))
        m, l, alpha1 = stage_probs(ss_s, ps_s, 1, m, l)
        acc = stage_values(ps_s, 0, vst_ref, sel_tile(i), alpha, acc)
        stage_scores(ss_s, 1, ksa_s, qs_t, sel_tile(i + 3))
        m, l, alpha2 = stage_probs(ss_s, ps_s, 0, m, l)
        acc = stage_values(ps_s, 1, vst_ref, sel_tile(i + 1), alpha1, acc)
        return m, l, alpha2, acc

    m, l, alpha, acc = lax.fori_loop(0, (n_seq - 2) // 2, sel_body, (m, l, alpha, acc0))
    m, l, alpha1 = stage_probs(ss_s, ps_s, 1, m, l)
    acc = stage_values(ps_s, 0, vst_ref, sel_tile(n_seq - 2), alpha, acc)
    acc = stage_values(ps_s, 1, vst_ref, sel_tile(n_seq - 1), alpha1, acc)
    o_s = acc * (1.0 / l)

    gates = _sigmoid(gt_ref[...] + gb_ref[...]).T
    gsel = jnp.where(g == 0, gates[16:16 + 3 * HPG], gates[16 + 3 * HPG:16 + 6 * HPG])
    outs = []
    for hh in range(HPG):
        cols = slice(hh * TQ, (hh + 1) * TQ)
        outs.append(gsel[3 * hh:3 * hh + 1] * o_c[:, cols] + gsel[3 * hh + 1:3 * hh + 2] * o_s[:, cols]
                    + gsel[3 * hh + 2:3 * hh + 3] * o_w[:, cols])
    o_ref[...] = jnp.concatenate(outs, axis=0).T.astype(o_ref.dtype)


def nsa_block(zb, zf, gate_bias, kcmp, vcmp_t, q_col_block, kv_col0):
    B, S, _ = zb.shape
    G, d = NSA_KV_HEADS, HEAD_DIM
    TQ = NSA_TQ
    n_cb = kcmp.shape[2]
    n_cmp = (S - CMP_LEN) // CMP_STRIDE + 1
    n_sb = S // SEL_LEN
    kaug = -(-(d + n_sb + 1) // LANES) * LANES
    R = NSA_HPG * TQ
    n_win = (WINDOW - 1 + NSA_TK - 1) // NSA_TK + 1
    assert S % NSA_TK == 0 and NSA_TK % TQ == 0 and TQ % SEL_LEN == 0

    def per_group(j, transposed):
        a = lax.slice_in_dim(zb, kv_col0 + j * G * d, kv_col0 + (j + 1) * G * d, axis=2).reshape(B, S, G, d)
        return a.transpose(0, 2, 3, 1) if transposed else a.transpose(0, 2, 1, 3)

    ks, vs_t, kw, vw_t = per_group(0, False), per_group(1, True), per_group(2, False), per_group(3, True)
    cidx = np.arange(n_cb)[None, :] * CMP_STRIDE
    sstart = np.arange(n_sb)[:, None] * SEL_LEN
    ovl = ((cidx < sstart + SEL_LEN) & (cidx + CMP_LEN - 1 >= sstart) & (np.arange(n_cb)[None, :] < n_cmp))
    ovl = jnp.asarray(ovl.astype(np.float32))
    pos = np.arange(S)
    aug = np.zeros((S, kaug - d), np.float32)
    aug[pos, pos // SEL_LEN] = 1.0
    aug[:, n_sb] = pos % SEL_LEN
    caug = np.zeros((n_cb, d), np.float32)
    caug[:, 0] = np.arange(n_cb)
    aug, caug = jnp.asarray(aug, BF16), jnp.asarray(caug, BF16)

    kspec = pl.BlockSpec((None, None, S, d), lambda b, g, i: (b, g, 0, 0))
    vspec = pl.BlockSpec((None, None, d, S), lambda b, g, i: (b, g, 0, 0))

    def const(a):
        return pl.BlockSpec(a.shape, lambda b, g, i: (0,) * a.ndim)

    return pl.pallas_call(
        functools.partial(_nsa_kernel, n_cmp=n_cmp), grid=(B, G, S // TQ),
        in_specs=[pl.BlockSpec((None, TQ, NSA_HPG * d), lambda b, g, i: (b, i, q_col_block + g)),
                  pl.BlockSpec((None, TQ, LANES), lambda b, g, i: (b, i, 6)),
                  const(gate_bias),
                  pl.BlockSpec((None, None, n_cb, d), lambda b, g, i: (b, g, 0, 0)),
                  pl.BlockSpec((None, None, d, n_cb), lambda b, g, i: (b, g, 0, 0)),
                  kspec, vspec, kspec, vspec, const(ovl), const(aug), const(caug)],
        out_specs=pl.BlockSpec((None, TQ, NSA_HPG * d), lambda b, g, i: (b, i, g)),
        out_shape=jax.ShapeDtypeStruct((B, S, NSA_W), BF16),
        scratch_shapes=[pltpu.VMEM((S, kaug), BF16), pltpu.VMEM((S, kaug), BF16), pltpu.VMEM((n_cb, 2 * d), BF16),
                        pltpu.VMEM((2, NSA_TK, R), F32), pltpu.VMEM((2, NSA_TK, R), BF16),
                        pltpu.VMEM((n_win, NSA_TK, R), F32), pltpu.VMEM((n_win, NSA_TK, R), BF16)],
        compiler_params=_cparams(("parallel", "parallel", "arbitrary")), name="nsa",
    )(zb, zf, gate_bias, kcmp, vcmp_t, ks, vs_t, kw, vw_t, ovl, aug, caug)


def _rglru_kernel(gate_ref, x_ref, halo_ref, cw_ref, cb_ref, wa_ref, wx_ref, ba_ref, bx_ref, lam_ref, o_ref, h_s):
    T, W = x_ref.shape
    first = pl.program_id(1) == 0

    @pl.when(first)
    def _():
        h_s[...] = jnp.zeros_like(h_s)

    xe = jnp.concatenate([jnp.where(first, 0.0, halo_ref[...]), x_ref[...]], axis=0)
    xc = cb_ref[...] + cw_ref[3:4, :] * xe
    for k in range(1, LRU_CONV):
        xc = xc + cw_ref[3 - k:4 - k, :] * pltpu.roll(xe, k, 0)
    xc = xc[LRU_HALO:, :]
    xcb = xc.astype(BF16)
    half = W // 2

    def blockdiag(w_ref):
        return jnp.concatenate([_dot(xcb[:, :half], w_ref[0]), _dot(xcb[:, half:], w_ref[1])], axis=1)

    r = _sigmoid(blockdiag(wa_ref) + ba_ref[...])
    i = _sigmoid(blockdiag(wx_ref) + bx_ref[...])
    nl = -lam_ref[...]
    softplus = jnp.maximum(nl, 0.0) + jnp.log1p(jnp.exp(-jnp.abs(nl)))
    log_a = -LRU_C * r * softplus
    a = jnp.exp(log_a)
    u = jnp.sqrt(-jnp.tanh(log_a) * (a * a + 1.0)) * (i * xc)
    row = lax.broadcasted_iota(jnp.int32, (T, W), 0)
    sft = 1
    while sft < T:
        keep = row >= sft
        u = a * jnp.where(keep, pltpu.roll(u, sft, 0), 0.0) + u
        a = a * jnp.where(keep, pltpu.roll(a, sft, 0), 1.0)
        sft *= 2
    h = u + a * h_s[0:1, :]
    h_s[...] = jnp.broadcast_to(h[T - 1:T, :], h_s.shape)
    o_ref[...] = (_gelu(gate_ref[...]) * h).astype(o_ref.dtype)


def rglru_block(z, conv_w, conv_b, wa, ba, wx, bx, lam, T=LRU_T):
    B, S, _ = z.shape
    W = LRU_W
    half = W // 2
    hb = T // LRU_HALO

    def bd(w):
        blocks = [jax.scipy.linalg.block_diag(*[w[h] for h in range(4 * j, 4 * j + 4)]) for j in range(2)]
        return jnp.stack(blocks).astype(BF16)

    vec = lambda a: a.reshape(1, W)
    vspec = pl.BlockSpec((1, W), lambda b, t: (0, 0))
    wspec = pl.BlockSpec((2, half, half), lambda b, t: (0, 0, 0))
    return pl.pallas_call(
        _rglru_kernel, grid=(B, S // T),
        in_specs=[pl.BlockSpec((None, T, W), lambda b, t: (b, t, 0)),
                  pl.BlockSpec((None, T, W), lambda b, t: (b, t, 1)),
                  pl.BlockSpec((None, LRU_HALO, W), lambda b, t: (b, jnp.maximum(t * hb - 1, 0), 1)),
                  pl.BlockSpec((LRU_CONV, W), lambda b, t: (0, 0)), vspec, wspec, wspec, vspec, vspec, vspec],
        out_specs=pl.BlockSpec((None, T, W), lambda b, t: (b, t, 0)),
        out_shape=jax.ShapeDtypeStruct((B, S, W), BF16),
        scratch_shapes=[pltpu.VMEM((8, W), F32)],
        compiler_params=_cparams(("parallel", "arbitrary")), name="rglru",
    )(z, z, z, conv_w, vec(conv_b), bd(wa), bd(wx), vec(ba), vec(bx), vec(lam))


def _sgu_kernel(u_ref, v_ref, g_ref, b_ref, w_ref, bias_ref, o_ref):
    C, W = v_ref.shape
    dg = W // SG_GROUPS
    v = _gelu(v_ref[...])
    mu = jnp.mean(v, axis=-1, keepdims=True)
    vc = v - mu
    vn = (vc * lax.rsqrt(jnp.mean(vc * vc, axis=-1, keepdims=True) + EPS) * g_ref[...] + b_ref[...]).astype(BF16)
    row = lax.broadcasted_iota(jnp.int32, (C, C), 0)
    col = lax.broadcasted_iota(jnp.int32, (C, C), 1)
    for gi in range(SG_GROUPS):
        sl = slice(gi * dg, (gi + 1) * dg)
        wc = jnp.where(col <= row, w_ref[gi], 0.0).astype(BF16)
        mixed = _dot(wc, vn[:, sl]) + bias_ref[:, sl]
        o_ref[:, sl] = (_gelu(u_ref[:, sl]) * mixed).astype(o_ref.dtype)


def sgu_block(z, ln_g, ln_b, w, b):
    B, S, _ = z.shape
    W, C = SG_W, SG_CHUNK
    bias = jnp.repeat(b.T, W // SG_GROUPS, axis=1)
    vspec = pl.BlockSpec((1, W), lambda bb, c: (0, 0))
    return pl.pallas_call(
        _sgu_kernel, grid=(B, S // C),
        in_specs=[pl.BlockSpec((None, C, W), lambda bb, c: (bb, c, 2)),
                  pl.BlockSpec((None, C, W), lambda bb, c: (bb, c, 3)),
                  vspec, vspec,
                  pl.BlockSpec((SG_GROUPS, C, C), lambda bb, c: (0, 0, 0)),
                  pl.BlockSpec((C, W), lambda bb, c: (0, 0))],
        out_specs=pl.BlockSpec((None, C, W), lambda bb, c: (bb, c, 0)),
        out_shape=jax.ShapeDtypeStruct((B, S, W), BF16),
        compiler_params=_cparams(("parallel", "parallel")), name="sgu",
    )(z, z, ln_g.reshape(1, W), ln_b.reshape(1, W), w, bias)


def _layer_ab(h2, B, S, pre_g, w_in, w_out, ml_gate_b, ml_norm_g, nsa_gate_b,
              k_pe, k_w1, k_w2, v_pe, v_w1, v_w2, post_g):
    D = h2.shape[1]
    G, d = NSA_KV_HEADS, HEAD_DIM
    offs = np.cumsum([0, ML_W, ML_W, ML_W, ML_W, 2 * ML_HEADS, NSA_W] + [G * d] * 6 + [3 * NSA_HEADS])
    mq, mk, mv, mo, mif, nq, kc, vc, ks, vs, kw, vw, ng = [w_in[:, offs[i]:offs[i + 1]] for i in range(13)]
    w_b = jnp.concatenate([mq, mk, mv, nq, ks, vs, kw, vw], axis=1).astype(BF16)
    gpad = LANES - 2 * ML_HEADS - 3 * NSA_HEADS
    w_f = jnp.concatenate([mo, kc, vc, mif, ng, jnp.zeros((D, gpad), w_in.dtype)], axis=1).astype(BF16)
    gate_bias = jnp.concatenate([ml_gate_b, nsa_gate_b, jnp.zeros((gpad,), F32)]).reshape(1, LANES)
    zb, zf = norm_proj(h2, pre_g, [w_b, w_f], [BF16, F32])
    zb = zb.reshape(B, S, -1)
    zf = zf.reshape(B, S, -1)
    h_ml = mlstm_block(zb, zf, gate_bias, ml_norm_g)
    kcmp, vcmp = compress_block(zf[:, :, ML_W:ML_W + G * d], zf[:, :, ML_W + G * d:ML_W + 2 * G * d],
                                k_pe, k_w1, k_w2, v_pe, v_w1, v_w2)
    h_nsa = nsa_block(zb, zf, gate_bias, kcmp, vcmp, q_col_block=3 * ML_W // (NSA_HPG * d), kv_col0=3 * ML_W + NSA_W)
    return mix_out(h_ml.reshape(B * S, ML_W), h_nsa.reshape(B * S, NSA_W), w_out, post_g, h2)


def _layer_cd(h2, B, S, pre_g, w_in, w_out, conv_w, conv_b, wa, ba, wx, bx, lam, sg_g, sg_bn, sg_w, sg_b, post_g):
    (z,) = norm_proj(h2, pre_g, [w_in.astype(BF16)], [F32])
    z = z.reshape(B, S, -1)
    y_lru = rglru_block(z, conv_w, conv_b, wa, ba, wx, bx, lam)
    y_sg = sgu_block(z, sg_g, sg_bn, sg_w, sg_b)
    return mix_out(y_lru.reshape(B * S, LRU_W), y_sg.reshape(B * S, SG_W), w_out, post_g, h2)


def kernel(x, pre_mix_g, post_mix_g, pre_ffn_g, post_ffn_g, ab_w_in, ab_w_out, ml_gate_b, ml_norm_g, nsa_gate_b, cmp_k_pe, cmp_k_w1, cmp_k_w2, cmp_v_pe, cmp_v_w1, cmp_v_w2, cd_w_in, cd_w_out, lru_conv_w, lru_conv_b, lru_wa, lru_ba, lru_wx, lru_bx, lru_lambda, sg_norm_g, sg_norm_b, sg_w, sg_b, ffn_w_up, ffn_conv_w, ffn_conv_b, ffn_w_down):
    B, S, D = x.shape
    depth = pre_mix_g.shape[0]
    h2 = x.reshape(B * S, D)
    for layer in range(depth):
        if layer % 2 == 0:
            e = layer // 2
            h2 = _layer_ab(h2, B, S, pre_mix_g[layer], ab_w_in[e], ab_w_out[e], ml_gate_b[e], ml_norm_g[e],
                           nsa_gate_b[e], cmp_k_pe[e], cmp_k_w1[e], cmp_k_w2[e], cmp_v_pe[e], cmp_v_w1[e],
                           cmp_v_w2[e], post_mix_g[layer])
        else:
            o = layer // 2
            h2 = _layer_cd(h2, B, S, pre_mix_g[layer], cd_w_in[o], cd_w_out[o], lru_conv_w[o], lru_conv_b[o],
                           lru_wa[o], lru_ba[o], lru_wx[o], lru_bx[o], lru_lambda[o], sg_norm_g[o], sg_norm_b[o],
                           sg_w[o], sg_b[o], post_mix_g[layer])
        h2 = ffn_block(h2, S, pre_ffn_g[layer], ffn_w_up[layer], ffn_conv_w[layer], ffn_conv_b[layer],
                       ffn_w_down[layer], post_ffn_g[layer])
    return h2.reshape(B, S, D)
```

```python
import functools

import numpy as np
import jax
import jax.numpy as jnp
from jax import lax
from jax.experimental import pallas as pl
from jax.experimental.pallas import tpu as pltpu

F32 = jnp.float32
BF16 = jnp.bfloat16

EPS = 1e-6
HEAD_DIM = 64
ML_HEADS = 8
ML_W = 512
GATE_SOFTCAP = 15.0
NSA_HEADS = 8
NSA_KV_HEADS = 2
NSA_HPG = NSA_HEADS // NSA_KV_HEADS
NSA_W = 512
CMP_LEN = 32
CMP_STRIDE = 16
CMP_HID = 128
SEL_LEN = 64
SEL_TOPN = 16
WINDOW = 512
LRU_W = 512
LRU_C = 8.0
LRU_CONV = 4
SG_GROUPS = 8
SG_W = 512
SG_CHUNK = 128
FFN_CONV = 3

LANES = 128
VMEM_LIMIT = 56 * 1024 * 1024
NEG_BIG = -1e30
HIGHEST = lax.Precision.HIGHEST
LOG2E = 1.4426950408889634


def _bf16_terms(x, n):
    terms = []
    for _ in range(n):
        bits = int(np.array(x, np.float32).view(np.uint32))
        t = float(np.array((bits + 0x7FFF + ((bits >> 16) & 1)) & 0xFFFF0000, np.uint32).view(np.float32))
        terms.append(t)
        x -= t
    return tuple(terms)


LOG2E_TERMS = _bf16_terms(LOG2E, 3)

ML_CHUNK = 128
ML_NB = 4
NSA_TQ = 256
NSA_TK = 256
ROW_TILE = 512
FFN_TM = 256
FFN_CK = 256
FFN_HALO = 16
LRU_T = 256
LRU_HALO = 8


def _cparams(sem):
    return pltpu.CompilerParams(dimension_semantics=sem, vmem_limit_bytes=VMEM_LIMIT)


def _rms(x, g):
    return x * lax.rsqrt(jnp.mean(x * x, axis=-1, keepdims=True) + EPS) * g


def _gelu(x):
    return 0.5 * x * (1.0 + jnp.tanh(0.7978845608028654 * (x + 0.044715 * (x * x * x))))


def _sigmoid(x):
    return 1.0 / (1.0 + jnp.exp(-x))


def _dot(a, b):
    return jnp.dot(a, b, preferred_element_type=F32)


def _dot_nt(a, b, precision=None):
    return lax.dot_general(a, b, (((1,), (1,)), ((), ())), preferred_element_type=F32, precision=precision)


def _dot_tn(a, b):
    return lax.dot_general(a, b, (((0,), (0,)), ((), ())), preferred_element_type=F32)


def _norm_proj_kernel(h_ref, g_ref, *refs, n_out, cn):
    w_refs, o_refs = refs[:n_out], refs[n_out:]
    xn = _rms(h_ref[...], g_ref[...]).astype(BF16)
    for w_ref, o_ref in zip(w_refs, o_refs):
        n = w_ref.shape[1]
        for c in range(0, n, cn):
            ce = min(c + cn, n)
            o_ref[:, c:ce] = _dot(xn, w_ref[:, c:ce]).astype(o_ref.dtype)


def norm_proj(h2, g, ws, dtypes, tm=ROW_TILE):
    M, D = h2.shape
    in_specs = [pl.BlockSpec((tm, D), lambda i: (i, 0)), pl.BlockSpec((1, D), lambda i: (0, 0))]
    in_specs += [pl.BlockSpec(w.shape, lambda i: (0, 0)) for w in ws]
    out_specs = [pl.BlockSpec((tm, w.shape[1]), lambda i: (i, 0)) for w in ws]
    out_shape = [jax.ShapeDtypeStruct((M, w.shape[1]), dt) for w, dt in zip(ws, dtypes)]
    return pl.pallas_call(
        functools.partial(_norm_proj_kernel, n_out=len(ws), cn=512),
        grid=(M // tm,), in_specs=in_specs, out_specs=out_specs, out_shape=out_shape,
        compiler_params=_cparams(("parallel",)), name="norm_proj",
    )(h2, g.reshape(1, D), *ws)


def _mix_out_kernel(a1_ref, a2_ref, w1_ref, w2_ref, g_ref, h_ref, o_ref):
    y = _dot(a1_ref[...], w1_ref[...]) + _dot(a2_ref[...], w2_ref[...])
    o_ref[...] = h_ref[...] + _rms(y, g_ref[...])


def mix_out(a1, a2, w_out, g, h2, tm=ROW_TILE):
    M, D = h2.shape
    K1, K2 = a1.shape[1], a2.shape[1]
    w1 = w_out[:K1].astype(BF16)
    w2 = w_out[K1:].astype(BF16)
    return pl.pallas_call(
        _mix_out_kernel, grid=(M // tm,),
        in_specs=[pl.BlockSpec((tm, K1), lambda i: (i, 0)), pl.BlockSpec((tm, K2), lambda i: (i, 0)),
                  pl.BlockSpec((K1, D), lambda i: (0, 0)), pl.BlockSpec((K2, D), lambda i: (0, 0)),
                  pl.BlockSpec((1, D), lambda i: (0, 0)), pl.BlockSpec((tm, D), lambda i: (i, 0))],
        out_specs=pl.BlockSpec((tm, D), lambda i: (i, 0)),
        out_shape=jax.ShapeDtypeStruct((M, D), F32),
        compiler_params=_cparams(("parallel",)), name="mix_out",
    )(a1, a2, w1, w2, g.reshape(1, D), h2)


def _ffn_kernel(h_ref, halo_ref, gpre_ref, wa_ref, wb_ref, cwa_ref, cwb_ref, cba_ref, cbb_ref, wd_ref, gpost_ref,
                o_ref, xn_s, acc_s, *, tiles_per_seq, n_chunks):
    tm = h_ref.shape[0]
    first = (pl.program_id(0) % tiles_per_seq) == 0
    x = h_ref[...]
    g = gpre_ref[...]
    xn_s[0:FFN_HALO, :] = jnp.where(first, 0.0, _rms(halo_ref[...], g)).astype(BF16)
    xn_s[FFN_HALO:, :] = _rms(x, g).astype(BF16)
    acc_s[...] = jnp.zeros_like(acc_s)

    def conv(u, cw, cb):
        y = cw[2:3, :] * u + cw[1:2, :] * pltpu.roll(u, 1, 0) + cw[0:1, :] * pltpu.roll(u, 2, 0) + cb
        return y[FFN_HALO:, :]

    def up(c):
        xn = xn_s[...]
        return _dot(xn, wa_ref[c]), _dot(xn, wb_ref[c])

    u = up(0)
    for c in range(n_chunks):
        u_next = up(c + 1) if c + 1 < n_chunks else None
        a = conv(u[0], cwa_ref[c], cba_ref[c])
        b = conv(u[1], cwb_ref[c], cbb_ref[c])
        act = (_gelu(a) * b).astype(BF16)
        acc_s[...] += _dot(act, wd_ref[c])
        u = u_next
    o_ref[...] = x + _rms(acc_s[...], gpost_ref[...])


def ffn_block(h2, seq_len, g_pre, w_up, conv_w, conv_b, w_down, g_post, tm=FFN_TM, ck=FFN_CK):
    M, D = h2.shape
    F = w_down.shape[0]
    nck = F // ck
    assert F % ck == 0 and seq_len % tm == 0 and tm % FFN_HALO == 0

    def chunk_cols(a):
        return a.reshape(a.shape[0], nck, ck).transpose(1, 0, 2)

    wa = chunk_cols(w_up[:, :F]).astype(BF16)
    wb = chunk_cols(w_up[:, F:]).astype(BF16)
    cwa = chunk_cols(conv_w[:, :F])
    cwb = chunk_cols(conv_w[:, F:])
    cba = chunk_cols(conv_b[None, :F])
    cbb = chunk_cols(conv_b[None, F:])
    wd = w_down.reshape(nck, ck, D).astype(BF16)
    hb = tm // FFN_HALO

    def full(a):
        return pl.BlockSpec(a.shape, lambda i: (0,) * a.ndim)

    return pl.pallas_call(
        functools.partial(_ffn_kernel, tiles_per_seq=seq_len // tm, n_chunks=nck),
        grid=(M // tm,),
        in_specs=[pl.BlockSpec((tm, D), lambda i: (i, 0)),
                  pl.BlockSpec((FFN_HALO, D), lambda i: (jnp.maximum(i * hb - 1, 0), 0)),
                  pl.BlockSpec((1, D), lambda i: (0, 0)),
                  full(wa), full(wb), full(cwa), full(cwb), full(cba), full(cbb), full(wd),
                  pl.BlockSpec((1, D), lambda i: (0, 0))],
        out_specs=pl.BlockSpec((tm, D), lambda i: (i, 0)),
        out_shape=jax.ShapeDtypeStruct((M, D), F32),
        scratch_shapes=[pltpu.VMEM((tm + FFN_HALO, D), BF16), pltpu.VMEM((tm, D), F32)],
        compiler_params=_cparams(("parallel",)), name="ffn_block",
    )(h2, h2, g_pre.reshape(1, D), wa, wb, cwa, cwb, cba, cbb, wd, g_post.reshape(1, D))


def _mlstm_kernel(qt_ref, vt_ref, k_ref, mo_ref, gt_ref, gb_ref, ng_ref, o_ref, cn_s, m_s):
    L = qt_ref.shape[2]
    d = HEAD_DIM

    @pl.when(pl.program_id(1) == 0)
    def _():
        cn_s[...] = jnp.zeros_like(cn_s)
        m_s[...] = jnp.zeros_like(m_s)

    src = lax.broadcasted_iota(jnp.int32, (L, L), 0)
    tgt = lax.broadcasted_iota(jnp.int32, (L, L), 1)
    causal = src <= tgt
    tri = (tgt <= src).astype(F32)
    for nb in range(qt_ref.shape[0]):
        _mlstm_chunk(qt_ref.at[nb], vt_ref.at[nb], k_ref.at[nb], mo_ref.at[nb], gt_ref.at[nb], gb_ref, ng_ref,
                     o_ref.at[nb], cn_s.at[nb], m_s.at[nb], causal, tri)


def _mlstm_chunk(qt_ref, vt_ref, k_ref, mo_ref, gt_ref, gb_ref, ng_ref, o_ref, cn_s, m_s, causal, tri):
    L = qt_ref.shape[1]
    d = HEAD_DIM
    gcap = GATE_SOFTCAP * jnp.tanh((gt_ref[...] + gb_ref[...]) * (1.0 / GATE_SOFTCAP))
    lf = jnp.minimum(gcap, 0.0) - jnp.log1p(jnp.exp(-jnp.abs(gcap)))
    b_col = jnp.dot(tri, lf, preferred_element_type=F32, precision=HIGHEST)
    b_row = _dot_nt(lf.T, tri, precision=HIGHEST)
    i_row = gcap.T
    c_col = b_col - pltpu.roll(gcap, ML_HEADS, 1)

    outs = []
    for h in range(ML_HEADS):
        rows = slice(h * d, (h + 1) * d)
        q_t = qt_ref[rows, :]
        v_t = vt_ref[rows, :]
        k = k_ref[h] * 0.125
        br = b_row[ML_HEADS + h:ML_HEADS + h + 1, :]
        ir = i_row[h:h + 1, :]
        g = br[:, L - 1:L]
        m_prev = m_s[h:h + 1, 0:1]
        cn_prev = cn_s[h]

        dlog = jnp.where(causal, br - c_col[:, ML_HEADS + h:ML_HEADS + h + 1], -jnp.inf)
        inter = br + m_prev
        m_row = jnp.maximum(inter, jnp.max(dlog, axis=0, keepdims=True))
        s = _dot(k, q_t) * jnp.exp(dlog - m_row)
        w_inter = jnp.exp(inter - m_row)
        carry = _dot(cn_prev.astype(BF16), q_t)
        num = _dot(v_t, s.astype(BF16)) + w_inter * carry[0:d]
        den = jnp.sum(s, axis=0, keepdims=True) + w_inter * carry[d:d + 1]
        hh = num * (1.0 / jnp.maximum(jnp.abs(den), jnp.exp(-m_row)))

        wlog = g - br + ir
        m_new = jnp.maximum(g + m_prev, jnp.max(wlog, axis=-1, keepdims=True))
        w_row = jnp.exp(wlog - m_new)
        decay = jnp.exp(g + m_prev - m_new)
        vw = jnp.concatenate([v_t.astype(F32) * w_row, jnp.broadcast_to(w_row, (8, L))], axis=0).astype(BF16)
        cn_s[h] = decay * cn_prev + _dot(vw, k)
        m_s[h:h + 1, :] = jnp.broadcast_to(m_new, (1, m_s.shape[1]))

        outs.append(hh * lax.rsqrt(jnp.mean(hh * hh, axis=0, keepdims=True) + EPS) * ng_ref[rows, :])

    o_ref[...] = (jnp.concatenate(outs, axis=0).T * _sigmoid(mo_ref[...])).astype(o_ref.dtype)


def mlstm_block(zb, zf, gate_bias, norm_g, L=ML_CHUNK, NB=ML_NB):
    B, S, _ = zb.shape
    W, H, d = ML_W, ML_HEADS, HEAD_DIM
    assert B % NB == 0 and S % L == 0
    q_t = zb[:, :, 0:W].transpose(0, 2, 1)
    k_h = zb[:, :, W:2 * W].reshape(B, S, H, d).transpose(0, 2, 1, 3)
    v_t = zb[:, :, 2 * W:3 * W].transpose(0, 2, 1)
    ng = jnp.broadcast_to(norm_g.reshape(W, 1), (W, L))
    tspec = pl.BlockSpec((NB, W, L), lambda b, c: (b, 0, c))
    return pl.pallas_call(
        _mlstm_kernel, grid=(B // NB, S // L),
        in_specs=[tspec, tspec,
                  pl.BlockSpec((NB, H, L, d), lambda b, c: (b, 0, c, 0)),
                  pl.BlockSpec((NB, L, W), lambda b, c: (b, c, 0)),
                  pl.BlockSpec((NB, L, LANES), lambda b, c: (b, c, 6)),
                  pl.BlockSpec((1, LANES), lambda b, c: (0, 0)),
                  pl.BlockSpec((W, L), lambda b, c: (0, 0))],
        out_specs=pl.BlockSpec((NB, L, W), lambda b, c: (b, c, 0)),
        out_shape=jax.ShapeDtypeStruct((B, S, W), BF16),
        scratch_shapes=[pltpu.VMEM((NB, H, d + 8, d), F32), pltpu.VMEM((NB, H, LANES), F32)],
        compiler_params=_cparams(("parallel", "arbitrary")), name="mlstm",
    )(q_t, v_t, k_h, zf, zf, gate_bias, ng)


def _compress_kernel(xk_ref, xv_ref, pek_ref, pev_ref, w1k_ref, w1v_ref, w2k_ref, w2v_ref, ok_ref, ov_ref):
    def one(x_ref, pe_ref, w1_ref, w2_ref):
        x = x_ref[...]
        n = x.shape[0]
        xa = (x + pe_ref[0:1, :]).astype(BF16)
        xb = (pltpu.roll(x, n - 1, 0) + pe_ref[1:2, :]).astype(BF16)
        hid = _gelu(_dot(xa, w1_ref[0]) + _dot(xb, w1_ref[1]))
        return _dot(hid.astype(BF16), w2_ref[...])

    ok_ref[...] = one(xk_ref, pek_ref, w1k_ref, w2k_ref).astype(ok_ref.dtype)
    ov_ref[...] = one(xv_ref, pev_ref, w1v_ref, w2v_ref).T.astype(ov_ref.dtype)


def compress_block(kc, vc, k_pe, k_w1, k_w2, v_pe, v_w1, v_w2):
    B, S, _ = kc.shape
    G, d = NSA_KV_HEADS, HEAD_DIM
    nh = S // CMP_STRIDE
    half = CMP_STRIDE * d

    def relayout(a):
        return a.reshape(B, nh, CMP_STRIDE, G, d).transpose(0, 3, 1, 2, 4).reshape(B, G, nh, half)

    def prep(pe, w1):
        return pe.reshape(2, half), w1.reshape(2, half, CMP_HID).astype(BF16)

    pek, w1k = prep(k_pe, k_w1)
    pev, w1v = prep(v_pe, v_w1)
    xspec = pl.BlockSpec((None, None, nh, half), lambda b, g: (b, g, 0, 0))
    ospec = pl.BlockSpec((None, None, nh, d), lambda b, g: (b, g, 0, 0))

    def full(a):
        return pl.BlockSpec(a.shape, lambda b, g: (0,) * a.ndim)

    w2k, w2v = k_w2.astype(BF16), v_w2.astype(BF16)
    return pl.pallas_call(
        _compress_kernel, grid=(B, G),
        in_specs=[xspec, xspec, full(pek), full(pev), full(w1k), full(w1v), full(w2k), full(w2v)],
        out_specs=[ospec, pl.BlockSpec((None, None, d, nh), lambda b, g: (b, g, 0, 0))],
        out_shape=[jax.ShapeDtypeStruct((B, G, nh, d), BF16), jax.ShapeDtypeStruct((B, G, d, nh), BF16)],
        compiler_params=_cparams(("parallel", "parallel")), name="nsa_compress",
    )(relayout(kc), relayout(vc), pek, pev, w1k, w1v, w2k, w2v)


def _nsa_kernel(q_ref, gt_ref, gb_ref, kc_ref, vct_ref, ks_ref, vst_ref, kw_ref, vwt_ref, ovl_ref, aug_ref, caug_ref,
                tail_ref, ctail_ref, o_ref, ksa_s, kwa_s, kca_s, ss_s, ps_s, sw_s, pw_s, *, n_cmp):
    TQ, TK, d, HPG = NSA_TQ, NSA_TK, HEAD_DIM, NSA_HPG
    R = HPG * TQ
    g = pl.program_id(1)
    qi = pl.program_id(2)
    q0 = qi * TQ
    kt_d = q0 // TK
    n_cb = kc_ref.shape[0]
    n_sb = ovl_ref.shape[0]

    @pl.when(qi == 0)
    def _():
        ksa_s[:, 0:d] = ks_ref[...]
        ksa_s[:, d:] = aug_ref[...]
        kwa_s[:, 0:d] = kw_ref[...]
        kwa_s[:, d:] = aug_ref[...]
        kca_s[:, 0:d] = kc_ref[...]
        kca_s[:, d:] = caug_ref[...]

    def slope(hh):
        return jnp.where(g == 0, 2.0 ** (-(hh + 1)), 2.0 ** (-(HPG + hh + 1))).astype(F32)

    def per_head(fn):
        return jnp.concatenate([fn(hh) for hh in range(HPG)], axis=1)

    def tile_heads(x):
        return jnp.concatenate([x] * HPG, axis=1)

    q_t = (q_ref[...].astype(F32) * (LOG2E * 0.125)).T

    def q_head(hh):
        return q_t[hh * d:(hh + 1) * d]

    qc_t = per_head(lambda hh: jnp.concatenate([q_head(hh), ctail_ref[...] * slope(hh)], axis=0)).astype(BF16)
    n_r = lax.broadcasted_iota(jnp.int32, (n_cb, TQ), 0)
    t_c = q0 + lax.broadcasted_iota(jnp.int32, (n_cb, TQ), 1)
    ok_c = (n_r * CMP_STRIDE + (CMP_LEN - 1) <= t_c) & (n_r < n_cmp)
    s_c = _dot(kca_s[...], qc_t) + tile_heads(jnp.where(ok_c, 0.0, NEG_BIG))
    e_c = jnp.exp2(s_c - jnp.max(s_c, axis=0, keepdims=True)) * tile_heads(jnp.where(ok_c, 1.0, 0.0))
    p_c = e_c * (1.0 / jnp.maximum(jnp.sum(e_c, axis=0, keepdims=True), 1.0))
    o_c = _dot(vct_ref[...], p_c.astype(BF16))

    p_sum = p_c[:, 0:TQ]
    for hh in range(1, HPG):
        p_sum = p_sum + p_c[:, hh * TQ:(hh + 1) * TQ]
    imp = jnp.dot(ovl_ref[...], p_sum, preferred_element_type=F32, precision=HIGHEST)
    jb = lax.broadcasted_iota(jnp.int32, (n_sb, TQ), 0)
    cur = (q0 + lax.broadcasted_iota(jnp.int32, (n_sb, TQ), 1)) // SEL_LEN
    valid = jb <= cur
    forced = (jb == 0) | (jb == cur) | (jb == cur - 1)
    score = jnp.where(forced, jnp.inf, jnp.where(valid, imp, -jnp.inf))
    n_grp = n_sb // 8
    grp = [score[8 * a:8 * a + 8] for a in range(n_grp)]
    rank = [jnp.zeros((8, TQ), jnp.int32) for _ in range(n_grp)]
    sub = lax.broadcasted_iota(jnp.int32, (8, TQ), 0)
    for j in range(n_sb):
        r = score[j:j + 1, :]
        for a in range(n_grp):
            if a > j // 8:
                ahead = (r >= grp[a]).astype(jnp.int32)
            elif a < j // 8:
                ahead = (r > grp[a]).astype(jnp.int32)
            else:
                ahead = jnp.where(sub > j % 8, (r >= grp[a]).astype(jnp.int32), (r > grp[a]).astype(jnp.int32))
            rank[a] = rank[a] + ahead
    picked = valid & (jnp.concatenate(rank, axis=0) < SEL_TOPN)

    def q_aug(block_rows):
        return per_head(lambda hh: jnp.concatenate(
            [q_head(hh), block_rows, tail_ref[...] * slope(hh)], axis=0)).astype(BF16)

    qs_t = q_aug(jnp.where(picked, 0.0, NEG_BIG))
    qw_t = q_aug(jnp.zeros((n_sb, TQ), F32))

    def scores(ka_s, qa_t, kt):
        return _dot(ka_s[pl.ds(pl.multiple_of(kt * TK, TK), TK), :], qa_t)

    def delta(kt):
        return ((q0 - kt * TK) + lax.broadcasted_iota(jnp.int32, (TK, TQ), 1)
                - lax.broadcasted_iota(jnp.int32, (TK, TQ), 0))

    def stage_scores(buf, slot, ka_s, qa_t, kt, mask_add=None):
        s = scores(ka_s, qa_t, kt)
        if mask_add is not None:
            s = s + mask_add
        buf[slot] = s
        return jnp.max(s, axis=0, keepdims=True)

    def stage_probs(sbuf, pbuf, slot, tile_max, m, l):
        m_new = jnp.maximum(m, tile_max)
        p = jnp.exp2(sbuf[slot] - m_new)
        alpha = jnp.exp2(m - m_new)
        pbuf[slot] = p.astype(BF16)
        return m_new, alpha * l + jnp.sum(p, axis=0, keepdims=True), alpha

    def stage_values(pbuf, slot, vt_ref, kt, alpha, acc):
        return alpha * acc + _dot(vt_ref[:, pl.ds(pl.multiple_of(kt * TK, TK), TK)], pbuf[slot])

    m0, l0, acc0 = jnp.full((1, R), NEG_BIG, F32), jnp.zeros((1, R), F32), jnp.zeros((d, R), F32)

    n_win = (WINDOW - 1 + TK - 1) // TK + 1
    kt_win, max_win = [], []
    for back in range(n_win):
        kt_raw = kt_d - back
        kt = jnp.maximum(kt_raw, 0)
        if back >= 1 and (back + 1) * TK <= WINDOW:
            mask_add = jnp.where(kt_raw >= 0, 0.0, NEG_BIG)
        else:
            dl = delta(kt)
            mask_add = tile_heads(jnp.where((dl >= 0) & (dl < WINDOW) & (kt_raw >= 0), 0.0, NEG_BIG))
        max_win.append(stage_scores(sw_s, back, kwa_s, qw_t, kt, mask_add))
        kt_win.append(kt)

    filler = 1 - kt_d % 2
    n_seq = kt_d + 1 + filler

    def sel_tile(i):
        return jnp.where(i == 0, kt_d, jnp.maximum(i - 1 - filler, 0))

    tmax0 = stage_scores(ss_s, 0, ksa_s, qs_t, kt_d, tile_heads(jnp.where(delta(kt_d) >= 0, 0.0, NEG_BIG)))
    m, l, alpha = stage_probs(ss_s, ps_s, 0, tmax0, m0, l0)
    tmax1 = stage_scores(ss_s, 1, ksa_s, qs_t, sel_tile(1), jnp.where(filler == 1, NEG_BIG, 0.0))

    mw, lw, accw = m0, l0, acc0
    for back in range(n_win):
        mw, lw, aw = stage_probs(sw_s, pw_s, back, max_win[back], mw, lw)
        accw = stage_values(pw_s, back, vwt_ref, kt_win[back], aw, accw)
    o_w = accw * (1.0 / lw)

    def sel_body(k, carry):
        m, l, alpha, acc, tmax1 = carry
        i = 2 * k
        tmax0 = stage_scores(ss_s, 0, ksa_s, qs_t, sel_tile(i + 2))
        m, l, alpha1 = stage_probs(ss_s, ps_s, 1, tmax1, m, l)
        acc = stage_values(ps_s, 0, vst_ref, sel_tile(i), alpha, acc)
        tmax1 = stage_scores(ss_s, 1, ksa_s, qs_t, sel_tile(i + 3))
        m, l, alpha2 = stage_probs(ss_s, ps_s, 0, tmax0, m, l)
        acc = stage_values(ps_s, 1, vst_ref, sel_tile(i + 1), alpha1, acc)
        return m, l, alpha2, acc, tmax1

    m, l, alpha, acc, tmax1 = lax.fori_loop(0, (n_seq - 2) // 2, sel_body, (m, l, alpha, acc0, tmax1))
    m, l, alpha1 = stage_probs(ss_s, ps_s, 1, tmax1, m, l)
    acc = stage_values(ps_s, 0, vst_ref, sel_tile(n_seq - 2), alpha, acc)
    acc = stage_values(ps_s, 1, vst_ref, sel_tile(n_seq - 1), alpha1, acc)
    o_s = acc * (1.0 / l)

    gates = _sigmoid(gt_ref[...] + gb_ref[...]).T
    gsel = jnp.where(g == 0, gates[16:16 + 3 * HPG], gates[16 + 3 * HPG:16 + 6 * HPG])
    outs = []
    for hh in range(HPG):
        cols = slice(hh * TQ, (hh + 1) * TQ)
        outs.append(gsel[3 * hh:3 * hh + 1] * o_c[:, cols] + gsel[3 * hh + 1:3 * hh + 2] * o_s[:, cols]
                    + gsel[3 * hh + 2:3 * hh + 3] * o_w[:, cols])
    o_ref[...] = jnp.concatenate(outs, axis=0).T.astype(o_ref.dtype)


def nsa_block(zb, zf, gate_bias, kcmp, vcmp_t, q_col_block, kv_col0):
    B, S, _ = zb.shape
    G, d = NSA_KV_HEADS, HEAD_DIM
    TQ = NSA_TQ
    n_cb = kcmp.shape[2]
    n_cmp = (S - CMP_LEN) // CMP_STRIDE + 1
    n_sb = S // SEL_LEN
    n_terms = len(LOG2E_TERMS)
    kaug = -(-(d + n_sb + 2 * n_terms) // LANES) * LANES
    R = NSA_HPG * TQ
    n_win = (WINDOW - 1 + NSA_TK - 1) // NSA_TK + 1
    assert S % NSA_TK == 0 and NSA_TK % TQ == 0 and TQ % SEL_LEN == 0

    def per_group(j, transposed):
        a = lax.slice_in_dim(zb, kv_col0 + j * G * d, kv_col0 + (j + 1) * G * d, axis=2).reshape(B, S, G, d)
        return a.transpose(0, 2, 3, 1) if transposed else a.transpose(0, 2, 1, 3)

    ks, vs_t, kw, vw_t = per_group(0, False), per_group(1, True), per_group(2, False), per_group(3, True)
    cidx = np.arange(n_cb)[None, :] * CMP_STRIDE
    sstart = np.arange(n_sb)[:, None] * SEL_LEN
    ovl = ((cidx < sstart + SEL_LEN) & (cidx + CMP_LEN - 1 >= sstart) & (np.arange(n_cb)[None, :] < n_cmp))
    ovl = jnp.asarray(ovl.astype(np.float32))
    pos = np.arange(S)
    aug = np.zeros((S, kaug - d), np.float32)
    aug[pos, pos // SEL_LEN] = 1.0
    tail = np.zeros((kaug - d - n_sb, TQ), np.float32)
    caug = np.zeros((n_cb, d), np.float32)
    ctail = np.zeros((d, TQ), np.float32)
    for i, term in enumerate(LOG2E_TERMS):
        aug[:, n_sb + i] = pos // SEL_LEN
        aug[:, n_sb + n_terms + i] = pos % SEL_LEN
        tail[i] = term * SEL_LEN
        tail[n_terms + i] = term
        caug[:, i] = np.arange(n_cb)
        ctail[i] = term * CMP_STRIDE
    aug, caug, tail, ctail = jnp.asarray(aug, BF16), jnp.asarray(caug, BF16), jnp.asarray(tail), jnp.asarray(ctail)

    kspec = pl.BlockSpec((None, None, S, d), lambda b, g, i: (b, g, 0, 0))
    vspec = pl.BlockSpec((None, None, d, S), lambda b, g, i: (b, g, 0, 0))

    def const(a):
        return pl.BlockSpec(a.shape, lambda b, g, i: (0,) * a.ndim)

    return pl.pallas_call(
        functools.partial(_nsa_kernel, n_cmp=n_cmp), grid=(B, G, S // TQ),
        in_specs=[pl.BlockSpec((None, TQ, NSA_HPG * d), lambda b, g, i: (b, i, q_col_block + g)),
                  pl.BlockSpec((None, TQ, LANES), lambda b, g, i: (b, i, 6)),
                  const(gate_bias),
                  pl.BlockSpec((None, None, n_cb, d), lambda b, g, i: (b, g, 0, 0)),
                  pl.BlockSpec((None, None, d, n_cb), lambda b, g, i: (b, g, 0, 0)),
                  kspec, vspec, kspec, vspec, const(ovl), const(aug), const(caug), const(tail), const(ctail)],
        out_specs=pl.BlockSpec((None, TQ, NSA_HPG * d), lambda b, g, i: (b, i, g)),
        out_shape=jax.ShapeDtypeStruct((B, S, NSA_W), BF16),
        scratch_shapes=[pltpu.VMEM((S, kaug), BF16), pltpu.VMEM((S, kaug), BF16), pltpu.VMEM((n_cb, 2 * d), BF16),
                        pltpu.VMEM((2, NSA_TK, R), F32), pltpu.VMEM((2, NSA_TK, R), BF16),
                        pltpu.VMEM((n_win, NSA_TK, R), F32), pltpu.VMEM((n_win, NSA_TK, R), BF16)],
        compiler_params=_cparams(("parallel", "parallel", "arbitrary")), name="nsa",
    )(zb, zf, gate_bias, kcmp, vcmp_t, ks, vs_t, kw, vw_t, ovl, aug, caug, tail, ctail)


def _rglru_kernel(gate_ref, x_ref, halo_ref, cw_ref, cb_ref, wa_ref, wx_ref, ba_ref, bx_ref, lam_ref, o_ref, h_s):
    T, W = x_ref.shape
    first = pl.program_id(1) == 0

    @pl.when(first)
    def _():
        h_s[...] = jnp.zeros_like(h_s)

    xe = jnp.concatenate([jnp.where(first, 0.0, halo_ref[...]), x_ref[...]], axis=0)
    xc = cb_ref[...] + cw_ref[3:4, :] * xe
    for k in range(1, LRU_CONV):
        xc = xc + cw_ref[3 - k:4 - k, :] * pltpu.roll(xe, k, 0)
    xc = xc[LRU_HALO:, :]
    xcb = xc.astype(BF16)
    half = W // 2

    def blockdiag(w_ref):
        return jnp.concatenate([_dot(xcb[:, :half], w_ref[0]), _dot(xcb[:, half:], w_ref[1])], axis=1)

    r = _sigmoid(blockdiag(wa_ref) + ba_ref[...])
    i = _sigmoid(blockdiag(wx_ref) + bx_ref[...])
    nl = -lam_ref[...]
    softplus = jnp.maximum(nl, 0.0) + jnp.log1p(jnp.exp(-jnp.abs(nl)))
    log_a = -LRU_C * r * softplus
    a = jnp.exp(log_a)
    u = jnp.sqrt(-jnp.tanh(log_a) * (a * a + 1.0)) * (i * xc)
    row = lax.broadcasted_iota(jnp.int32, (T, W), 0)
    sft = 1
    while sft < T:
        keep = row >= sft
        u = a * jnp.where(keep, pltpu.roll(u, sft, 0), 0.0) + u
        a = a * jnp.where(keep, pltpu.roll(a, sft, 0), 1.0)
        sft *= 2
    h = u + a * h_s[0:1, :]
    h_s[...] = jnp.broadcast_to(h[T - 1:T, :], h_s.shape)
    o_ref[...] = (_gelu(gate_ref[...]) * h).astype(o_ref.dtype)


def rglru_block(z, conv_w, conv_b, wa, ba, wx, bx, lam, T=LRU_T):
    B, S, _ = z.shape
    W = LRU_W
    half = W // 2
    hb = T // LRU_HALO

    def bd(w):
        blocks = [jax.scipy.linalg.block_diag(*[w[h] for h in range(4 * j, 4 * j + 4)]) for j in range(2)]
        return jnp.stack(blocks).astype(BF16)

    vec = lambda a: a.reshape(1, W)
    vspec = pl.BlockSpec((1, W), lambda b, t: (0, 0))
    wspec = pl.BlockSpec((2, half, half), lambda b, t: (0, 0, 0))
    return pl.pallas_call(
        _rglru_kernel, grid=(B, S // T),
        in_specs=[pl.BlockSpec((None, T, W), lambda b, t: (b, t, 0)),
                  pl.BlockSpec((None, T, W), lambda b, t: (b, t, 1)),
                  pl.BlockSpec((None, LRU_HALO, W), lambda b, t: (b, jnp.maximum(t * hb - 1, 0), 1)),
                  pl.BlockSpec((LRU_CONV, W), lambda b, t: (0, 0)), vspec, wspec, wspec, vspec, vspec, vspec],
        out_specs=pl.BlockSpec((None, T, W), lambda b, t: (b, t, 0)),
        out_shape=jax.ShapeDtypeStruct((B, S, W), BF16),
        scratch_shapes=[pltpu.VMEM((8, W), F32)],
        compiler_params=_cparams(("parallel", "arbitrary")), name="rglru",
    )(z, z, z, conv_w, vec(conv_b), bd(wa), bd(wx), vec(ba), vec(bx), vec(lam))


def _sgu_kernel(u_ref, v_ref, g_ref, b_ref, w_ref, bias_ref, o_ref):
    C, W = v_ref.shape
    dg = W // SG_GROUPS
    v = _gelu(v_ref[...])
    mu = jnp.mean(v, axis=-1, keepdims=True)
    vc = v - mu
    vn = (vc * lax.rsqrt(jnp.mean(vc * vc, axis=-1, keepdims=True) + EPS) * g_ref[...] + b_ref[...]).astype(BF16)
    row = lax.broadcasted_iota(jnp.int32, (C, C), 0)
    col = lax.broadcasted_iota(jnp.int32, (C, C), 1)
    for gi in range(SG_GROUPS):
        sl = slice(gi * dg, (gi + 1) * dg)
        wc = jnp.where(col <= row, w_ref[gi], 0.0).astype(BF16)
        mixed = _dot(wc, vn[:, sl]) + bias_ref[:, sl]
        o_ref[:, sl] = (_gelu(u_ref[:, sl]) * mixed).astype(o_ref.dtype)


def sgu_block(z, ln_g, ln_b, w, b):
    B, S, _ = z.shape
    W, C = SG_W, SG_CHUNK
    bias = jnp.repeat(b.T, W // SG_GROUPS, axis=1)
    vspec = pl.BlockSpec((1, W), lambda bb, c: (0, 0))
    return pl.pallas_call(
        _sgu_kernel, grid=(B, S // C),
        in_specs=[pl.BlockSpec((None, C, W), lambda bb, c: (bb, c, 2)),
                  pl.BlockSpec((None, C, W), lambda bb, c: (bb, c, 3)),
                  vspec, vspec,
                  pl.BlockSpec((SG_GROUPS, C, C), lambda bb, c: (0, 0, 0)),
                  pl.BlockSpec((C, W), lambda bb, c: (0, 0))],
        out_specs=pl.BlockSpec((None, C, W), lambda bb, c: (bb, c, 0)),
        out_shape=jax.ShapeDtypeStruct((B, S, W), BF16),
        compiler_params=_cparams(("parallel", "parallel")), name="sgu",
    )(z, z, ln_g.reshape(1, W), ln_b.reshape(1, W), w, bias)


def _layer_ab(h2, B, S, pre_g, w_in, w_out, ml_gate_b, ml_norm_g, nsa_gate_b,
              k_pe, k_w1, k_w2, v_pe, v_w1, v_w2, post_g):
    D = h2.shape[1]
    G, d = NSA_KV_HEADS, HEAD_DIM
    offs = np.cumsum([0, ML_W, ML_W, ML_W, ML_W, 2 * ML_HEADS, NSA_W] + [G * d] * 6 + [3 * NSA_HEADS])
    mq, mk, mv, mo, mif, nq, kc, vc, ks, vs, kw, vw, ng = [w_in[:, offs[i]:offs[i + 1]] for i in range(13)]
    w_b = jnp.concatenate([mq, mk, mv, nq, ks, vs, kw, vw], axis=1).astype(BF16)
    gpad = LANES - 2 * ML_HEADS - 3 * NSA_HEADS
    w_f = jnp.concatenate([mo, kc, vc, mif, ng, jnp.zeros((D, gpad), w_in.dtype)], axis=1).astype(BF16)
    gate_bias = jnp.concatenate([ml_gate_b, nsa_gate_b, jnp.zeros((gpad,), F32)]).reshape(1, LANES)
    zb, zf = norm_proj(h2, pre_g, [w_b, w_f], [BF16, F32])
    zb = zb.reshape(B, S, -1)
    zf = zf.reshape(B, S, -1)
    h_ml = mlstm_block(zb, zf, gate_bias, ml_norm_g)
    kcmp, vcmp = compress_block(zf[:, :, ML_W:ML_W + G * d], zf[:, :, ML_W + G * d:ML_W + 2 * G * d],
                                k_pe, k_w1, k_w2, v_pe, v_w1, v_w2)
    h_nsa = nsa_block(zb, zf, gate_bias, kcmp, vcmp, q_col_block=3 * ML_W // (NSA_HPG * d), kv_col0=3 * ML_W + NSA_W)
    return mix_out(h_ml.reshape(B * S, ML_W), h_nsa.reshape(B * S, NSA_W), w_out, post_g, h2)


def _layer_cd(h2, B, S, pre_g, w_in, w_out, conv_w, conv_b, wa, ba, wx, bx, lam, sg_g, sg_bn, sg_w, sg_b, post_g):
    (z,) = norm_proj(h2, pre_g, [w_in.astype(BF16)], [F32])
    z = z.reshape(B, S, -1)
    y_lru = rglru_block(z, conv_w, conv_b, wa, ba, wx, bx, lam)
    y_sg = sgu_block(z, sg_g, sg_bn, sg_w, sg_b)
    return mix_out(y_lru.reshape(B * S, LRU_W), y_sg.reshape(B * S, SG_W), w_out, post_g, h2)


def kernel(x, pre_mix_g, post_mix_g, pre_ffn_g, post_ffn_g, ab_w_in, ab_w_out, ml_gate_b, ml_norm_g, nsa_gate_b, cmp_k_pe, cmp_k_w1, cmp_k_w2, cmp_v_pe, cmp_v_w1, cmp_v_w2, cd_w_in, cd_w_out, lru_conv_w, lru_conv_b, lru_wa, lru_ba, lru_wx, lru_bx, lru_lambda, sg_norm_g, sg_norm_b, sg_w, sg_b, ffn_w_up, ffn_conv_w, ffn_conv_b, ffn_w_down):
    B, S, D = x.shape
    depth = pre_mix_g.shape[0]
    h2 = x.reshape(B * S, D)
    for layer in range(depth):
        if layer % 2 == 0:
            e = layer // 2
            h2 = _layer_ab(h2, B, S, pre_mix_g[layer], ab_w_in[e], ab_w_out[e], ml_gate_b[e], ml_norm_g[e],
                           nsa_gate_b[e], cmp_k_pe[e], cmp_k_w1[e], cmp_k_w2[e], cmp_v_pe[e], cmp_v_w1[e],
                           cmp_v_w2[e], post_mix_g[layer])
        else:
            o = layer // 2
            h2 = _layer_cd(h2, B, S, pre_mix_g[layer], cd_w_in[o], cd_w_out[o], lru_conv_w[o], lru_conv_b[o],
                           lru_wa[o], lru_ba[o], lru_wx[o], lru_bx[o], lru_lambda[o], sg_norm_g[o], sg_norm_b[o],
                           sg_w[o], sg_b[o], post_mix_g[layer])
        h2 = ffn_block(h2, S, pre_ffn_g[layer], ffn_w_up[layer], ffn_conv_w[layer], ffn_conv_b[layer],
                       ffn_w_down[layer], post_ffn_g[layer])
    return h2.reshape(B, S, D)
```

```python
import functools

import numpy as np
import jax
import jax.numpy as jnp
from jax import lax
from jax.experimental import pallas as pl
from jax.experimental.pallas import tpu as pltpu

F32 = jnp.float32
BF16 = jnp.bfloat16

EPS = 1e-6
HEAD_DIM = 64
ML_HEADS = 8
ML_W = 512
GATE_SOFTCAP = 15.0
NSA_HEADS = 8
NSA_KV_HEADS = 2
NSA_HPG = NSA_HEADS // NSA_KV_HEADS
NSA_W = 512
CMP_LEN = 32
CMP_STRIDE = 16
CMP_HID = 128
SEL_LEN = 64
SEL_TOPN = 16
WINDOW = 512
LRU_W = 512
LRU_C = 8.0
LRU_CONV = 4
SG_GROUPS = 8
SG_W = 512
SG_CHUNK = 128
FFN_CONV = 3

LANES = 128
VMEM_LIMIT = 56 * 1024 * 1024
NEG_BIG = -1e30
HIGHEST = lax.Precision.HIGHEST
LOG2E = 1.4426950408889634


def _bf16_terms(x, n):
    terms = []
    for _ in range(n):
        bits = int(np.array(x, np.float32).view(np.uint32))
        t = float(np.array((bits + 0x7FFF + ((bits >> 16) & 1)) & 0xFFFF0000, np.uint32).view(np.float32))
        terms.append(t)
        x -= t
    return tuple(terms)


LOG2E_TERMS = _bf16_terms(LOG2E, 3)

ML_CHUNK = 128
ML_NB = 4
NSA_TQ = 256
NSA_TK = 256
ROW_TILE = 512
FFN_TM = 256
FFN_CK = 256
FFN_HALO = 16
LRU_T = 256
LRU_HALO = 8


def _cparams(sem):
    return pltpu.CompilerParams(dimension_semantics=sem, vmem_limit_bytes=VMEM_LIMIT)


def _rms(x, g):
    return x * lax.rsqrt(jnp.mean(x * x, axis=-1, keepdims=True) + EPS) * g


def _gelu(x):
    return 0.5 * x * (1.0 + jnp.tanh(0.7978845608028654 * (x + 0.044715 * (x * x * x))))


def _sigmoid(x):
    return 1.0 / (1.0 + jnp.exp(-x))


def _dot(a, b):
    return jnp.dot(a, b, preferred_element_type=F32)


def _dot_nt(a, b, precision=None):
    return lax.dot_general(a, b, (((1,), (1,)), ((), ())), preferred_element_type=F32, precision=precision)


def _dot_tn(a, b):
    return lax.dot_general(a, b, (((0,), (0,)), ((), ())), preferred_element_type=F32)


def _norm_proj_kernel(h_ref, g_ref, *refs, n_row, n_t, cn):
    w_refs, wt_refs = refs[:n_row], refs[n_row:n_row + n_t]
    o_refs, ot_refs = refs[n_row + n_t:2 * n_row + n_t], refs[2 * n_row + n_t:]
    xn = _rms(h_ref[...], g_ref[...]).astype(BF16)
    for w_ref, o_ref in zip(w_refs, o_refs):
        n = w_ref.shape[1]
        for c in range(0, n, cn):
            ce = min(c + cn, n)
            o_ref[:, c:ce] = _dot(xn, w_ref[:, c:ce]).astype(o_ref.dtype)
    for wt_ref, ot_ref in zip(wt_refs, ot_refs):
        n = wt_ref.shape[0]
        for c in range(0, n, cn):
            ce = min(c + cn, n)
            ot_ref[c:ce, :] = _dot_nt(wt_ref[c:ce, :], xn).astype(ot_ref.dtype)


def norm_proj(h2, g, ws, dtypes, wts=(), batch=1, tm=ROW_TILE):
    M, D = h2.shape
    tps = M // batch // tm
    in_specs = [pl.BlockSpec((tm, D), lambda i: (i, 0)), pl.BlockSpec((1, D), lambda i: (0, 0))]
    in_specs += [pl.BlockSpec(w.shape, lambda i: (0, 0)) for w in (*ws, *wts)]
    out_specs = [pl.BlockSpec((tm, w.shape[1]), lambda i: (i, 0)) for w in ws]
    out_specs += [pl.BlockSpec((None, w.shape[0], tm), lambda i: (i // tps, 0, i % tps)) for w in wts]
    out_shape = [jax.ShapeDtypeStruct((M, w.shape[1]), dt) for w, dt in zip(ws, dtypes)]
    out_shape += [jax.ShapeDtypeStruct((batch, w.shape[0], M // batch), BF16) for w in wts]
    return pl.pallas_call(
        functools.partial(_norm_proj_kernel, n_row=len(ws), n_t=len(wts), cn=512),
        grid=(M // tm,), in_specs=in_specs, out_specs=out_specs, out_shape=out_shape,
        compiler_params=_cparams(("parallel",)), name="norm_proj",
    )(h2, g.reshape(1, D), *ws, *wts)


def _mix_out_kernel(a1_ref, a2_ref, w1_ref, w2_ref, g_ref, h_ref, o_ref):
    y = _dot(a1_ref[...], w1_ref[...]) + _dot(a2_ref[...], w2_ref[...])
    o_ref[...] = h_ref[...] + _rms(y, g_ref[...])


def mix_out(a1, a2, w_out, g, h2, tm=ROW_TILE):
    M, D = h2.shape
    K1, K2 = a1.shape[1], a2.shape[1]
    w1 = w_out[:K1].astype(BF16)
    w2 = w_out[K1:].astype(BF16)
    return pl.pallas_call(
        _mix_out_kernel, grid=(M // tm,),
        in_specs=[pl.BlockSpec((tm, K1), lambda i: (i, 0)), pl.BlockSpec((tm, K2), lambda i: (i, 0)),
                  pl.BlockSpec((K1, D), lambda i: (0, 0)), pl.BlockSpec((K2, D), lambda i: (0, 0)),
                  pl.BlockSpec((1, D), lambda i: (0, 0)), pl.BlockSpec((tm, D), lambda i: (i, 0))],
        out_specs=pl.BlockSpec((tm, D), lambda i: (i, 0)),
        out_shape=jax.ShapeDtypeStruct((M, D), F32),
        compiler_params=_cparams(("parallel",)), name="mix_out",
    )(a1, a2, w1, w2, g.reshape(1, D), h2)


def _ffn_kernel(h_ref, halo_ref, gpre_ref, wu_ref, cw_ref, cb_ref, wd_ref, gpost_ref, o_ref, xn_s, acc_s,
                *, tiles_per_seq, ck):
    tm = h_ref.shape[0]
    F = wd_ref.shape[0]
    first = (pl.program_id(0) % tiles_per_seq) == 0
    x = h_ref[...]
    g = gpre_ref[...]
    xn_s[0:FFN_HALO, :] = jnp.where(first, 0.0, _rms(halo_ref[...], g)).astype(BF16)
    xn_s[FFN_HALO:, :] = _rms(x, g).astype(BF16)
    acc_s[...] = jnp.zeros_like(acc_s)

    def conv(u, cols):
        y = (cw_ref[2:3, cols] * u + cw_ref[1:2, cols] * pltpu.roll(u, 1, 0) + cw_ref[0:1, cols] * pltpu.roll(u, 2, 0)
             + cb_ref[:, cols])
        return y[FFN_HALO:, :]

    def cols_of(c, half):
        return slice(half * F + c * ck, half * F + (c + 1) * ck)

    def up(c):
        xn = xn_s[...]
        return _dot(xn, wu_ref[:, cols_of(c, 0)]), _dot(xn, wu_ref[:, cols_of(c, 1)])

    n_chunks = F // ck
    u = up(0)
    for c in range(n_chunks):
        u_next = up(c + 1) if c + 1 < n_chunks else None
        act = (_gelu(conv(u[0], cols_of(c, 0))) * conv(u[1], cols_of(c, 1))).astype(BF16)
        acc_s[...] += _dot(act, wd_ref[c * ck:(c + 1) * ck, :])
        u = u_next
    o_ref[...] = x + _rms(acc_s[...], gpost_ref[...])


def ffn_block(h2, seq_len, g_pre, w_up, conv_w, conv_b, w_down, g_post, tm=FFN_TM, ck=FFN_CK):
    M, D = h2.shape
    F = w_down.shape[0]
    assert F % ck == 0 and seq_len % tm == 0 and tm % FFN_HALO == 0
    hb = tm // FFN_HALO
    args = (h2, h2, g_pre.reshape(1, D), w_up.astype(BF16), conv_w, conv_b.reshape(1, 2 * F), w_down.astype(BF16),
            g_post.reshape(1, D))

    def full(a):
        return pl.BlockSpec(a.shape, lambda i: (0,) * a.ndim)

    return pl.pallas_call(
        functools.partial(_ffn_kernel, tiles_per_seq=seq_len // tm, ck=ck),
        grid=(M // tm,),
        in_specs=[pl.BlockSpec((tm, D), lambda i: (i, 0)),
                  pl.BlockSpec((FFN_HALO, D), lambda i: (jnp.maximum(i * hb - 1, 0), 0))] + [full(a) for a in args[2:]],
        out_specs=pl.BlockSpec((tm, D), lambda i: (i, 0)),
        out_shape=jax.ShapeDtypeStruct((M, D), F32),
        scratch_shapes=[pltpu.VMEM((tm + FFN_HALO, D), BF16), pltpu.VMEM((tm, D), F32)],
        compiler_params=_cparams(("parallel",)), name="ffn_block",
    )(*args)


def _mlstm_kernel(qt_ref, vt_ref, k_ref, mo_ref, gt_ref, gb_ref, ng_ref, o_ref, cn_s, m_s):
    L = qt_ref.shape[2]
    d = HEAD_DIM

    @pl.when(pl.program_id(1) == 0)
    def _():
        cn_s[...] = jnp.zeros_like(cn_s)
        m_s[...] = jnp.zeros_like(m_s)

    src = lax.broadcasted_iota(jnp.int32, (L, L), 0)
    tgt = lax.broadcasted_iota(jnp.int32, (L, L), 1)
    causal = src <= tgt
    tri = (tgt <= src).astype(F32)
    for nb in range(qt_ref.shape[0]):
        _mlstm_chunk(qt_ref.at[nb], vt_ref.at[nb], k_ref.at[nb], mo_ref.at[nb], gt_ref.at[nb], gb_ref, ng_ref,
                     o_ref.at[nb], cn_s.at[nb], m_s.at[nb], causal, tri)


def _mlstm_chunk(qt_ref, vt_ref, k_ref, mo_ref, gt_ref, gb_ref, ng_ref, o_ref, cn_s, m_s, causal, tri):
    L = qt_ref.shape[1]
    d = HEAD_DIM
    gcap = GATE_SOFTCAP * jnp.tanh((gt_ref[...] + gb_ref[...]) * (1.0 / GATE_SOFTCAP))
    lf = jnp.minimum(gcap, 0.0) - jnp.log1p(jnp.exp(-jnp.abs(gcap)))
    b_col = jnp.dot(tri, lf, preferred_element_type=F32, precision=HIGHEST)
    b_row = _dot_nt(lf.T, tri, precision=HIGHEST)
    i_row = gcap.T
    c_col = b_col - pltpu.roll(gcap, ML_HEADS, 1)

    outs = []
    for h in range(ML_HEADS):
        rows = slice(h * d, (h + 1) * d)
        q_t = qt_ref[rows, :]
        v_t = vt_ref[rows, :]
        k = k_ref[:, rows] * 0.125
        br = b_row[ML_HEADS + h:ML_HEADS + h + 1, :]
        ir = i_row[h:h + 1, :]
        g = br[:, L - 1:L]
        m_prev = m_s[h:h + 1, 0:1]
        cn_prev = cn_s[h]

        dlog = jnp.where(causal, br - c_col[:, ML_HEADS + h:ML_HEADS + h + 1], -jnp.inf)
        inter = br + m_prev
        m_row = jnp.maximum(inter, jnp.max(dlog, axis=0, keepdims=True))
        s = _dot(k, q_t) * jnp.exp(dlog - m_row)
        w_inter = jnp.exp(inter - m_row)
        carry = _dot(cn_prev.astype(BF16), q_t)
        num = _dot(v_t, s.astype(BF16)) + w_inter * carry[0:d]
        den = jnp.sum(s, axis=0, keepdims=True) + w_inter * carry[d:d + 1]
        hh = num * (1.0 / jnp.maximum(jnp.abs(den), jnp.exp(-m_row)))

        wlog = g - br + ir
        m_new = jnp.maximum(g + m_prev, jnp.max(wlog, axis=-1, keepdims=True))
        w_row = jnp.exp(wlog - m_new)
        decay = jnp.exp(g + m_prev - m_new)
        vw = jnp.concatenate([v_t.astype(F32) * w_row, jnp.broadcast_to(w_row, (8, L))], axis=0).astype(BF16)
        cn_s[h] = decay * cn_prev + _dot(vw, k)
        m_s[h:h + 1, :] = jnp.broadcast_to(m_new, (1, m_s.shape[1]))

        outs.append(hh * lax.rsqrt(jnp.mean(hh * hh, axis=0, keepdims=True) + EPS) * ng_ref[rows, :])

    o_ref[...] = (jnp.concatenate(outs, axis=0).T * _sigmoid(mo_ref[...])).astype(o_ref.dtype)


def mlstm_block(zt, zb, zf, gate_bias, norm_g, L=ML_CHUNK, NB=ML_NB):
    B, S, _ = zb.shape
    W, H, d = ML_W, ML_HEADS, HEAD_DIM
    assert B % NB == 0 and S % L == 0
    ng = jnp.broadcast_to(norm_g.reshape(W, 1), (W, L))
    return pl.pallas_call(
        _mlstm_kernel, grid=(B // NB, S // L),
        in_specs=[pl.BlockSpec((NB, W, L), lambda b, c: (b, 0, c)),
                  pl.BlockSpec((NB, W, L), lambda b, c: (b, 1, c)),
                  pl.BlockSpec((NB, L, W), lambda b, c: (b, c, 0)),
                  pl.BlockSpec((NB, L, W), lambda b, c: (b, c, 0)),
                  pl.BlockSpec((NB, L, LANES), lambda b, c: (b, c, 6)),
                  pl.BlockSpec((1, LANES), lambda b, c: (0, 0)),
                  pl.BlockSpec((W, L), lambda b, c: (0, 0))],
        out_specs=pl.BlockSpec((NB, L, W), lambda b, c: (b, c, 0)),
        out_shape=jax.ShapeDtypeStruct((B, S, W), BF16),
        scratch_shapes=[pltpu.VMEM((NB, H, d + 8, d), F32), pltpu.VMEM((NB, H, LANES), F32)],
        compiler_params=_cparams(("parallel", "arbitrary")), name="mlstm",
    )(zt, zt, zb, zf, zf, gate_bias, ng)


def _compress_kernel(kc_ref, vc_ref, pek_ref, pev_ref, w1k_ref, w1v_ref, w2k_ref, w2v_ref, ok_ref, ov_ref):
    G = NSA_KV_HEADS
    nh = kc_ref.shape[0] // CMP_STRIDE

    def one(x_ref, pe_ref, w1_ref, w2_ref):
        hid = None
        for l in range(CMP_STRIDE):
            y = x_ref[pl.ds(l, nh, stride=CMP_STRIDE), :]
            ya = (y + pe_ref[l:l + 1, :]).astype(BF16)
            yb = (pltpu.roll(y, nh - 1, 0) + pe_ref[CMP_STRIDE + l:CMP_STRIDE + l + 1, :]).astype(BF16)
            t = _dot(ya, w1_ref[l]) + _dot(yb, w1_ref[CMP_STRIDE + l])
            hid = t if hid is None else hid + t
        hid = _gelu(hid).astype(BF16)
        return [_dot(hid[:, gi * CMP_HID:(gi + 1) * CMP_HID], w2_ref[...]) for gi in range(G)]

    for gi, (ko, vo) in enumerate(zip(one(kc_ref, pek_ref, w1k_ref, w2k_ref), one(vc_ref, pev_ref, w1v_ref, w2v_ref))):
        ok_ref[gi] = ko.astype(ok_ref.dtype)
        ov_ref[gi] = vo.T.astype(ov_ref.dtype)


def compress_block(zf, kc_block, vc_block, k_pe, k_w1, k_w2, v_pe, v_w1, v_w2):
    B, S, _ = zf.shape
    G, d = NSA_KV_HEADS, HEAD_DIM
    nh = S // CMP_STRIDE

    def prep(pe, w1):
        w1bd = jnp.zeros((CMP_LEN, G * d, G * CMP_HID), w1.dtype)
        for gi in range(G):
            w1bd = w1bd.at[:, gi * d:(gi + 1) * d, gi * CMP_HID:(gi + 1) * CMP_HID].set(w1)
        return jnp.tile(pe, (1, G)), w1bd.astype(BF16)

    pek, w1k = prep(k_pe, k_w1)
    pev, w1v = prep(v_pe, v_w1)

    def full(a):
        return pl.BlockSpec(a.shape, lambda b: (0,) * a.ndim)

    w2k, w2v = k_w2.astype(BF16), v_w2.astype(BF16)
    return pl.pallas_call(
        _compress_kernel, grid=(B,),
        in_specs=[pl.BlockSpec((None, S, G * d), lambda b: (b, 0, kc_block)),
                  pl.BlockSpec((None, S, G * d), lambda b: (b, 0, vc_block)),
                  full(pek), full(pev), full(w1k), full(w1v), full(w2k), full(w2v)],
        out_specs=[pl.BlockSpec((None, G, nh, d), lambda b: (b, 0, 0, 0)),
                   pl.BlockSpec((None, G, d, nh), lambda b: (b, 0, 0, 0))],
        out_shape=[jax.ShapeDtypeStruct((B, G, nh, d), BF16), jax.ShapeDtypeStruct((B, G, d, nh), BF16)],
        compiler_params=_cparams(("parallel",)), name="nsa_compress",
    )(zf, zf, pek, pev, w1k, w1v, w2k, w2v)


def _nsa_kernel(q_ref, gt_ref, gb_ref, kc_ref, vct_ref, ks_ref, vst_ref, kw_ref, vwt_ref, ovl_ref, aug_ref, caug_ref,
                tail_ref, ctail_ref, o_ref, ksa_s, kwa_s, kca_s, ss_s, ps_s, sw_s, pw_s, *, n_cmp):
    TQ, TK, d, HPG = NSA_TQ, NSA_TK, HEAD_DIM, NSA_HPG
    R = HPG * TQ
    g = pl.program_id(1)
    qi = pl.program_id(2)
    q0 = qi * TQ
    kt_d = q0 // TK
    n_cb = kc_ref.shape[0]
    n_sb = ovl_ref.shape[0]

    @pl.when(qi == 0)
    def _():
        ksa_s[:, 0:d] = jnp.where(g == 0, ks_ref[:, 0:d], ks_ref[:, d:2 * d])
        ksa_s[:, d:] = aug_ref[...]
        kwa_s[:, 0:d] = jnp.where(g == 0, kw_ref[:, 0:d], kw_ref[:, d:2 * d])
        kwa_s[:, d:] = aug_ref[...]
        kca_s[:, 0:d] = kc_ref[...]
        kca_s[:, d:] = caug_ref[...]

    def slope(hh):
        return jnp.where(g == 0, 2.0 ** (-(hh + 1)), 2.0 ** (-(HPG + hh + 1))).astype(F32)

    def per_head(fn):
        return jnp.concatenate([fn(hh) for hh in range(HPG)], axis=1)

    def tile_heads(x):
        return jnp.concatenate([x] * HPG, axis=1)

    q_t = (q_ref[...].astype(F32) * (LOG2E * 0.125)).T

    def q_head(hh):
        return q_t[hh * d:(hh + 1) * d]

    qc_t = per_head(lambda hh: jnp.concatenate([q_head(hh), ctail_ref[...] * slope(hh)], axis=0)).astype(BF16)
    n_r = lax.broadcasted_iota(jnp.int32, (n_cb, TQ), 0)
    t_c = q0 + lax.broadcasted_iota(jnp.int32, (n_cb, TQ), 1)
    ok_c = (n_r * CMP_STRIDE + (CMP_LEN - 1) <= t_c) & (n_r < n_cmp)
    s_c = _dot(kca_s[...], qc_t) + tile_heads(jnp.where(ok_c, 0.0, NEG_BIG))
    e_c = jnp.exp2(s_c - jnp.max(s_c, axis=0, keepdims=True)) * tile_heads(jnp.where(ok_c, 1.0, 0.0))
    p_c = e_c * (1.0 / jnp.maximum(jnp.sum(e_c, axis=0, keepdims=True), 1.0))
    o_c = _dot(vct_ref[...], p_c.astype(BF16))

    p_sum = p_c[:, 0:TQ]
    for hh in range(1, HPG):
        p_sum = p_sum + p_c[:, hh * TQ:(hh + 1) * TQ]
    imp = jnp.dot(ovl_ref[...], p_sum, preferred_element_type=F32, precision=HIGHEST)
    jb = lax.broadcasted_iota(jnp.int32, (n_sb, TQ), 0)
    cur = (q0 + lax.broadcasted_iota(jnp.int32, (n_sb, TQ), 1)) // SEL_LEN
    valid = jb <= cur
    forced = (jb == 0) | (jb == cur) | (jb == cur - 1)
    score = jnp.where(forced, jnp.inf, jnp.where(valid, imp, -jnp.inf))
    n_grp = n_sb // 8
    grp = [score[8 * a:8 * a + 8] for a in range(n_grp)]
    rank = [jnp.zeros((8, TQ), jnp.int32) for _ in range(n_grp)]
    sub = lax.broadcasted_iota(jnp.int32, (8, TQ), 0)
    for j in range(n_sb):
        r = score[j:j + 1, :]
        for a in range(n_grp):
            if a > j // 8:
                ahead = (r >= grp[a]).astype(jnp.int32)
            elif a < j // 8:
                ahead = (r > grp[a]).astype(jnp.int32)
            else:
                ahead = jnp.where(sub > j % 8, (r >= grp[a]).astype(jnp.int32), (r > grp[a]).astype(jnp.int32))
            rank[a] = rank[a] + ahead
    picked = valid & (jnp.concatenate(rank, axis=0) < SEL_TOPN)

    def q_aug(block_rows):
        return per_head(lambda hh: jnp.concatenate(
            [q_head(hh), block_rows, tail_ref[...] * slope(hh)], axis=0)).astype(BF16)

    qs_t = q_aug(jnp.where(picked, 0.0, NEG_BIG))
    qw_t = q_aug(jnp.zeros((n_sb, TQ), F32))

    def scores(ka_s, qa_t, kt):
        return _dot(ka_s[pl.ds(pl.multiple_of(kt * TK, TK), TK), :], qa_t)

    def delta(kt):
        return ((q0 - kt * TK) + lax.broadcasted_iota(jnp.int32, (TK, TQ), 1)
                - lax.broadcasted_iota(jnp.int32, (TK, TQ), 0))

    def stage_scores(buf, slot, ka_s, qa_t, kt, mask_add=None):
        s = scores(ka_s, qa_t, kt)
        if mask_add is not None:
            s = s + mask_add
        buf[slot] = s
        return jnp.max(s, axis=0, keepdims=True)

    def stage_probs(sbuf, pbuf, slot, tile_max, m, l):
        m_new = jnp.maximum(m, tile_max)
        p = jnp.exp2(sbuf[slot] - m_new)
        alpha = jnp.exp2(m - m_new)
        pbuf[slot] = p.astype(BF16)
        return m_new, alpha * l + jnp.sum(p, axis=0, keepdims=True), alpha

    def stage_values(pbuf, slot, vt_ref, kt, alpha, acc):
        return alpha * acc + _dot(vt_ref[:, pl.ds(pl.multiple_of(kt * TK, TK), TK)], pbuf[slot])

    m0, l0, acc0 = jnp.full((1, R), NEG_BIG, F32), jnp.zeros((1, R), F32), jnp.zeros((d, R), F32)

    n_win = (WINDOW - 1 + TK - 1) // TK + 1
    kt_win, max_win = [], []
    for back in range(n_win):
        kt_raw = kt_d - back
        kt = jnp.maximum(kt_raw, 0)
        if back >= 1 and (back + 1) * TK <= WINDOW:
            mask_add = jnp.where(kt_raw >= 0, 0.0, NEG_BIG)
        else:
            dl = delta(kt)
            mask_add = tile_heads(jnp.where((dl >= 0) & (dl < WINDOW) & (kt_raw >= 0), 0.0, NEG_BIG))
        max_win.append(stage_scores(sw_s, back, kwa_s, qw_t, kt, mask_add))
        kt_win.append(kt)

    filler = 1 - kt_d % 2
    n_seq = kt_d + 1 + filler

    def sel_tile(i):
        return jnp.where(i == 0, kt_d, jnp.maximum(i - 1 - filler, 0))

    tmax0 = stage_scores(ss_s, 0, ksa_s, qs_t, kt_d, tile_heads(jnp.where(delta(kt_d) >= 0, 0.0, NEG_BIG)))
    m, l, alpha = stage_probs(ss_s, ps_s, 0, tmax0, m0, l0)
    tmax1 = stage_scores(ss_s, 1, ksa_s, qs_t, sel_tile(1), jnp.where(filler == 1, NEG_BIG, 0.0))

    mw, lw, accw = m0, l0, acc0
    for back in range(n_win):
        mw, lw, aw = stage_probs(sw_s, pw_s, back, max_win[back], mw, lw)
        accw = stage_values(pw_s, back, vwt_ref, kt_win[back], aw, accw)
    o_w = accw * (1.0 / lw)

    def sel_body(k, carry):
        m, l, alpha, acc, tmax1 = carry
        i = 2 * k
        tmax0 = stage_scores(ss_s, 0, ksa_s, qs_t, sel_tile(i + 2))
        m, l, alpha1 = stage_probs(ss_s, ps_s, 1, tmax1, m, l)
        acc = stage_values(ps_s, 0, vst_ref, sel_tile(i), alpha, acc)
        tmax1 = stage_scores(ss_s, 1, ksa_s, qs_t, sel_tile(i + 3))
        m, l, alpha2 = stage_probs(ss_s, ps_s, 0, tmax0, m, l)
        acc = stage_values(ps_s, 1, vst_ref, sel_tile(i + 1), alpha1, acc)
        return m, l, alpha2, acc, tmax1

    m, l, alpha, acc, tmax1 = lax.fori_loop(0, (n_seq - 2) // 2, sel_body, (m, l, alpha, acc0, tmax1))
    m, l, alpha1 = stage_probs(ss_s, ps_s, 1, tmax1, m, l)
    acc = stage_values(ps_s, 0, vst_ref, sel_tile(n_seq - 2), alpha, acc)
    acc = stage_values(ps_s, 1, vst_ref, sel_tile(n_seq - 1), alpha1, acc)
    o_s = acc * (1.0 / l)

    gates = _sigmoid(gt_ref[...] + gb_ref[...]).T
    gsel = jnp.where(g == 0, gates[16:16 + 3 * HPG], gates[16 + 3 * HPG:16 + 6 * HPG])
    outs = []
    for hh in range(HPG):
        cols = slice(hh * TQ, (hh + 1) * TQ)
        outs.append(gsel[3 * hh:3 * hh + 1] * o_c[:, cols] + gsel[3 * hh + 1:3 * hh + 2] * o_s[:, cols]
                    + gsel[3 * hh + 2:3 * hh + 3] * o_w[:, cols])
    o_ref[...] = jnp.concatenate(outs, axis=0).T.astype(o_ref.dtype)


def nsa_block(zt, zb, zf, gate_bias, kcmp, vcmp_t, q_col_block, k_col_blocks, v_row_blocks):
    B, S, _ = zb.shape
    G, d = NSA_KV_HEADS, HEAD_DIM
    TQ = NSA_TQ
    n_cb = kcmp.shape[2]
    n_cmp = (S - CMP_LEN) // CMP_STRIDE + 1
    n_sb = S // SEL_LEN
    n_terms = len(LOG2E_TERMS)
    kaug = -(-(d + n_sb + 2 * n_terms) // LANES) * LANES
    R = NSA_HPG * TQ
    n_win = (WINDOW - 1 + NSA_TK - 1) // NSA_TK + 1
    assert S % NSA_TK == 0 and NSA_TK % TQ == 0 and TQ % SEL_LEN == 0

    cidx = np.arange(n_cb)[None, :] * CMP_STRIDE
    sstart = np.arange(n_sb)[:, None] * SEL_LEN
    ovl = ((cidx < sstart + SEL_LEN) & (cidx + CMP_LEN - 1 >= sstart) & (np.arange(n_cb)[None, :] < n_cmp))
    ovl = jnp.asarray(ovl.astype(np.float32))
    pos = np.arange(S)
    aug = np.zeros((S, kaug - d), np.float32)
    aug[pos, pos // SEL_LEN] = 1.0
    tail = np.zeros((kaug - d - n_sb, TQ), np.float32)
    caug = np.zeros((n_cb, d), np.float32)
    ctail = np.zeros((d, TQ), np.float32)
    for i, term in enumerate(LOG2E_TERMS):
        aug[:, n_sb + i] = pos // SEL_LEN
        aug[:, n_sb + n_terms + i] = pos % SEL_LEN
        tail[i] = term * SEL_LEN
        tail[n_terms + i] = term
        caug[:, i] = np.arange(n_cb)
        ctail[i] = term * CMP_STRIDE
    aug, caug, tail, ctail = jnp.asarray(aug, BF16), jnp.asarray(caug, BF16), jnp.asarray(tail), jnp.asarray(ctail)

    def kspec(j):
        return pl.BlockSpec((None, S, G * d), lambda b, g, i: (b, 0, k_col_blocks[j]))

    def vspec(j):
        return pl.BlockSpec((None, d, S), lambda b, g, i: (b, v_row_blocks[j] + g, 0))

    def const(a):
        return pl.BlockSpec(a.shape, lambda b, g, i: (0,) * a.ndim)

    return pl.pallas_call(
        functools.partial(_nsa_kernel, n_cmp=n_cmp), grid=(B, G, S // TQ),
        in_specs=[pl.BlockSpec((None, TQ, NSA_HPG * d), lambda b, g, i: (b, i, q_col_block + g)),
                  pl.BlockSpec((None, TQ, LANES), lambda b, g, i: (b, i, 6)),
                  const(gate_bias),
                  pl.BlockSpec((None, None, n_cb, d), lambda b, g, i: (b, g, 0, 0)),
                  pl.BlockSpec((None, None, d, n_cb), lambda b, g, i: (b, g, 0, 0)),
                  kspec(0), vspec(0), kspec(1), vspec(1), const(ovl), const(aug), const(caug), const(tail),
                  const(ctail)],
        out_specs=pl.BlockSpec((None, TQ, NSA_HPG * d), lambda b, g, i: (b, i, g)),
        out_shape=jax.ShapeDtypeStruct((B, S, NSA_W), BF16),
        scratch_shapes=[pltpu.VMEM((S, kaug), BF16), pltpu.VMEM((S, kaug), BF16), pltpu.VMEM((n_cb, 2 * d), BF16),
                        pltpu.VMEM((2, NSA_TK, R), F32), pltpu.VMEM((2, NSA_TK, R), BF16),
                        pltpu.VMEM((n_win, NSA_TK, R), F32), pltpu.VMEM((n_win, NSA_TK, R), BF16)],
        compiler_params=_cparams(("parallel", "parallel", "arbitrary")), name="nsa",
    )(zb, zf, gate_bias, kcmp, vcmp_t, zb, zt, zb, zt, ovl, aug, caug, tail, ctail)


def _rglru_kernel(gate_ref, x_ref, halo_ref, cw_ref, cb_ref, wa_ref, wx_ref, ba_ref, bx_ref, lam_ref, o_ref, h_s):
    T, W = x_ref.shape
    first = pl.program_id(1) == 0

    @pl.when(first)
    def _():
        h_s[...] = jnp.zeros_like(h_s)

    xe = jnp.concatenate([jnp.where(first, 0.0, halo_ref[...]), x_ref[...]], axis=0)
    xc = cb_ref[...] + cw_ref[3:4, :] * xe
    for k in range(1, LRU_CONV):
        xc = xc + cw_ref[3 - k:4 - k, :] * pltpu.roll(xe, k, 0)
    xc = xc[LRU_HALO:, :]
    xcb = xc.astype(BF16)
    half = W // 2

    def blockdiag(w_ref):
        return jnp.concatenate([_dot(xcb[:, :half], w_ref[0]), _dot(xcb[:, half:], w_ref[1])], axis=1)

    r = _sigmoid(blockdiag(wa_ref) + ba_ref[...])
    i = _sigmoid(blockdiag(wx_ref) + bx_ref[...])
    nl = -lam_ref[...]
    softplus = jnp.maximum(nl, 0.0) + jnp.log1p(jnp.exp(-jnp.abs(nl)))
    log_a = -LRU_C * r * softplus
    a = jnp.exp(log_a)
    u = jnp.sqrt(-jnp.tanh(log_a) * (a * a + 1.0)) * (i * xc)
    row = lax.broadcasted_iota(jnp.int32, (T, W), 0)
    sft = 1
    while sft < T:
        keep = row >= sft
        u = a * jnp.where(keep, pltpu.roll(u, sft, 0), 0.0) + u
        a = a * jnp.where(keep, pltpu.roll(a, sft, 0), 1.0)
        sft *= 2
    h = u + a * h_s[0:1, :]
    h_s[...] = jnp.broadcast_to(h[T - 1:T, :], h_s.shape)
    o_ref[...] = (_gelu(gate_ref[...]) * h).astype(o_ref.dtype)


def rglru_block(z, conv_w, conv_b, wa, ba, wx, bx, lam, T=LRU_T):
    B, S, _ = z.shape
    W = LRU_W
    half = W // 2
    hb = T // LRU_HALO

    def bd(w):
        blocks = [jax.scipy.linalg.block_diag(*[w[h] for h in range(4 * j, 4 * j + 4)]) for j in range(2)]
        return jnp.stack(blocks).astype(BF16)

    vec = lambda a: a.reshape(1, W)
    vspec = pl.BlockSpec((1, W), lambda b, t: (0, 0))
    wspec = pl.BlockSpec((2, half, half), lambda b, t: (0, 0, 0))
    return pl.pallas_call(
        _rglru_kernel, grid=(B, S // T),
        in_specs=[pl.BlockSpec((None, T, W), lambda b, t: (b, t, 0)),
                  pl.BlockSpec((None, T, W), lambda b, t: (b, t, 1)),
                  pl.BlockSpec((None, LRU_HALO, W), lambda b, t: (b, jnp.maximum(t * hb - 1, 0), 1)),
                  pl.BlockSpec((LRU_CONV, W), lambda b, t: (0, 0)), vspec, wspec, wspec, vspec, vspec, vspec],
        out_specs=pl.BlockSpec((None, T, W), lambda b, t: (b, t, 0)),
        out_shape=jax.ShapeDtypeStruct((B, S, W), BF16),
        scratch_shapes=[pltpu.VMEM((8, W), F32)],
        compiler_params=_cparams(("parallel", "arbitrary")), name="rglru",
    )(z, z, z, conv_w, vec(conv_b), bd(wa), bd(wx), vec(ba), vec(bx), vec(lam))


def _sgu_kernel(u_ref, v_ref, g_ref, b_ref, w_ref, bias_ref, o_ref):
    C, W = v_ref.shape
    dg = W // SG_GROUPS
    v = _gelu(v_ref[...])
    mu = jnp.mean(v, axis=-1, keepdims=True)
    vc = v - mu
    vn = (vc * lax.rsqrt(jnp.mean(vc * vc, axis=-1, keepdims=True) + EPS) * g_ref[...] + b_ref[...]).astype(BF16)
    row = lax.broadcasted_iota(jnp.int32, (C, C), 0)
    col = lax.broadcasted_iota(jnp.int32, (C, C), 1)
    for gi in range(SG_GROUPS):
        sl = slice(gi * dg, (gi + 1) * dg)
        wc = jnp.where(col <= row, w_ref[gi], 0.0).astype(BF16)
        mixed = _dot(wc, vn[:, sl]) + bias_ref[:, sl]
        o_ref[:, sl] = (_gelu(u_ref[:, sl]) * mixed).astype(o_ref.dtype)


def sgu_block(z, ln_g, ln_b, w, b):
    B, S, _ = z.shape
    W, C = SG_W, SG_CHUNK
    bias = jnp.repeat(b.T, W // SG_GROUPS, axis=1)
    vspec = pl.BlockSpec((1, W), lambda bb, c: (0, 0))
    return pl.pallas_call(
        _sgu_kernel, grid=(B, S // C),
        in_specs=[pl.BlockSpec((None, C, W), lambda bb, c: (bb, c, 2)),
                  pl.BlockSpec((None, C, W), lambda bb, c: (bb, c, 3)),
                  vspec, vspec,
                  pl.BlockSpec((SG_GROUPS, C, C), lambda bb, c: (0, 0, 0)),
                  pl.BlockSpec((C, W), lambda bb, c: (0, 0))],
        out_specs=pl.BlockSpec((None, C, W), lambda bb, c: (bb, c, 0)),
        out_shape=jax.ShapeDtypeStruct((B, S, W), BF16),
        compiler_params=_cparams(("parallel", "parallel")), name="sgu",
    )(z, z, ln_g.reshape(1, W), ln_b.reshape(1, W), w, bias)


def _layer_ab(h2, B, S, pre_g, w_in, w_out, ml_gate_b, ml_norm_g, nsa_gate_b,
              k_pe, k_w1, k_w2, v_pe, v_w1, v_w2, post_g):
    D = h2.shape[1]
    G, d = NSA_KV_HEADS, HEAD_DIM
    offs = np.cumsum([0, ML_W, ML_W, ML_W, ML_W, 2 * ML_HEADS, NSA_W] + [G * d] * 6 + [3 * NSA_HEADS])
    mq, mk, mv, mo, mif, nq, kc, vc, ks, vs, kw, vw, ng = [w_in[:, offs[i]:offs[i + 1]] for i in range(13)]
    w_b = jnp.concatenate([mk, nq, ks, kw], axis=1).astype(BF16)
    w_t = jnp.concatenate([mq, mv, vs, vw], axis=1).T.astype(BF16)
    gpad = LANES - 2 * ML_HEADS - 3 * NSA_HEADS
    w_f = jnp.concatenate([mo, kc, vc, mif, ng, jnp.zeros((D, gpad), w_in.dtype)], axis=1).astype(BF16)
    gate_bias = jnp.concatenate([ml_gate_b, nsa_gate_b, jnp.zeros((gpad,), F32)]).reshape(1, LANES)
    zb, zf, zt = norm_proj(h2, pre_g, [w_b, w_f], [BF16, F32], wts=[w_t], batch=B)
    zb = zb.reshape(B, S, -1)
    zf = zf.reshape(B, S, -1)
    h_ml = mlstm_block(zt, zb, zf, gate_bias, ml_norm_g)
    kcmp, vcmp_t = compress_block(zf, ML_W // (G * d), ML_W // (G * d) + 1, k_pe, k_w1, k_w2, v_pe, v_w1, v_w2)
    h_nsa = nsa_block(zt, zb, zf, gate_bias, kcmp, vcmp_t, q_col_block=ML_W // (NSA_HPG * d),
                      k_col_blocks=((ML_W + NSA_W) // (G * d), (ML_W + NSA_W) // (G * d) + 1),
                      v_row_blocks=(2 * ML_W // d, 2 * ML_W // d + G))
    return mix_out(h_ml.reshape(B * S, ML_W), h_nsa.reshape(B * S, NSA_W), w_out, post_g, h2)


def _layer_cd(h2, B, S, pre_g, w_in, w_out, conv_w, conv_b, wa, ba, wx, bx, lam, sg_g, sg_bn, sg_w, sg_b, post_g):
    (z,) = norm_proj(h2, pre_g, [w_in.astype(BF16)], [F32])
    z = z.reshape(B, S, -1)
    y_lru = rglru_block(z, conv_w, conv_b, wa, ba, wx, bx, lam)
    y_sg = sgu_block(z, sg_g, sg_bn, sg_w, sg_b)
    return mix_out(y_lru.reshape(B * S, LRU_W), y_sg.reshape(B * S, SG_W), w_out, post_g, h2)


def kernel(x, pre_mix_g, post_mix_g, pre_ffn_g, post_ffn_g, ab_w_in, ab_w_out, ml_gate_b, ml_norm_g, nsa_gate_b, cmp_k_pe, cmp_k_w1, cmp_k_w2, cmp_v_pe, cmp_v_w1, cmp_v_w2, cd_w_in, cd_w_out, lru_conv_w, lru_conv_b, lru_wa, lru_ba, lru_wx, lru_bx, lru_lambda, sg_norm_g, sg_norm_b, sg_w, sg_b, ffn_w_up, ffn_conv_w, ffn_conv_b, ffn_w_down):
    B, S, D = x.shape
    depth = pre_mix_g.shape[0]
    h2 = x.reshape(B * S, D)
    for layer in range(depth):
        if layer % 2 == 0:
            e = layer // 2
            h2 = _layer_ab(h2, B, S, pre_mix_g[layer], ab_w_in[e], ab_w_out[e], ml_gate_b[e], ml_norm_g[e],
                           nsa_gate_b[e], cmp_k_pe[e], cmp_k_w1[e], cmp_k_w2[e], cmp_v_pe[e], cmp_v_w1[e],
                           cmp_v_w2[e], post_mix_g[layer])
        else:
            o = layer // 2
            h2 = _layer_cd(h2, B, S, pre_mix_g[layer], cd_w_in[o], cd_w_out[o], lru_conv_w[o], lru_conv_b[o],
                           lru_wa[o], lru_ba[o], lru_wx[o], lru_bx[o], lru_lambda[o], sg_norm_g[o], sg_norm_b[o],
                           sg_w[o], sg_b[o], post_mix_g[layer])
        h2 = ffn_block(h2, S, pre_ffn_g[layer], ffn_w_up[layer], ffn_conv_w[layer], ffn_conv_b[layer],
                       ffn_w_down[layer], post_ffn_g[layer])
    return h2.reshape(B, S, D)
```

```python
import functools

import numpy as np
import jax
import jax.numpy as jnp
from jax import lax
from jax.experimental import pallas as pl
from jax.experimental.pallas import tpu as pltpu

F32 = jnp.float32
BF16 = jnp.bfloat16

EPS = 1e-6
HEAD_DIM = 64
ML_HEADS = 8
ML_W = 512
GATE_SOFTCAP = 15.0
NSA_HEADS = 8
NSA_KV_HEADS = 2
NSA_HPG = NSA_HEADS // NSA_KV_HEADS
NSA_W = 512
CMP_LEN = 32
CMP_STRIDE = 16
CMP_HID = 128
SEL_LEN = 64
SEL_TOPN = 16
WINDOW = 512
LRU_W = 512
LRU_C = 8.0
LRU_CONV = 4
SG_GROUPS = 8
SG_W = 512
SG_CHUNK = 128
FFN_CONV = 3

LANES = 128
VMEM_LIMIT = 56 * 1024 * 1024
NEG_BIG = -1e30
HIGHEST = lax.Precision.HIGHEST
LOG2E = 1.4426950408889634


def _bf16_terms(x, n):
    terms = []
    for _ in range(n):
        bits = int(np.array(x, np.float32).view(np.uint32))
        t = float(np.array((bits + 0x7FFF + ((bits >> 16) & 1)) & 0xFFFF0000, np.uint32).view(np.float32))
        terms.append(t)
        x -= t
    return tuple(terms)


LOG2E_TERMS = _bf16_terms(LOG2E, 3)

ML_CHUNK = 128
ML_NB = 4
NSA_TQ = 256
NSA_TK = 256
ROW_TILE = 512
FFN_TM = 256
FFN_CK = 256
FFN_HALO = 16
LRU_T = 256
LRU_HALO = 8


def _cparams(sem):
    return pltpu.CompilerParams(dimension_semantics=sem, vmem_limit_bytes=VMEM_LIMIT)


def _rms(x, g):
    return x * lax.rsqrt(jnp.mean(x * x, axis=-1, keepdims=True) + EPS) * g


def _gelu(x):
    return 0.5 * x * (1.0 + jnp.tanh(0.7978845608028654 * (x + 0.044715 * (x * x * x))))


def _sigmoid(x):
    return 1.0 / (1.0 + jnp.exp(-x))


def _dot(a, b):
    return jnp.dot(a, b, preferred_element_type=F32)


def _dot_nt(a, b, precision=None):
    return lax.dot_general(a, b, (((1,), (1,)), ((), ())), preferred_element_type=F32, precision=precision)


def _dot_tn(a, b):
    return lax.dot_general(a, b, (((0,), (0,)), ((), ())), preferred_element_type=F32)


def _norm_proj_kernel(h_ref, g_ref, *refs, n_row, n_t, cn):
    w_refs, wt_refs = refs[:n_row], refs[n_row:n_row + n_t]
    o_refs, ot_refs = refs[n_row + n_t:2 * n_row + n_t], refs[2 * n_row + n_t:]
    xn = _rms(h_ref[...], g_ref[...]).astype(BF16)
    for w_ref, o_ref in zip(w_refs, o_refs):
        n = w_ref.shape[1]
        for c in range(0, n, cn):
            ce = min(c + cn, n)
            o_ref[:, c:ce] = _dot(xn, w_ref[:, c:ce]).astype(o_ref.dtype)
    for wt_ref, ot_ref in zip(wt_refs, ot_refs):
        n = wt_ref.shape[0]
        for c in range(0, n, cn):
            ce = min(c + cn, n)
            ot_ref[c:ce, :] = _dot_nt(wt_ref[c:ce, :], xn).astype(ot_ref.dtype)


def norm_proj(h2, g, ws, dtypes, wts=(), batch=1, tm=ROW_TILE):
    M, D = h2.shape
    tps = M // batch // tm
    in_specs = [pl.BlockSpec((tm, D), lambda i: (i, 0)), pl.BlockSpec((1, D), lambda i: (0, 0))]
    in_specs += [pl.BlockSpec(w.shape, lambda i: (0, 0)) for w in (*ws, *wts)]
    out_specs = [pl.BlockSpec((tm, w.shape[1]), lambda i: (i, 0)) for w in ws]
    out_specs += [pl.BlockSpec((None, w.shape[0], tm), lambda i: (i // tps, 0, i % tps)) for w in wts]
    out_shape = [jax.ShapeDtypeStruct((M, w.shape[1]), dt) for w, dt in zip(ws, dtypes)]
    out_shape += [jax.ShapeDtypeStruct((batch, w.shape[0], M // batch), BF16) for w in wts]
    return pl.pallas_call(
        functools.partial(_norm_proj_kernel, n_row=len(ws), n_t=len(wts), cn=512),
        grid=(M // tm,), in_specs=in_specs, out_specs=out_specs, out_shape=out_shape,
        compiler_params=_cparams(("parallel",)), name="norm_proj",
    )(h2, g.reshape(1, D), *ws, *wts)


def _mix_ffn_kernel(h_ref, hh_ref, a1_ref, a1h_ref, a2_ref, a2h_ref, wo1_ref, wo2_ref, gmix_ref, gpre_ref, wu_ref,
                    cw_ref, cb_ref, wd_ref, gpost_ref, o_ref, xn_s, acc_s, *, tiles_per_seq, ck):
    F = wd_ref.shape[0]
    first = (pl.program_id(0) % tiles_per_seq) == 0

    def mixed(h_r, a1_r, a2_r):
        y = _dot(a1_r[...], wo1_ref[...]) + _dot(a2_r[...], wo2_ref[...])
        return h_r[...] + _rms(y, gmix_ref[...])

    x = mixed(h_ref, a1_ref, a2_ref)
    g = gpre_ref[...]
    xn_s[0:FFN_HALO, :] = jnp.where(first, 0.0, _rms(mixed(hh_ref, a1h_ref, a2h_ref), g)).astype(BF16)
    xn_s[FFN_HALO:, :] = _rms(x, g).astype(BF16)
    acc_s[...] = jnp.zeros_like(acc_s)

    def conv(u, cols):
        y = (cw_ref[2:3, cols] * u + cw_ref[1:2, cols] * pltpu.roll(u, 1, 0) + cw_ref[0:1, cols] * pltpu.roll(u, 2, 0)
             + cb_ref[:, cols])
        return y[FFN_HALO:, :]

    def cols_of(c, half):
        return slice(half * F + c * ck, half * F + (c + 1) * ck)

    def up(c):
        xn = xn_s[...]
        return _dot(xn, wu_ref[:, cols_of(c, 0)]), _dot(xn, wu_ref[:, cols_of(c, 1)])

    n_chunks = F // ck
    u = up(0)
    for c in range(n_chunks):
        u_next = up(c + 1) if c + 1 < n_chunks else None
        act = (_gelu(conv(u[0], cols_of(c, 0))) * conv(u[1], cols_of(c, 1))).astype(BF16)
        acc_s[...] += _dot(act, wd_ref[c * ck:(c + 1) * ck, :])
        u = u_next
    o_ref[...] = x + _rms(acc_s[...], gpost_ref[...])


def mix_ffn_block(h2, a1, a2, seq_len, w_out, g_mix, g_pre, w_up, conv_w, conv_b, w_down, g_post, tm=FFN_TM, ck=FFN_CK):
    M, D = h2.shape
    F = w_down.shape[0]
    K1, K2 = a1.shape[1], a2.shape[1]
    assert F % ck == 0 and seq_len % tm == 0 and tm % FFN_HALO == 0
    hb = tm // FFN_HALO
    consts = (w_out[:K1].astype(BF16), w_out[K1:].astype(BF16), g_mix.reshape(1, D), g_pre.reshape(1, D),
              w_up.astype(BF16), conv_w, conv_b.reshape(1, 2 * F), w_down.astype(BF16), g_post.reshape(1, D))

    def tile(width):
        return pl.BlockSpec((tm, width), lambda i: (i, 0))

    def halo(width):
        return pl.BlockSpec((FFN_HALO, width), lambda i: (jnp.maximum(i * hb - 1, 0), 0))

    def full(a):
        return pl.BlockSpec(a.shape, lambda i: (0,) * a.ndim)

    return pl.pallas_call(
        functools.partial(_mix_ffn_kernel, tiles_per_seq=seq_len // tm, ck=ck),
        grid=(M // tm,),
        in_specs=[tile(D), halo(D), tile(K1), halo(K1), tile(K2), halo(K2)] + [full(a) for a in consts],
        out_specs=tile(D),
        out_shape=jax.ShapeDtypeStruct((M, D), F32),
        scratch_shapes=[pltpu.VMEM((tm + FFN_HALO, D), BF16), pltpu.VMEM((tm, D), F32)],
        compiler_params=_cparams(("parallel",)), name="mix_ffn",
    )(h2, h2, a1, a1, a2, a2, *consts)


def _mlstm_kernel(qt_ref, vt_ref, k_ref, mo_ref, gt_ref, gb_ref, ng_ref, o_ref, cn_s, m_s):
    L = qt_ref.shape[2]
    d = HEAD_DIM

    @pl.when(pl.program_id(1) == 0)
    def _():
        cn_s[...] = jnp.zeros_like(cn_s)
        m_s[...] = jnp.zeros_like(m_s)

    src = lax.broadcasted_iota(jnp.int32, (L, L), 0)
    tgt = lax.broadcasted_iota(jnp.int32, (L, L), 1)
    causal = src <= tgt
    tri = (tgt <= src).astype(F32)
    for nb in range(qt_ref.shape[0]):
        _mlstm_chunk(qt_ref.at[nb], vt_ref.at[nb], k_ref.at[nb], mo_ref.at[nb], gt_ref.at[nb], gb_ref, ng_ref,
                     o_ref.at[nb], cn_s.at[nb], m_s.at[nb], causal, tri)


def _mlstm_chunk(qt_ref, vt_ref, k_ref, mo_ref, gt_ref, gb_ref, ng_ref, o_ref, cn_s, m_s, causal, tri):
    L = qt_ref.shape[1]
    d = HEAD_DIM
    gcap = GATE_SOFTCAP * jnp.tanh((gt_ref[...] + gb_ref[...]) * (1.0 / GATE_SOFTCAP))
    lf = jnp.minimum(gcap, 0.0) - jnp.log1p(jnp.exp(-jnp.abs(gcap)))
    b_col = jnp.dot(tri, lf, preferred_element_type=F32, precision=HIGHEST)
    b_row = _dot_nt(lf.T, tri, precision=HIGHEST)
    i_row = gcap.T
    c_col = b_col - pltpu.roll(gcap, ML_HEADS, 1)

    outs = []
    for h in range(ML_HEADS):
        rows = slice(h * d, (h + 1) * d)
        q_t = qt_ref[rows, :]
        v_t = vt_ref[rows, :]
        k = k_ref[:, rows] * 0.125
        br = b_row[ML_HEADS + h:ML_HEADS + h + 1, :]
        ir = i_row[h:h + 1, :]
        g = br[:, L - 1:L]
        m_prev = m_s[h:h + 1, 0:1]
        cn_prev = cn_s[h]

        dlog = jnp.where(causal, br - c_col[:, ML_HEADS + h:ML_HEADS + h + 1], -jnp.inf)
        inter = br + m_prev
        m_row = jnp.maximum(inter, jnp.max(dlog, axis=0, keepdims=True))
        s = _dot(k, q_t) * jnp.exp(dlog - m_row)
        w_inter = jnp.exp(inter - m_row)
        carry = _dot(cn_prev.astype(BF16), q_t)
        num = _dot(v_t, s.astype(BF16)) + w_inter * carry[0:d]
        den = jnp.sum(s, axis=0, keepdims=True) + w_inter * carry[d:d + 1]
        hh = num * (1.0 / jnp.maximum(jnp.abs(den), jnp.exp(-m_row)))

        wlog = g - br + ir
        m_new = jnp.maximum(g + m_prev, jnp.max(wlog, axis=-1, keepdims=True))
        w_row = jnp.exp(wlog - m_new)
        decay = jnp.exp(g + m_prev - m_new)
        vw = jnp.concatenate([v_t.astype(F32) * w_row, jnp.broadcast_to(w_row, (8, L))], axis=0).astype(BF16)
        cn_s[h] = decay * cn_prev + _dot(vw, k)
        m_s[h:h + 1, :] = jnp.broadcast_to(m_new, (1, m_s.shape[1]))

        outs.append(hh * lax.rsqrt(jnp.mean(hh * hh, axis=0, keepdims=True) + EPS) * ng_ref[rows, :])

    o_ref[...] = (jnp.concatenate(outs, axis=0).T * _sigmoid(mo_ref[...])).astype(o_ref.dtype)


def mlstm_block(zt, zb, zf, gate_bias, norm_g, L=ML_CHUNK, NB=ML_NB):
    B, S, _ = zb.shape
    W, H, d = ML_W, ML_HEADS, HEAD_DIM
    assert B % NB == 0 and S % L == 0
    ng = jnp.broadcast_to(norm_g.reshape(W, 1), (W, L))
    return pl.pallas_call(
        _mlstm_kernel, grid=(B // NB, S // L),
        in_specs=[pl.BlockSpec((NB, W, L), lambda b, c: (b, 0, c)),
                  pl.BlockSpec((NB, W, L), lambda b, c: (b, 1, c)),
                  pl.BlockSpec((NB, L, W), lambda b, c: (b, c, 0)),
                  pl.BlockSpec((NB, L, W), lambda b, c: (b, c, 0)),
                  pl.BlockSpec((NB, L, LANES), lambda b, c: (b, c, 6)),
                  pl.BlockSpec((1, LANES), lambda b, c: (0, 0)),
                  pl.BlockSpec((W, L), lambda b, c: (0, 0))],
        out_specs=pl.BlockSpec((NB, L, W), lambda b, c: (b, c, 0)),
        out_shape=jax.ShapeDtypeStruct((B, S, W), BF16),
        scratch_shapes=[pltpu.VMEM((NB, H, d + 8, d), F32), pltpu.VMEM((NB, H, LANES), F32)],
        compiler_params=_cparams(("parallel", "arbitrary")), name="mlstm",
    )(zt, zt, zb, zf, zf, gate_bias, ng)


def _compress_kernel(kc_ref, vc_ref, pek_ref, pev_ref, w1k_ref, w1v_ref, w2k_ref, w2v_ref, ok_ref, ov_ref):
    G = NSA_KV_HEADS
    nh = kc_ref.shape[0] // CMP_STRIDE

    def one(x_ref, pe_ref, w1_ref, w2_ref):
        hid = None
        for l in range(CMP_STRIDE):
            y = x_ref[pl.ds(l, nh, stride=CMP_STRIDE), :]
            ya = (y + pe_ref[l:l + 1, :]).astype(BF16)
            yb = (pltpu.roll(y, nh - 1, 0) + pe_ref[CMP_STRIDE + l:CMP_STRIDE + l + 1, :]).astype(BF16)
            t = _dot(ya, w1_ref[l]) + _dot(yb, w1_ref[CMP_STRIDE + l])
            hid = t if hid is None else hid + t
        hid = _gelu(hid).astype(BF16)
        return [_dot(hid[:, gi * CMP_HID:(gi + 1) * CMP_HID], w2_ref[...]) for gi in range(G)]

    for gi, (ko, vo) in enumerate(zip(one(kc_ref, pek_ref, w1k_ref, w2k_ref), one(vc_ref, pev_ref, w1v_ref, w2v_ref))):
        ok_ref[gi] = ko.astype(ok_ref.dtype)
        ov_ref[gi] = vo.T.astype(ov_ref.dtype)


def compress_block(zf, kc_block, vc_block, k_pe, k_w1, k_w2, v_pe, v_w1, v_w2):
    B, S, _ = zf.shape
    G, d = NSA_KV_HEADS, HEAD_DIM
    nh = S // CMP_STRIDE

    def prep(pe, w1):
        w1bd = jnp.zeros((CMP_LEN, G * d, G * CMP_HID), w1.dtype)
        for gi in range(G):
            w1bd = w1bd.at[:, gi * d:(gi + 1) * d, gi * CMP_HID:(gi + 1) * CMP_HID].set(w1)
        return jnp.tile(pe, (1, G)), w1bd.astype(BF16)

    pek, w1k = prep(k_pe, k_w1)
    pev, w1v = prep(v_pe, v_w1)

    def full(a):
        return pl.BlockSpec(a.shape, lambda b: (0,) * a.ndim)

    w2k, w2v = k_w2.astype(BF16), v_w2.astype(BF16)
    return pl.pallas_call(
        _compress_kernel, grid=(B,),
        in_specs=[pl.BlockSpec((None, S, G * d), lambda b: (b, 0, kc_block)),
                  pl.BlockSpec((None, S, G * d), lambda b: (b, 0, vc_block)),
                  full(pek), full(pev), full(w1k), full(w1v), full(w2k), full(w2v)],
        out_specs=[pl.BlockSpec((None, G, nh, d), lambda b: (b, 0, 0, 0)),
                   pl.BlockSpec((None, G, d, nh), lambda b: (b, 0, 0, 0))],
        out_shape=[jax.ShapeDtypeStruct((B, G, nh, d), BF16), jax.ShapeDtypeStruct((B, G, d, nh), BF16)],
        compiler_params=_cparams(("parallel",)), name="nsa_compress",
    )(zf, zf, pek, pev, w1k, w1v, w2k, w2v)


def _nsa_kernel(q_ref, gt_ref, gb_ref, kc_ref, vct_ref, ks_ref, vst_ref, kw_ref, vwt_ref, ovl_ref, aug_ref, caug_ref,
                tail_ref, ctail_ref, o_ref, ksa_s, kwa_s, kca_s, ss_s, ps_s, sw_s, pw_s, *, n_cmp):
    TQ, TK, d, HPG = NSA_TQ, NSA_TK, HEAD_DIM, NSA_HPG
    R = HPG * TQ
    g = pl.program_id(1)
    qi = pl.program_id(2)
    q0 = qi * TQ
    kt_d = q0 // TK
    n_cb = kc_ref.shape[0]
    n_sb = ovl_ref.shape[0]

    @pl.when(qi == 0)
    def _():
        ksa_s[:, 0:d] = jnp.where(g == 0, ks_ref[:, 0:d], ks_ref[:, d:2 * d])
        ksa_s[:, d:] = aug_ref[...]
        kwa_s[:, 0:d] = jnp.where(g == 0, kw_ref[:, 0:d], kw_ref[:, d:2 * d])
        kwa_s[:, d:] = aug_ref[...]
        kca_s[:, 0:d] = kc_ref[...]
        kca_s[:, d:] = caug_ref[...]

    def slope(hh):
        return jnp.where(g == 0, 2.0 ** (-(hh + 1)), 2.0 ** (-(HPG + hh + 1))).astype(F32)

    def per_head(fn):
        return jnp.concatenate([fn(hh) for hh in range(HPG)], axis=1)

    def tile_heads(x):
        return jnp.concatenate([x] * HPG, axis=1)

    q_t = (q_ref[...].astype(F32) * (LOG2E * 0.125)).T

    def q_head(hh):
        return q_t[hh * d:(hh + 1) * d]

    qc_t = per_head(lambda hh: jnp.concatenate([q_head(hh), ctail_ref[...] * slope(hh)], axis=0)).astype(BF16)
    n_r = lax.broadcasted_iota(jnp.int32, (n_cb, TQ), 0)
    t_c = q0 + lax.broadcasted_iota(jnp.int32, (n_cb, TQ), 1)
    ok_c = (n_r * CMP_STRIDE + (CMP_LEN - 1) <= t_c) & (n_r < n_cmp)
    s_c = _dot(kca_s[...], qc_t) + tile_heads(jnp.where(ok_c, 0.0, NEG_BIG))
    e_c = jnp.exp2(s_c - jnp.max(s_c, axis=0, keepdims=True)) * tile_heads(jnp.where(ok_c, 1.0, 0.0))
    p_c = e_c * (1.0 / jnp.maximum(jnp.sum(e_c, axis=0, keepdims=True), 1.0))
    o_c = _dot(vct_ref[...], p_c.astype(BF16))

    p_sum = p_c[:, 0:TQ]
    for hh in range(1, HPG):
        p_sum = p_sum + p_c[:, hh * TQ:(hh + 1) * TQ]
    imp = jnp.dot(ovl_ref[...], p_sum, preferred_element_type=F32, precision=HIGHEST)
    jb = lax.broadcasted_iota(jnp.int32, (n_sb, TQ), 0)
    cur = (q0 + lax.broadcasted_iota(jnp.int32, (n_sb, TQ), 1)) // SEL_LEN
    valid = jb <= cur
    forced = (jb == 0) | (jb == cur) | (jb == cur - 1)
    score = jnp.where(forced, jnp.inf, jnp.where(valid, imp, -jnp.inf))
    n_grp = n_sb // 8
    grp = [score[8 * a:8 * a + 8] for a in range(n_grp)]
    rank = [jnp.zeros((8, TQ), jnp.int32) for _ in range(n_grp)]
    sub = lax.broadcasted_iota(jnp.int32, (8, TQ), 0)
    for j in range(n_sb):
        r = score[j:j + 1, :]
        for a in range(n_grp):
            if a > j // 8:
                ahead = (r >= grp[a]).astype(jnp.int32)
            elif a < j // 8:
                ahead = (r > grp[a]).astype(jnp.int32)
            else:
                ahead = jnp.where(sub > j % 8, (r >= grp[a]).astype(jnp.int32), (r > grp[a]).astype(jnp.int32))
            rank[a] = rank[a] + ahead
    picked = valid & (jnp.concatenate(rank, axis=0) < SEL_TOPN)

    def q_aug(block_rows):
        return per_head(lambda hh: jnp.concatenate(
            [q_head(hh), block_rows, tail_ref[...] * slope(hh)], axis=0)).astype(BF16)

    qs_t = q_aug(jnp.where(picked, 0.0, NEG_BIG))
    qw_t = q_aug(jnp.zeros((n_sb, TQ), F32))

    def scores(ka_s, qa_t, kt):
        return _dot(ka_s[pl.ds(pl.multiple_of(kt * TK, TK), TK), :], qa_t)

    def delta(kt):
        return ((q0 - kt * TK) + lax.broadcasted_iota(jnp.int32, (TK, TQ), 1)
                - lax.broadcasted_iota(jnp.int32, (TK, TQ), 0))

    def stage_scores(buf, slot, ka_s, qa_t, kt, mask_add=None):
        s = scores(ka_s, qa_t, kt)
        if mask_add is not None:
            s = s + mask_add
        buf[slot] = s
        return jnp.max(s, axis=0, keepdims=True)

    def stage_probs(sbuf, pbuf, slot, tile_max, m, l):
        m_new = jnp.maximum(m, tile_max)
        p = jnp.exp2(sbuf[slot] - m_new)
        alpha = jnp.exp2(m - m_new)
        pbuf[slot] = p.astype(BF16)
        return m_new, alpha * l + jnp.sum(p, axis=0, keepdims=True), alpha

    def stage_values(pbuf, slot, vt_ref, kt, alpha, acc):
        return alpha * acc + _dot(vt_ref[:, pl.ds(pl.multiple_of(kt * TK, TK), TK)], pbuf[slot])

    m0, l0, acc0 = jnp.full((1, R), NEG_BIG, F32), jnp.zeros((1, R), F32), jnp.zeros((d, R), F32)

    n_win = (WINDOW - 1 + TK - 1) // TK + 1
    kt_win, max_win = [], []
    for back in range(n_win):
        kt_raw = kt_d - back
        kt = jnp.maximum(kt_raw, 0)
        if back >= 1 and (back + 1) * TK <= WINDOW:
            mask_add = jnp.where(kt_raw >= 0, 0.0, NEG_BIG)
        else:
            dl = delta(kt)
            mask_add = tile_heads(jnp.where((dl >= 0) & (dl < WINDOW) & (kt_raw >= 0), 0.0, NEG_BIG))
        max_win.append(stage_scores(sw_s, back, kwa_s, qw_t, kt, mask_add))
        kt_win.append(kt)

    filler = 1 - kt_d % 2
    n_seq = kt_d + 1 + filler

    def sel_tile(i):
        return jnp.where(i == 0, kt_d, jnp.maximum(i - 1 - filler, 0))

    tmax0 = stage_scores(ss_s, 0, ksa_s, qs_t, kt_d, tile_heads(jnp.where(delta(kt_d) >= 0, 0.0, NEG_BIG)))
    m, l, alpha = stage_probs(ss_s, ps_s, 0, tmax0, m0, l0)
    tmax1 = stage_scores(ss_s, 1, ksa_s, qs_t, sel_tile(1), jnp.where(filler == 1, NEG_BIG, 0.0))

    mw, lw, accw = m0, l0, acc0
    for back in range(n_win):
        mw, lw, aw = stage_probs(sw_s, pw_s, back, max_win[back], mw, lw)
        accw = stage_values(pw_s, back, vwt_ref, kt_win[back], aw, accw)
    o_w = accw * (1.0 / lw)

    def sel_body(k, carry):
        m, l, alpha, acc, tmax1 = carry
        i = 2 * k
        tmax0 = stage_scores(ss_s, 0, ksa_s, qs_t, sel_tile(i + 2))
        m, l, alpha1 = stage_probs(ss_s, ps_s, 1, tmax1, m, l)
        acc = stage_values(ps_s, 0, vst_ref, sel_tile(i), alpha, acc)
        tmax1 = stage_scores(ss_s, 1, ksa_s, qs_t, sel_tile(i + 3))
        m, l, alpha2 = stage_probs(ss_s, ps_s, 0, tmax0, m, l)
        acc = stage_values(ps_s, 1, vst_ref, sel_tile(i + 1), alpha1, acc)
        return m, l, alpha2, acc, tmax1

    m, l, alpha, acc, tmax1 = lax.fori_loop(0, (n_seq - 2) // 2, sel_body, (m, l, alpha, acc0, tmax1))
    m, l, alpha1 = stage_probs(ss_s, ps_s, 1, tmax1, m, l)
    acc = stage_values(ps_s, 0, vst_ref, sel_tile(n_seq - 2), alpha, acc)
    acc = stage_values(ps_s, 1, vst_ref, sel_tile(n_seq - 1), alpha1, acc)
    o_s = acc * (1.0 / l)

    gates = _sigmoid(gt_ref[...] + gb_ref[...]).T
    gsel = jnp.where(g == 0, gates[16:16 + 3 * HPG], gates[16 + 3 * HPG:16 + 6 * HPG])
    outs = []
    for hh in range(HPG):
        cols = slice(hh * TQ, (hh + 1) * TQ)
        outs.append(gsel[3 * hh:3 * hh + 1] * o_c[:, cols] + gsel[3 * hh + 1:3 * hh + 2] * o_s[:, cols]
                    + gsel[3 * hh + 2:3 * hh + 3] * o_w[:, cols])
    o_ref[...] = jnp.concatenate(outs, axis=0).T.astype(o_ref.dtype)


def nsa_block(zt, zb, zf, gate_bias, kcmp, vcmp_t, q_col_block, k_col_blocks, v_row_blocks):
    B, S, _ = zb.shape
    G, d = NSA_KV_HEADS, HEAD_DIM
    TQ = NSA_TQ
    n_cb = kcmp.shape[2]
    n_cmp = (S - CMP_LEN) // CMP_STRIDE + 1
    n_sb = S // SEL_LEN
    n_terms = len(LOG2E_TERMS)
    kaug = -(-(d + n_sb + 2 * n_terms) // LANES) * LANES
    R = NSA_HPG * TQ
    n_win = (WINDOW - 1 + NSA_TK - 1) // NSA_TK + 1
    assert S % NSA_TK == 0 and NSA_TK % TQ == 0 and TQ % SEL_LEN == 0

    cidx = np.arange(n_cb)[None, :] * CMP_STRIDE
    sstart = np.arange(n_sb)[:, None] * SEL_LEN
    ovl = ((cidx < sstart + SEL_LEN) & (cidx + CMP_LEN - 1 >= sstart) & (np.arange(n_cb)[None, :] < n_cmp))
    ovl = jnp.asarray(ovl.astype(np.float32))
    pos = np.arange(S)
    aug = np.zeros((S, kaug - d), np.float32)
    aug[pos, pos // SEL_LEN] = 1.0
    tail = np.zeros((kaug - d - n_sb, TQ), np.float32)
    caug = np.zeros((n_cb, d), np.float32)
    ctail = np.zeros((d, TQ), np.float32)
    for i, term in enumerate(LOG2E_TERMS):
        aug[:, n_sb + i] = pos // SEL_LEN
        aug[:, n_sb + n_terms + i] = pos % SEL_LEN
        tail[i] = term * SEL_LEN
        tail[n_terms + i] = term
        caug[:, i] = np.arange(n_cb)
        ctail[i] = term * CMP_STRIDE
    aug, caug, tail, ctail = jnp.asarray(aug, BF16), jnp.asarray(caug, BF16), jnp.asarray(tail), jnp.asarray(ctail)

    def kspec(j):
        return pl.BlockSpec((None, S, G * d), lambda b, g, i: (b, 0, k_col_blocks[j]))

    def vspec(j):
        return pl.BlockSpec((None, d, S), lambda b, g, i: (b, v_row_blocks[j] + g, 0))

    def const(a):
        return pl.BlockSpec(a.shape, lambda b, g, i: (0,) * a.ndim)

    return pl.pallas_call(
        functools.partial(_nsa_kernel, n_cmp=n_cmp), grid=(B, G, S // TQ),
        in_specs=[pl.BlockSpec((None, TQ, NSA_HPG * d), lambda b, g, i: (b, i, q_col_block + g)),
                  pl.BlockSpec((None, TQ, LANES), lambda b, g, i: (b, i, 6)),
                  const(gate_bias),
                  pl.BlockSpec((None, None, n_cb, d), lambda b, g, i: (b, g, 0, 0)),
                  pl.BlockSpec((None, None, d, n_cb), lambda b, g, i: (b, g, 0, 0)),
                  kspec(0), vspec(0), kspec(1), vspec(1), const(ovl), const(aug), const(caug), const(tail),
                  const(ctail)],
        out_specs=pl.BlockSpec((None, TQ, NSA_HPG * d), lambda b, g, i: (b, i, g)),
        out_shape=jax.ShapeDtypeStruct((B, S, NSA_W), BF16),
        scratch_shapes=[pltpu.VMEM((S, kaug), BF16), pltpu.VMEM((S, kaug), BF16), pltpu.VMEM((n_cb, 2 * d), BF16),
                        pltpu.VMEM((2, NSA_TK, R), F32), pltpu.VMEM((2, NSA_TK, R), BF16),
                        pltpu.VMEM((n_win, NSA_TK, R), F32), pltpu.VMEM((n_win, NSA_TK, R), BF16)],
        compiler_params=_cparams(("parallel", "parallel", "arbitrary")), name="nsa",
    )(zb, zf, gate_bias, kcmp, vcmp_t, zb, zt, zb, zt, ovl, aug, caug, tail, ctail)


def _rglru_kernel(gate_ref, x_ref, halo_ref, cw_ref, cb_ref, wa_ref, wx_ref, ba_ref, bx_ref, lam_ref, o_ref, h_s,
                  au_s):
    T, W = x_ref.shape
    first = pl.program_id(1) == 0

    @pl.when(first)
    def _():
        h_s[...] = jnp.zeros_like(h_s)

    xe = jnp.concatenate([jnp.where(first, 0.0, halo_ref[...]), x_ref[...]], axis=0)
    xc = cb_ref[...] + cw_ref[3:4, :] * xe
    for k in range(1, LRU_CONV):
        xc = xc + cw_ref[3 - k:4 - k, :] * pltpu.roll(xe, k, 0)
    xc = xc[LRU_HALO:, :]
    xcb = xc.astype(BF16)
    half = W // 2

    def blockdiag(w_ref):
        return jnp.concatenate([_dot(xcb[:, :half], w_ref[0]), _dot(xcb[:, half:], w_ref[1])], axis=1)

    r = _sigmoid(blockdiag(wa_ref) + ba_ref[...])
    i = _sigmoid(blockdiag(wx_ref) + bx_ref[...])
    nl = -lam_ref[...]
    softplus = jnp.maximum(nl, 0.0) + jnp.log1p(jnp.exp(-jnp.abs(nl)))
    log_a = -LRU_C * r * softplus
    a = jnp.exp(log_a)
    one_m_a2 = -jnp.tanh(log_a) * (a * a + 1.0)
    u = jnp.where(one_m_a2 > 0.0, one_m_a2 * lax.rsqrt(one_m_a2), 0.0) * (i * xc)

    n_grp, n_slab = T // 8, W // LANES

    def phases(x, slab):
        for c in range(n_slab):
            au_s[slab, c] = x[:, c * LANES:(c + 1) * LANES]
        return [jnp.concatenate([au_s[slab, c, pl.ds(j, n_grp, stride=8), :] for c in range(n_slab)], axis=1)
                for j in range(8)]

    a_ph, u_ph = phases(a, 0), phases(u, 1)
    prod, part = [a_ph[0]], [u_ph[0]]
    for j in range(1, 8):
        part.append(a_ph[j] * part[-1] + u_ph[j])
        prod.append(a_ph[j] * prod[-1])
    grp = lax.broadcasted_iota(jnp.int32, (n_grp, W), 0)
    ag, ug = prod[7], part[7]
    sft = 1
    while sft < n_grp:
        keep = grp >= sft
        ug = ag * jnp.where(keep, pltpu.roll(ug, sft, 0), 0.0) + ug
        ag = ag * jnp.where(keep, pltpu.roll(ag, sft, 0), 1.0)
        sft *= 2
    h_prev = h_s[0:1, :]
    hg = ug + ag * h_prev
    carry_in = jnp.where(grp >= 1, pltpu.roll(hg, 1, 0), h_prev)
    for j in range(8):
        hj = part[j] + prod[j] * carry_in
        for c in range(n_slab):
            au_s[0, c, pl.ds(j, n_grp, stride=8), :] = hj[:, c * LANES:(c + 1) * LANES]
    h = jnp.concatenate([au_s[0, c] for c in range(n_slab)], axis=1)
    h_s[...] = jnp.broadcast_to(hg[n_grp - 1:n_grp, :], h_s.shape)
    o_ref[...] = (_gelu(gate_ref[...]) * h).astype(o_ref.dtype)


def rglru_block(z, conv_w, conv_b, wa, ba, wx, bx, lam, T=LRU_T):
    B, S, _ = z.shape
    W = LRU_W
    half = W // 2
    hb = T // LRU_HALO

    def bd(w):
        blocks = [jax.scipy.linalg.block_diag(*[w[h] for h in range(4 * j, 4 * j + 4)]) for j in range(2)]
        return jnp.stack(blocks).astype(BF16)

    vec = lambda a: a.reshape(1, W)
    vspec = pl.BlockSpec((1, W), lambda b, t: (0, 0))
    wspec = pl.BlockSpec((2, half, half), lambda b, t: (0, 0, 0))
    return pl.pallas_call(
        _rglru_kernel, grid=(B, S // T),
        in_specs=[pl.BlockSpec((None, T, W), lambda b, t: (b, t, 0)),
                  pl.BlockSpec((None, T, W), lambda b, t: (b, t, 1)),
                  pl.BlockSpec((None, LRU_HALO, W), lambda b, t: (b, jnp.maximum(t * hb - 1, 0), 1)),
                  pl.BlockSpec((LRU_CONV, W), lambda b, t: (0, 0)), vspec, wspec, wspec, vspec, vspec, vspec],
        out_specs=pl.BlockSpec((None, T, W), lambda b, t: (b, t, 0)),
        out_shape=jax.ShapeDtypeStruct((B, S, W), BF16),
        scratch_shapes=[pltpu.VMEM((8, W), F32), pltpu.VMEM((2, W // LANES, T, LANES), F32)],
        compiler_params=_cparams(("parallel", "arbitrary")), name="rglru",
    )(z, z, z, conv_w, vec(conv_b), bd(wa), bd(wx), vec(ba), vec(bx), vec(lam))


def _sgu_kernel(u_ref, v_ref, g_ref, b_ref, w_ref, bias_ref, o_ref):
    C, W = v_ref.shape
    dg = W // SG_GROUPS
    v = _gelu(v_ref[...])
    mu = jnp.mean(v, axis=-1, keepdims=True)
    vc = v - mu
    vn = (vc * lax.rsqrt(jnp.mean(vc * vc, axis=-1, keepdims=True) + EPS) * g_ref[...] + b_ref[...]).astype(BF16)
    row = lax.broadcasted_iota(jnp.int32, (C, C), 0)
    col = lax.broadcasted_iota(jnp.int32, (C, C), 1)
    for gi in range(SG_GROUPS):
        sl = slice(gi * dg, (gi + 1) * dg)
        wc = jnp.where(col <= row, w_ref[gi], 0.0).astype(BF16)
        mixed = _dot(wc, vn[:, sl]) + bias_ref[:, sl]
        o_ref[:, sl] = (_gelu(u_ref[:, sl]) * mixed).astype(o_ref.dtype)


def sgu_block(z, ln_g, ln_b, w, b):
    B, S, _ = z.shape
    W, C = SG_W, SG_CHUNK
    bias = jnp.repeat(b.T, W // SG_GROUPS, axis=1)
    vspec = pl.BlockSpec((1, W), lambda bb, c: (0, 0))
    return pl.pallas_call(
        _sgu_kernel, grid=(B, S // C),
        in_specs=[pl.BlockSpec((None, C, W), lambda bb, c: (bb, c, 2)),
                  pl.BlockSpec((None, C, W), lambda bb, c: (bb, c, 3)),
                  vspec, vspec,
                  pl.BlockSpec((SG_GROUPS, C, C), lambda bb, c: (0, 0, 0)),
                  pl.BlockSpec((C, W), lambda bb, c: (0, 0))],
        out_specs=pl.BlockSpec((None, C, W), lambda bb, c: (bb, c, 0)),
        out_shape=jax.ShapeDtypeStruct((B, S, W), BF16),
        compiler_params=_cparams(("parallel", "parallel")), name="sgu",
    )(z, z, ln_g.reshape(1, W), ln_b.reshape(1, W), w, bias)


def _mixer_ab(h2, B, S, pre_g, w_in, ml_gate_b, ml_norm_g, nsa_gate_b, k_pe, k_w1, k_w2, v_pe, v_w1, v_w2):
    D = h2.shape[1]
    G, d = NSA_KV_HEADS, HEAD_DIM
    offs = np.cumsum([0, ML_W, ML_W, ML_W, ML_W, 2 * ML_HEADS, NSA_W] + [G * d] * 6 + [3 * NSA_HEADS])
    mq, mk, mv, mo, mif, nq, kc, vc, ks, vs, kw, vw, ng = [w_in[:, offs[i]:offs[i + 1]] for i in range(13)]
    w_b = jnp.concatenate([mk, nq, ks, kw], axis=1).astype(BF16)
    w_t = jnp.concatenate([mq, mv, vs, vw], axis=1).T.astype(BF16)
    gpad = LANES - 2 * ML_HEADS - 3 * NSA_HEADS
    w_f = jnp.concatenate([mo, kc, vc, mif, ng, jnp.zeros((D, gpad), w_in.dtype)], axis=1).astype(BF16)
    gate_bias = jnp.concatenate([ml_gate_b, nsa_gate_b, jnp.zeros((gpad,), F32)]).reshape(1, LANES)
    zb, zf, zt = norm_proj(h2, pre_g, [w_b, w_f], [BF16, F32], wts=[w_t], batch=B)
    zb = zb.reshape(B, S, -1)
    zf = zf.reshape(B, S, -1)
    h_ml = mlstm_block(zt, zb, zf, gate_bias, ml_norm_g)
    kcmp, vcmp_t = compress_block(zf, ML_W // (G * d), ML_W // (G * d) + 1, k_pe, k_w1, k_w2, v_pe, v_w1, v_w2)
    h_nsa = nsa_block(zt, zb, zf, gate_bias, kcmp, vcmp_t, q_col_block=ML_W // (NSA_HPG * d),
                      k_col_blocks=((ML_W + NSA_W) // (G * d), (ML_W + NSA_W) // (G * d) + 1),
                      v_row_blocks=(2 * ML_W // d, 2 * ML_W // d + G))
    return h_ml.reshape(B * S, ML_W), h_nsa.reshape(B * S, NSA_W)


def _mixer_cd(h2, B, S, pre_g, w_in, conv_w, conv_b, wa, ba, wx, bx, lam, sg_g, sg_bn, sg_w, sg_b):
    (z,) = norm_proj(h2, pre_g, [w_in.astype(BF16)], [F32])
    z = z.reshape(B, S, -1)
    y_lru = rglru_block(z, conv_w, conv_b, wa, ba, wx, bx, lam)
    y_sg = sgu_block(z, sg_g, sg_bn, sg_w, sg_b)
    return y_lru.reshape(B * S, LRU_W), y_sg.reshape(B * S, SG_W)


def kernel(x, pre_mix_g, post_mix_g, pre_ffn_g, post_ffn_g, ab_w_in, ab_w_out, ml_gate_b, ml_norm_g, nsa_gate_b, cmp_k_pe, cmp_k_w1, cmp_k_w2, cmp_v_pe, cmp_v_w1, cmp_v_w2, cd_w_in, cd_w_out, lru_conv_w, lru_conv_b, lru_wa, lru_ba, lru_wx, lru_bx, lru_lambda, sg_norm_g, sg_norm_b, sg_w, sg_b, ffn_w_up, ffn_conv_w, ffn_conv_b, ffn_w_down):
    B, S, D = x.shape
    depth = pre_mix_g.shape[0]
    h2 = x.reshape(B * S, D)
    for layer in range(depth):
        if layer % 2 == 0:
            e = layer // 2
            a1, a2 = _mixer_ab(h2, B, S, pre_mix_g[layer], ab_w_in[e], ml_gate_b[e], ml_norm_g[e], nsa_gate_b[e],
                               cmp_k_pe[e], cmp_k_w1[e], cmp_k_w2[e], cmp_v_pe[e], cmp_v_w1[e], cmp_v_w2[e])
            w_out = ab_w_out[e]
        else:
            o = layer // 2
            a1, a2 = _mixer_cd(h2, B, S, pre_mix_g[layer], cd_w_in[o], lru_conv_w[o], lru_conv_b[o], lru_wa[o],
                               lru_ba[o], lru_wx[o], lru_bx[o], lru_lambda[o], sg_norm_g[o], sg_norm_b[o], sg_w[o],
                               sg_b[o])
            w_out = cd_w_out[o]
        h2 = mix_ffn_block(h2, a1, a2, S, w_out, post_mix_g[layer], pre_ffn_g[layer], ffn_w_up[layer],
                           ffn_conv_w[layer], ffn_conv_b[layer], ffn_w_down[layer], post_ffn_g[layer])
    return h2.reshape(B, S, D)
```

```python
import functools

import numpy as np
import jax
import jax.numpy as jnp
from jax import lax
from jax.experimental import pallas as pl
from jax.experimental.pallas import tpu as pltpu

F32 = jnp.float32
BF16 = jnp.bfloat16

EPS = 1e-6
HEAD_DIM = 64
ML_HEADS = 8
ML_W = 512
GATE_SOFTCAP = 15.0
NSA_HEADS = 8
NSA_KV_HEADS = 2
NSA_HPG = NSA_HEADS // NSA_KV_HEADS
NSA_W = 512
CMP_LEN = 32
CMP_STRIDE = 16
CMP_HID = 128
SEL_LEN = 64
SEL_TOPN = 16
WINDOW = 512
LRU_W = 512
LRU_C = 8.0
LRU_CONV = 4
SG_GROUPS = 8
SG_W = 512
SG_CHUNK = 128
FFN_CONV = 3

LANES = 128
VMEM_LIMIT = 56 * 1024 * 1024
NEG_BIG = -1e30
HIGHEST = lax.Precision.HIGHEST
LOG2E = 1.4426950408889634


def _bf16_terms(x, n):
    terms = []
    for _ in range(n):
        bits = int(np.array(x, np.float32).view(np.uint32))
        t = float(np.array((bits + 0x7FFF + ((bits >> 16) & 1)) & 0xFFFF0000, np.uint32).view(np.float32))
        terms.append(t)
        x -= t
    return tuple(terms)


LOG2E_TERMS = _bf16_terms(LOG2E, 3)

ML_CHUNK = 128
ML_NB = 4
NSA_TQ = 256
NSA_TK = 256
NSA_ONES = 16
ROW_TILE = 512
FFN_TM = 256
FFN_CK = 256
FFN_HALO = 16
SG_ROWS = 512
LRU_T = 256
LRU_HALO = 8


def _cparams(sem):
    return pltpu.CompilerParams(dimension_semantics=sem, vmem_limit_bytes=VMEM_LIMIT)


def _rms(x, g):
    return x * lax.rsqrt(jnp.mean(x * x, axis=-1, keepdims=True) + EPS) * g


def _gelu(x):
    return 0.5 * x * (1.0 + jnp.tanh(0.7978845608028654 * (x + 0.044715 * (x * x * x))))


def _sigmoid(x):
    return 1.0 / (1.0 + jnp.exp(-x))


def _dot(a, b):
    return jnp.dot(a, b, preferred_element_type=F32)


def _dot_nt(a, b, precision=None):
    return lax.dot_general(a, b, (((1,), (1,)), ((), ())), preferred_element_type=F32, precision=precision)


def _dot_tn(a, b):
    return lax.dot_general(a, b, (((0,), (0,)), ((), ())), preferred_element_type=F32)


def _norm_proj_kernel(h_ref, g_ref, *refs, n_row, n_t, cn):
    w_refs, wt_refs = refs[:n_row], refs[n_row:n_row + n_t]
    o_refs, ot_refs = refs[n_row + n_t:2 * n_row + n_t], refs[2 * n_row + n_t:]
    xn = _rms(h_ref[...], g_ref[...]).astype(BF16)
    for w_ref, o_ref in zip(w_refs, o_refs):
        n = w_ref.shape[1]
        for c in range(0, n, cn):
            ce = min(c + cn, n)
            o_ref[:, c:ce] = _dot(xn, w_ref[:, c:ce]).astype(o_ref.dtype)
    for wt_ref, ot_ref in zip(wt_refs, ot_refs):
        n = wt_ref.shape[0]
        for c in range(0, n, cn):
            ce = min(c + cn, n)
            ot_ref[c:ce, :] = _dot_nt(wt_ref[c:ce, :], xn).astype(ot_ref.dtype)


def norm_proj(h2, g, ws, dtypes, wts=(), batch=1, tm=ROW_TILE):
    M, D = h2.shape
    tps = M // batch // tm
    in_specs = [pl.BlockSpec((tm, D), lambda i: (i, 0)), pl.BlockSpec((1, D), lambda i: (0, 0))]
    in_specs += [pl.BlockSpec(w.shape, lambda i: (0, 0)) for w in (*ws, *wts)]
    out_specs = [pl.BlockSpec((tm, w.shape[1]), lambda i: (i, 0)) for w in ws]
    out_specs += [pl.BlockSpec((None, w.shape[0], tm), lambda i: (i // tps, 0, i % tps)) for w in wts]
    out_shape = [jax.ShapeDtypeStruct((M, w.shape[1]), dt) for w, dt in zip(ws, dtypes)]
    out_shape += [jax.ShapeDtypeStruct((batch, w.shape[0], M // batch), BF16) for w in wts]
    return pl.pallas_call(
        functools.partial(_norm_proj_kernel, n_row=len(ws), n_t=len(wts), cn=512),
        grid=(M // tm,), in_specs=in_specs, out_specs=out_specs, out_shape=out_shape,
        compiler_params=_cparams(("parallel",)), name="norm_proj",
    )(h2, g.reshape(1, D), *ws, *wts)


def _mix_ffn_kernel(h_ref, hh_ref, a1_ref, a1h_ref, a2_ref, a2h_ref, wo1_ref, wo2_ref, gmix_ref, gpre_ref, wu_ref,
                    cw_ref, cb_ref, wd_ref, gpost_ref, o_ref, xn_s, acc_s, *, tiles_per_seq, ck):
    F = wd_ref.shape[0]
    first = (pl.program_id(0) % tiles_per_seq) == 0

    def mixed(h_r, a1_r, a2_r):
        y = _dot(a1_r[...], wo1_ref[...]) + _dot(a2_r[...], wo2_ref[...])
        return h_r[...] + _rms(y, gmix_ref[...])

    x = mixed(h_ref, a1_ref, a2_ref)
    g = gpre_ref[...]
    xn_s[0:FFN_HALO, :] = jnp.where(first, 0.0, _rms(mixed(hh_ref, a1h_ref, a2h_ref), g)).astype(BF16)
    xn_s[FFN_HALO:, :] = _rms(x, g).astype(BF16)
    acc_s[...] = jnp.zeros_like(acc_s)

    def conv(u, cols):
        y = (cw_ref[2:3, cols] * u + cw_ref[1:2, cols] * pltpu.roll(u, 1, 0) + cw_ref[0:1, cols] * pltpu.roll(u, 2, 0)
             + cb_ref[:, cols])
        return y[FFN_HALO:, :]

    def cols_of(c, half):
        return slice(half * F + c * ck, half * F + (c + 1) * ck)

    def up(c):
        xn = xn_s[...]
        return _dot(xn, wu_ref[:, cols_of(c, 0)]), _dot(xn, wu_ref[:, cols_of(c, 1)])

    n_chunks = F // ck
    u = up(0)
    for c in range(n_chunks):
        u_next = up(c + 1) if c + 1 < n_chunks else None
        act = (_gelu(conv(u[0], cols_of(c, 0))) * conv(u[1], cols_of(c, 1))).astype(BF16)
        acc_s[...] += _dot(act, wd_ref[c * ck:(c + 1) * ck, :])
        u = u_next
    o_ref[...] = x + _rms(acc_s[...], gpost_ref[...])


def mix_ffn_block(h2, a1, a2, seq_len, w_out, g_mix, g_pre, w_up, conv_w, conv_b, w_down, g_post, tm=FFN_TM, ck=FFN_CK):
    M, D = h2.shape
    F = w_down.shape[0]
    K1, K2 = a1.shape[1], a2.shape[1]
    assert F % ck == 0 and seq_len % tm == 0 and tm % FFN_HALO == 0
    hb = tm // FFN_HALO
    consts = (w_out[:K1].astype(BF16), w_out[K1:].astype(BF16), g_mix.reshape(1, D), g_pre.reshape(1, D),
              w_up.astype(BF16), conv_w, conv_b.reshape(1, 2 * F), w_down.astype(BF16), g_post.reshape(1, D))

    def tile(width):
        return pl.BlockSpec((tm, width), lambda i: (i, 0))

    def halo(width):
        return pl.BlockSpec((FFN_HALO, width), lambda i: (jnp.maximum(i * hb - 1, 0), 0))

    def full(a):
        return pl.BlockSpec(a.shape, lambda i: (0,) * a.ndim)

    return pl.pallas_call(
        functools.partial(_mix_ffn_kernel, tiles_per_seq=seq_len // tm, ck=ck),
        grid=(M // tm,),
        in_specs=[tile(D), halo(D), tile(K1), halo(K1), tile(K2), halo(K2)] + [full(a) for a in consts],
        out_specs=tile(D),
        out_shape=jax.ShapeDtypeStruct((M, D), F32),
        scratch_shapes=[pltpu.VMEM((tm + FFN_HALO, D), BF16), pltpu.VMEM((tm, D), F32)],
        compiler_params=_cparams(("parallel",)), name="mix_ffn",
    )(h2, h2, a1, a1, a2, a2, *consts)


def _mlstm_kernel(qt_ref, vt_ref, k_ref, mo_ref, gt_ref, gb_ref, ng_ref, o_ref, cn_s, m_s):
    L = qt_ref.shape[2]
    d = HEAD_DIM

    @pl.when(pl.program_id(1) == 0)
    def _():
        cn_s[...] = jnp.zeros_like(cn_s)
        m_s[...] = jnp.zeros_like(m_s)

    src = lax.broadcasted_iota(jnp.int32, (L, L), 0)
    tgt = lax.broadcasted_iota(jnp.int32, (L, L), 1)
    causal = src <= tgt
    tri = (tgt <= src).astype(F32)
    for nb in range(qt_ref.shape[0]):
        _mlstm_chunk(qt_ref.at[nb], vt_ref.at[nb], k_ref.at[nb], mo_ref.at[nb], gt_ref.at[nb], gb_ref, ng_ref,
                     o_ref.at[nb], cn_s.at[nb], m_s.at[nb], causal, tri)


def _mlstm_chunk(qt_ref, vt_ref, k_ref, mo_ref, gt_ref, gb_ref, ng_ref, o_ref, cn_s, m_s, causal, tri):
    L = qt_ref.shape[1]
    d = HEAD_DIM
    gcap = GATE_SOFTCAP * jnp.tanh((gt_ref[...] + gb_ref[...]) * (1.0 / GATE_SOFTCAP))
    lf = jnp.minimum(gcap, 0.0) - jnp.log1p(jnp.exp(-jnp.abs(gcap)))
    b_col = jnp.dot(tri, lf, preferred_element_type=F32, precision=HIGHEST)
    b_row = _dot_nt(lf.T, tri, precision=HIGHEST)
    i_row = gcap.T
    c_col = b_col - pltpu.roll(gcap, ML_HEADS, 1)

    outs = []
    for h in range(ML_HEADS):
        rows = slice(h * d, (h + 1) * d)
        q_t = qt_ref[rows, :]
        v_t = vt_ref[rows, :]
        k = k_ref[:, rows] * 0.125
        br = b_row[ML_HEADS + h:ML_HEADS + h + 1, :]
        ir = i_row[h:h + 1, :]
        g = br[:, L - 1:L]
        m_prev = m_s[h:h + 1, 0:1]
        cn_prev = cn_s[h]

        dlog = jnp.where(causal, br - c_col[:, ML_HEADS + h:ML_HEADS + h + 1], -jnp.inf)
        inter = br + m_prev
        m_row = jnp.maximum(inter, jnp.max(dlog, axis=0, keepdims=True))
        s = _dot(k, q_t) * jnp.exp(dlog - m_row)
        w_inter = jnp.exp(inter - m_row)
        carry = _dot(cn_prev.astype(BF16), q_t)
        num = _dot(v_t, s.astype(BF16)) + w_inter * carry[0:d]
        den = jnp.sum(s, axis=0, keepdims=True) + w_inter * carry[d:d + 1]
        hh = num * (1.0 / jnp.maximum(jnp.abs(den), jnp.exp(-m_row)))

        wlog = g - br + ir
        m_new = jnp.maximum(g + m_prev, jnp.max(wlog, axis=-1, keepdims=True))
        w_row = jnp.exp(wlog - m_new)
        decay = jnp.exp(g + m_prev - m_new)
        vw = jnp.concatenate([v_t.astype(F32) * w_row, jnp.broadcast_to(w_row, (8, L))], axis=0).astype(BF16)
        cn_s[h] = decay * cn_prev + _dot(vw, k)
        m_s[h:h + 1, :] = jnp.broadcast_to(m_new, (1, m_s.shape[1]))

        outs.append(hh * lax.rsqrt(jnp.mean(hh * hh, axis=0, keepdims=True) + EPS) * ng_ref[rows, :])

    o_ref[...] = (jnp.concatenate(outs, axis=0).T * _sigmoid(mo_ref[...])).astype(o_ref.dtype)


def mlstm_block(zt, zb, zf, gate_bias, norm_g, L=ML_CHUNK, NB=ML_NB):
    B, S, _ = zb.shape
    W, H, d = ML_W, ML_HEADS, HEAD_DIM
    assert B % NB == 0 and S % L == 0
    ng = jnp.broadcast_to(norm_g.reshape(W, 1), (W, L))
    return pl.pallas_call(
        _mlstm_kernel, grid=(B // NB, S // L),
        in_specs=[pl.BlockSpec((NB, W, L), lambda b, c: (b, 0, c)),
                  pl.BlockSpec((NB, W, L), lambda b, c: (b, 1, c)),
                  pl.BlockSpec((NB, L, W), lambda b, c: (b, c, 0)),
                  pl.BlockSpec((NB, L, W), lambda b, c: (b, c, 0)),
                  pl.BlockSpec((NB, L, LANES), lambda b, c: (b, c, 6)),
                  pl.BlockSpec((1, LANES), lambda b, c: (0, 0)),
                  pl.BlockSpec((W, L), lambda b, c: (0, 0))],
        out_specs=pl.BlockSpec((NB, L, W), lambda b, c: (b, c, 0)),
        out_shape=jax.ShapeDtypeStruct((B, S, W), BF16),
        scratch_shapes=[pltpu.VMEM((NB, H, d + 8, d), F32), pltpu.VMEM((NB, H, LANES), F32)],
        compiler_params=_cparams(("parallel", "arbitrary")), name="mlstm",
    )(zt, zt, zb, zf, zf, gate_bias, ng)


def _compress_kernel(kc_ref, vc_ref, pek_ref, pev_ref, w1k_ref, w1v_ref, w2k_ref, w2v_ref, ok_ref, ov_ref):
    G = NSA_KV_HEADS
    nh = kc_ref.shape[0] // CMP_STRIDE

    def one(x_ref, pe_ref, w1_ref, w2_ref):
        hid = None
        for l in range(CMP_STRIDE):
            y = x_ref[pl.ds(l, nh, stride=CMP_STRIDE), :]
            ya = (y + pe_ref[l:l + 1, :]).astype(BF16)
            yb = (pltpu.roll(y, nh - 1, 0) + pe_ref[CMP_STRIDE + l:CMP_STRIDE + l + 1, :]).astype(BF16)
            t = _dot(ya, w1_ref[l]) + _dot(yb, w1_ref[CMP_STRIDE + l])
            hid = t if hid is None else hid + t
        hid = _gelu(hid).astype(BF16)
        return [_dot(hid[:, gi * CMP_HID:(gi + 1) * CMP_HID], w2_ref[...]) for gi in range(G)]

    for gi, (ko, vo) in enumerate(zip(one(kc_ref, pek_ref, w1k_ref, w2k_ref), one(vc_ref, pev_ref, w1v_ref, w2v_ref))):
        ok_ref[gi] = ko.astype(ok_ref.dtype)
        ov_ref[gi] = vo.T.astype(ov_ref.dtype)


def compress_block(zf, kc_block, vc_block, k_pe, k_w1, k_w2, v_pe, v_w1, v_w2):
    B, S, _ = zf.shape
    G, d = NSA_KV_HEADS, HEAD_DIM
    nh = S // CMP_STRIDE

    def prep(pe, w1):
        w1bd = jnp.zeros((CMP_LEN, G * d, G * CMP_HID), w1.dtype)
        for gi in range(G):
            w1bd = w1bd.at[:, gi * d:(gi + 1) * d, gi * CMP_HID:(gi + 1) * CMP_HID].set(w1)
        return jnp.tile(pe, (1, G)), w1bd.astype(BF16)

    pek, w1k = prep(k_pe, k_w1)
    pev, w1v = prep(v_pe, v_w1)

    def full(a):
        return pl.BlockSpec(a.shape, lambda b: (0,) * a.ndim)

    w2k, w2v = k_w2.astype(BF16), v_w2.astype(BF16)
    return pl.pallas_call(
        _compress_kernel, grid=(B,),
        in_specs=[pl.BlockSpec((None, S, G * d), lambda b: (b, 0, kc_block)),
                  pl.BlockSpec((None, S, G * d), lambda b: (b, 0, vc_block)),
                  full(pek), full(pev), full(w1k), full(w1v), full(w2k), full(w2v)],
        out_specs=[pl.BlockSpec((None, G, nh, d), lambda b: (b, 0, 0, 0)),
                   pl.BlockSpec((None, G, d, nh), lambda b: (b, 0, 0, 0))],
        out_shape=[jax.ShapeDtypeStruct((B, G, nh, d), BF16), jax.ShapeDtypeStruct((B, G, d, nh), BF16)],
        compiler_params=_cparams(("parallel",)), name="nsa_compress",
    )(zf, zf, pek, pev, w1k, w1v, w2k, w2v)


def _nsa_kernel(q_ref, gt_ref, gb_ref, kc_ref, vct_ref, ks_ref, vst_ref, kw_ref, vwt_ref, ovl_ref, aug_ref, caug_ref,
                tail_ref, ctail_ref, o_ref, ksa_s, kwa_s, kca_s, vsa_s, vwa_s, ss_s, ps_s, sw_s, pw_s, *, n_cmp):
    TQ, TK, d, HPG = NSA_TQ, NSA_TK, HEAD_DIM, NSA_HPG
    R = HPG * TQ
    g = pl.program_id(1)
    qi = pl.program_id(2)
    q0 = qi * TQ
    kt_d = q0 // TK
    n_cb = kc_ref.shape[0]
    n_sb = ovl_ref.shape[0]

    @pl.when(qi == 0)
    def _():
        ksa_s[:, 0:d] = jnp.where(g == 0, ks_ref[:, 0:d], ks_ref[:, d:2 * d])
        ksa_s[:, d:] = aug_ref[...]
        kwa_s[:, 0:d] = jnp.where(g == 0, kw_ref[:, 0:d], kw_ref[:, d:2 * d])
        kwa_s[:, d:] = aug_ref[...]
        kca_s[:, 0:d] = kc_ref[...]
        kca_s[:, d:] = caug_ref[...]
        vsa_s[0:d, :] = vst_ref[...]
        vsa_s[d:, :] = jnp.ones((vsa_s.shape[0] - d, vsa_s.shape[1]), BF16)
        vwa_s[0:d, :] = vwt_ref[...]
        vwa_s[d:, :] = jnp.ones((vwa_s.shape[0] - d, vwa_s.shape[1]), BF16)

    def slope(hh):
        return jnp.where(g == 0, 2.0 ** (-(hh + 1)), 2.0 ** (-(HPG + hh + 1))).astype(F32)

    def per_head(fn):
        return jnp.concatenate([fn(hh) for hh in range(HPG)], axis=1)

    def tile_heads(x):
        return jnp.concatenate([x] * HPG, axis=1)

    q_t = (q_ref[...].astype(F32) * (LOG2E * 0.125)).T

    def q_head(hh):
        return q_t[hh * d:(hh + 1) * d]

    qc_t = per_head(lambda hh: jnp.concatenate([q_head(hh), ctail_ref[...] * slope(hh)], axis=0)).astype(BF16)
    n_r = lax.broadcasted_iota(jnp.int32, (n_cb, TQ), 0)
    t_c = q0 + lax.broadcasted_iota(jnp.int32, (n_cb, TQ), 1)
    ok_c = (n_r * CMP_STRIDE + (CMP_LEN - 1) <= t_c) & (n_r < n_cmp)
    s_c = _dot(kca_s[...], qc_t) + tile_heads(jnp.where(ok_c, 0.0, NEG_BIG))
    e_c = jnp.exp2(s_c - jnp.max(s_c, axis=0, keepdims=True)) * tile_heads(jnp.where(ok_c, 1.0, 0.0))
    p_c = e_c * (1.0 / jnp.maximum(jnp.sum(e_c, axis=0, keepdims=True), 1.0))
    o_c = _dot(vct_ref[...], p_c.astype(BF16))

    p_sum = p_c[:, 0:TQ]
    for hh in range(1, HPG):
        p_sum = p_sum + p_c[:, hh * TQ:(hh + 1) * TQ]
    imp = jnp.dot(ovl_ref[...], p_sum, preferred_element_type=F32, precision=HIGHEST)
    jb = lax.broadcasted_iota(jnp.int32, (n_sb, TQ), 0)
    cur = (q0 + lax.broadcasted_iota(jnp.int32, (n_sb, TQ), 1)) // SEL_LEN
    valid = jb <= cur
    forced = (jb == 0) | (jb == cur) | (jb == cur - 1)
    score = jnp.where(forced, jnp.inf, jnp.where(valid, imp, -jnp.inf))
    n_grp = n_sb // 8
    grp = [score[8 * a:8 * a + 8] for a in range(n_grp)]
    rank = [jnp.zeros((8, TQ), jnp.int32) for _ in range(n_grp)]
    sub = lax.broadcasted_iota(jnp.int32, (8, TQ), 0)
    for j in range(n_sb):
        r = score[j:j + 1, :]
        for a in range(n_grp):
            if a > j // 8:
                ahead = (r >= grp[a]).astype(jnp.int32)
            elif a < j // 8:
                ahead = (r > grp[a]).astype(jnp.int32)
            else:
                ahead = jnp.where(sub > j % 8, (r >= grp[a]).astype(jnp.int32), (r > grp[a]).astype(jnp.int32))
            rank[a] = rank[a] + ahead
    picked = valid & (jnp.concatenate(rank, axis=0) < SEL_TOPN)

    def q_aug(block_rows):
        return per_head(lambda hh: jnp.concatenate(
            [q_head(hh), block_rows, tail_ref[...] * slope(hh)], axis=0)).astype(BF16)

    qs_t = q_aug(jnp.where(picked, 0.0, NEG_BIG))
    qw_t = q_aug(jnp.zeros((n_sb, TQ), F32))

    def scores(ka_s, qa_t, kt):
        return _dot(ka_s[pl.ds(pl.multiple_of(kt * TK, TK), TK), :], qa_t)

    def delta(kt):
        return ((q0 - kt * TK) + lax.broadcasted_iota(jnp.int32, (TK, TQ), 1)
                - lax.broadcasted_iota(jnp.int32, (TK, TQ), 0))

    def stage_scores(buf, slot, ka_s, qa_t, kt, mask_add=None):
        s = scores(ka_s, qa_t, kt)
        if mask_add is not None:
            s = s + mask_add
        buf[slot] = s
        return jnp.max(s, axis=0, keepdims=True)

    def stage_probs(sbuf, pbuf, slot, tile_max, m):
        m_new = jnp.maximum(m, tile_max)
        pbuf[slot] = jnp.exp2(sbuf[slot] - m_new).astype(BF16)
        return m_new, jnp.exp2(m - m_new)

    def stage_values(pbuf, slot, va_s, kt, alpha, acc):
        return alpha * acc + _dot(va_s[:, pl.ds(pl.multiple_of(kt * TK, TK), TK)], pbuf[slot])

    def normalised(acc):
        return acc[0:d] * (1.0 / acc[d:d + 1])

    m0, acc0 = jnp.full((1, R), NEG_BIG, F32), jnp.zeros((vsa_s.shape[0], R), F32)

    n_win = (WINDOW - 1 + TK - 1) // TK + 1
    kt_win, max_win = [], []
    for back in range(n_win):
        kt_raw = kt_d - back
        kt = jnp.maximum(kt_raw, 0)
        if back >= 1 and (back + 1) * TK <= WINDOW:
            mask_add = jnp.where(kt_raw >= 0, 0.0, NEG_BIG)
        else:
            dl = delta(kt)
            mask_add = tile_heads(jnp.where((dl >= 0) & (dl < WINDOW) & (kt_raw >= 0), 0.0, NEG_BIG))
        max_win.append(stage_scores(sw_s, back, kwa_s, qw_t, kt, mask_add))
        kt_win.append(kt)

    filler = 1 - kt_d % 2
    n_seq = kt_d + 1 + filler

    def sel_tile(i):
        return jnp.where(i == 0, kt_d, jnp.maximum(i - 1 - filler, 0))

    tmax0 = stage_scores(ss_s, 0, ksa_s, qs_t, kt_d, tile_heads(jnp.where(delta(kt_d) >= 0, 0.0, NEG_BIG)))
    m, alpha = stage_probs(ss_s, ps_s, 0, tmax0, m0)
    tmax1 = stage_scores(ss_s, 1, ksa_s, qs_t, sel_tile(1), jnp.where(filler == 1, NEG_BIG, 0.0))

    mw, accw = m0, acc0
    for back in range(n_win):
        mw, aw = stage_probs(sw_s, pw_s, back, max_win[back], mw)
        accw = stage_values(pw_s, back, vwa_s, kt_win[back], aw, accw)
    o_w = normalised(accw)

    def sel_body(k, carry):
        m, alpha, acc, tmax1 = carry
        i = 2 * k
        tmax0 = stage_scores(ss_s, 0, ksa_s, qs_t, sel_tile(i + 2))
        m, alpha1 = stage_probs(ss_s, ps_s, 1, tmax1, m)
        acc = stage_values(ps_s, 0, vsa_s, sel_tile(i), alpha, acc)
        tmax1 = stage_scores(ss_s, 1, ksa_s, qs_t, sel_tile(i + 3))
        m, alpha2 = stage_probs(ss_s, ps_s, 0, tmax0, m)
        acc = stage_values(ps_s, 1, vsa_s, sel_tile(i + 1), alpha1, acc)
        return m, alpha2, acc, tmax1

    m, alpha, acc, tmax1 = lax.fori_loop(0, (n_seq - 2) // 2, sel_body, (m, alpha, acc0, tmax1))
    m, alpha1 = stage_probs(ss_s, ps_s, 1, tmax1, m)
    acc = stage_values(ps_s, 0, vsa_s, sel_tile(n_seq - 2), alpha, acc)
    acc = stage_values(ps_s, 1, vsa_s, sel_tile(n_seq - 1), alpha1, acc)
    o_s = normalised(acc)

    gates = _sigmoid(gt_ref[...] + gb_ref[...]).T
    gsel = jnp.where(g == 0, gates[16:16 + 3 * HPG], gates[16 + 3 * HPG:16 + 6 * HPG])
    outs = []
    for hh in range(HPG):
        cols = slice(hh * TQ, (hh + 1) * TQ)
        outs.append(gsel[3 * hh:3 * hh + 1] * o_c[:, cols] + gsel[3 * hh + 1:3 * hh + 2] * o_s[:, cols]
                    + gsel[3 * hh + 2:3 * hh + 3] * o_w[:, cols])
    o_ref[...] = jnp.concatenate(outs, axis=0).T.astype(o_ref.dtype)


def nsa_block(zt, zb, zf, gate_bias, kcmp, vcmp_t, q_col_block, k_col_blocks, v_row_blocks):
    B, S, _ = zb.shape
    G, d = NSA_KV_HEADS, HEAD_DIM
    TQ = NSA_TQ
    n_cb = kcmp.shape[2]
    n_cmp = (S - CMP_LEN) // CMP_STRIDE + 1
    n_sb = S // SEL_LEN
    n_terms = len(LOG2E_TERMS)
    kaug = -(-(d + n_sb + 2 * n_terms) // LANES) * LANES
    R = NSA_HPG * TQ
    n_win = (WINDOW - 1 + NSA_TK - 1) // NSA_TK + 1
    assert S % NSA_TK == 0 and NSA_TK % TQ == 0 and TQ % SEL_LEN == 0

    cidx = np.arange(n_cb)[None, :] * CMP_STRIDE
    sstart = np.arange(n_sb)[:, None] * SEL_LEN
    ovl = ((cidx < sstart + SEL_LEN) & (cidx + CMP_LEN - 1 >= sstart) & (np.arange(n_cb)[None, :] < n_cmp))
    ovl = jnp.asarray(ovl.astype(np.float32))
    pos = np.arange(S)
    aug = np.zeros((S, kaug - d), np.float32)
    aug[pos, pos // SEL_LEN] = 1.0
    tail = np.zeros((kaug - d - n_sb, TQ), np.float32)
    caug = np.zeros((n_cb, d), np.float32)
    ctail = np.zeros((d, TQ), np.float32)
    for i, term in enumerate(LOG2E_TERMS):
        aug[:, n_sb + i] = pos // SEL_LEN
        aug[:, n_sb + n_terms + i] = pos % SEL_LEN
        tail[i] = term * SEL_LEN
        tail[n_terms + i] = term
        caug[:, i] = np.arange(n_cb)
        ctail[i] = term * CMP_STRIDE
    aug, caug, tail, ctail = jnp.asarray(aug, BF16), jnp.asarray(caug, BF16), jnp.asarray(tail), jnp.asarray(ctail)

    def kspec(j):
        return pl.BlockSpec((None, S, G * d), lambda b, g, i: (b, 0, k_col_blocks[j]))

    def vspec(j):
        return pl.BlockSpec((None, d, S), lambda b, g, i: (b, v_row_blocks[j] + g, 0))

    def const(a):
        return pl.BlockSpec(a.shape, lambda b, g, i: (0,) * a.ndim)

    return pl.pallas_call(
        functools.partial(_nsa_kernel, n_cmp=n_cmp), grid=(B, G, S // TQ),
        in_specs=[pl.BlockSpec((None, TQ, NSA_HPG * d), lambda b, g, i: (b, i, q_col_block + g)),
                  pl.BlockSpec((None, TQ, LANES), lambda b, g, i: (b, i, 6)),
                  const(gate_bias),
                  pl.BlockSpec((None, None, n_cb, d), lambda b, g, i: (b, g, 0, 0)),
                  pl.BlockSpec((None, None, d, n_cb), lambda b, g, i: (b, g, 0, 0)),
                  kspec(0), vspec(0), kspec(1), vspec(1), const(ovl), const(aug), const(caug), const(tail),
                  const(ctail)],
        out_specs=pl.BlockSpec((None, TQ, NSA_HPG * d), lambda b, g, i: (b, i, g)),
        out_shape=jax.ShapeDtypeStruct((B, S, NSA_W), BF16),
        scratch_shapes=[pltpu.VMEM((S, kaug), BF16), pltpu.VMEM((S, kaug), BF16), pltpu.VMEM((n_cb, 2 * d), BF16),
                        pltpu.VMEM((d + NSA_ONES, S), BF16), pltpu.VMEM((d + NSA_ONES, S), BF16),
                        pltpu.VMEM((2, NSA_TK, R), F32), pltpu.VMEM((2, NSA_TK, R), BF16),
                        pltpu.VMEM((n_win, NSA_TK, R), F32), pltpu.VMEM((n_win, NSA_TK, R), BF16)],
        compiler_params=_cparams(("parallel", "parallel", "arbitrary")), name="nsa",
    )(zb, zf, gate_bias, kcmp, vcmp_t, zb, zt, zb, zt, ovl, aug, caug, tail, ctail)


def _rglru_kernel(gate_ref, x_ref, halo_ref, cw_ref, cb_ref, wa_ref, wx_ref, ba_ref, bx_ref, lam_ref, o_ref, h_s,
                  au_s):
    T, W = x_ref.shape
    first = pl.program_id(1) == 0

    @pl.when(first)
    def _():
        h_s[...] = jnp.zeros_like(h_s)

    xe = jnp.concatenate([jnp.where(first, 0.0, halo_ref[...]), x_ref[...]], axis=0)
    xc = cb_ref[...] + cw_ref[3:4, :] * xe
    for k in range(1, LRU_CONV):
        xc = xc + cw_ref[3 - k:4 - k, :] * pltpu.roll(xe, k, 0)
    xc = xc[LRU_HALO:, :]
    xcb = xc.astype(BF16)
    half = W // 2

    def blockdiag(w_ref):
        return jnp.concatenate([_dot(xcb[:, :half], w_ref[0]), _dot(xcb[:, half:], w_ref[1])], axis=1)

    r = _sigmoid(blockdiag(wa_ref) + ba_ref[...])
    i = _sigmoid(blockdiag(wx_ref) + bx_ref[...])
    nl = -lam_ref[...]
    softplus = jnp.maximum(nl, 0.0) + jnp.log1p(jnp.exp(-jnp.abs(nl)))
    log_a = -LRU_C * r * softplus
    a = jnp.exp(log_a)
    one_m_a2 = -jnp.tanh(log_a) * (a * a + 1.0)
    u = jnp.where(one_m_a2 > 0.0, one_m_a2 * lax.rsqrt(one_m_a2), 0.0) * (i * xc)

    n_grp, n_slab = T // 8, W // LANES

    def phases(x, slab):
        for c in range(n_slab):
            au_s[slab, c] = x[:, c * LANES:(c + 1) * LANES]
        return [jnp.concatenate([au_s[slab, c, pl.ds(j, n_grp, stride=8), :] for c in range(n_slab)], axis=1)
                for j in range(8)]

    a_ph, u_ph = phases(a, 0), phases(u, 1)
    prod, part = [a_ph[0]], [u_ph[0]]
    for j in range(1, 8):
        part.append(a_ph[j] * part[-1] + u_ph[j])
        prod.append(a_ph[j] * prod[-1])
    grp = lax.broadcasted_iota(jnp.int32, (n_grp, W), 0)
    ag, ug = prod[7], part[7]
    sft = 1
    while sft < n_grp:
        keep = grp >= sft
        ug = ag * jnp.where(keep, pltpu.roll(ug, sft, 0), 0.0) + ug
        ag = ag * jnp.where(keep, pltpu.roll(ag, sft, 0), 1.0)
        sft *= 2
    h_prev = h_s[0:1, :]
    hg = ug + ag * h_prev
    carry_in = jnp.where(grp >= 1, pltpu.roll(hg, 1, 0), h_prev)
    for j in range(8):
        hj = part[j] + prod[j] * carry_in
        for c in range(n_slab):
            au_s[0, c, pl.ds(j, n_grp, stride=8), :] = hj[:, c * LANES:(c + 1) * LANES]
    h = jnp.concatenate([au_s[0, c] for c in range(n_slab)], axis=1)
    h_s[...] = jnp.broadcast_to(hg[n_grp - 1:n_grp, :], h_s.shape)
    o_ref[...] = (_gelu(gate_ref[...]) * h).astype(o_ref.dtype)


def rglru_block(z, conv_w, conv_b, wa, ba, wx, bx, lam, T=LRU_T):
    B, S, _ = z.shape
    W = LRU_W
    half = W // 2
    hb = T // LRU_HALO

    def bd(w):
        blocks = [jax.scipy.linalg.block_diag(*[w[h] for h in range(4 * j, 4 * j + 4)]) for j in range(2)]
        return jnp.stack(blocks).astype(BF16)

    vec = lambda a: a.reshape(1, W)
    vspec = pl.BlockSpec((1, W), lambda b, t: (0, 0))
    wspec = pl.BlockSpec((2, half, half), lambda b, t: (0, 0, 0))
    return pl.pallas_call(
        _rglru_kernel, grid=(B, S // T),
        in_specs=[pl.BlockSpec((None, T, W), lambda b, t: (b, t, 0)),
                  pl.BlockSpec((None, T, W), lambda b, t: (b, t, 1)),
                  pl.BlockSpec((None, LRU_HALO, W), lambda b, t: (b, jnp.maximum(t * hb - 1, 0), 1)),
                  pl.BlockSpec((LRU_CONV, W), lambda b, t: (0, 0)), vspec, wspec, wspec, vspec, vspec, vspec],
        out_specs=pl.BlockSpec((None, T, W), lambda b, t: (b, t, 0)),
        out_shape=jax.ShapeDtypeStruct((B, S, W), BF16),
        scratch_shapes=[pltpu.VMEM((8, W), F32), pltpu.VMEM((2, W // LANES, T, LANES), F32)],
        compiler_params=_cparams(("parallel", "arbitrary")), name="rglru",
    )(z, z, z, conv_w, vec(conv_b), bd(wa), bd(wx), vec(ba), vec(bx), vec(lam))


def _sgu_kernel(u_ref, v_ref, g_ref, b_ref, w_ref, bias_ref, o_ref):
    C, W = SG_CHUNK, v_ref.shape[1]
    dg = W // SG_GROUPS
    v = _gelu(v_ref[...])
    mu = jnp.mean(v, axis=-1, keepdims=True)
    vc = v - mu
    vn = (vc * lax.rsqrt(jnp.mean(vc * vc, axis=-1, keepdims=True) + EPS) * g_ref[...] + b_ref[...]).astype(BF16)
    row = lax.broadcasted_iota(jnp.int32, (C, C), 0)
    col = lax.broadcasted_iota(jnp.int32, (C, C), 1)
    for gi in range(SG_GROUPS):
        sl = slice(gi * dg, (gi + 1) * dg)
        wc = jnp.where(col <= row, w_ref[gi], 0.0).astype(BF16)
        for c in range(v_ref.shape[0] // C):
            rows = slice(c * C, (c + 1) * C)
            mixed = _dot(wc, vn[rows, sl]) + bias_ref[:, sl]
            o_ref[rows, sl] = (_gelu(u_ref[rows, sl]) * mixed).astype(o_ref.dtype)


def sgu_block(z, ln_g, ln_b, w, b):
    B, S, _ = z.shape
    W, C, T = SG_W, SG_CHUNK, SG_ROWS
    assert S % T == 0 and T % C == 0
    bias = jnp.repeat(b.T, W // SG_GROUPS, axis=1)
    vspec = pl.BlockSpec((1, W), lambda bb, c: (0, 0))
    return pl.pallas_call(
        _sgu_kernel, grid=(B, S // T),
        in_specs=[pl.BlockSpec((None, T, W), lambda bb, c: (bb, c, 2)),
                  pl.BlockSpec((None, T, W), lambda bb, c: (bb, c, 3)),
                  vspec, vspec,
                  pl.BlockSpec((SG_GROUPS, C, C), lambda bb, c: (0, 0, 0)),
                  pl.BlockSpec((C, W), lambda bb, c: (0, 0))],
        out_specs=pl.BlockSpec((None, T, W), lambda bb, c: (bb, c, 0)),
        out_shape=jax.ShapeDtypeStruct((B, S, W), BF16),
        compiler_params=_cparams(("parallel", "parallel")), name="sgu",
    )(z, z, ln_g.reshape(1, W), ln_b.reshape(1, W), w, bias)


def _mixer_ab(h2, B, S, pre_g, w_in, ml_gate_b, ml_norm_g, nsa_gate_b, k_pe, k_w1, k_w2, v_pe, v_w1, v_w2):
    D = h2.shape[1]
    G, d = NSA_KV_HEADS, HEAD_DIM
    offs = np.cumsum([0, ML_W, ML_W, ML_W, ML_W, 2 * ML_HEADS, NSA_W] + [G * d] * 6 + [3 * NSA_HEADS])
    mq, mk, mv, mo, mif, nq, kc, vc, ks, vs, kw, vw, ng = [w_in[:, offs[i]:offs[i + 1]] for i in range(13)]
    w_b = jnp.concatenate([mk, nq, ks, kw], axis=1).astype(BF16)
    w_t = jnp.concatenate([mq, mv, vs, vw], axis=1).T.astype(BF16)
    gpad = LANES - 2 * ML_HEADS - 3 * NSA_HEADS
    w_f = jnp.concatenate([mo, kc, vc, mif, ng, jnp.zeros((D, gpad), w_in.dtype)], axis=1).astype(BF16)
    gate_bias = jnp.concatenate([ml_gate_b, nsa_gate_b, jnp.zeros((gpad,), F32)]).reshape(1, LANES)
    zb, zf, zt = norm_proj(h2, pre_g, [w_b, w_f], [BF16, F32], wts=[w_t], batch=B)
    zb = zb.reshape(B, S, -1)
    zf = zf.reshape(B, S, -1)
    h_ml = mlstm_block(zt, zb, zf, gate_bias, ml_norm_g)
    kcmp, vcmp_t = compress_block(zf, ML_W // (G * d), ML_W // (G * d) + 1, k_pe, k_w1, k_w2, v_pe, v_w1, v_w2)
    h_nsa = nsa_block(zt, zb, zf, gate_bias, kcmp, vcmp_t, q_col_block=ML_W // (NSA_HPG * d),
                      k_col_blocks=((ML_W + NSA_W) // (G * d), (ML_W + NSA_W) // (G * d) + 1),
                      v_row_blocks=(2 * ML_W // d, 2 * ML_W // d + G))
    return h_ml.reshape(B * S, ML_W), h_nsa.reshape(B * S, NSA_W)


def _mixer_cd(h2, B, S, pre_g, w_in, conv_w, conv_b, wa, ba, wx, bx, lam, sg_g, sg_bn, sg_w, sg_b):
    (z,) = norm_proj(h2, pre_g, [w_in.astype(BF16)], [F32])
    z = z.reshape(B, S, -1)
    y_lru = rglru_block(z, conv_w, conv_b, wa, ba, wx, bx, lam)
    y_sg = sgu_block(z, sg_g, sg_bn, sg_w, sg_b)
    return y_lru.reshape(B * S, LRU_W), y_sg.reshape(B * S, SG_W)


def kernel(x, pre_mix_g, post_mix_g, pre_ffn_g, post_ffn_g, ab_w_in, ab_w_out, ml_gate_b, ml_norm_g, nsa_gate_b, cmp_k_pe, cmp_k_w1, cmp_k_w2, cmp_v_pe, cmp_v_w1, cmp_v_w2, cd_w_in, cd_w_out, lru_conv_w, lru_conv_b, lru_wa, lru_ba, lru_wx, lru_bx, lru_lambda, sg_norm_g, sg_norm_b, sg_w, sg_b, ffn_w_up, ffn_conv_w, ffn_conv_b, ffn_w_down):
    B, S, D = x.shape
    depth = pre_mix_g.shape[0]
    h2 = x.reshape(B * S, D)
    for layer in range(depth):
        if layer % 2 == 0:
            e = layer // 2
            a1, a2 = _mixer_ab(h2, B, S, pre_mix_g[layer], ab_w_in[e], ml_gate_b[e], ml_norm_g[e], nsa_gate_b[e],
                               cmp_k_pe[e], cmp_k_w1[e], cmp_k_w2[e], cmp_v_pe[e], cmp_v_w1[e], cmp_v_w2[e])
            w_out = ab_w_out[e]
        else:
            o = layer // 2
            a1, a2 = _mixer_cd(h2, B, S, pre_mix_g[layer], cd_w_in[o], lru_conv_w[o], lru_conv_b[o], lru_wa[o],
                               lru_ba[o], lru_wx[o], lru_bx[o], lru_lambda[o], sg_norm_g[o], sg_norm_b[o], sg_w[o],
                               sg_b[o])
            w_out = cd_w_out[o]
        h2 = mix_ffn_block(h2, a1, a2, S, w_out, post_mix_g[layer], pre_ffn_g[layer], ffn_w_up[layer],
                           ffn_conv_w[layer], ffn_conv_b[layer], ffn_w_down[layer], post_ffn_g[layer])
    return h2.reshape(B, S, D)
```

```python
import functools

import numpy as np
import jax
import jax.numpy as jnp
from jax import lax
from jax.experimental import pallas as pl
from jax.experimental.pallas import tpu as pltpu

F32 = jnp.float32
BF16 = jnp.bfloat16

EPS = 1e-6
HEAD_DIM = 64
ML_HEADS = 8
ML_W = 512
GATE_SOFTCAP = 15.0
NSA_HEADS = 8
NSA_KV_HEADS = 2
NSA_HPG = NSA_HEADS // NSA_KV_HEADS
NSA_W = 512
CMP_LEN = 32
CMP_STRIDE = 16
CMP_HID = 128
SEL_LEN = 64
SEL_TOPN = 16
WINDOW = 512
LRU_W = 512
LRU_C = 8.0
LRU_CONV = 4
SG_GROUPS = 8
SG_W = 512
SG_CHUNK = 128
FFN_CONV = 3

LANES = 128
VMEM_LIMIT = 56 * 1024 * 1024
NEG_BIG = -1e30
HIGHEST = lax.Precision.HIGHEST
LOG2E = 1.4426950408889634


def _bf16_terms(x, n):
    terms = []
    for _ in range(n):
        bits = int(np.array(x, np.float32).view(np.uint32))
        t = float(np.array((bits + 0x7FFF + ((bits >> 16) & 1)) & 0xFFFF0000, np.uint32).view(np.float32))
        terms.append(t)
        x -= t
    return tuple(terms)


LOG2E_TERMS = _bf16_terms(LOG2E, 3)

ML_CHUNK = 128
ML_NB = 4
NSA_TQ = 256
NSA_TK = 256
NSA_ONES = 16
ROW_TILE = 512
FFN_TM = 256
FFN_CK = 256
FFN_HALO = 16
SG_ROWS = 512
LRU_T = 256
LRU_HALO = 8


def _cparams(sem):
    return pltpu.CompilerParams(dimension_semantics=sem, vmem_limit_bytes=VMEM_LIMIT)


def _rms(x, g):
    return x * lax.rsqrt(jnp.mean(x * x, axis=-1, keepdims=True) + EPS) * g


def _gelu(x):
    return 0.5 * x * (1.0 + jnp.tanh(0.7978845608028654 * (x + 0.044715 * (x * x * x))))


def _sigmoid(x):
    return 1.0 / (1.0 + jnp.exp(-x))


def _dot(a, b):
    return jnp.dot(a, b, preferred_element_type=F32)


def _dot_nt(a, b, precision=None):
    return lax.dot_general(a, b, (((1,), (1,)), ((), ())), preferred_element_type=F32, precision=precision)


def _dot_tn(a, b):
    return lax.dot_general(a, b, (((0,), (0,)), ((), ())), preferred_element_type=F32)


def _norm_proj_kernel(h_ref, g_ref, *refs, n_row, n_t, cn):
    w_refs, wt_refs = refs[:n_row], refs[n_row:n_row + n_t]
    o_refs, ot_refs = refs[n_row + n_t:2 * n_row + n_t], refs[2 * n_row + n_t:]
    xn = _rms(h_ref[...], g_ref[...]).astype(BF16)
    for w_ref, o_ref in zip(w_refs, o_refs):
        n = w_ref.shape[1]
        for c in range(0, n, cn):
            ce = min(c + cn, n)
            o_ref[:, c:ce] = _dot(xn, w_ref[:, c:ce]).astype(o_ref.dtype)
    for wt_ref, ot_ref in zip(wt_refs, ot_refs):
        n = wt_ref.shape[0]
        for c in range(0, n, cn):
            ce = min(c + cn, n)
            ot_ref[c:ce, :] = _dot_nt(wt_ref[c:ce, :], xn).astype(ot_ref.dtype)


def norm_proj(h2, g, ws, dtypes, wts=(), batch=1, tm=ROW_TILE):
    M, D = h2.shape
    tps = M // batch // tm
    in_specs = [pl.BlockSpec((tm, D), lambda i: (i, 0)), pl.BlockSpec((1, D), lambda i: (0, 0))]
    in_specs += [pl.BlockSpec(w.shape, lambda i: (0, 0)) for w in (*ws, *wts)]
    out_specs = [pl.BlockSpec((tm, w.shape[1]), lambda i: (i, 0)) for w in ws]
    out_specs += [pl.BlockSpec((None, w.shape[0], tm), lambda i: (i // tps, 0, i % tps)) for w in wts]
    out_shape = [jax.ShapeDtypeStruct((M, w.shape[1]), dt) for w, dt in zip(ws, dtypes)]
    out_shape += [jax.ShapeDtypeStruct((batch, w.shape[0], M // batch), BF16) for w in wts]
    return pl.pallas_call(
        functools.partial(_norm_proj_kernel, n_row=len(ws), n_t=len(wts), cn=512),
        grid=(M // tm,), in_specs=in_specs, out_specs=out_specs, out_shape=out_shape,
        compiler_params=_cparams(("parallel",)), name="norm_proj",
    )(h2, g.reshape(1, D), *ws, *wts)


def _mix_ffn_kernel(h_ref, hh_ref, a1_ref, a1h_ref, a2_ref, a2h_ref, wo1_ref, wo2_ref, gmix_ref, gpre_ref, wu_ref,
                    cw_ref, cb_ref, wd_ref, gpost_ref, o_ref, xn_s, acc_s, *, tiles_per_seq, ck):
    F = wd_ref.shape[0]
    first = (pl.program_id(0) % tiles_per_seq) == 0

    def mixed(h_r, a1_r, a2_r):
        y = _dot(a1_r[...], wo1_ref[...]) + _dot(a2_r[...], wo2_ref[...])
        return h_r[...] + _rms(y, gmix_ref[...])

    x = mixed(h_ref, a1_ref, a2_ref)
    g = gpre_ref[...]
    xn_s[0:FFN_HALO, :] = jnp.where(first, 0.0, _rms(mixed(hh_ref, a1h_ref, a2h_ref), g)).astype(BF16)
    xn_s[FFN_HALO:, :] = _rms(x, g).astype(BF16)
    acc_s[...] = jnp.zeros_like(acc_s)

    def conv(u, cols):
        y = (cw_ref[2:3, cols] * u + cw_ref[1:2, cols] * pltpu.roll(u, 1, 0) + cw_ref[0:1, cols] * pltpu.roll(u, 2, 0)
             + cb_ref[:, cols])
        return y[FFN_HALO:, :]

    def cols_of(c, half):
        return slice(half * F + c * ck, half * F + (c + 1) * ck)

    def up(c):
        xn = xn_s[...]
        return _dot(xn, wu_ref[:, cols_of(c, 0)]), _dot(xn, wu_ref[:, cols_of(c, 1)])

    n_chunks = F // ck
    u = up(0)
    for c in range(n_chunks):
        u_next = up(c + 1) if c + 1 < n_chunks else None
        act = (_gelu(conv(u[0], cols_of(c, 0))) * conv(u[1], cols_of(c, 1))).astype(BF16)
        acc_s[...] += _dot(act, wd_ref[c * ck:(c + 1) * ck, :])
        u = u_next
    o_ref[...] = x + _rms(acc_s[...], gpost_ref[...])


def mix_ffn_block(h2, a1, a2, seq_len, w_out, g_mix, g_pre, w_up, conv_w, conv_b, w_down, g_post, tm=FFN_TM, ck=FFN_CK):
    M, D = h2.shape
    F = w_down.shape[0]
    K1, K2 = a1.shape[1], a2.shape[1]
    assert F % ck == 0 and seq_len % tm == 0 and tm % FFN_HALO == 0
    hb = tm // FFN_HALO
    consts = (w_out[:K1].astype(BF16), w_out[K1:].astype(BF16), g_mix.reshape(1, D), g_pre.reshape(1, D),
              w_up.astype(BF16), conv_w, conv_b.reshape(1, 2 * F), w_down.astype(BF16), g_post.reshape(1, D))

    def tile(width):
        return pl.BlockSpec((tm, width), lambda i: (i, 0))

    def halo(width):
        return pl.BlockSpec((FFN_HALO, width), lambda i: (jnp.maximum(i * hb - 1, 0), 0))

    def full(a):
        return pl.BlockSpec(a.shape, lambda i: (0,) * a.ndim)

    return pl.pallas_call(
        functools.partial(_mix_ffn_kernel, tiles_per_seq=seq_len // tm, ck=ck),
        grid=(M // tm,),
        in_specs=[tile(D), halo(D), tile(K1), halo(K1), tile(K2), halo(K2)] + [full(a) for a in consts],
        out_specs=tile(D),
        out_shape=jax.ShapeDtypeStruct((M, D), F32),
        scratch_shapes=[pltpu.VMEM((tm + FFN_HALO, D), BF16), pltpu.VMEM((tm, D), F32)],
        compiler_params=_cparams(("parallel",)), name="mix_ffn",
    )(h2, h2, a1, a1, a2, a2, *consts)


def _mlstm_kernel(qt_ref, vt_ref, k_ref, mo_ref, gt_ref, gb_ref, ng_ref, o_ref, cn_s, m_s):
    NB, _, L = qt_ref.shape
    d = HEAD_DIM

    @pl.when(pl.program_id(1) == 0)
    def _():
        cn_s[...] = jnp.zeros_like(cn_s)
        m_s[...] = jnp.zeros_like(m_s)

    src = lax.broadcasted_iota(jnp.int32, (L, L), 0)
    tgt = lax.broadcasted_iota(jnp.int32, (L, L), 1)
    causal = src <= tgt
    tri = (tgt <= src).astype(F32)

    gcap = [GATE_SOFTCAP * jnp.tanh((gt_ref[nb] + gb_ref[...]) * (1.0 / GATE_SOFTCAP)) for nb in range(NB)]
    lf = [jnp.minimum(x, 0.0) - jnp.log1p(jnp.exp(-jnp.abs(x))) for x in gcap]
    b_col = [jnp.dot(tri, x, preferred_element_type=F32, precision=HIGHEST) for x in lf]
    b_row = [_dot_nt(x.T, tri, precision=HIGHEST) for x in lf]
    i_row = [x.T for x in gcap]
    c_col = [b_col[nb] - pltpu.roll(gcap[nb], ML_HEADS, 1) for nb in range(NB)]

    chains = [(nb, h) for nb in range(NB) for h in range(ML_HEADS)]
    n = range(len(chains))

    def rows(h):
        return slice(h * d, (h + 1) * d)

    q_t = [qt_ref[nb, rows(h), :] for nb, h in chains]
    v_t = [vt_ref[nb, rows(h), :] for nb, h in chains]
    k = [k_ref[nb, :, rows(h)] * 0.125 for nb, h in chains]
    br = [b_row[nb][ML_HEADS + h:ML_HEADS + h + 1, :] for nb, h in chains]
    ir = [i_row[nb][h:h + 1, :] for nb, h in chains]
    g = [x[:, L - 1:L] for x in br]
    m_prev = [m_s[nb, h:h + 1, 0:1] for nb, h in chains]
    cn_prev = [cn_s[nb, h] for nb, h in chains]
    kq = [_dot(k[c], q_t[c]) for c in n]
    carry = [_dot(cn_prev[c].astype(BF16), q_t[c]) for c in n]
    dlog = [jnp.where(causal, br[c] - c_col[nb][:, ML_HEADS + h:ML_HEADS + h + 1], -jnp.inf)
            for c, (nb, h) in enumerate(chains)]
    inter = [br[c] + m_prev[c] for c in n]
    m_row = [jnp.maximum(inter[c], jnp.max(dlog[c], axis=0, keepdims=True)) for c in n]
    s = [kq[c] * jnp.exp(dlog[c] - m_row[c]) for c in n]
    w_inter = [jnp.exp(inter[c] - m_row[c]) for c in n]
    num = [_dot(v_t[c], s[c].astype(BF16)) + w_inter[c] * carry[c][0:d] for c in n]
    den = [jnp.sum(s[c], axis=0, keepdims=True) + w_inter[c] * carry[c][d:d + 1] for c in n]
    hh = [num[c] * (1.0 / jnp.maximum(jnp.abs(den[c]), jnp.exp(-m_row[c]))) for c in n]
    wlog = [g[c] - br[c] + ir[c] for c in n]
    m_new = [jnp.maximum(g[c] + m_prev[c], jnp.max(wlog[c], axis=-1, keepdims=True)) for c in n]
    w_row = [jnp.exp(wlog[c] - m_new[c]) for c in n]
    decay = [jnp.exp(g[c] + m_prev[c] - m_new[c]) for c in n]
    for c, (nb, h) in enumerate(chains):
        vw = jnp.concatenate([v_t[c].astype(F32) * w_row[c], jnp.broadcast_to(w_row[c], (8, L))], axis=0).astype(BF16)
        cn_s[nb, h] = decay[c] * cn_prev[c] + _dot(vw, k[c])
        m_s[nb, h:h + 1, :] = jnp.broadcast_to(m_new[c], (1, m_s.shape[2]))
    outs = [hh[c] * lax.rsqrt(jnp.mean(hh[c] * hh[c], axis=0, keepdims=True) + EPS) * ng_ref[rows(h), :]
            for c, (nb, h) in enumerate(chains)]
    for nb in range(NB):
        out_t = jnp.concatenate(outs[nb * ML_HEADS:(nb + 1) * ML_HEADS], axis=0)
        o_ref[nb] = (out_t.T * _sigmoid(mo_ref[nb])).astype(o_ref.dtype)


def mlstm_block(zt, zb, zf, gate_bias, norm_g, L=ML_CHUNK, NB=ML_NB):
    B, S, _ = zb.shape
    W, H, d = ML_W, ML_HEADS, HEAD_DIM
    assert B % NB == 0 and S % L == 0
    ng = jnp.broadcast_to(norm_g.reshape(W, 1), (W, L))
    return pl.pallas_call(
        _mlstm_kernel, grid=(B // NB, S // L),
        in_specs=[pl.BlockSpec((NB, W, L), lambda b, c: (b, 0, c)),
                  pl.BlockSpec((NB, W, L), lambda b, c: (b, 1, c)),
                  pl.BlockSpec((NB, L, W), lambda b, c: (b, c, 0)),
                  pl.BlockSpec((NB, L, W), lambda b, c: (b, c, 0)),
                  pl.BlockSpec((NB, L, LANES), lambda b, c: (b, c, 6)),
                  pl.BlockSpec((1, LANES), lambda b, c: (0, 0)),
                  pl.BlockSpec((W, L), lambda b, c: (0, 0))],
        out_specs=pl.BlockSpec((NB, L, W), lambda b, c: (b, c, 0)),
        out_shape=jax.ShapeDtypeStruct((B, S, W), BF16),
        scratch_shapes=[pltpu.VMEM((NB, H, d + 8, d), F32), pltpu.VMEM((NB, H, LANES), F32)],
        compiler_params=_cparams(("parallel", "arbitrary")), name="mlstm",
    )(zt, zt, zb, zf, zf, gate_bias, ng)


def _compress_kernel(kc_ref, vc_ref, pek_ref, pev_ref, w1k_ref, w1v_ref, w2k_ref, w2v_ref, ok_ref, ov_ref):
    G = NSA_KV_HEADS
    nh = kc_ref.shape[0] // CMP_STRIDE

    def one(x_ref, pe_ref, w1_ref, w2_ref):
        hid = None
        for l in range(CMP_STRIDE):
            y = x_ref[pl.ds(l, nh, stride=CMP_STRIDE), :]
            ya = (y + pe_ref[l:l + 1, :]).astype(BF16)
            yb = (pltpu.roll(y, nh - 1, 0) + pe_ref[CMP_STRIDE + l:CMP_STRIDE + l + 1, :]).astype(BF16)
            t = _dot(ya, w1_ref[l]) + _dot(yb, w1_ref[CMP_STRIDE + l])
            hid = t if hid is None else hid + t
        hid = _gelu(hid).astype(BF16)
        return [_dot(hid[:, gi * CMP_HID:(gi + 1) * CMP_HID], w2_ref[...]) for gi in range(G)]

    for gi, (ko, vo) in enumerate(zip(one(kc_ref, pek_ref, w1k_ref, w2k_ref), one(vc_ref, pev_ref, w1v_ref, w2v_ref))):
        ok_ref[gi] = ko.astype(ok_ref.dtype)
        ov_ref[gi] = vo.T.astype(ov_ref.dtype)


def compress_block(zf, kc_block, vc_block, k_pe, k_w1, k_w2, v_pe, v_w1, v_w2):
    B, S, _ = zf.shape
    G, d = NSA_KV_HEADS, HEAD_DIM
    nh = S // CMP_STRIDE

    def prep(pe, w1):
        w1bd = jnp.zeros((CMP_LEN, G * d, G * CMP_HID), w1.dtype)
        for gi in range(G):
            w1bd = w1bd.at[:, gi * d:(gi + 1) * d, gi * CMP_HID:(gi + 1) * CMP_HID].set(w1)
        return jnp.tile(pe, (1, G)), w1bd.astype(BF16)

    pek, w1k = prep(k_pe, k_w1)
    pev, w1v = prep(v_pe, v_w1)

    def full(a):
        return pl.BlockSpec(a.shape, lambda b: (0,) * a.ndim)

    w2k, w2v = k_w2.astype(BF16), v_w2.astype(BF16)
    return pl.pallas_call(
        _compress_kernel, grid=(B,),
        in_specs=[pl.BlockSpec((None, S, G * d), lambda b: (b, 0, kc_block)),
                  pl.BlockSpec((None, S, G * d), lambda b: (b, 0, vc_block)),
                  full(pek), full(pev), full(w1k), full(w1v), full(w2k), full(w2v)],
        out_specs=[pl.BlockSpec((None, G, nh, d), lambda b: (b, 0, 0, 0)),
                   pl.BlockSpec((None, G, d, nh), lambda b: (b, 0, 0, 0))],
        out_shape=[jax.ShapeDtypeStruct((B, G, nh, d), BF16), jax.ShapeDtypeStruct((B, G, d, nh), BF16)],
        compiler_params=_cparams(("parallel",)), name="nsa_compress",
    )(zf, zf, pek, pev, w1k, w1v, w2k, w2v)


def _nsa_kernel(q_ref, gt_ref, gb_ref, kc_ref, vct_ref, ks_ref, vst_ref, kw_ref, vwt_ref, ovl_ref, aug_ref, caug_ref,
                tail_ref, ctail_ref, o_ref, ksa_s, kwa_s, kca_s, vsa_s, vwa_s, ss_s, ps_s, sw_s, pw_s, *, n_cmp):
    TQ, TK, d, HPG = NSA_TQ, NSA_TK, HEAD_DIM, NSA_HPG
    R = HPG * TQ
    g = pl.program_id(1)
    qi = pl.program_id(2)
    q0 = qi * TQ
    kt_d = q0 // TK
    n_cb = kc_ref.shape[0]
    n_sb = ovl_ref.shape[0]

    @pl.when(qi == 0)
    def _():
        ksa_s[:, 0:d] = jnp.where(g == 0, ks_ref[:, 0:d], ks_ref[:, d:2 * d])
        ksa_s[:, d:] = aug_ref[...]
        kwa_s[:, 0:d] = jnp.where(g == 0, kw_ref[:, 0:d], kw_ref[:, d:2 * d])
        kwa_s[:, d:] = aug_ref[...]
        kca_s[:, 0:d] = kc_ref[...]
        kca_s[:, d:] = caug_ref[...]
        vsa_s[0:d, :] = vst_ref[...]
        vsa_s[d:, :] = jnp.ones((vsa_s.shape[0] - d, vsa_s.shape[1]), BF16)
        vwa_s[0:d, :] = vwt_ref[...]
        vwa_s[d:, :] = jnp.ones((vwa_s.shape[0] - d, vwa_s.shape[1]), BF16)

    def slope(hh):
        return jnp.where(g == 0, 2.0 ** (-(hh + 1)), 2.0 ** (-(HPG + hh + 1))).astype(F32)

    def per_head(fn):
        return jnp.concatenate([fn(hh) for hh in range(HPG)], axis=1)

    def tile_heads(x):
        return jnp.concatenate([x] * HPG, axis=1)

    q_t = (q_ref[...].astype(F32) * (LOG2E * 0.125)).T

    def q_head(hh):
        return q_t[hh * d:(hh + 1) * d]

    qc_t = per_head(lambda hh: jnp.concatenate([q_head(hh), ctail_ref[...] * slope(hh)], axis=0)).astype(BF16)
    n_r = lax.broadcasted_iota(jnp.int32, (n_cb, TQ), 0)
    t_c = q0 + lax.broadcasted_iota(jnp.int32, (n_cb, TQ), 1)
    ok_c = (n_r * CMP_STRIDE + (CMP_LEN - 1) <= t_c) & (n_r < n_cmp)
    s_c = _dot(kca_s[...], qc_t) + tile_heads(jnp.where(ok_c, 0.0, NEG_BIG))
    e_c = jnp.exp2(s_c - jnp.max(s_c, axis=0, keepdims=True)) * tile_heads(jnp.where(ok_c, 1.0, 0.0))
    p_c = e_c * (1.0 / jnp.maximum(jnp.sum(e_c, axis=0, keepdims=True), 1.0))
    o_c = _dot(vct_ref[...], p_c.astype(BF16))

    p_sum = p_c[:, 0:TQ]
    for hh in range(1, HPG):
        p_sum = p_sum + p_c[:, hh * TQ:(hh + 1) * TQ]
    imp = jnp.dot(ovl_ref[...], p_sum, preferred_element_type=F32, precision=HIGHEST)
    jb = lax.broadcasted_iota(jnp.int32, (n_sb, TQ), 0)
    cur = (q0 + lax.broadcasted_iota(jnp.int32, (n_sb, TQ), 1)) // SEL_LEN
    valid = jb <= cur
    forced = (jb == 0) | (jb == cur) | (jb == cur - 1)
    score = jnp.where(forced, jnp.inf, jnp.where(valid, imp, -jnp.inf))
    n_grp = n_sb // 8
    grp = [score[8 * a:8 * a + 8] for a in range(n_grp)]
    rank = [jnp.zeros((8, TQ), jnp.int32) for _ in range(n_grp)]
    sub = lax.broadcasted_iota(jnp.int32, (8, TQ), 0)
    for j in range(n_sb):
        r = score[j:j + 1, :]
        for a in range(n_grp):
            if a > j // 8:
                ahead = (r >= grp[a]).astype(jnp.int32)
            elif a < j // 8:
                ahead = (r > grp[a]).astype(jnp.int32)
            else:
                ahead = jnp.where(sub > j % 8, (r >= grp[a]).astype(jnp.int32), (r > grp[a]).astype(jnp.int32))
            rank[a] = rank[a] + ahead
    picked = valid & (jnp.concatenate(rank, axis=0) < SEL_TOPN)

    def q_aug(block_rows):
        return per_head(lambda hh: jnp.concatenate(
            [q_head(hh), block_rows, tail_ref[...] * slope(hh)], axis=0)).astype(BF16)

    qs_t = q_aug(jnp.where(picked, 0.0, NEG_BIG))
    qw_t = q_aug(jnp.zeros((n_sb, TQ), F32))

    def scores(ka_s, qa_t, kt):
        return _dot(ka_s[pl.ds(pl.multiple_of(kt * TK, TK), TK), :], qa_t)

    def delta(kt):
        return ((q0 - kt * TK) + lax.broadcasted_iota(jnp.int32, (TK, TQ), 1)
                - lax.broadcasted_iota(jnp.int32, (TK, TQ), 0))

    def stage_scores(buf, slot, ka_s, qa_t, kt, mask_add=None):
        s = scores(ka_s, qa_t, kt)
        if mask_add is not None:
            s = s + mask_add
        buf[slot] = s
        return jnp.max(s, axis=0, keepdims=True)

    def stage_probs(sbuf, pbuf, slot, tile_max, m):
        m_new = jnp.maximum(m, tile_max)
        pbuf[slot] = jnp.exp2(sbuf[slot] - m_new).astype(BF16)
        return m_new, jnp.exp2(m - m_new)

    def stage_values(pbuf, slot, va_s, kt, alpha, acc):
        return alpha * acc + _dot(va_s[:, pl.ds(pl.multiple_of(kt * TK, TK), TK)], pbuf[slot])

    def normalised(acc):
        return acc[0:d] * (1.0 / acc[d:d + 1])

    m0, acc0 = jnp.full((1, R), NEG_BIG, F32), jnp.zeros((vsa_s.shape[0], R), F32)

    n_win = (WINDOW - 1 + TK - 1) // TK + 1
    kt_win, max_win = [], []
    for back in range(n_win):
        kt_raw = kt_d - back
        kt = jnp.maximum(kt_raw, 0)
        if back >= 1 and (back + 1) * TK <= WINDOW:
            mask_add = jnp.where(kt_raw >= 0, 0.0, NEG_BIG)
        else:
            dl = delta(kt)
            mask_add = tile_heads(jnp.where((dl >= 0) & (dl < WINDOW) & (kt_raw >= 0), 0.0, NEG_BIG))
        max_win.append(stage_scores(sw_s, back, kwa_s, qw_t, kt, mask_add))
        kt_win.append(kt)

    filler = 1 - kt_d % 2
    n_seq = kt_d + 1 + filler

    def sel_tile(i):
        return jnp.where(i == 0, kt_d, jnp.maximum(i - 1 - filler, 0))

    tmax0 = stage_scores(ss_s, 0, ksa_s, qs_t, kt_d, tile_heads(jnp.where(delta(kt_d) >= 0, 0.0, NEG_BIG)))
    m, alpha = stage_probs(ss_s, ps_s, 0, tmax0, m0)
    tmax1 = stage_scores(ss_s, 1, ksa_s, qs_t, sel_tile(1), jnp.where(filler == 1, NEG_BIG, 0.0))

    mw, accw = m0, acc0
    for back in range(n_win):
        mw, aw = stage_probs(sw_s, pw_s, back, max_win[back], mw)
        accw = stage_values(pw_s, back, vwa_s, kt_win[back], aw, accw)
    o_w = normalised(accw)

    def sel_body(k, carry):
        m, alpha, acc, tmax1 = carry
        i = 2 * k
        tmax0 = stage_scores(ss_s, 0, ksa_s, qs_t, sel_tile(i + 2))
        m, alpha1 = stage_probs(ss_s, ps_s, 1, tmax1, m)
        tmax1 = stage_scores(ss_s, 1, ksa_s, qs_t, sel_tile(i + 3))
        acc = stage_values(ps_s, 0, vsa_s, sel_tile(i), alpha, acc)
        m, alpha2 = stage_probs(ss_s, ps_s, 0, tmax0, m)
        acc = stage_values(ps_s, 1, vsa_s, sel_tile(i + 1), alpha1, acc)
        return m, alpha2, acc, tmax1

    m, alpha, acc, tmax1 = lax.fori_loop(0, (n_seq - 2) // 2, sel_body, (m, alpha, acc0, tmax1))
    m, alpha1 = stage_probs(ss_s, ps_s, 1, tmax1, m)
    acc = stage_values(ps_s, 0, vsa_s, sel_tile(n_seq - 2), alpha, acc)
    acc = stage_values(ps_s, 1, vsa_s, sel_tile(n_seq - 1), alpha1, acc)
    o_s = normalised(acc)

    gates = _sigmoid(gt_ref[...] + gb_ref[...]).T
    gsel = jnp.where(g == 0, gates[16:16 + 3 * HPG], gates[16 + 3 * HPG:16 + 6 * HPG])
    outs = []
    for hh in range(HPG):
        cols = slice(hh * TQ, (hh + 1) * TQ)
        outs.append(gsel[3 * hh:3 * hh + 1] * o_c[:, cols] + gsel[3 * hh + 1:3 * hh + 2] * o_s[:, cols]
                    + gsel[3 * hh + 2:3 * hh + 3] * o_w[:, cols])
    o_ref[...] = jnp.concatenate(outs, axis=0).T.astype(o_ref.dtype)


def nsa_block(zt, zb, zf, gate_bias, kcmp, vcmp_t, q_col_block, k_col_blocks, v_row_blocks):
    B, S, _ = zb.shape
    G, d = NSA_KV_HEADS, HEAD_DIM
    TQ = NSA_TQ
    n_cb = kcmp.shape[2]
    n_cmp = (S - CMP_LEN) // CMP_STRIDE + 1
    n_sb = S // SEL_LEN
    n_terms = len(LOG2E_TERMS)
    kaug = -(-(d + n_sb + 2 * n_terms) // LANES) * LANES
    R = NSA_HPG * TQ
    n_win = (WINDOW - 1 + NSA_TK - 1) // NSA_TK + 1
    assert S % NSA_TK == 0 and NSA_TK % TQ == 0 and TQ % SEL_LEN == 0

    cidx = np.arange(n_cb)[None, :] * CMP_STRIDE
    sstart = np.arange(n_sb)[:, None] * SEL_LEN
    ovl = ((cidx < sstart + SEL_LEN) & (cidx + CMP_LEN - 1 >= sstart) & (np.arange(n_cb)[None, :] < n_cmp))
    ovl = jnp.asarray(ovl.astype(np.float32))
    pos = np.arange(S)
    aug = np.zeros((S, kaug - d), np.float32)
    aug[pos, pos // SEL_LEN] = 1.0
    tail = np.zeros((kaug - d - n_sb, TQ), np.float32)
    caug = np.zeros((n_cb, d), np.float32)
    ctail = np.zeros((d, TQ), np.float32)
    for i, term in enumerate(LOG2E_TERMS):
        aug[:, n_sb + i] = pos // SEL_LEN
        aug[:, n_sb + n_terms + i] = pos % SEL_LEN
        tail[i] = term * SEL_LEN
        tail[n_terms + i] = term
        caug[:, i] = np.arange(n_cb)
        ctail[i] = term * CMP_STRIDE
    aug, caug, tail, ctail = jnp.asarray(aug, BF16), jnp.asarray(caug, BF16), jnp.asarray(tail), jnp.asarray(ctail)

    def kspec(j):
        return pl.BlockSpec((None, S, G * d), lambda b, g, i: (b, 0, k_col_blocks[j]))

    def vspec(j):
        return pl.BlockSpec((None, d, S), lambda b, g, i: (b, v_row_blocks[j] + g, 0))

    def const(a):
        return pl.BlockSpec(a.shape, lambda b, g, i: (0,) * a.ndim)

    return pl.pallas_call(
        functools.partial(_nsa_kernel, n_cmp=n_cmp), grid=(B, G, S // TQ),
        in_specs=[pl.BlockSpec((None, TQ, NSA_HPG * d), lambda b, g, i: (b, i, q_col_block + g)),
                  pl.BlockSpec((None, TQ, LANES), lambda b, g, i: (b, i, 6)),
                  const(gate_bias),
                  pl.BlockSpec((None, None, n_cb, d), lambda b, g, i: (b, g, 0, 0)),
                  pl.BlockSpec((None, None, d, n_cb), lambda b, g, i: (b, g, 0, 0)),
                  kspec(0), vspec(0), kspec(1), vspec(1), const(ovl), const(aug), const(caug), const(tail),
                  const(ctail)],
        out_specs=pl.BlockSpec((None, TQ, NSA_HPG * d), lambda b, g, i: (b, i, g)),
        out_shape=jax.ShapeDtypeStruct((B, S, NSA_W), BF16),
        scratch_shapes=[pltpu.VMEM((S, kaug), BF16), pltpu.VMEM((S, kaug), BF16), pltpu.VMEM((n_cb, 2 * d), BF16),
                        pltpu.VMEM((d + NSA_ONES, S), BF16), pltpu.VMEM((d + NSA_ONES, S), BF16),
                        pltpu.VMEM((2, NSA_TK, R), F32), pltpu.VMEM((2, NSA_TK, R), BF16),
                        pltpu.VMEM((n_win, NSA_TK, R), F32), pltpu.VMEM((n_win, NSA_TK, R), BF16)],
        compiler_params=_cparams(("parallel", "parallel", "arbitrary")), name="nsa",
    )(zb, zf, gate_bias, kcmp, vcmp_t, zb, zt, zb, zt, ovl, aug, caug, tail, ctail)


def _rglru_kernel(gate_ref, x_ref, halo_ref, cw_ref, cb_ref, wa_ref, wx_ref, ba_ref, bx_ref, lam_ref, o_ref, h_s,
                  au_s):
    T, W = x_ref.shape
    first = pl.program_id(1) == 0

    @pl.when(first)
    def _():
        h_s[...] = jnp.zeros_like(h_s)

    xe = jnp.concatenate([jnp.where(first, 0.0, halo_ref[...]), x_ref[...]], axis=0)
    xc = cb_ref[...] + cw_ref[3:4, :] * xe
    for k in range(1, LRU_CONV):
        xc = xc + cw_ref[3 - k:4 - k, :] * pltpu.roll(xe, k, 0)
    xc = xc[LRU_HALO:, :]
    xcb = xc.astype(BF16)
    half = W // 2

    def blockdiag(w_ref):
        return jnp.concatenate([_dot(xcb[:, :half], w_ref[0]), _dot(xcb[:, half:], w_ref[1])], axis=1)

    r = _sigmoid(blockdiag(wa_ref) + ba_ref[...])
    i = _sigmoid(blockdiag(wx_ref) + bx_ref[...])
    nl = -lam_ref[...]
    softplus = jnp.maximum(nl, 0.0) + jnp.log1p(jnp.exp(-jnp.abs(nl)))
    log_a = -LRU_C * r * softplus
    a = jnp.exp(log_a)
    one_m_a2 = -jnp.tanh(log_a) * (a * a + 1.0)
    u = jnp.where(one_m_a2 > 0.0, one_m_a2 * lax.rsqrt(one_m_a2), 0.0) * (i * xc)

    n_grp, n_slab = T // 8, W // LANES

    def phases(x, slab):
        for c in range(n_slab):
            au_s[slab, c] = x[:, c * LANES:(c + 1) * LANES]
        return [jnp.concatenate([au_s[slab, c, pl.ds(j, n_grp, stride=8), :] for c in range(n_slab)], axis=1)
                for j in range(8)]

    a_ph, u_ph = phases(a, 0), phases(u, 1)
    prod, part = [a_ph[0]], [u_ph[0]]
    for j in range(1, 8):
        part.append(a_ph[j] * part[-1] + u_ph[j])
        prod.append(a_ph[j] * prod[-1])
    grp = lax.broadcasted_iota(jnp.int32, (n_grp, W), 0)
    ag, ug = prod[7], part[7]
    sft = 1
    while sft < n_grp:
        keep = grp >= sft
        ug = ag * jnp.where(keep, pltpu.roll(ug, sft, 0), 0.0) + ug
        ag = ag * jnp.where(keep, pltpu.roll(ag, sft, 0), 1.0)
        sft *= 2
    h_prev = h_s[0:1, :]
    hg = ug + ag * h_prev
    carry_in = jnp.where(grp >= 1, pltpu.roll(hg, 1, 0), h_prev)
    for j in range(8):
        hj = part[j] + prod[j] * carry_in
        for c in range(n_slab):
            au_s[0, c, pl.ds(j, n_grp, stride=8), :] = hj[:, c * LANES:(c + 1) * LANES]
    h = jnp.concatenate([au_s[0, c] for c in range(n_slab)], axis=1)
    h_s[...] = jnp.broadcast_to(hg[n_grp - 1:n_grp, :], h_s.shape)
    o_ref[...] = (_gelu(gate_ref[...]) * h).astype(o_ref.dtype)


def rglru_block(z, conv_w, conv_b, wa, ba, wx, bx, lam, T=LRU_T):
    B, S, _ = z.shape
    W = LRU_W
    half = W // 2
    hb = T // LRU_HALO

    def bd(w):
        blocks = [jax.scipy.linalg.block_diag(*[w[h] for h in range(4 * j, 4 * j + 4)]) for j in range(2)]
        return jnp.stack(blocks).astype(BF16)

    vec = lambda a: a.reshape(1, W)
    vspec = pl.BlockSpec((1, W), lambda b, t: (0, 0))
    wspec = pl.BlockSpec((2, half, half), lambda b, t: (0, 0, 0))
    return pl.pallas_call(
        _rglru_kernel, grid=(B, S // T),
        in_specs=[pl.BlockSpec((None, T, W), lambda b, t: (b, t, 0)),
                  pl.BlockSpec((None, T, W), lambda b, t: (b, t, 1)),
                  pl.BlockSpec((None, LRU_HALO, W), lambda b, t: (b, jnp.maximum(t * hb - 1, 0), 1)),
                  pl.BlockSpec((LRU_CONV, W), lambda b, t: (0, 0)), vspec, wspec, wspec, vspec, vspec, vspec],
        out_specs=pl.BlockSpec((None, T, W), lambda b, t: (b, t, 0)),
        out_shape=jax.ShapeDtypeStruct((B, S, W), BF16),
        scratch_shapes=[pltpu.VMEM((8, W), F32), pltpu.VMEM((2, W // LANES, T, LANES), F32)],
        compiler_params=_cparams(("parallel", "arbitrary")), name="rglru",
    )(z, z, z, conv_w, vec(conv_b), bd(wa), bd(wx), vec(ba), vec(bx), vec(lam))


def _sgu_kernel(u_ref, v_ref, g_ref, b_ref, w_ref, bias_ref, o_ref):
    C, W = SG_CHUNK, v_ref.shape[1]
    dg = W // SG_GROUPS
    v = _gelu(v_ref[...])
    mu = jnp.mean(v, axis=-1, keepdims=True)
    vc = v - mu
    vn = (vc * lax.rsqrt(jnp.mean(vc * vc, axis=-1, keepdims=True) + EPS) * g_ref[...] + b_ref[...]).astype(BF16)
    row = lax.broadcasted_iota(jnp.int32, (C, C), 0)
    col = lax.broadcasted_iota(jnp.int32, (C, C), 1)
    for gi in range(SG_GROUPS):
        sl = slice(gi * dg, (gi + 1) * dg)
        wc = jnp.where(col <= row, w_ref[gi], 0.0).astype(BF16)
        for c in range(v_ref.shape[0] // C):
            rows = slice(c * C, (c + 1) * C)
            mixed = _dot(wc, vn[rows, sl]) + bias_ref[:, sl]
            o_ref[rows, sl] = (_gelu(u_ref[rows, sl]) * mixed).astype(o_ref.dtype)


def sgu_block(z, ln_g, ln_b, w, b):
    B, S, _ = z.shape
    W, C, T = SG_W, SG_CHUNK, SG_ROWS
    assert S % T == 0 and T % C == 0
    bias = jnp.repeat(b.T, W // SG_GROUPS, axis=1)
    vspec = pl.BlockSpec((1, W), lambda bb, c: (0, 0))
    return pl.pallas_call(
        _sgu_kernel, grid=(B, S // T),
        in_specs=[pl.BlockSpec((None, T, W), lambda bb, c: (bb, c, 2)),
                  pl.BlockSpec((None, T, W), lambda bb, c: (bb, c, 3)),
                  vspec, vspec,
                  pl.BlockSpec((SG_GROUPS, C, C), lambda bb, c: (0, 0, 0)),
                  pl.BlockSpec((C, W), lambda bb, c: (0, 0))],
        out_specs=pl.BlockSpec((None, T, W), lambda bb, c: (bb, c, 0)),
        out_shape=jax.ShapeDtypeStruct((B, S, W), BF16),
        compiler_params=_cparams(("parallel", "parallel")), name="sgu",
    )(z, z, ln_g.reshape(1, W), ln_b.reshape(1, W), w, bias)


def _mixer_ab(h2, B, S, pre_g, w_in, ml_gate_b, ml_norm_g, nsa_gate_b, k_pe, k_w1, k_w2, v_pe, v_w1, v_w2):
    D = h2.shape[1]
    G, d = NSA_KV_HEADS, HEAD_DIM
    offs = np.cumsum([0, ML_W, ML_W, ML_W, ML_W, 2 * ML_HEADS, NSA_W] + [G * d] * 6 + [3 * NSA_HEADS])
    mq, mk, mv, mo, mif, nq, kc, vc, ks, vs, kw, vw, ng = [w_in[:, offs[i]:offs[i + 1]] for i in range(13)]
    w_b = jnp.concatenate([mk, nq, ks, kw], axis=1).astype(BF16)
    w_t = jnp.concatenate([mq, mv, vs, vw], axis=1).T.astype(BF16)
    gpad = LANES - 2 * ML_HEADS - 3 * NSA_HEADS
    w_f = jnp.concatenate([mo, kc, vc, mif, ng, jnp.zeros((D, gpad), w_in.dtype)], axis=1).astype(BF16)
    gate_bias = jnp.concatenate([ml_gate_b, nsa_gate_b, jnp.zeros((gpad,), F32)]).reshape(1, LANES)
    zb, zf, zt = norm_proj(h2, pre_g, [w_b, w_f], [BF16, F32], wts=[w_t], batch=B)
    zb = zb.reshape(B, S, -1)
    zf = zf.reshape(B, S, -1)
    h_ml = mlstm_block(zt, zb, zf, gate_bias, ml_norm_g)
    kcmp, vcmp_t = compress_block(zf, ML_W // (G * d), ML_W // (G * d) + 1, k_pe, k_w1, k_w2, v_pe, v_w1, v_w2)
    h_nsa = nsa_block(zt, zb, zf, gate_bias, kcmp, vcmp_t, q_col_block=ML_W // (NSA_HPG * d),
                      k_col_blocks=((ML_W + NSA_W) // (G * d), (ML_W + NSA_W) // (G * d) + 1),
                      v_row_blocks=(2 * ML_W // d, 2 * ML_W // d + G))
    return h_ml.reshape(B * S, ML_W), h_nsa.reshape(B * S, NSA_W)


def _mixer_cd(h2, B, S, pre_g, w_in, conv_w, conv_b, wa, ba, wx, bx, lam, sg_g, sg_bn, sg_w, sg_b):
    (z,) = norm_proj(h2, pre_g, [w_in.astype(BF16)], [F32])
    z = z.reshape(B, S, -1)
    y_lru = rglru_block(z, conv_w, conv_b, wa, ba, wx, bx, lam)
    y_sg = sgu_block(z, sg_g, sg_bn, sg_w, sg_b)
    return y_lru.reshape(B * S, LRU_W), y_sg.reshape(B * S, SG_W)


def kernel(x, pre_mix_g, post_mix_g, pre_ffn_g, post_ffn_g, ab_w_in, ab_w_out, ml_gate_b, ml_norm_g, nsa_gate_b, cmp_k_pe, cmp_k_w1, cmp_k_w2, cmp_v_pe, cmp_v_w1, cmp_v_w2, cd_w_in, cd_w_out, lru_conv_w, lru_conv_b, lru_wa, lru_ba, lru_wx, lru_bx, lru_lambda, sg_norm_g, sg_norm_b, sg_w, sg_b, ffn_w_up, ffn_conv_w, ffn_conv_b, ffn_w_down):
    B, S, D = x.shape
    depth = pre_mix_g.shape[0]
    h2 = x.reshape(B * S, D)
    for layer in range(depth):
        if layer % 2 == 0:
            e = layer // 2
            a1, a2 = _mixer_ab(h2, B, S, pre_mix_g[layer], ab_w_in[e], ml_gate_b[e], ml_norm_g[e], nsa_gate_b[e],
                               cmp_k_pe[e], cmp_k_w1[e], cmp_k_w2[e], cmp_v_pe[e], cmp_v_w1[e], cmp_v_w2[e])
            w_out = ab_w_out[e]
        else:
            o = layer // 2
            a1, a2 = _mixer_cd(h2, B, S, pre_mix_g[layer], cd_w_in[o], lru_conv_w[o], lru_conv_b[o], lru_wa[o],
                               lru_ba[o], lru_wx[o], lru_bx[o], lru_lambda[o], sg_norm_g[o], sg_norm_b[o], sg_w[o],
                               sg_b[o])
            w_out = cd_w_out[o]
        h2 = mix_ffn_block(h2, a1, a2, S, w_out, post_mix_g[layer], pre_ffn_g[layer], ffn_w_up[layer],
                           ffn_conv_w[layer], ffn_conv_b[layer], ffn_w_down[layer], post_ffn_g[layer])
    return h2.reshape(B, S, D)
```

```python
import functools

import numpy as np
import jax
import jax.numpy as jnp
from jax import lax
from jax.experimental import pallas as pl
from jax.experimental.pallas import tpu as pltpu

F32 = jnp.float32
BF16 = jnp.bfloat16

EPS = 1e-6
HEAD_DIM = 64
ML_HEADS = 8
ML_W = 512
GATE_SOFTCAP = 15.0
NSA_HEADS = 8
NSA_KV_HEADS = 2
NSA_HPG = NSA_HEADS // NSA_KV_HEADS
NSA_W = 512
CMP_LEN = 32
CMP_STRIDE = 16
CMP_HID = 128
SEL_LEN = 64
SEL_TOPN = 16
WINDOW = 512
LRU_W = 512
LRU_C = 8.0
LRU_CONV = 4
SG_GROUPS = 8
SG_W = 512
SG_CHUNK = 128
FFN_CONV = 3

LANES = 128
VMEM_LIMIT = 56 * 1024 * 1024
NEG_BIG = -1e30
HIGHEST = lax.Precision.HIGHEST
LOG2E = 1.4426950408889634


def _bf16_terms(x, n):
    terms = []
    for _ in range(n):
        bits = int(np.array(x, np.float32).view(np.uint32))
        t = float(np.array((bits + 0x7FFF + ((bits >> 16) & 1)) & 0xFFFF0000, np.uint32).view(np.float32))
        terms.append(t)
        x -= t
    return tuple(terms)


LOG2E_TERMS = _bf16_terms(LOG2E, 3)

ML_CHUNK = 128
ML_NB = 4
NSA_TQ = 256
NSA_TK = 256
NSA_ONES = 16
ROW_TILE = 512
FFN_TM = 512
FFN_SUB = 256
FFN_CK = 256
FFN_HALO = 16
SG_ROWS = 512
LRU_T = 256
LRU_HALO = 8


def _cparams(sem):
    return pltpu.CompilerParams(dimension_semantics=sem, vmem_limit_bytes=VMEM_LIMIT)


def _rms(x, g):
    return x * lax.rsqrt(jnp.mean(x * x, axis=-1, keepdims=True) + EPS) * g


def _gelu(x):
    return 0.5 * x * (1.0 + jnp.tanh(0.7978845608028654 * (x + 0.044715 * (x * x * x))))


def _sigmoid(x):
    return 1.0 / (1.0 + jnp.exp(-x))


def _dot(a, b):
    return jnp.dot(a, b, preferred_element_type=F32)


def _dot_nt(a, b, precision=None):
    return lax.dot_general(a, b, (((1,), (1,)), ((), ())), preferred_element_type=F32, precision=precision)


def _dot_tn(a, b):
    return lax.dot_general(a, b, (((0,), (0,)), ((), ())), preferred_element_type=F32)


def _norm_proj_kernel(h_ref, g_ref, *refs, n_row, n_t, cn):
    w_refs, wt_refs = refs[:n_row], refs[n_row:n_row + n_t]
    o_refs, ot_refs = refs[n_row + n_t:2 * n_row + n_t], refs[2 * n_row + n_t:]
    xn = _rms(h_ref[...], g_ref[...]).astype(BF16)
    for w_ref, o_ref in zip(w_refs, o_refs):
        n = w_ref.shape[1]
        for c in range(0, n, cn):
            ce = min(c + cn, n)
            o_ref[:, c:ce] = _dot(xn, w_ref[:, c:ce]).astype(o_ref.dtype)
    for wt_ref, ot_ref in zip(wt_refs, ot_refs):
        n = wt_ref.shape[0]
        for c in range(0, n, cn):
            ce = min(c + cn, n)
            ot_ref[c:ce, :] = _dot_nt(wt_ref[c:ce, :], xn).astype(ot_ref.dtype)


def norm_proj(h2, g, ws, dtypes, wts=(), batch=1, tm=ROW_TILE):
    M, D = h2.shape
    tps = M // batch // tm
    in_specs = [pl.BlockSpec((tm, D), lambda i: (i, 0)), pl.BlockSpec((1, D), lambda i: (0, 0))]
    in_specs += [pl.BlockSpec(w.shape, lambda i: (0, 0)) for w in (*ws, *wts)]
    out_specs = [pl.BlockSpec((tm, w.shape[1]), lambda i: (i, 0)) for w in ws]
    out_specs += [pl.BlockSpec((None, w.shape[0], tm), lambda i: (i // tps, 0, i % tps)) for w in wts]
    out_shape = [jax.ShapeDtypeStruct((M, w.shape[1]), dt) for w, dt in zip(ws, dtypes)]
    out_shape += [jax.ShapeDtypeStruct((batch, w.shape[0], M // batch), BF16) for w in wts]
    return pl.pallas_call(
        functools.partial(_norm_proj_kernel, n_row=len(ws), n_t=len(wts), cn=512),
        grid=(M // tm,), in_specs=in_specs, out_specs=out_specs, out_shape=out_shape,
        compiler_params=_cparams(("parallel",)), name="norm_proj",
    )(h2, g.reshape(1, D), *ws, *wts)


def _mix_ffn_kernel(h_ref, hh_ref, a1_ref, a1h_ref, a2_ref, a2h_ref, wo1_ref, wo2_ref, gmix_ref, gpre_ref, wu_ref,
                    cw_ref, cb_ref, wd_ref, gpost_ref, o_ref, xn_s, acc_s, *, tiles_per_seq, ck):
    F = wd_ref.shape[0]
    n_chunks = F // ck
    n_sub = h_ref.shape[0] // FFN_SUB
    first = (pl.program_id(0) % tiles_per_seq) == 0
    g = gpre_ref[...]

    def mixed(h, a1, a2):
        return h + _rms(_dot(a1, wo1_ref[...]) + _dot(a2, wo2_ref[...]), gmix_ref[...])

    def conv(u, cols):
        y = (cw_ref[2:3, cols] * u + cw_ref[1:2, cols] * pltpu.roll(u, 1, 0) + cw_ref[0:1, cols] * pltpu.roll(u, 2, 0)
             + cb_ref[:, cols])
        return y[FFN_HALO:, :]

    def cols_of(c, half):
        return slice(half * F + c * ck, half * F + (c + 1) * ck)

    x = {}

    def head(j):
        rows = slice(j * FFN_SUB, (j + 1) * FFN_SUB)
        x[j] = mixed(h_ref[rows, :], a1_ref[rows, :], a2_ref[rows, :])
        if j == 0:
            halo = jnp.where(first, 0.0, _rms(mixed(hh_ref[...], a1h_ref[...], a2h_ref[...]), g))
            xn_s[0:FFN_HALO, :] = halo.astype(BF16)
        xn_s[FFN_HALO + j * FFN_SUB:FFN_HALO + (j + 1) * FFN_SUB, :] = _rms(x[j], g).astype(BF16)
        acc_s[rows, :] = jnp.zeros((FFN_SUB, acc_s.shape[1]), F32)

    def up(j, c):
        xn = xn_s[j * FFN_SUB:(j + 1) * FFN_SUB + FFN_HALO, :]
        return _dot(xn, wu_ref[:, cols_of(c, 0)]), _dot(xn, wu_ref[:, cols_of(c, 1)])

    def tail(j):
        rows = slice(j * FFN_SUB, (j + 1) * FFN_SUB)
        o_ref[rows, :] = x[j] + _rms(acc_s[rows, :], gpost_ref[...])

    head(0)
    for j in range(n_sub):
        u = up(j, 0)
        for c in range(n_chunks):
            u_next = up(j, c + 1) if c + 1 < n_chunks else None
            if c == 0 and j + 1 < n_sub:
                head(j + 1)
            if c == n_chunks // 3 and j >= 1:
                tail(j - 1)
            act = (_gelu(conv(u[0], cols_of(c, 0))) * conv(u[1], cols_of(c, 1))).astype(BF16)
            acc_s[j * FFN_SUB:(j + 1) * FFN_SUB, :] += _dot(act, wd_ref[c * ck:(c + 1) * ck, :])
            u = u_next
    tail(n_sub - 1)


def mix_ffn_block(h2, a1, a2, seq_len, w_out, g_mix, g_pre, w_up, conv_w, conv_b, w_down, g_post, tm=FFN_TM, ck=FFN_CK):
    M, D = h2.shape
    F = w_down.shape[0]
    K1, K2 = a1.shape[1], a2.shape[1]
    assert F % ck == 0 and seq_len % tm == 0 and tm % FFN_SUB == 0 and FFN_SUB % FFN_HALO == 0
    hb = tm // FFN_HALO
    consts = (w_out[:K1].astype(BF16), w_out[K1:].astype(BF16), g_mix.reshape(1, D), g_pre.reshape(1, D),
              w_up.astype(BF16), conv_w, conv_b.reshape(1, 2 * F), w_down.astype(BF16), g_post.reshape(1, D))

    def tile(width):
        return pl.BlockSpec((tm, width), lambda i: (i, 0))

    def halo(width):
        return pl.BlockSpec((FFN_HALO, width), lambda i: (jnp.maximum(i * hb - 1, 0), 0))

    def full(a):
        return pl.BlockSpec(a.shape, lambda i: (0,) * a.ndim)

    return pl.pallas_call(
        functools.partial(_mix_ffn_kernel, tiles_per_seq=seq_len // tm, ck=ck),
        grid=(M // tm,),
        in_specs=[tile(D), halo(D), tile(K1), halo(K1), tile(K2), halo(K2)] + [full(a) for a in consts],
        out_specs=tile(D),
        out_shape=jax.ShapeDtypeStruct((M, D), F32),
        scratch_shapes=[pltpu.VMEM((tm + FFN_HALO, D), BF16), pltpu.VMEM((tm, D), F32)],
        compiler_params=_cparams(("parallel",)), name="mix_ffn",
    )(h2, h2, a1, a1, a2, a2, *consts)


def _mlstm_kernel(qt_ref, vt_ref, k_ref, mo_ref, gt_ref, gb_ref, ng_ref, o_ref, cn_s, m_s):
    NB, _, L = qt_ref.shape
    d = HEAD_DIM

    @pl.when(pl.program_id(1) == 0)
    def _():
        cn_s[...] = jnp.zeros_like(cn_s)
        m_s[...] = jnp.zeros_like(m_s)

    src = lax.broadcasted_iota(jnp.int32, (L, L), 0)
    tgt = lax.broadcasted_iota(jnp.int32, (L, L), 1)
    causal = src <= tgt
    tri = (tgt <= src).astype(F32)

    gcap = [GATE_SOFTCAP * jnp.tanh((gt_ref[nb] + gb_ref[...]) * (1.0 / GATE_SOFTCAP)) for nb in range(NB)]
    lf = [jnp.minimum(x, 0.0) - jnp.log1p(jnp.exp(-jnp.abs(x))) for x in gcap]
    b_col = [jnp.dot(tri, x, preferred_element_type=F32, precision=HIGHEST) for x in lf]
    b_row = [_dot_nt(x.T, tri, precision=HIGHEST) for x in lf]
    i_row = [x.T for x in gcap]
    c_col = [b_col[nb] - pltpu.roll(gcap[nb], ML_HEADS, 1) for nb in range(NB)]

    chains = [(nb, h) for nb in range(NB) for h in range(ML_HEADS)]
    n = range(len(chains))

    def rows(h):
        return slice(h * d, (h + 1) * d)

    q_t = [qt_ref[nb, rows(h), :] for nb, h in chains]
    v_t = [vt_ref[nb, rows(h), :] for nb, h in chains]
    k = [k_ref[nb, :, rows(h)] * 0.125 for nb, h in chains]
    br = [b_row[nb][ML_HEADS + h:ML_HEADS + h + 1, :] for nb, h in chains]
    ir = [i_row[nb][h:h + 1, :] for nb, h in chains]
    g = [x[:, L - 1:L] for x in br]
    m_prev = [m_s[nb, h:h + 1, 0:1] for nb, h in chains]
    cn_prev = [cn_s[nb, h] for nb, h in chains]
    kq = [_dot(k[c], q_t[c]) for c in n]
    carry = [_dot(cn_prev[c].astype(BF16), q_t[c]) for c in n]
    dlog = [jnp.where(causal, br[c] - c_col[nb][:, ML_HEADS + h:ML_HEADS + h + 1], -jnp.inf)
            for c, (nb, h) in enumerate(chains)]
    inter = [br[c] + m_prev[c] for c in n]
    m_row = [jnp.maximum(inter[c], jnp.max(dlog[c], axis=0, keepdims=True)) for c in n]
    s = [kq[c] * jnp.exp(dlog[c] - m_row[c]) for c in n]
    w_inter = [jnp.exp(inter[c] - m_row[c]) for c in n]
    num = [_dot(v_t[c], s[c].astype(BF16)) + w_inter[c] * carry[c][0:d] for c in n]
    den = [jnp.sum(s[c], axis=0, keepdims=True) + w_inter[c] * carry[c][d:d + 1] for c in n]
    hh = [num[c] * (1.0 / jnp.maximum(jnp.abs(den[c]), jnp.exp(-m_row[c]))) for c in n]
    wlog = [g[c] - br[c] + ir[c] for c in n]
    m_new = [jnp.maximum(g[c] + m_prev[c], jnp.max(wlog[c], axis=-1, keepdims=True)) for c in n]
    w_row = [jnp.exp(wlog[c] - m_new[c]) for c in n]
    decay = [jnp.exp(g[c] + m_prev[c] - m_new[c]) for c in n]
    for c, (nb, h) in enumerate(chains):
        vw = jnp.concatenate([v_t[c].astype(F32) * w_row[c], jnp.broadcast_to(w_row[c], (8, L))], axis=0).astype(BF16)
        cn_s[nb, h] = decay[c] * cn_prev[c] + _dot(vw, k[c])
        m_s[nb, h:h + 1, :] = jnp.broadcast_to(m_new[c], (1, m_s.shape[2]))
    outs = [hh[c] * lax.rsqrt(jnp.mean(hh[c] * hh[c], axis=0, keepdims=True) + EPS) * ng_ref[rows(h), :]
            for c, (nb, h) in enumerate(chains)]
    for nb in range(NB):
        out_t = jnp.concatenate(outs[nb * ML_HEADS:(nb + 1) * ML_HEADS], axis=0)
        o_ref[nb] = (out_t.T * _sigmoid(mo_ref[nb])).astype(o_ref.dtype)


def mlstm_block(zt, zb, zf, gate_bias, norm_g, L=ML_CHUNK, NB=ML_NB):
    B, S, _ = zb.shape
    W, H, d = ML_W, ML_HEADS, HEAD_DIM
    assert B % NB == 0 and S % L == 0
    ng = jnp.broadcast_to(norm_g.reshape(W, 1), (W, L))
    return pl.pallas_call(
        _mlstm_kernel, grid=(B // NB, S // L),
        in_specs=[pl.BlockSpec((NB, W, L), lambda b, c: (b, 0, c)),
                  pl.BlockSpec((NB, W, L), lambda b, c: (b, 1, c)),
                  pl.BlockSpec((NB, L, W), lambda b, c: (b, c, 0)),
                  pl.BlockSpec((NB, L, W), lambda b, c: (b, c, 0)),
                  pl.BlockSpec((NB, L, LANES), lambda b, c: (b, c, 6)),
                  pl.BlockSpec((1, LANES), lambda b, c: (0, 0)),
                  pl.BlockSpec((W, L), lambda b, c: (0, 0))],
        out_specs=pl.BlockSpec((NB, L, W), lambda b, c: (b, c, 0)),
        out_shape=jax.ShapeDtypeStruct((B, S, W), BF16),
        scratch_shapes=[pltpu.VMEM((NB, H, d + 8, d), F32), pltpu.VMEM((NB, H, LANES), F32)],
        compiler_params=_cparams(("parallel", "arbitrary")), name="mlstm",
    )(zt, zt, zb, zf, zf, gate_bias, ng)


def _compress_kernel(kc_ref, vc_ref, pek_ref, pev_ref, w1k_ref, w1v_ref, w2k_ref, w2v_ref, ok_ref, ov_ref):
    G = NSA_KV_HEADS
    nh = kc_ref.shape[0] // CMP_STRIDE

    def one(x_ref, pe_ref, w1_ref, w2_ref):
        hid = None
        for l in range(CMP_STRIDE):
            y = x_ref[pl.ds(l, nh, stride=CMP_STRIDE), :]
            ya = (y + pe_ref[l:l + 1, :]).astype(BF16)
            yb = (pltpu.roll(y, nh - 1, 0) + pe_ref[CMP_STRIDE + l:CMP_STRIDE + l + 1, :]).astype(BF16)
            t = _dot(ya, w1_ref[l]) + _dot(yb, w1_ref[CMP_STRIDE + l])
            hid = t if hid is None else hid + t
        hid = _gelu(hid).astype(BF16)
        return [_dot(hid[:, gi * CMP_HID:(gi + 1) * CMP_HID], w2_ref[...]) for gi in range(G)]

    for gi, (ko, vo) in enumerate(zip(one(kc_ref, pek_ref, w1k_ref, w2k_ref), one(vc_ref, pev_ref, w1v_ref, w2v_ref))):
        ok_ref[gi] = ko.astype(ok_ref.dtype)
        ov_ref[gi] = vo.T.astype(ov_ref.dtype)


def compress_block(zf, kc_block, vc_block, k_pe, k_w1, k_w2, v_pe, v_w1, v_w2):
    B, S, _ = zf.shape
    G, d = NSA_KV_HEADS, HEAD_DIM
    nh = S // CMP_STRIDE

    def prep(pe, w1):
        w1bd = jnp.zeros((CMP_LEN, G * d, G * CMP_HID), w1.dtype)
        for gi in range(G):
            w1bd = w1bd.at[:, gi * d:(gi + 1) * d, gi * CMP_HID:(gi + 1) * CMP_HID].set(w1)
        return jnp.tile(pe, (1, G)), w1bd.astype(BF16)

    pek, w1k = prep(k_pe, k_w1)
    pev, w1v = prep(v_pe, v_w1)

    def full(a):
        return pl.BlockSpec(a.shape, lambda b: (0,) * a.ndim)

    w2k, w2v = k_w2.astype(BF16), v_w2.astype(BF16)
    return pl.pallas_call(
        _compress_kernel, grid=(B,),
        in_specs=[pl.BlockSpec((None, S, G * d), lambda b: (b, 0, kc_block)),
                  pl.BlockSpec((None, S, G * d), lambda b: (b, 0, vc_block)),
                  full(pek), full(pev), full(w1k), full(w1v), full(w2k), full(w2v)],
        out_specs=[pl.BlockSpec((None, G, nh, d), lambda b: (b, 0, 0, 0)),
                   pl.BlockSpec((None, G, d, nh), lambda b: (b, 0, 0, 0))],
        out_shape=[jax.ShapeDtypeStruct((B, G, nh, d), BF16), jax.ShapeDtypeStruct((B, G, d, nh), BF16)],
        compiler_params=_cparams(("parallel",)), name="nsa_compress",
    )(zf, zf, pek, pev, w1k, w1v, w2k, w2v)


def _nsa_kernel(q_ref, gt_ref, gb_ref, kc_ref, vct_ref, ks_ref, vst_ref, kw_ref, vwt_ref, ovl_ref, aug_ref, caug_ref,
                tail_ref, ctail_ref, wmask_ref, o_ref, ksa_s, kwa_s, kca_s, vsa_s, vwa_s, ss_s, ps_s, sw_s, pw_s, *, n_cmp):
    TQ, TK, d, HPG = NSA_TQ, NSA_TK, HEAD_DIM, NSA_HPG
    R = HPG * TQ
    g = pl.program_id(1)
    qi = pl.program_id(2)
    q0 = qi * TQ
    kt_d = q0 // TK
    n_cb = kc_ref.shape[0]
    n_sb = ovl_ref.shape[0]

    @pl.when(qi == 0)
    def _():
        ksa_s[:, 0:d] = jnp.where(g == 0, ks_ref[:, 0:d], ks_ref[:, d:2 * d])
        ksa_s[:, d:] = aug_ref[...]
        kwa_s[:, 0:d] = jnp.where(g == 0, kw_ref[:, 0:d], kw_ref[:, d:2 * d])
        kwa_s[:, d:] = aug_ref[...]
        kca_s[:, 0:d] = kc_ref[...]
        kca_s[:, d:] = caug_ref[...]
        vsa_s[0:d, :] = vst_ref[...]
        vsa_s[d:, :] = jnp.ones((vsa_s.shape[0] - d, vsa_s.shape[1]), BF16)
        vwa_s[0:d, :] = vwt_ref[...]
        vwa_s[d:, :] = jnp.ones((vwa_s.shape[0] - d, vwa_s.shape[1]), BF16)

    def slope(hh):
        return jnp.where(g == 0, 2.0 ** (-(hh + 1)), 2.0 ** (-(HPG + hh + 1))).astype(F32)

    def per_head(fn):
        return jnp.concatenate([fn(hh) for hh in range(HPG)], axis=1)

    def tile_heads(x):
        return jnp.concatenate([x] * HPG, axis=1)

    q_t = (q_ref[...].astype(F32) * (LOG2E * 0.125)).T

    def q_head(hh):
        return q_t[hh * d:(hh + 1) * d]

    qc_t = per_head(lambda hh: jnp.concatenate([q_head(hh), ctail_ref[...] * slope(hh)], axis=0)).astype(BF16)
    n_r = lax.broadcasted_iota(jnp.int32, (n_cb, TQ), 0)
    t_c = q0 + lax.broadcasted_iota(jnp.int32, (n_cb, TQ), 1)
    ok_c = (n_r * CMP_STRIDE + (CMP_LEN - 1) <= t_c) & (n_r < n_cmp)
    s_c = _dot(kca_s[...], qc_t) + tile_heads(jnp.where(ok_c, 0.0, NEG_BIG))
    e_c = jnp.exp2(s_c - jnp.max(s_c, axis=0, keepdims=True)) * tile_heads(jnp.where(ok_c, 1.0, 0.0))
    p_c = e_c * (1.0 / jnp.maximum(jnp.sum(e_c, axis=0, keepdims=True), 1.0))
    o_c = _dot(vct_ref[...], p_c.astype(BF16))

    p_sum = p_c[:, 0:TQ]
    for hh in range(1, HPG):
        p_sum = p_sum + p_c[:, hh * TQ:(hh + 1) * TQ]
    imp = jnp.dot(ovl_ref[...], p_sum, preferred_element_type=F32, precision=HIGHEST)
    jb = lax.broadcasted_iota(jnp.int32, (n_sb, TQ), 0)
    cur = (q0 + lax.broadcasted_iota(jnp.int32, (n_sb, TQ), 1)) // SEL_LEN
    valid = jb <= cur
    forced = (jb == 0) | (jb == cur) | (jb == cur - 1)
    score = jnp.where(forced, jnp.inf, jnp.where(valid, imp, -jnp.inf))
    n_grp = n_sb // 8
    grp = [score[8 * a:8 * a + 8] for a in range(n_grp)]
    rank = [jnp.zeros((8, TQ), jnp.int32) for _ in range(n_grp)]
    sub = lax.broadcasted_iota(jnp.int32, (8, TQ), 0)
    for j in range(n_sb):
        r = score[j:j + 1, :]
        for a in range(n_grp):
            if a > j // 8:
                ahead = (r >= grp[a]).astype(jnp.int32)
            elif a < j // 8:
                ahead = (r > grp[a]).astype(jnp.int32)
            else:
                ahead = jnp.where(sub > j % 8, (r >= grp[a]).astype(jnp.int32), (r > grp[a]).astype(jnp.int32))
            rank[a] = rank[a] + ahead
    picked = valid & (jnp.concatenate(rank, axis=0) < SEL_TOPN)

    def q_aug(block_rows):
        return per_head(lambda hh: jnp.concatenate(
            [q_head(hh), block_rows, tail_ref[...] * slope(hh)], axis=0)).astype(BF16)

    qs_t = q_aug(jnp.where(picked, 0.0, NEG_BIG))
    qw_t = q_aug(jnp.zeros((n_sb, TQ), F32))

    def scores(ka_s, qa_t, kt):
        return _dot(ka_s[pl.ds(pl.multiple_of(kt * TK, TK), TK), :], qa_t)

    def stage_scores(buf, slot, ka_s, qa_t, kt, mask_add=None):
        s = scores(ka_s, qa_t, kt)
        if mask_add is not None:
            s = s + mask_add
        buf[slot] = s
        return jnp.max(s, axis=0, keepdims=True)

    def stage_probs(sbuf, pbuf, slot, tile_max, m):
        m_new = jnp.maximum(m, tile_max)
        pbuf[slot] = jnp.exp2(sbuf[slot] - m_new).astype(BF16)
        return m_new, jnp.exp2(m - m_new)

    def stage_values(pbuf, slot, va_s, kt, alpha, acc):
        return alpha * acc + _dot(va_s[:, pl.ds(pl.multiple_of(kt * TK, TK), TK)], pbuf[slot])

    def normalised(acc):
        return acc[0:d] * (1.0 / acc[d:d + 1])

    m0, acc0 = jnp.full((1, R), NEG_BIG, F32), jnp.zeros((vsa_s.shape[0], R), F32)

    n_win = (WINDOW - 1 + TK - 1) // TK + 1
    kt_win, max_win = [], []
    for back in range(n_win):
        kt_raw = kt_d - back
        kt = jnp.maximum(kt_raw, 0)
        if back == 0:
            mask_add = tile_heads(wmask_ref[0])
        elif (back + 1) * TK <= WINDOW:
            mask_add = jnp.where(kt_raw >= 0, 0.0, NEG_BIG)
        else:
            mask_add = tile_heads(wmask_ref[back] + jnp.where(kt_raw >= 0, 0.0, NEG_BIG))
        max_win.append(stage_scores(sw_s, back, kwa_s, qw_t, kt, mask_add))
        kt_win.append(kt)

    filler = 1 - kt_d % 2
    n_seq = kt_d + 1 + filler

    def sel_tile(i):
        return jnp.where(i == 0, kt_d, jnp.maximum(i - 1 - filler, 0))

    tmax0 = stage_scores(ss_s, 0, ksa_s, qs_t, kt_d, tile_heads(wmask_ref[0]))
    m, alpha = stage_probs(ss_s, ps_s, 0, tmax0, m0)
    tmax1 = stage_scores(ss_s, 1, ksa_s, qs_t, sel_tile(1), jnp.where(filler == 1, NEG_BIG, 0.0))

    mw, accw = m0, acc0
    for back in range(n_win):
        mw, aw = stage_probs(sw_s, pw_s, back, max_win[back], mw)
        accw = stage_values(pw_s, back, vwa_s, kt_win[back], aw, accw)
    o_w = normalised(accw)

    def sel_body(k, carry):
        m, alpha, acc, tmax1 = carry
        i = 2 * k
        tmax0 = stage_scores(ss_s, 0, ksa_s, qs_t, sel_tile(i + 2))
        m, alpha1 = stage_probs(ss_s, ps_s, 1, tmax1, m)
        tmax1 = stage_scores(ss_s, 1, ksa_s, qs_t, sel_tile(i + 3))
        acc = stage_values(ps_s, 0, vsa_s, sel_tile(i), alpha, acc)
        m, alpha2 = stage_probs(ss_s, ps_s, 0, tmax0, m)
        acc = stage_values(ps_s, 1, vsa_s, sel_tile(i + 1), alpha1, acc)
        return m, alpha2, acc, tmax1

    m, alpha, acc, tmax1 = lax.fori_loop(0, (n_seq - 2) // 2, sel_body, (m, alpha, acc0, tmax1))
    m, alpha1 = stage_probs(ss_s, ps_s, 1, tmax1, m)
    acc = stage_values(ps_s, 0, vsa_s, sel_tile(n_seq - 2), alpha, acc)
    acc = stage_values(ps_s, 1, vsa_s, sel_tile(n_seq - 1), alpha1, acc)
    o_s = normalised(acc)

    gates = _sigmoid(gt_ref[...] + gb_ref[...]).T
    gsel = jnp.where(g == 0, gates[16:16 + 3 * HPG], gates[16 + 3 * HPG:16 + 6 * HPG])
    outs = []
    for hh in range(HPG):
        cols = slice(hh * TQ, (hh + 1) * TQ)
        outs.append(gsel[3 * hh:3 * hh + 1] * o_c[:, cols] + gsel[3 * hh + 1:3 * hh + 2] * o_s[:, cols]
                    + gsel[3 * hh + 2:3 * hh + 3] * o_w[:, cols])
    o_ref[...] = jnp.concatenate(outs, axis=0).T.astype(o_ref.dtype)


def nsa_block(zt, zb, zf, gate_bias, kcmp, vcmp_t, q_col_block, k_col_blocks, v_row_blocks):
    B, S, _ = zb.shape
    G, d = NSA_KV_HEADS, HEAD_DIM
    TQ = NSA_TQ
    n_cb = kcmp.shape[2]
    n_cmp = (S - CMP_LEN) // CMP_STRIDE + 1
    n_sb = S // SEL_LEN
    n_terms = len(LOG2E_TERMS)
    kaug = -(-(d + n_sb + 2 * n_terms) // LANES) * LANES
    R = NSA_HPG * TQ
    n_win = (WINDOW - 1 + NSA_TK - 1) // NSA_TK + 1
    assert S % NSA_TK == 0 and NSA_TK == TQ and TQ % SEL_LEN == 0

    cidx = np.arange(n_cb)[None, :] * CMP_STRIDE
    sstart = np.arange(n_sb)[:, None] * SEL_LEN
    ovl = ((cidx < sstart + SEL_LEN) & (cidx + CMP_LEN - 1 >= sstart) & (np.arange(n_cb)[None, :] < n_cmp))
    ovl = jnp.asarray(ovl.astype(np.float32))
    pos = np.arange(S)
    aug = np.zeros((S, kaug - d), np.float32)
    aug[pos, pos // SEL_LEN] = 1.0
    tail = np.zeros((kaug - d - n_sb, TQ), np.float32)
    caug = np.zeros((n_cb, d), np.float32)
    ctail = np.zeros((d, TQ), np.float32)
    for i, term in enumerate(LOG2E_TERMS):
        aug[:, n_sb + i] = pos // SEL_LEN
        aug[:, n_sb + n_terms + i] = pos % SEL_LEN
        tail[i] = term * SEL_LEN
        tail[n_terms + i] = term
        caug[:, i] = np.arange(n_cb)
        ctail[i] = term * CMP_STRIDE
    aug, caug, tail, ctail = jnp.asarray(aug, BF16), jnp.asarray(caug, BF16), jnp.asarray(tail), jnp.asarray(ctail)
    dist = np.arange(n_win)[:, None, None] * NSA_TK + np.arange(TQ)[None, None, :] - np.arange(NSA_TK)[None, :, None]
    wmask = jnp.asarray(np.where((dist >= 0) & (dist < WINDOW), 0.0, NEG_BIG).astype(np.float32))

    def kspec(j):
        return pl.BlockSpec((None, S, G * d), lambda b, g, i: (b, 0, k_col_blocks[j]))

    def vspec(j):
        return pl.BlockSpec((None, d, S), lambda b, g, i: (b, v_row_blocks[j] + g, 0))

    def const(a):
        return pl.BlockSpec(a.shape, lambda b, g, i: (0,) * a.ndim)

    return pl.pallas_call(
        functools.partial(_nsa_kernel, n_cmp=n_cmp), grid=(B, G, S // TQ),
        in_specs=[pl.BlockSpec((None, TQ, NSA_HPG * d), lambda b, g, i: (b, i, q_col_block + g)),
                  pl.BlockSpec((None, TQ, LANES), lambda b, g, i: (b, i, 6)),
                  const(gate_bias),
                  pl.BlockSpec((None, None, n_cb, d), lambda b, g, i: (b, g, 0, 0)),
                  pl.BlockSpec((None, None, d, n_cb), lambda b, g, i: (b, g, 0, 0)),
                  kspec(0), vspec(0), kspec(1), vspec(1), const(ovl), const(aug), const(caug), const(tail),
                  const(ctail), const(wmask)],
        out_specs=pl.BlockSpec((None, TQ, NSA_HPG * d), lambda b, g, i: (b, i, g)),
        out_shape=jax.ShapeDtypeStruct((B, S, NSA_W), BF16),
        scratch_shapes=[pltpu.VMEM((S, kaug), BF16), pltpu.VMEM((S, kaug), BF16), pltpu.VMEM((n_cb, 2 * d), BF16),
                        pltpu.VMEM((d + NSA_ONES, S), BF16), pltpu.VMEM((d + NSA_ONES, S), BF16),
                        pltpu.VMEM((2, NSA_TK, R), F32), pltpu.VMEM((2, NSA_TK, R), BF16),
                        pltpu.VMEM((n_win, NSA_TK, R), F32), pltpu.VMEM((n_win, NSA_TK, R), BF16)],
        compiler_params=_cparams(("parallel", "parallel", "arbitrary")), name="nsa",
    )(zb, zf, gate_bias, kcmp, vcmp_t, zb, zt, zb, zt, ovl, aug, caug, tail, ctail, wmask)


def _rglru_kernel(gate_ref, x_ref, halo_ref, cw_ref, cb_ref, wa_ref, wx_ref, ba_ref, bx_ref, lam_ref, o_ref, h_s,
                  au_s):
    T, W = x_ref.shape
    first = pl.program_id(1) == 0

    @pl.when(first)
    def _():
        h_s[...] = jnp.zeros_like(h_s)

    xe = jnp.concatenate([jnp.where(first, 0.0, halo_ref[...]), x_ref[...]], axis=0)
    xc = cb_ref[...] + cw_ref[3:4, :] * xe
    for k in range(1, LRU_CONV):
        xc = xc + cw_ref[3 - k:4 - k, :] * pltpu.roll(xe, k, 0)
    xc = xc[LRU_HALO:, :]
    xcb = xc.astype(BF16)
    half = W // 2

    def blockdiag(w_ref):
        return jnp.concatenate([_dot(xcb[:, :half], w_ref[0]), _dot(xcb[:, half:], w_ref[1])], axis=1)

    r = _sigmoid(blockdiag(wa_ref) + ba_ref[...])
    i = _sigmoid(blockdiag(wx_ref) + bx_ref[...])
    nl = -lam_ref[...]
    softplus = jnp.maximum(nl, 0.0) + jnp.log1p(jnp.exp(-jnp.abs(nl)))
    log_a = -LRU_C * r * softplus
    a = jnp.exp(log_a)
    one_m_a2 = -jnp.tanh(log_a) * (a * a + 1.0)
    u = jnp.where(one_m_a2 > 0.0, one_m_a2 * lax.rsqrt(one_m_a2), 0.0) * (i * xc)

    n_grp, n_slab = T // 8, W // LANES

    def phases(x, slab):
        for c in range(n_slab):
            au_s[slab, c] = x[:, c * LANES:(c + 1) * LANES]
        return [jnp.concatenate([au_s[slab, c, pl.ds(j, n_grp, stride=8), :] for c in range(n_slab)], axis=1)
                for j in range(8)]

    a_ph, u_ph = phases(a, 0), phases(u, 1)
    prod, part = [a_ph[0]], [u_ph[0]]
    for j in range(1, 8):
        part.append(a_ph[j] * part[-1] + u_ph[j])
        prod.append(a_ph[j] * prod[-1])
    grp = lax.broadcasted_iota(jnp.int32, (n_grp, W), 0)
    ag, ug = prod[7], part[7]
    sft = 1
    while sft < n_grp:
        keep = grp >= sft
        ug = ag * jnp.where(keep, pltpu.roll(ug, sft, 0), 0.0) + ug
        ag = ag * jnp.where(keep, pltpu.roll(ag, sft, 0), 1.0)
        sft *= 2
    h_prev = h_s[0:1, :]
    hg = ug + ag * h_prev
    carry_in = jnp.where(grp >= 1, pltpu.roll(hg, 1, 0), h_prev)
    for j in range(8):
        hj = part[j] + prod[j] * carry_in
        for c in range(n_slab):
            au_s[0, c, pl.ds(j, n_grp, stride=8), :] = hj[:, c * LANES:(c + 1) * LANES]
    h = jnp.concatenate([au_s[0, c] for c in range(n_slab)], axis=1)
    h_s[...] = jnp.broadcast_to(hg[n_grp - 1:n_grp, :], h_s.shape)
    o_ref[...] = (_gelu(gate_ref[...]) * h).astype(o_ref.dtype)


def rglru_block(z, conv_w, conv_b, wa, ba, wx, bx, lam, T=LRU_T):
    B, S, _ = z.shape
    W = LRU_W
    half = W // 2
    hb = T // LRU_HALO

    def bd(w):
        blocks = [jax.scipy.linalg.block_diag(*[w[h] for h in range(4 * j, 4 * j + 4)]) for j in range(2)]
        return jnp.stack(blocks).astype(BF16)

    vec = lambda a: a.reshape(1, W)
    vspec = pl.BlockSpec((1, W), lambda b, t: (0, 0))
    wspec = pl.BlockSpec((2, half, half), lambda b, t: (0, 0, 0))
    return pl.pallas_call(
        _rglru_kernel, grid=(B, S // T),
        in_specs=[pl.BlockSpec((None, T, W), lambda b, t: (b, t, 0)),
                  pl.BlockSpec((None, T, W), lambda b, t: (b, t, 1)),
                  pl.BlockSpec((None, LRU_HALO, W), lambda b, t: (b, jnp.maximum(t * hb - 1, 0), 1)),
                  pl.BlockSpec((LRU_CONV, W), lambda b, t: (0, 0)), vspec, wspec, wspec, vspec, vspec, vspec],
        out_specs=pl.BlockSpec((None, T, W), lambda b, t: (b, t, 0)),
        out_shape=jax.ShapeDtypeStruct((B, S, W), BF16),
        scratch_shapes=[pltpu.VMEM((8, W), F32), pltpu.VMEM((2, W // LANES, T, LANES), F32)],
        compiler_params=_cparams(("parallel", "arbitrary")), name="rglru",
    )(z, z, z, conv_w, vec(conv_b), bd(wa), bd(wx), vec(ba), vec(bx), vec(lam))


def _sgu_kernel(u_ref, v_ref, g_ref, b_ref, w_ref, bias_ref, o_ref):
    C, W = SG_CHUNK, v_ref.shape[1]
    dg = W // SG_GROUPS
    v = _gelu(v_ref[...])
    mu = jnp.mean(v, axis=-1, keepdims=True)
    vc = v - mu
    vn = (vc * lax.rsqrt(jnp.mean(vc * vc, axis=-1, keepdims=True) + EPS) * g_ref[...] + b_ref[...]).astype(BF16)
    row = lax.broadcasted_iota(jnp.int32, (C, C), 0)
    col = lax.broadcasted_iota(jnp.int32, (C, C), 1)
    for gi in range(SG_GROUPS):
        sl = slice(gi * dg, (gi + 1) * dg)
        wc = jnp.where(col <= row, w_ref[gi], 0.0).astype(BF16)
        for c in range(v_ref.shape[0] // C):
            rows = slice(c * C, (c + 1) * C)
            mixed = _dot(wc, vn[rows, sl]) + bias_ref[:, sl]
            o_ref[rows, sl] = (_gelu(u_ref[rows, sl]) * mixed).astype(o_ref.dtype)


def sgu_block(z, ln_g, ln_b, w, b):
    B, S, _ = z.shape
    W, C, T = SG_W, SG_CHUNK, SG_ROWS
    assert S % T == 0 and T % C == 0
    bias = jnp.repeat(b.T, W // SG_GROUPS, axis=1)
    vspec = pl.BlockSpec((1, W), lambda bb, c: (0, 0))
    return pl.pallas_call(
        _sgu_kernel, grid=(B, S // T),
        in_specs=[pl.BlockSpec((None, T, W), lambda bb, c: (bb, c, 2)),
                  pl.BlockSpec((None, T, W), lambda bb, c: (bb, c, 3)),
                  vspec, vspec,
                  pl.BlockSpec((SG_GROUPS, C, C), lambda bb, c: (0, 0, 0)),
                  pl.BlockSpec((C, W), lambda bb, c: (0, 0))],
        out_specs=pl.BlockSpec((None, T, W), lambda bb, c: (bb, c, 0)),
        out_shape=jax.ShapeDtypeStruct((B, S, W), BF16),
        compiler_params=_cparams(("parallel", "parallel")), name="sgu",
    )(z, z, ln_g.reshape(1, W), ln_b.reshape(1, W), w, bias)


def _mixer_ab(h2, B, S, pre_g, w_in, ml_gate_b, ml_norm_g, nsa_gate_b, k_pe, k_w1, k_w2, v_pe, v_w1, v_w2):
    D = h2.shape[1]
    G, d = NSA_KV_HEADS, HEAD_DIM
    offs = np.cumsum([0, ML_W, ML_W, ML_W, ML_W, 2 * ML_HEADS, NSA_W] + [G * d] * 6 + [3 * NSA_HEADS])
    mq, mk, mv, mo, mif, nq, kc, vc, ks, vs, kw, vw, ng = [w_in[:, offs[i]:offs[i + 1]] for i in range(13)]
    w_b = jnp.concatenate([mk, nq, ks, kw], axis=1).astype(BF16)
    w_t = jnp.concatenate([mq, mv, vs, vw], axis=1).T.astype(BF16)
    gpad = LANES - 2 * ML_HEADS - 3 * NSA_HEADS
    w_f = jnp.concatenate([mo, kc, vc, mif, ng, jnp.zeros((D, gpad), w_in.dtype)], axis=1).astype(BF16)
    gate_bias = jnp.concatenate([ml_gate_b, nsa_gate_b, jnp.zeros((gpad,), F32)]).reshape(1, LANES)
    zb, zf, zt = norm_proj(h2, pre_g, [w_b, w_f], [BF16, F32], wts=[w_t], batch=B)
    zb = zb.reshape(B, S, -1)
    zf = zf.reshape(B, S, -1)
    h_ml = mlstm_block(zt, zb, zf, gate_bias, ml_norm_g)
    kcmp, vcmp_t = compress_block(zf, ML_W // (G * d), ML_W // (G * d) + 1, k_pe, k_w1, k_w2, v_pe, v_w1, v_w2)
    h_nsa = nsa_block(zt, zb, zf, gate_bias, kcmp, vcmp_t, q_col_block=ML_W // (NSA_HPG * d),
                      k_col_blocks=((ML_W + NSA_W) // (G * d), (ML_W + NSA_W) // (G * d) + 1),
                      v_row_blocks=(2 * ML_W // d, 2 * ML_W // d + G))
    return h_ml.reshape(B * S, ML_W), h_nsa.reshape(B * S, NSA_W)


def _mixer_cd(h2, B, S, pre_g, w_in, conv_w, conv_b, wa, ba, wx, bx, lam, sg_g, sg_bn, sg_w, sg_b):
    (z,) = norm_proj(h2, pre_g, [w_in.astype(BF16)], [F32])
    z = z.reshape(B, S, -1)
    y_lru = rglru_block(z, conv_w, conv_b, wa, ba, wx, bx, lam)
    y_sg = sgu_block(z, sg_g, sg_bn, sg_w, sg_b)
    return y_lru.reshape(B * S, LRU_W), y_sg.reshape(B * S, SG_W)


def kernel(x, pre_mix_g, post_mix_g, pre_ffn_g, post_ffn_g, ab_w_in, ab_w_out, ml_gate_b, ml_norm_g, nsa_gate_b, cmp_k_pe, cmp_k_w1, cmp_k_w2, cmp_v_pe, cmp_v_w1, cmp_v_w2, cd_w_in, cd_w_out, lru_conv_w, lru_conv_b, lru_wa, lru_ba, lru_wx, lru_bx, lru_lambda, sg_norm_g, sg_norm_b, sg_w, sg_b, ffn_w_up, ffn_conv_w, ffn_conv_b, ffn_w_down):
    B, S, D = x.shape
    depth = pre_mix_g.shape[0]
    h2 = x.reshape(B * S, D)
    for layer in range(depth):
        if layer % 2 == 0:
            e = layer // 2
            a1, a2 = _mixer_ab(h2, B, S, pre_mix_g[layer], ab_w_in[e], ml_gate_b[e], ml_norm_g[e], nsa_gate_b[e],
                               cmp_k_pe[e], cmp_k_w1[e], cmp_k_w2[e], cmp_v_pe[e], cmp_v_w1[e], cmp_v_w2[e])
            w_out = ab_w_out[e]
        else:
            o = layer // 2
            a1, a2 = _mixer_cd(h2, B, S, pre_mix_g[layer], cd_w_in[o], lru_conv_w[o], lru_conv_b[o], lru_wa[o],
                               lru_ba[o], lru_wx[o], lru_bx[o], lru_lambda[o], sg_norm_g[o], sg_norm_b[o], sg_w[o],
                               sg_b[o])
            w_out = cd_w_out[o]
        h2 = mix_ffn_block(h2, a1, a2, S, w_out, post_mix_g[layer], pre_ffn_g[layer], ffn_w_up[layer],
                           ffn_conv_w[layer], ffn_conv_b[layer], ffn_w_down[layer], post_ffn_g[layer])
    return h2.reshape(B, S, D)
```

```python
import functools

import numpy as np
import jax
import jax.numpy as jnp
from jax import lax
from jax.experimental import pallas as pl
from jax.experimental.pallas import tpu as pltpu

F32 = jnp.float32
BF16 = jnp.bfloat16

EPS = 1e-6
HEAD_DIM = 64
ML_HEADS = 8
ML_W = 512
GATE_SOFTCAP = 15.0
NSA_HEADS = 8
NSA_KV_HEADS = 2
NSA_HPG = NSA_HEADS // NSA_KV_HEADS
NSA_W = 512
CMP_LEN = 32
CMP_STRIDE = 16
CMP_HID = 128
SEL_LEN = 64
SEL_TOPN = 16
WINDOW = 512
LRU_W = 512
LRU_C = 8.0
LRU_CONV = 4
SG_GROUPS = 8
SG_W = 512
SG_CHUNK = 128
FFN_CONV = 3

LANES = 128
VMEM_LIMIT = 56 * 1024 * 1024
NEG_BIG = -1e30
HIGHEST = lax.Precision.HIGHEST
LOG2E = 1.4426950408889634


def _bf16_terms(x, n):
    terms = []
    for _ in range(n):
        bits = int(np.array(x, np.float32).view(np.uint32))
        t = float(np.array((bits + 0x7FFF + ((bits >> 16) & 1)) & 0xFFFF0000, np.uint32).view(np.float32))
        terms.append(t)
        x -= t
    return tuple(terms)


LOG2E_TERMS = _bf16_terms(LOG2E, 3)

ML_CHUNK = 128
ML_NB = 4
NSA_TQ = 256
NSA_TK = 256
NSA_ONES = 16
ROW_TILE = 512
FFN_TM = 512
FFN_SUB = 256
FFN_CK = 256
FFN_HALO = 16
SG_ROWS = 512
LRU_T = 256
LRU_HALO = 8


def _cparams(sem):
    return pltpu.CompilerParams(dimension_semantics=sem, vmem_limit_bytes=VMEM_LIMIT)


def _rms(x, g):
    return x * lax.rsqrt(jnp.mean(x * x, axis=-1, keepdims=True) + EPS) * g


def _gelu(x):
    return 0.5 * x * (1.0 + jnp.tanh(0.7978845608028654 * (x + 0.044715 * (x * x * x))))


def _sigmoid(x):
    return 1.0 / (1.0 + jnp.exp(-x))


def _dot(a, b):
    return jnp.dot(a, b, preferred_element_type=F32)


def _dot_nt(a, b, precision=None):
    return lax.dot_general(a, b, (((1,), (1,)), ((), ())), preferred_element_type=F32, precision=precision)


def _dot_tn(a, b):
    return lax.dot_general(a, b, (((0,), (0,)), ((), ())), preferred_element_type=F32)


def _norm_proj_kernel(h_ref, g_ref, *refs, n_row, n_t, cn):
    w_refs, wt_refs = refs[:n_row], refs[n_row:n_row + n_t]
    o_refs, ot_refs = refs[n_row + n_t:2 * n_row + n_t], refs[2 * n_row + n_t:]
    xn = _rms(h_ref[...], g_ref[...]).astype(BF16)
    for w_ref, o_ref in zip(w_refs, o_refs):
        n = w_ref.shape[1]
        for c in range(0, n, cn):
            ce = min(c + cn, n)
            o_ref[:, c:ce] = _dot(xn, w_ref[:, c:ce]).astype(o_ref.dtype)
    for wt_ref, ot_ref in zip(wt_refs, ot_refs):
        n = wt_ref.shape[0]
        for c in range(0, n, cn):
            ce = min(c + cn, n)
            ot_ref[c:ce, :] = _dot_nt(wt_ref[c:ce, :], xn).astype(ot_ref.dtype)


def norm_proj(h2, g, ws, dtypes, wts=(), batch=1, tm=ROW_TILE):
    M, D = h2.shape
    tps = M // batch // tm
    in_specs = [pl.BlockSpec((tm, D), lambda i: (i, 0)), pl.BlockSpec((1, D), lambda i: (0, 0))]
    in_specs += [pl.BlockSpec(w.shape, lambda i: (0, 0)) for w in (*ws, *wts)]
    out_specs = [pl.BlockSpec((tm, w.shape[1]), lambda i: (i, 0)) for w in ws]
    out_specs += [pl.BlockSpec((None, w.shape[0], tm), lambda i: (i // tps, 0, i % tps)) for w in wts]
    out_shape = [jax.ShapeDtypeStruct((M, w.shape[1]), dt) for w, dt in zip(ws, dtypes)]
    out_shape += [jax.ShapeDtypeStruct((batch, w.shape[0], M // batch), BF16) for w in wts]
    return pl.pallas_call(
        functools.partial(_norm_proj_kernel, n_row=len(ws), n_t=len(wts), cn=512),
        grid=(M // tm,), in_specs=in_specs, out_specs=out_specs, out_shape=out_shape,
        compiler_params=_cparams(("parallel",)), name="norm_proj",
    )(h2, g.reshape(1, D), *ws, *wts)


def _mix_ffn_kernel(h_ref, hh_ref, a1_ref, a1h_ref, a2_ref, a2h_ref, wo1_ref, wo2_ref, gmix_ref, gpre_ref, wu_ref,
                    cw_ref, cb_ref, wd_ref, gpost_ref, o_ref, xn_s, acc_s, *, tiles_per_seq, ck):
    F = wd_ref.shape[0]
    n_chunks = F // ck
    n_sub = h_ref.shape[0] // FFN_SUB
    first = (pl.program_id(0) % tiles_per_seq) == 0
    g = gpre_ref[...]

    def mixed(h, a1, a2):
        return h + _rms(_dot(a1, wo1_ref[...]) + _dot(a2, wo2_ref[...]), gmix_ref[...])

    def conv(u, cols):
        y = (cw_ref[2:3, cols] * u + cw_ref[1:2, cols] * pltpu.roll(u, 1, 0) + cw_ref[0:1, cols] * pltpu.roll(u, 2, 0)
             + cb_ref[:, cols])
        return y[FFN_HALO:, :]

    def cols_of(c, half):
        return slice(half * F + c * ck, half * F + (c + 1) * ck)

    x = {}

    def head(j):
        rows = slice(j * FFN_SUB, (j + 1) * FFN_SUB)
        x[j] = mixed(h_ref[rows, :], a1_ref[rows, :], a2_ref[rows, :])
        if j == 0:
            halo = jnp.where(first, 0.0, _rms(mixed(hh_ref[...], a1h_ref[...], a2h_ref[...]), g))
            xn_s[0:FFN_HALO, :] = halo.astype(BF16)
        xn_s[FFN_HALO + j * FFN_SUB:FFN_HALO + (j + 1) * FFN_SUB, :] = _rms(x[j], g).astype(BF16)
        acc_s[rows, :] = jnp.zeros((FFN_SUB, acc_s.shape[1]), F32)

    def up(j, c):
        xn = xn_s[j * FFN_SUB:(j + 1) * FFN_SUB + FFN_HALO, :]
        return _dot(xn, wu_ref[:, cols_of(c, 0)]), _dot(xn, wu_ref[:, cols_of(c, 1)])

    def tail(j):
        rows = slice(j * FFN_SUB, (j + 1) * FFN_SUB)
        o_ref[rows, :] = x[j] + _rms(acc_s[rows, :], gpost_ref[...])

    head(0)
    for j in range(n_sub):
        u = up(j, 0)
        for c in range(n_chunks):
            u_next = up(j, c + 1) if c + 1 < n_chunks else None
            if c == 0 and j + 1 < n_sub:
                head(j + 1)
            if c == n_chunks // 3 and j >= 1:
                tail(j - 1)
            act = (_gelu(conv(u[0], cols_of(c, 0))) * conv(u[1], cols_of(c, 1))).astype(BF16)
            acc_s[j * FFN_SUB:(j + 1) * FFN_SUB, :] += _dot(act, wd_ref[c * ck:(c + 1) * ck, :])
            u = u_next
    tail(n_sub - 1)


def mix_ffn_block(h2, a1, a2, seq_len, w_out, g_mix, g_pre, w_up, conv_w, conv_b, w_down, g_post, tm=FFN_TM, ck=FFN_CK):
    M, D = h2.shape
    F = w_down.shape[0]
    K1, K2 = a1.shape[1], a2.shape[1]
    assert F % ck == 0 and seq_len % tm == 0 and tm % FFN_SUB == 0 and FFN_SUB % FFN_HALO == 0
    hb = tm // FFN_HALO
    consts = (w_out[:K1].astype(BF16), w_out[K1:].astype(BF16), g_mix.reshape(1, D), g_pre.reshape(1, D),
              w_up.astype(BF16), conv_w, conv_b.reshape(1, 2 * F), w_down.astype(BF16), g_post.reshape(1, D))

    def tile(width):
        return pl.BlockSpec((tm, width), lambda i: (i, 0))

    def halo(width):
        return pl.BlockSpec((FFN_HALO, width), lambda i: (jnp.maximum(i * hb - 1, 0), 0))

    def full(a):
        return pl.BlockSpec(a.shape, lambda i: (0,) * a.ndim)

    return pl.pallas_call(
        functools.partial(_mix_ffn_kernel, tiles_per_seq=seq_len // tm, ck=ck),
        grid=(M // tm,),
        in_specs=[tile(D), halo(D), tile(K1), halo(K1), tile(K2), halo(K2)] + [full(a) for a in consts],
        out_specs=tile(D),
        out_shape=jax.ShapeDtypeStruct((M, D), F32),
        scratch_shapes=[pltpu.VMEM((tm + FFN_HALO, D), BF16), pltpu.VMEM((tm, D), F32)],
        compiler_params=_cparams(("parallel",)), name="mix_ffn",
    )(h2, h2, a1, a1, a2, a2, *consts)


def _mlstm_kernel(qt_ref, vt_ref, k_ref, mo_ref, gt_ref, gb_ref, ng_ref, o_ref, cn_s, m_s):
    NB, _, L = qt_ref.shape
    d = HEAD_DIM

    @pl.when(pl.program_id(1) == 0)
    def _():
        cn_s[...] = jnp.zeros_like(cn_s)
        m_s[...] = jnp.zeros_like(m_s)

    src = lax.broadcasted_iota(jnp.int32, (L, L), 0)
    tgt = lax.broadcasted_iota(jnp.int32, (L, L), 1)
    causal = src <= tgt
    tri = (tgt <= src).astype(F32)

    gcap = [GATE_SOFTCAP * jnp.tanh((gt_ref[nb] + gb_ref[...]) * (1.0 / GATE_SOFTCAP)) for nb in range(NB)]
    lf = [jnp.minimum(x, 0.0) - jnp.log1p(jnp.exp(-jnp.abs(x))) for x in gcap]
    b_col = [jnp.dot(tri, x, preferred_element_type=F32, precision=HIGHEST) for x in lf]
    b_row = [_dot_nt(x.T, tri, precision=HIGHEST) for x in lf]
    i_row = [x.T for x in gcap]
    c_col = [b_col[nb] - pltpu.roll(gcap[nb], ML_HEADS, 1) for nb in range(NB)]

    chains = [(nb, h) for nb in range(NB) for h in range(ML_HEADS)]
    n = range(len(chains))

    def rows(h):
        return slice(h * d, (h + 1) * d)

    q_t = [qt_ref[nb, rows(h), :] for nb, h in chains]
    v_t = [vt_ref[nb, rows(h), :] for nb, h in chains]
    k = [k_ref[nb, :, rows(h)] * 0.125 for nb, h in chains]
    br = [b_row[nb][ML_HEADS + h:ML_HEADS + h + 1, :] for nb, h in chains]
    ir = [i_row[nb][h:h + 1, :] for nb, h in chains]
    g = [x[:, L - 1:L] for x in br]
    m_prev = [m_s[nb, h:h + 1, 0:1] for nb, h in chains]
    cn_prev = [cn_s[nb, h] for nb, h in chains]
    kq = [_dot(k[c], q_t[c]) for c in n]
    carry = [_dot(cn_prev[c].astype(BF16), q_t[c]) for c in n]
    dlog = [jnp.where(causal, br[c] - c_col[nb][:, ML_HEADS + h:ML_HEADS + h + 1], -jnp.inf)
            for c, (nb, h) in enumerate(chains)]
    inter = [br[c] + m_prev[c] for c in n]
    m_row = [jnp.maximum(inter[c], jnp.max(dlog[c], axis=0, keepdims=True)) for c in n]
    s = [kq[c] * jnp.exp(dlog[c] - m_row[c]) for c in n]
    w_inter = [jnp.exp(inter[c] - m_row[c]) for c in n]
    num = [_dot(v_t[c], s[c].astype(BF16)) + w_inter[c] * carry[c][0:d] for c in n]
    den = [jnp.sum(s[c], axis=0, keepdims=True) + w_inter[c] * carry[c][d:d + 1] for c in n]
    hh = [num[c] * (1.0 / jnp.maximum(jnp.abs(den[c]), jnp.exp(-m_row[c]))) for c in n]
    wlog = [g[c] - br[c] + ir[c] for c in n]
    m_new = [jnp.maximum(g[c] + m_prev[c], jnp.max(wlog[c], axis=-1, keepdims=True)) for c in n]
    w_row = [jnp.exp(wlog[c] - m_new[c]) for c in n]
    decay = [jnp.exp(g[c] + m_prev[c] - m_new[c]) for c in n]
    for c, (nb, h) in enumerate(chains):
        vw = jnp.concatenate([v_t[c].astype(F32) * w_row[c], jnp.broadcast_to(w_row[c], (8, L))], axis=0).astype(BF16)
        cn_s[nb, h] = decay[c] * cn_prev[c] + _dot(vw, k[c])
        m_s[nb, h:h + 1, :] = jnp.broadcast_to(m_new[c], (1, m_s.shape[2]))
    outs = [hh[c] * lax.rsqrt(jnp.mean(hh[c] * hh[c], axis=0, keepdims=True) + EPS) * ng_ref[rows(h), :]
            for c, (nb, h) in enumerate(chains)]
    for nb in range(NB):
        out_t = jnp.concatenate(outs[nb * ML_HEADS:(nb + 1) * ML_HEADS], axis=0)
        o_ref[nb] = (out_t.T * _sigmoid(mo_ref[nb])).astype(o_ref.dtype)


def mlstm_block(zt, zb, zf, gate_bias, norm_g, L=ML_CHUNK, NB=ML_NB):
    B, S, _ = zb.shape
    W, H, d = ML_W, ML_HEADS, HEAD_DIM
    assert B % NB == 0 and S % L == 0
    ng = jnp.broadcast_to(norm_g.reshape(W, 1), (W, L))
    return pl.pallas_call(
        _mlstm_kernel, grid=(B // NB, S // L),
        in_specs=[pl.BlockSpec((NB, W, L), lambda b, c: (b, 0, c)),
                  pl.BlockSpec((NB, W, L), lambda b, c: (b, 1, c)),
                  pl.BlockSpec((NB, L, W), lambda b, c: (b, c, 0)),
                  pl.BlockSpec((NB, L, W), lambda b, c: (b, c, 0)),
                  pl.BlockSpec((NB, L, LANES), lambda b, c: (b, c, 6)),
                  pl.BlockSpec((1, LANES), lambda b, c: (0, 0)),
                  pl.BlockSpec((W, L), lambda b, c: (0, 0))],
        out_specs=pl.BlockSpec((NB, L, W), lambda b, c: (b, c, 0)),
        out_shape=jax.ShapeDtypeStruct((B, S, W), BF16),
        scratch_shapes=[pltpu.VMEM((NB, H, d + 8, d), F32), pltpu.VMEM((NB, H, LANES), F32)],
        compiler_params=_cparams(("parallel", "arbitrary")), name="mlstm",
    )(zt, zt, zb, zf, zf, gate_bias, ng)


def _compress_kernel(kc_ref, vc_ref, pek_ref, pev_ref, w1k_ref, w1v_ref, w2k_ref, w2v_ref, ok_ref, ov_ref):
    G = NSA_KV_HEADS
    nh = kc_ref.shape[0] // CMP_STRIDE

    def one(x_ref, pe_ref, w1_ref, w2_ref):
        hid = None
        for l in range(CMP_STRIDE):
            y = x_ref[pl.ds(l, nh, stride=CMP_STRIDE), :]
            ya = (y + pe_ref[l:l + 1, :]).astype(BF16)
            yb = (pltpu.roll(y, nh - 1, 0) + pe_ref[CMP_STRIDE + l:CMP_STRIDE + l + 1, :]).astype(BF16)
            t = _dot(ya, w1_ref[l]) + _dot(yb, w1_ref[CMP_STRIDE + l])
            hid = t if hid is None else hid + t
        hid = _gelu(hid).astype(BF16)
        return [_dot(hid[:, gi * CMP_HID:(gi + 1) * CMP_HID], w2_ref[...]) for gi in range(G)]

    for gi, (ko, vo) in enumerate(zip(one(kc_ref, pek_ref, w1k_ref, w2k_ref), one(vc_ref, pev_ref, w1v_ref, w2v_ref))):
        ok_ref[gi] = ko.astype(ok_ref.dtype)
        ov_ref[gi] = vo.T.astype(ov_ref.dtype)


def compress_block(zf, kc_block, vc_block, k_pe, k_w1, k_w2, v_pe, v_w1, v_w2):
    B, S, _ = zf.shape
    G, d = NSA_KV_HEADS, HEAD_DIM
    nh = S // CMP_STRIDE

    def prep(pe, w1):
        w1bd = jnp.zeros((CMP_LEN, G * d, G * CMP_HID), w1.dtype)
        for gi in range(G):
            w1bd = w1bd.at[:, gi * d:(gi + 1) * d, gi * CMP_HID:(gi + 1) * CMP_HID].set(w1)
        return jnp.tile(pe, (1, G)), w1bd.astype(BF16)

    pek, w1k = prep(k_pe, k_w1)
    pev, w1v = prep(v_pe, v_w1)

    def full(a):
        return pl.BlockSpec(a.shape, lambda b: (0,) * a.ndim)

    w2k, w2v = k_w2.astype(BF16), v_w2.astype(BF16)
    return pl.pallas_call(
        _compress_kernel, grid=(B,),
        in_specs=[pl.BlockSpec((None, S, G * d), lambda b: (b, 0, kc_block)),
                  pl.BlockSpec((None, S, G * d), lambda b: (b, 0, vc_block)),
                  full(pek), full(pev), full(w1k), full(w1v), full(w2k), full(w2v)],
        out_specs=[pl.BlockSpec((None, G, nh, d), lambda b: (b, 0, 0, 0)),
                   pl.BlockSpec((None, G, d, nh), lambda b: (b, 0, 0, 0))],
        out_shape=[jax.ShapeDtypeStruct((B, G, nh, d), BF16), jax.ShapeDtypeStruct((B, G, d, nh), BF16)],
        compiler_params=_cparams(("parallel",)), name="nsa_compress",
    )(zf, zf, pek, pev, w1k, w1v, w2k, w2v)


def _nsa_kernel(q_ref, gt_ref, gb_ref, kc_ref, vct_ref, ks_ref, vst_ref, kw_ref, vwt_ref, ovl_ref, aug_ref, caug_ref,
                tail_ref, ctail_ref, wmask_ref, o_ref, ksa_s, kwa_s, kca_s, vsa_s, vwa_s, ss_s, ps_s, sw_s, pw_s, *, n_cmp):
    TQ, TK, d, HPG, G = NSA_TQ, NSA_TK, HEAD_DIM, NSA_HPG, NSA_KV_HEADS
    R = HPG * TQ
    groups = range(G)
    qi = pl.program_id(1)
    q0 = qi * TQ
    kt_d = q0 // TK
    n_cb = kc_ref.shape[1]
    n_sb = ovl_ref.shape[0]

    @pl.when(qi == 0)
    def _():
        for g in groups:
            ksa_s[g, :, 0:d] = ks_ref[:, g * d:(g + 1) * d]
            ksa_s[g, :, d:] = aug_ref[...]
            kwa_s[g, :, 0:d] = kw_ref[:, g * d:(g + 1) * d]
            kwa_s[g, :, d:] = aug_ref[...]
            kca_s[g, :, 0:d] = kc_ref[g]
            kca_s[g, :, d:] = caug_ref[...]
            vsa_s[g, 0:d, :] = vst_ref[g * d:(g + 1) * d, :]
            vsa_s[g, d:, :] = jnp.ones((vsa_s.shape[1] - d, vsa_s.shape[2]), BF16)
            vwa_s[g, 0:d, :] = vwt_ref[g * d:(g + 1) * d, :]
            vwa_s[g, d:, :] = jnp.ones((vwa_s.shape[1] - d, vwa_s.shape[2]), BF16)

    def slope(g, hh):
        return 2.0 ** (-(g * HPG + hh + 1))

    def per_head(fn):
        return jnp.concatenate([fn(hh) for hh in range(HPG)], axis=1)

    def tile_heads(x):
        return jnp.concatenate([x] * HPG, axis=1)

    q_t = [(q_ref[:, g * HPG * d:(g + 1) * HPG * d].astype(F32) * (LOG2E * 0.125)).T for g in groups]

    def q_head(g, hh):
        return q_t[g][hh * d:(hh + 1) * d]

    def q_aug(g, block_rows):
        return per_head(lambda hh: jnp.concatenate(
            [q_head(g, hh), block_rows, tail_ref[...] * slope(g, hh)], axis=0)).astype(BF16)

    def scores(ka, qa_t, kt):
        return _dot(ka[pl.ds(pl.multiple_of(kt * TK, TK), TK), :], qa_t)

    def stage_scores(buf, ka, qa_t, kt, mask_add=None):
        s = scores(ka, qa_t, kt)
        if mask_add is not None:
            s = s + mask_add
        buf[...] = s
        return jnp.max(s, axis=0, keepdims=True)

    def stage_probs(sbuf, pbuf, tile_max, m):
        m_new = jnp.maximum(m, tile_max)
        pbuf[...] = jnp.exp2(sbuf[...] - m_new).astype(BF16)
        return m_new, jnp.exp2(m - m_new)

    def stage_values(pbuf, va, kt, alpha, acc):
        return alpha * acc + _dot(va[:, pl.ds(pl.multiple_of(kt * TK, TK), TK)], pbuf[...])

    def normalised(acc):
        return acc[0:d] * (1.0 / acc[d:d + 1])

    m0, acc0 = jnp.full((1, R), NEG_BIG, F32), jnp.zeros((vsa_s.shape[1], R), F32)

    qc_t = [per_head(lambda hh: jnp.concatenate([q_head(g, hh), ctail_ref[...] * slope(g, hh)], axis=0)).astype(BF16)
            for g in groups]
    qw_t = [q_aug(g, jnp.zeros((n_sb, TQ), F32)) for g in groups]
    n_r = lax.broadcasted_iota(jnp.int32, (n_cb, TQ), 0)
    t_c = q0 + lax.broadcasted_iota(jnp.int32, (n_cb, TQ), 1)
    ok_c = (n_r * CMP_STRIDE + (CMP_LEN - 1) <= t_c) & (n_r < n_cmp)
    add_c, keep_c = tile_heads(jnp.where(ok_c, 0.0, NEG_BIG)), tile_heads(jnp.where(ok_c, 1.0, 0.0))
    s_c = [_dot(kca_s[g], qc_t[g]) + add_c for g in groups]
    n_win = (WINDOW - 1 + TK - 1) // TK + 1
    kt_win, max_win = [], []
    for back in range(n_win):
        kt_raw = kt_d - back
        kt_win.append(jnp.maximum(kt_raw, 0))
        if back == 0:
            mask_add = tile_heads(wmask_ref[0])
        elif (back + 1) * TK <= WINDOW:
            mask_add = jnp.where(kt_raw >= 0, 0.0, NEG_BIG)
        else:
            mask_add = tile_heads(wmask_ref[back] + jnp.where(kt_raw >= 0, 0.0, NEG_BIG))
        max_win.append([stage_scores(sw_s.at[g, back], kwa_s.at[g], qw_t[g], kt_win[back], mask_add) for g in groups])

    e_c = [jnp.exp2(s_c[g] - jnp.max(s_c[g], axis=0, keepdims=True)) * keep_c for g in groups]
    p_c = [e_c[g] * (1.0 / jnp.maximum(jnp.sum(e_c[g], axis=0, keepdims=True), 1.0)) for g in groups]
    o_c = [_dot(vct_ref[g], p_c[g].astype(BF16)) for g in groups]

    jb = lax.broadcasted_iota(jnp.int32, (n_sb, TQ), 0)
    cur = (q0 + lax.broadcasted_iota(jnp.int32, (n_sb, TQ), 1)) // SEL_LEN
    valid = jb <= cur
    forced = (jb == 0) | (jb == cur) | (jb == cur - 1)
    sub = lax.broadcasted_iota(jnp.int32, (8, TQ), 0)
    n_grp = n_sb // 8
    score, grp, rank = [], [], []
    for g in groups:
        p_sum = p_c[g][:, 0:TQ]
        for hh in range(1, HPG):
            p_sum = p_sum + p_c[g][:, hh * TQ:(hh + 1) * TQ]
        imp = jnp.dot(ovl_ref[...], p_sum, preferred_element_type=F32, precision=HIGHEST)
        score.append(jnp.where(forced, jnp.inf, jnp.where(valid, imp, -jnp.inf)))
        grp.append([score[g][8 * a:8 * a + 8] for a in range(n_grp)])
        rank.append([jnp.zeros((8, TQ), jnp.int32) for _ in range(n_grp)])
    for j in range(n_sb):
        for g in groups:
            r = score[g][j:j + 1, :]
            for a in range(n_grp):
                if a > j // 8:
                    ahead = (r >= grp[g][a]).astype(jnp.int32)
                elif a < j // 8:
                    ahead = (r > grp[g][a]).astype(jnp.int32)
                else:
                    ahead = jnp.where(sub > j % 8, (r >= grp[g][a]).astype(jnp.int32), (r > grp[g][a]).astype(jnp.int32))
                rank[g][a] = rank[g][a] + ahead
    picked = [valid & (jnp.concatenate(rank[g], axis=0) < SEL_TOPN) for g in groups]
    qs_t = [q_aug(g, jnp.where(picked[g], 0.0, NEG_BIG)) for g in groups]

    filler = 1 - kt_d % 2
    n_seq = kt_d + 1 + filler

    def sel_tile(i):
        return jnp.where(i == 0, kt_d, jnp.maximum(i - 1 - filler, 0))

    def a_stage(slot, kt, mask_add=None):
        return [stage_scores(ss_s.at[g, slot], ksa_s.at[g], qs_t[g], kt, mask_add) for g in groups]

    def b_stage(slot, tmax, m):
        out = [stage_probs(ss_s.at[g, slot], ps_s.at[g, slot], tmax[g], m[g]) for g in groups]
        return [o[0] for o in out], [o[1] for o in out]

    def c_stage(slot, kt, alpha, acc):
        return [stage_values(ps_s.at[g, slot], vsa_s.at[g], kt, alpha[g], acc[g]) for g in groups]

    tmax0 = a_stage(0, kt_d, tile_heads(wmask_ref[0]))
    m, alpha = b_stage(0, tmax0, [m0] * G)
    tmax1 = a_stage(1, sel_tile(1), jnp.where(filler == 1, NEG_BIG, 0.0))

    mw, accw = [m0] * G, [acc0] * G
    for back in range(n_win):
        outw = [stage_probs(sw_s.at[g, back], pw_s.at[g, back], max_win[back][g], mw[g]) for g in groups]
        mw = [o[0] for o in outw]
        accw = [stage_values(pw_s.at[g, back], vwa_s.at[g], kt_win[back], outw[g][1], accw[g]) for g in groups]
    o_w = [normalised(accw[g]) for g in groups]

    def sel_body(k, carry):
        m, alpha, acc, tmax1 = carry
        i = 2 * k
        tmax0 = a_stage(0, sel_tile(i + 2))
        m, alpha1 = b_stage(1, tmax1, m)
        tmax1 = a_stage(1, sel_tile(i + 3))
        acc = c_stage(0, sel_tile(i), alpha, acc)
        m, alpha2 = b_stage(0, tmax0, m)
        acc = c_stage(1, sel_tile(i + 1), alpha1, acc)
        return m, alpha2, acc, tmax1

    m, alpha, acc, tmax1 = lax.fori_loop(0, (n_seq - 2) // 2, sel_body, (m, alpha, [acc0] * G, tmax1))
    m, alpha1 = b_stage(1, tmax1, m)
    acc = c_stage(0, sel_tile(n_seq - 2), alpha, acc)
    acc = c_stage(1, sel_tile(n_seq - 1), alpha1, acc)
    o_s = [normalised(acc[g]) for g in groups]

    gates = _sigmoid(gt_ref[...] + gb_ref[...]).T
    outs = []
    for g in groups:
        for hh in range(HPG):
            cols = slice(hh * TQ, (hh + 1) * TQ)
            row = 16 + 3 * (g * HPG + hh)
            outs.append(gates[row:row + 1] * o_c[g][:, cols] + gates[row + 1:row + 2] * o_s[g][:, cols]
                        + gates[row + 2:row + 3] * o_w[g][:, cols])
    o_ref[...] = jnp.concatenate(outs, axis=0).T.astype(o_ref.dtype)


def nsa_block(zt, zb, zf, gate_bias, kcmp, vcmp_t, q_col_block, k_col_blocks, v_row_blocks):
    B, S, _ = zb.shape
    G, d = NSA_KV_HEADS, HEAD_DIM
    TQ = NSA_TQ
    n_cb = kcmp.shape[2]
    n_cmp = (S - CMP_LEN) // CMP_STRIDE + 1
    n_sb = S // SEL_LEN
    n_terms = len(LOG2E_TERMS)
    kaug = -(-(d + n_sb + 2 * n_terms) // LANES) * LANES
    R = NSA_HPG * TQ
    n_win = (WINDOW - 1 + NSA_TK - 1) // NSA_TK + 1
    assert S % NSA_TK == 0 and NSA_TK == TQ and TQ % SEL_LEN == 0

    cidx = np.arange(n_cb)[None, :] * CMP_STRIDE
    sstart = np.arange(n_sb)[:, None] * SEL_LEN
    ovl = ((cidx < sstart + SEL_LEN) & (cidx + CMP_LEN - 1 >= sstart) & (np.arange(n_cb)[None, :] < n_cmp))
    ovl = jnp.asarray(ovl.astype(np.float32))
    pos = np.arange(S)
    aug = np.zeros((S, kaug - d), np.float32)
    aug[pos, pos // SEL_LEN] = 1.0
    tail = np.zeros((kaug - d - n_sb, TQ), np.float32)
    caug = np.zeros((n_cb, d), np.float32)
    ctail = np.zeros((d, TQ), np.float32)
    for i, term in enumerate(LOG2E_TERMS):
        aug[:, n_sb + i] = pos // SEL_LEN
        aug[:, n_sb + n_terms + i] = pos % SEL_LEN
        tail[i] = term * SEL_LEN
        tail[n_terms + i] = term
        caug[:, i] = np.arange(n_cb)
        ctail[i] = term * CMP_STRIDE
    aug, caug, tail, ctail = jnp.asarray(aug, BF16), jnp.asarray(caug, BF16), jnp.asarray(tail), jnp.asarray(ctail)
    dist = np.arange(n_win)[:, None, None] * NSA_TK + np.arange(TQ)[None, None, :] - np.arange(NSA_TK)[None, :, None]
    wmask = jnp.asarray(np.where((dist >= 0) & (dist < WINDOW), 0.0, NEG_BIG).astype(np.float32))

    def kspec(j):
        return pl.BlockSpec((None, S, G * d), lambda b, i: (b, 0, k_col_blocks[j]))

    def vspec(j):
        return pl.BlockSpec((None, G * d, S), lambda b, i: (b, v_row_blocks[j], 0))

    def const(a):
        return pl.BlockSpec(a.shape, lambda b, i: (0,) * a.ndim)

    W = G * NSA_HPG * d
    return pl.pallas_call(
        functools.partial(_nsa_kernel, n_cmp=n_cmp), grid=(B, S // TQ),
        in_specs=[pl.BlockSpec((None, TQ, W), lambda b, i: (b, i, q_col_block)),
                  pl.BlockSpec((None, TQ, LANES), lambda b, i: (b, i, 6)),
                  const(gate_bias),
                  pl.BlockSpec((None, G, n_cb, d), lambda b, i: (b, 0, 0, 0)),
                  pl.BlockSpec((None, G, d, n_cb), lambda b, i: (b, 0, 0, 0)),
                  kspec(0), vspec(0), kspec(1), vspec(1), const(ovl), const(aug), const(caug), const(tail),
                  const(ctail), const(wmask)],
        out_specs=pl.BlockSpec((None, TQ, W), lambda b, i: (b, i, 0)),
        out_shape=jax.ShapeDtypeStruct((B, S, NSA_W), BF16),
        scratch_shapes=[pltpu.VMEM((G, S, kaug), BF16), pltpu.VMEM((G, S, kaug), BF16),
                        pltpu.VMEM((G, n_cb, 2 * d), BF16),
                        pltpu.VMEM((G, d + NSA_ONES, S), BF16), pltpu.VMEM((G, d + NSA_ONES, S), BF16),
                        pltpu.VMEM((G, 2, NSA_TK, R), F32), pltpu.VMEM((G, 2, NSA_TK, R), BF16),
                        pltpu.VMEM((G, n_win, NSA_TK, R), F32), pltpu.VMEM((G, n_win, NSA_TK, R), BF16)],
        compiler_params=_cparams(("parallel", "arbitrary")), name="nsa",
    )(zb, zf, gate_bias, kcmp, vcmp_t, zb, zt, zb, zt, ovl, aug, caug, tail, ctail, wmask)


def _rglru_kernel(gate_ref, x_ref, halo_ref, cw_ref, cb_ref, wa_ref, wx_ref, ba_ref, bx_ref, lam_ref, o_ref, h_s,
                  au_s):
    T, W = x_ref.shape
    first = pl.program_id(1) == 0

    @pl.when(first)
    def _():
        h_s[...] = jnp.zeros_like(h_s)

    xe = jnp.concatenate([jnp.where(first, 0.0, halo_ref[...]), x_ref[...]], axis=0)
    xc = cb_ref[...] + cw_ref[3:4, :] * xe
    for k in range(1, LRU_CONV):
        xc = xc + cw_ref[3 - k:4 - k, :] * pltpu.roll(xe, k, 0)
    xc = xc[LRU_HALO:, :]
    xcb = xc.astype(BF16)
    half = W // 2

    def blockdiag(w_ref):
        return jnp.concatenate([_dot(xcb[:, :half], w_ref[0]), _dot(xcb[:, half:], w_ref[1])], axis=1)

    r = _sigmoid(blockdiag(wa_ref) + ba_ref[...])
    i = _sigmoid(blockdiag(wx_ref) + bx_ref[...])
    nl = -lam_ref[...]
    softplus = jnp.maximum(nl, 0.0) + jnp.log1p(jnp.exp(-jnp.abs(nl)))
    log_a = -LRU_C * r * softplus
    a = jnp.exp(log_a)
    one_m_a2 = -jnp.tanh(log_a) * (a * a + 1.0)
    u = jnp.where(one_m_a2 > 0.0, one_m_a2 * lax.rsqrt(one_m_a2), 0.0) * (i * xc)

    n_grp, n_slab = T // 8, W // LANES

    def phases(x, slab):
        for c in range(n_slab):
            au_s[slab, c] = x[:, c * LANES:(c + 1) * LANES]
        return [jnp.concatenate([au_s[slab, c, pl.ds(j, n_grp, stride=8), :] for c in range(n_slab)], axis=1)
                for j in range(8)]

    a_ph, u_ph = phases(a, 0), phases(u, 1)
    prod, part = [a_ph[0]], [u_ph[0]]
    for j in range(1, 8):
        part.append(a_ph[j] * part[-1] + u_ph[j])
        prod.append(a_ph[j] * prod[-1])
    grp = lax.broadcasted_iota(jnp.int32, (n_grp, W), 0)
    ag, ug = prod[7], part[7]
    sft = 1
    while sft < n_grp:
        keep = grp >= sft
        ug = ag * jnp.where(keep, pltpu.roll(ug, sft, 0), 0.0) + ug
        ag = ag * jnp.where(keep, pltpu.roll(ag, sft, 0), 1.0)
        sft *= 2
    h_prev = h_s[0:1, :]
    hg = ug + ag * h_prev
    carry_in = jnp.where(grp >= 1, pltpu.roll(hg, 1, 0), h_prev)
    for j in range(8):
        hj = part[j] + prod[j] * carry_in
        for c in range(n_slab):
            au_s[0, c, pl.ds(j, n_grp, stride=8), :] = hj[:, c * LANES:(c + 1) * LANES]
    h = jnp.concatenate([au_s[0, c] for c in range(n_slab)], axis=1)
    h_s[...] = jnp.broadcast_to(hg[n_grp - 1:n_grp, :], h_s.shape)
    o_ref[...] = (_gelu(gate_ref[...]) * h).astype(o_ref.dtype)


def rglru_block(z, conv_w, conv_b, wa, ba, wx, bx, lam, T=LRU_T):
    B, S, _ = z.shape
    W = LRU_W
    half = W // 2
    hb = T // LRU_HALO

    def bd(w):
        blocks = [jax.scipy.linalg.block_diag(*[w[h] for h in range(4 * j, 4 * j + 4)]) for j in range(2)]
        return jnp.stack(blocks).astype(BF16)

    vec = lambda a: a.reshape(1, W)
    vspec = pl.BlockSpec((1, W), lambda b, t: (0, 0))
    wspec = pl.BlockSpec((2, half, half), lambda b, t: (0, 0, 0))
    return pl.pallas_call(
        _rglru_kernel, grid=(B, S // T),
        in_specs=[pl.BlockSpec((None, T, W), lambda b, t: (b, t, 0)),
                  pl.BlockSpec((None, T, W), lambda b, t: (b, t, 1)),
                  pl.BlockSpec((None, LRU_HALO, W), lambda b, t: (b, jnp.maximum(t * hb - 1, 0), 1)),
                  pl.BlockSpec((LRU_CONV, W), lambda b, t: (0, 0)), vspec, wspec, wspec, vspec, vspec, vspec],
        out_specs=pl.BlockSpec((None, T, W), lambda b, t: (b, t, 0)),
        out_shape=jax.ShapeDtypeStruct((B, S, W), BF16),
        scratch_shapes=[pltpu.VMEM((8, W), F32), pltpu.VMEM((2, W // LANES, T, LANES), F32)],
        compiler_params=_cparams(("parallel", "arbitrary")), name="rglru",
    )(z, z, z, conv_w, vec(conv_b), bd(wa), bd(wx), vec(ba), vec(bx), vec(lam))


def _sgu_kernel(u_ref, v_ref, g_ref, b_ref, w_ref, bias_ref, o_ref):
    C, W = SG_CHUNK, v_ref.shape[1]
    dg = W // SG_GROUPS
    v = _gelu(v_ref[...])
    mu = jnp.mean(v, axis=-1, keepdims=True)
    vc = v - mu
    vn = (vc * lax.rsqrt(jnp.mean(vc * vc, axis=-1, keepdims=True) + EPS) * g_ref[...] + b_ref[...]).astype(BF16)
    row = lax.broadcasted_iota(jnp.int32, (C, C), 0)
    col = lax.broadcasted_iota(jnp.int32, (C, C), 1)
    for gi in range(SG_GROUPS):
        sl = slice(gi * dg, (gi + 1) * dg)
        wc = jnp.where(col <= row, w_ref[gi], 0.0).astype(BF16)
        for c in range(v_ref.shape[0] // C):
            rows = slice(c * C, (c + 1) * C)
            mixed = _dot(wc, vn[rows, sl]) + bias_ref[:, sl]
            o_ref[rows, sl] = (_gelu(u_ref[rows, sl]) * mixed).astype(o_ref.dtype)


def sgu_block(z, ln_g, ln_b, w, b):
    B, S, _ = z.shape
    W, C, T = SG_W, SG_CHUNK, SG_ROWS
    assert S % T == 0 and T % C == 0
    bias = jnp.repeat(b.T, W // SG_GROUPS, axis=1)
    vspec = pl.BlockSpec((1, W), lambda bb, c: (0, 0))
    return pl.pallas_call(
        _sgu_kernel, grid=(B, S // T),
        in_specs=[pl.BlockSpec((None, T, W), lambda bb, c: (bb, c, 2)),
                  pl.BlockSpec((None, T, W), lambda bb, c: (bb, c, 3)),
                  vspec, vspec,
                  pl.BlockSpec((SG_GROUPS, C, C), lambda bb, c: (0, 0, 0)),
                  pl.BlockSpec((C, W), lambda bb, c: (0, 0))],
        out_specs=pl.BlockSpec((None, T, W), lambda bb, c: (bb, c, 0)),
        out_shape=jax.ShapeDtypeStruct((B, S, W), BF16),
        compiler_params=_cparams(("parallel", "parallel")), name="sgu",
    )(z, z, ln_g.reshape(1, W), ln_b.reshape(1, W), w, bias)


def _mixer_ab(h2, B, S, pre_g, w_in, ml_gate_b, ml_norm_g, nsa_gate_b, k_pe, k_w1, k_w2, v_pe, v_w1, v_w2):
    D = h2.shape[1]
    G, d = NSA_KV_HEADS, HEAD_DIM
    offs = np.cumsum([0, ML_W, ML_W, ML_W, ML_W, 2 * ML_HEADS, NSA_W] + [G * d] * 6 + [3 * NSA_HEADS])
    mq, mk, mv, mo, mif, nq, kc, vc, ks, vs, kw, vw, ng = [w_in[:, offs[i]:offs[i + 1]] for i in range(13)]
    w_b = jnp.concatenate([mk, nq, ks, kw], axis=1).astype(BF16)
    w_t = jnp.concatenate([mq, mv, vs, vw], axis=1).T.astype(BF16)
    gpad = LANES - 2 * ML_HEADS - 3 * NSA_HEADS
    w_f = jnp.concatenate([mo, kc, vc, mif, ng, jnp.zeros((D, gpad), w_in.dtype)], axis=1).astype(BF16)
    gate_bias = jnp.concatenate([ml_gate_b, nsa_gate_b, jnp.zeros((gpad,), F32)]).reshape(1, LANES)
    zb, zf, zt = norm_proj(h2, pre_g, [w_b, w_f], [BF16, F32], wts=[w_t], batch=B)
    zb = zb.reshape(B, S, -1)
    zf = zf.reshape(B, S, -1)
    h_ml = mlstm_block(zt, zb, zf, gate_bias, ml_norm_g)
    kcmp, vcmp_t = compress_block(zf, ML_W // (G * d), ML_W // (G * d) + 1, k_pe, k_w1, k_w2, v_pe, v_w1, v_w2)
    h_nsa = nsa_block(zt, zb, zf, gate_bias, kcmp, vcmp_t, q_col_block=ML_W // NSA_W,
                      k_col_blocks=((ML_W + NSA_W) // (G * d), (ML_W + NSA_W) // (G * d) + 1),
                      v_row_blocks=(2 * ML_W // (G * d), 2 * ML_W // (G * d) + 1))
    return h_ml.reshape(B * S, ML_W), h_nsa.reshape(B * S, NSA_W)


def _mixer_cd(h2, B, S, pre_g, w_in, conv_w, conv_b, wa, ba, wx, bx, lam, sg_g, sg_bn, sg_w, sg_b):
    (z,) = norm_proj(h2, pre_g, [w_in.astype(BF16)], [F32])
    z = z.reshape(B, S, -1)
    y_lru = rglru_block(z, conv_w, conv_b, wa, ba, wx, bx, lam)
    y_sg = sgu_block(z, sg_g, sg_bn, sg_w, sg_b)
    return y_lru.reshape(B * S, LRU_W), y_sg.reshape(B * S, SG_W)


def kernel(x, pre_mix_g, post_mix_g, pre_ffn_g, post_ffn_g, ab_w_in, ab_w_out, ml_gate_b, ml_norm_g, nsa_gate_b, cmp_k_pe, cmp_k_w1, cmp_k_w2, cmp_v_pe, cmp_v_w1, cmp_v_w2, cd_w_in, cd_w_out, lru_conv_w, lru_conv_b, lru_wa, lru_ba, lru_wx, lru_bx, lru_lambda, sg_norm_g, sg_norm_b, sg_w, sg_b, ffn_w_up, ffn_conv_w, ffn_conv_b, ffn_w_down):
    B, S, D = x.shape
    depth = pre_mix_g.shape[0]
    h2 = x.reshape(B * S, D)
    for layer in range(depth):
        if layer % 2 == 0:
            e = layer // 2
            a1, a2 = _mixer_ab(h2, B, S, pre_mix_g[layer], ab_w_in[e], ml_gate_b[e], ml_norm_g[e], nsa_gate_b[e],
                               cmp_k_pe[e], cmp_k_w1[e], cmp_k_w2[e], cmp_v_pe[e], cmp_v_w1[e], cmp_v_w2[e])
            w_out = ab_w_out[e]
        else:
            o = layer // 2
            a1, a2 = _mixer_cd(h2, B, S, pre_mix_g[layer], cd_w_in[o], lru_conv_w[o], lru_conv_b[o], lru_wa[o],
                               lru_ba[o], lru_wx[o], lru_bx[o], lru_lambda[o], sg_norm_g[o], sg_norm_b[o], sg_w[o],
                               sg_b[o])
            w_out = cd_w_out[o]
        h2 = mix_ffn_block(h2, a1, a2, S, w_out, post_mix_g[layer], pre_ffn_g[layer], ffn_w_up[layer],
                           ffn_conv_w[layer], ffn_conv_b[layer], ffn_w_down[layer], post_ffn_g[layer])
    return h2.reshape(B, S, D)
```

```python
import functools

import numpy as np
import jax
import jax.numpy as jnp
from jax import lax
from jax.experimental import pallas as pl
from jax.experimental.pallas import tpu as pltpu

F32 = jnp.float32
BF16 = jnp.bfloat16

EPS = 1e-6
HEAD_DIM = 64
ML_HEADS = 8
ML_W = 512
GATE_SOFTCAP = 15.0
NSA_HEADS = 8
NSA_KV_HEADS = 2
NSA_HPG = NSA_HEADS // NSA_KV_HEADS
NSA_W = 512
CMP_LEN = 32
CMP_STRIDE = 16
CMP_HID = 128
SEL_LEN = 64
SEL_TOPN = 16
WINDOW = 512
LRU_W = 512
LRU_C = 8.0
LRU_CONV = 4
SG_GROUPS = 8
SG_W = 512
SG_CHUNK = 128
FFN_CONV = 3

LANES = 128
VMEM_LIMIT = 56 * 1024 * 1024
NEG_BIG = -1e30
HIGHEST = lax.Precision.HIGHEST
LOG2E = 1.4426950408889634


def _bf16_terms(x, n):
    terms = []
    for _ in range(n):
        bits = int(np.array(x, np.float32).view(np.uint32))
        t = float(np.array((bits + 0x7FFF + ((bits >> 16) & 1)) & 0xFFFF0000, np.uint32).view(np.float32))
        terms.append(t)
        x -= t
    return tuple(terms)


LOG2E_TERMS = _bf16_terms(LOG2E, 3)

ML_CHUNK = 128
ML_NB = 4
NSA_TQ = 256
NSA_TK = 256
NSA_ONES = 16
ROW_TILE = 512
FFN_TM = 512
FFN_SUB = 256
FFN_CK = 256
FFN_HALO = 16
SG_ROWS = 1024
SG_SUB = 256
LRU_T = 256
LRU_HALO = 8
LRU_NB = 4


def _cparams(sem):
    return pltpu.CompilerParams(dimension_semantics=sem, vmem_limit_bytes=VMEM_LIMIT)


def _rms(x, g):
    return x * lax.rsqrt(jnp.mean(x * x, axis=-1, keepdims=True) + EPS) * g


def _gelu(x):
    return 0.5 * x * (1.0 + jnp.tanh(0.7978845608028654 * (x + 0.044715 * (x * x * x))))


def _sigmoid(x):
    return 1.0 / (1.0 + jnp.exp(-x))


def _dot(a, b):
    return jnp.dot(a, b, preferred_element_type=F32)


def _dot_nt(a, b, precision=None):
    return lax.dot_general(a, b, (((1,), (1,)), ((), ())), preferred_element_type=F32, precision=precision)


def _dot_tn(a, b):
    return lax.dot_general(a, b, (((0,), (0,)), ((), ())), preferred_element_type=F32)


def _norm_proj_kernel(h_ref, g_ref, *refs, n_row, n_t, cn):
    w_refs, wt_refs = refs[:n_row], refs[n_row:n_row + n_t]
    o_refs, ot_refs = refs[n_row + n_t:2 * n_row + n_t], refs[2 * n_row + n_t:]
    xn = _rms(h_ref[...], g_ref[...]).astype(BF16)
    for w_ref, o_ref in zip(w_refs, o_refs):
        n = w_ref.shape[1]
        for c in range(0, n, cn):
            ce = min(c + cn, n)
            o_ref[:, c:ce] = _dot(xn, w_ref[:, c:ce]).astype(o_ref.dtype)
    for wt_ref, ot_ref in zip(wt_refs, ot_refs):
        n = wt_ref.shape[0]
        for c in range(0, n, cn):
            ce = min(c + cn, n)
            ot_ref[c:ce, :] = _dot_nt(wt_ref[c:ce, :], xn).astype(ot_ref.dtype)


def norm_proj(h2, g, ws, dtypes, wts=(), batch=1, tm=ROW_TILE):
    M, D = h2.shape
    tps = M // batch // tm
    in_specs = [pl.BlockSpec((tm, D), lambda i: (i, 0)), pl.BlockSpec((1, D), lambda i: (0, 0))]
    in_specs += [pl.BlockSpec(w.shape, lambda i: (0, 0)) for w in (*ws, *wts)]
    out_specs = [pl.BlockSpec((tm, w.shape[1]), lambda i: (i, 0)) for w in ws]
    out_specs += [pl.BlockSpec((None, w.shape[0], tm), lambda i: (i // tps, 0, i % tps)) for w in wts]
    out_shape = [jax.ShapeDtypeStruct((M, w.shape[1]), dt) for w, dt in zip(ws, dtypes)]
    out_shape += [jax.ShapeDtypeStruct((batch, w.shape[0], M // batch), BF16) for w in wts]
    return pl.pallas_call(
        functools.partial(_norm_proj_kernel, n_row=len(ws), n_t=len(wts), cn=512),
        grid=(M // tm,), in_specs=in_specs, out_specs=out_specs, out_shape=out_shape,
        compiler_params=_cparams(("parallel",)), name="norm_proj",
    )(h2, g.reshape(1, D), *ws, *wts)


def _mix_ffn_kernel(h_ref, hh_ref, a1_ref, a1h_ref, a2_ref, a2h_ref, wo1_ref, wo2_ref, gmix_ref, gpre_ref, wu_ref,
                    cw_ref, cb_ref, wd_ref, gpost_ref, o_ref, xn_s, acc_s, *, tiles_per_seq, ck):
    F = wd_ref.shape[0]
    n_chunks = F // ck
    n_sub = h_ref.shape[0] // FFN_SUB
    first = (pl.program_id(0) % tiles_per_seq) == 0
    g = gpre_ref[...]

    def mixed(h, a1, a2):
        return h + _rms(_dot(a1, wo1_ref[...]) + _dot(a2, wo2_ref[...]), gmix_ref[...])

    def conv(u, cols):
        y = (cw_ref[2:3, cols] * u + cw_ref[1:2, cols] * pltpu.roll(u, 1, 0) + cw_ref[0:1, cols] * pltpu.roll(u, 2, 0)
             + cb_ref[:, cols])
        return y[FFN_HALO:, :]

    def cols_of(c, half):
        return slice(half * F + c * ck, half * F + (c + 1) * ck)

    x = {}

    def head(j):
        rows = slice(j * FFN_SUB, (j + 1) * FFN_SUB)
        x[j] = mixed(h_ref[rows, :], a1_ref[rows, :], a2_ref[rows, :])
        if j == 0:
            halo = jnp.where(first, 0.0, _rms(mixed(hh_ref[...], a1h_ref[...], a2h_ref[...]), g))
            xn_s[0:FFN_HALO, :] = halo.astype(BF16)
        xn_s[FFN_HALO + j * FFN_SUB:FFN_HALO + (j + 1) * FFN_SUB, :] = _rms(x[j], g).astype(BF16)
        acc_s[rows, :] = jnp.zeros((FFN_SUB, acc_s.shape[1]), F32)

    def up(j, c):
        xn = xn_s[j * FFN_SUB:(j + 1) * FFN_SUB + FFN_HALO, :]
        return _dot(xn, wu_ref[:, cols_of(c, 0)]), _dot(xn, wu_ref[:, cols_of(c, 1)])

    def tail(j):
        rows = slice(j * FFN_SUB, (j + 1) * FFN_SUB)
        o_ref[rows, :] = x[j] + _rms(acc_s[rows, :], gpost_ref[...])

    head(0)
    for j in range(n_sub):
        u = up(j, 0)
        for c in range(n_chunks):
            u_next = up(j, c + 1) if c + 1 < n_chunks else None
            if c == 0 and j + 1 < n_sub:
                head(j + 1)
            if c == n_chunks // 3 and j >= 1:
                tail(j - 1)
            act = (_gelu(conv(u[0], cols_of(c, 0))) * conv(u[1], cols_of(c, 1))).astype(BF16)
            acc_s[j * FFN_SUB:(j + 1) * FFN_SUB, :] += _dot(act, wd_ref[c * ck:(c + 1) * ck, :])
            u = u_next
    tail(n_sub - 1)


def mix_ffn_block(h2, a1, a2, seq_len, w_out, g_mix, g_pre, w_up, conv_w, conv_b, w_down, g_post, tm=FFN_TM, ck=FFN_CK):
    M, D = h2.shape
    F = w_down.shape[0]
    K1, K2 = a1.shape[1], a2.shape[1]
    assert F % ck == 0 and seq_len % tm == 0 and tm % FFN_SUB == 0 and FFN_SUB % FFN_HALO == 0
    hb = tm // FFN_HALO
    consts = (w_out[:K1].astype(BF16), w_out[K1:].astype(BF16), g_mix.reshape(1, D), g_pre.reshape(1, D),
              w_up.astype(BF16), conv_w, conv_b.reshape(1, 2 * F), w_down.astype(BF16), g_post.reshape(1, D))

    def tile(width):
        return pl.BlockSpec((tm, width), lambda i: (i, 0))

    def halo(width):
        return pl.BlockSpec((FFN_HALO, width), lambda i: (jnp.maximum(i * hb - 1, 0), 0))

    def full(a):
        return pl.BlockSpec(a.shape, lambda i: (0,) * a.ndim)

    return pl.pallas_call(
        functools.partial(_mix_ffn_kernel, tiles_per_seq=seq_len // tm, ck=ck),
        grid=(M // tm,),
        in_specs=[tile(D), halo(D), tile(K1), halo(K1), tile(K2), halo(K2)] + [full(a) for a in consts],
        out_specs=tile(D),
        out_shape=jax.ShapeDtypeStruct((M, D), F32),
        scratch_shapes=[pltpu.VMEM((tm + FFN_HALO, D), BF16), pltpu.VMEM((tm, D), F32)],
        compiler_params=_cparams(("parallel",)), name="mix_ffn",
    )(h2, h2, a1, a1, a2, a2, *consts)


def _mlstm_kernel(qt_ref, vt_ref, k_ref, mo_ref, gt_ref, gb_ref, ng_ref, o_ref, cn_s, m_s):
    NB, _, L = qt_ref.shape
    d = HEAD_DIM

    @pl.when(pl.program_id(1) == 0)
    def _():
        cn_s[...] = jnp.zeros_like(cn_s)
        m_s[...] = jnp.zeros_like(m_s)

    src = lax.broadcasted_iota(jnp.int32, (L, L), 0)
    tgt = lax.broadcasted_iota(jnp.int32, (L, L), 1)
    causal = src <= tgt
    tri = (tgt <= src).astype(F32)

    gcap = [GATE_SOFTCAP * jnp.tanh((gt_ref[nb] + gb_ref[...]) * (1.0 / GATE_SOFTCAP)) for nb in range(NB)]
    lf = [jnp.minimum(x, 0.0) - jnp.log1p(jnp.exp(-jnp.abs(x))) for x in gcap]
    b_col = [jnp.dot(tri, x, preferred_element_type=F32, precision=HIGHEST) for x in lf]
    b_row = [_dot_nt(x.T, tri, precision=HIGHEST) for x in lf]
    i_row = [x.T for x in gcap]
    c_col = [b_col[nb] - pltpu.roll(gcap[nb], ML_HEADS, 1) for nb in range(NB)]

    chains = [(nb, h) for nb in range(NB) for h in range(ML_HEADS)]
    n = range(len(chains))

    def rows(h):
        return slice(h * d, (h + 1) * d)

    q_t = [qt_ref[nb, rows(h), :] for nb, h in chains]
    v_t = [vt_ref[nb, rows(h), :] for nb, h in chains]
    k = [k_ref[nb, :, rows(h)] * 0.125 for nb, h in chains]
    br = [b_row[nb][ML_HEADS + h:ML_HEADS + h + 1, :] for nb, h in chains]
    ir = [i_row[nb][h:h + 1, :] for nb, h in chains]
    g = [x[:, L - 1:L] for x in br]
    m_prev = [m_s[nb, h:h + 1, 0:1] for nb, h in chains]
    cn_prev = [cn_s[nb, h] for nb, h in chains]
    kq = [_dot(k[c], q_t[c]) for c in n]
    carry = [_dot(cn_prev[c].astype(BF16), q_t[c]) for c in n]
    dlog = [jnp.where(causal, br[c] - c_col[nb][:, ML_HEADS + h:ML_HEADS + h + 1], -jnp.inf)
            for c, (nb, h) in enumerate(chains)]
    inter = [br[c] + m_prev[c] for c in n]
    m_row = [jnp.maximum(inter[c], jnp.max(dlog[c], axis=0, keepdims=True)) for c in n]
    s = [kq[c] * jnp.exp(dlog[c] - m_row[c]) for c in n]
    w_inter = [jnp.exp(inter[c] - m_row[c]) for c in n]
    num = [_dot(v_t[c], s[c].astype(BF16)) + w_inter[c] * carry[c][0:d] for c in n]
    den = [jnp.sum(s[c], axis=0, keepdims=True) + w_inter[c] * carry[c][d:d + 1] for c in n]
    hh = [num[c] * (1.0 / jnp.maximum(jnp.abs(den[c]), jnp.exp(-m_row[c]))) for c in n]
    wlog = [g[c] - br[c] + ir[c] for c in n]
    m_new = [jnp.maximum(g[c] + m_prev[c], jnp.max(wlog[c], axis=-1, keepdims=True)) for c in n]
    w_row = [jnp.exp(wlog[c] - m_new[c]) for c in n]
    decay = [jnp.exp(g[c] + m_prev[c] - m_new[c]) for c in n]
    for c, (nb, h) in enumerate(chains):
        vw = jnp.concatenate([v_t[c].astype(F32) * w_row[c], jnp.broadcast_to(w_row[c], (8, L))], axis=0).astype(BF16)
        cn_s[nb, h] = decay[c] * cn_prev[c] + _dot(vw, k[c])
        m_s[nb, h:h + 1, :] = jnp.broadcast_to(m_new[c], (1, m_s.shape[2]))
    outs = [hh[c] * lax.rsqrt(jnp.mean(hh[c] * hh[c], axis=0, keepdims=True) + EPS) * ng_ref[rows(h), :]
            for c, (nb, h) in enumerate(chains)]
    for nb in range(NB):
        out_t = jnp.concatenate(outs[nb * ML_HEADS:(nb + 1) * ML_HEADS], axis=0)
        o_ref[nb] = (out_t.T * _sigmoid(mo_ref[nb])).astype(o_ref.dtype)


def mlstm_block(zt, zb, zf, gate_bias, norm_g, L=ML_CHUNK, NB=ML_NB):
    B, S, _ = zb.shape
    W, H, d = ML_W, ML_HEADS, HEAD_DIM
    assert B % NB == 0 and S % L == 0
    ng = jnp.broadcast_to(norm_g.reshape(W, 1), (W, L))
    return pl.pallas_call(
        _mlstm_kernel, grid=(B // NB, S // L),
        in_specs=[pl.BlockSpec((NB, W, L), lambda b, c: (b, 0, c)),
                  pl.BlockSpec((NB, W, L), lambda b, c: (b, 1, c)),
                  pl.BlockSpec((NB, L, W), lambda b, c: (b, c, 0)),
                  pl.BlockSpec((NB, L, W), lambda b, c: (b, c, 0)),
                  pl.BlockSpec((NB, L, LANES), lambda b, c: (b, c, 6)),
                  pl.BlockSpec((1, LANES), lambda b, c: (0, 0)),
                  pl.BlockSpec((W, L), lambda b, c: (0, 0))],
        out_specs=pl.BlockSpec((NB, L, W), lambda b, c: (b, c, 0)),
        out_shape=jax.ShapeDtypeStruct((B, S, W), BF16),
        scratch_shapes=[pltpu.VMEM((NB, H, d + 8, d), F32), pltpu.VMEM((NB, H, LANES), F32)],
        compiler_params=_cparams(("parallel", "arbitrary")), name="mlstm",
    )(zt, zt, zb, zf, zf, gate_bias, ng)


def _compress_kernel(kc_ref, vc_ref, pek_ref, pev_ref, w1k_ref, w1v_ref, w2k_ref, w2v_ref, ok_ref, ov_ref):
    G = NSA_KV_HEADS
    nh = kc_ref.shape[0] // CMP_STRIDE

    def one(x_ref, pe_ref, w1_ref, w2_ref):
        hid = None
        for l in range(CMP_STRIDE):
            y = x_ref[pl.ds(l, nh, stride=CMP_STRIDE), :]
            ya = (y + pe_ref[l:l + 1, :]).astype(BF16)
            yb = (pltpu.roll(y, nh - 1, 0) + pe_ref[CMP_STRIDE + l:CMP_STRIDE + l + 1, :]).astype(BF16)
            t = _dot(ya, w1_ref[l]) + _dot(yb, w1_ref[CMP_STRIDE + l])
            hid = t if hid is None else hid + t
        hid = _gelu(hid).astype(BF16)
        return [_dot(hid[:, gi * CMP_HID:(gi + 1) * CMP_HID], w2_ref[...]) for gi in range(G)]

    for gi, (ko, vo) in enumerate(zip(one(kc_ref, pek_ref, w1k_ref, w2k_ref), one(vc_ref, pev_ref, w1v_ref, w2v_ref))):
        ok_ref[gi] = ko.astype(ok_ref.dtype)
        ov_ref[gi] = vo.T.astype(ov_ref.dtype)


def compress_block(zf, kc_block, vc_block, k_pe, k_w1, k_w2, v_pe, v_w1, v_w2):
    B, S, _ = zf.shape
    G, d = NSA_KV_HEADS, HEAD_DIM
    nh = S // CMP_STRIDE

    def prep(pe, w1):
        w1bd = jnp.zeros((CMP_LEN, G * d, G * CMP_HID), w1.dtype)
        for gi in range(G):
            w1bd = w1bd.at[:, gi * d:(gi + 1) * d, gi * CMP_HID:(gi + 1) * CMP_HID].set(w1)
        return jnp.tile(pe, (1, G)), w1bd.astype(BF16)

    pek, w1k = prep(k_pe, k_w1)
    pev, w1v = prep(v_pe, v_w1)

    def full(a):
        return pl.BlockSpec(a.shape, lambda b: (0,) * a.ndim)

    w2k, w2v = k_w2.astype(BF16), v_w2.astype(BF16)
    return pl.pallas_call(
        _compress_kernel, grid=(B,),
        in_specs=[pl.BlockSpec((None, S, G * d), lambda b: (b, 0, kc_block)),
                  pl.BlockSpec((None, S, G * d), lambda b: (b, 0, vc_block)),
                  full(pek), full(pev), full(w1k), full(w1v), full(w2k), full(w2v)],
        out_specs=[pl.BlockSpec((None, G, nh, d), lambda b: (b, 0, 0, 0)),
                   pl.BlockSpec((None, G, d, nh), lambda b: (b, 0, 0, 0))],
        out_shape=[jax.ShapeDtypeStruct((B, G, nh, d), BF16), jax.ShapeDtypeStruct((B, G, d, nh), BF16)],
        compiler_params=_cparams(("parallel",)), name="nsa_compress",
    )(zf, zf, pek, pev, w1k, w1v, w2k, w2v)


def _nsa_kernel(q_ref, gt_ref, gb_ref, kc_ref, vct_ref, ks_ref, vst_ref, kw_ref, vwt_ref, ovl_ref, aug_ref, caug_ref,
                tail_ref, ctail_ref, wmask_ref, o_ref, ksa_s, kwa_s, kca_s, vsa_s, vwa_s, ss_s, ps_s, sw_s, pw_s, *, n_cmp):
    TQ, TK, d, HPG, G = NSA_TQ, NSA_TK, HEAD_DIM, NSA_HPG, NSA_KV_HEADS
    R = HPG * TQ
    groups = range(G)
    qi = pl.program_id(1)
    q0 = qi * TQ
    kt_d = q0 // TK
    n_cb = kc_ref.shape[1]
    n_sb = ovl_ref.shape[0]

    @pl.when(qi == 0)
    def _():
        for g in groups:
            ksa_s[g, :, 0:d] = ks_ref[:, g * d:(g + 1) * d]
            ksa_s[g, :, d:] = aug_ref[...]
            kwa_s[g, :, 0:d] = kw_ref[:, g * d:(g + 1) * d]
            kwa_s[g, :, d:] = aug_ref[...]
            kca_s[g, :, 0:d] = kc_ref[g]
            kca_s[g, :, d:] = caug_ref[...]
            vsa_s[g, 0:d, :] = vst_ref[g * d:(g + 1) * d, :]
            vsa_s[g, d:, :] = jnp.ones((vsa_s.shape[1] - d, vsa_s.shape[2]), BF16)
            vwa_s[g, 0:d, :] = vwt_ref[g * d:(g + 1) * d, :]
            vwa_s[g, d:, :] = jnp.ones((vwa_s.shape[1] - d, vwa_s.shape[2]), BF16)

    def slope(g, hh):
        return 2.0 ** (-(g * HPG + hh + 1))

    def per_head(fn):
        return jnp.concatenate([fn(hh) for hh in range(HPG)], axis=1)

    def tile_heads(x):
        return jnp.concatenate([x] * HPG, axis=1)

    q_t = [(q_ref[:, g * HPG * d:(g + 1) * HPG * d].astype(F32) * (LOG2E * 0.125)).T for g in groups]

    def q_head(g, hh):
        return q_t[g][hh * d:(hh + 1) * d]

    def q_aug(g, block_rows):
        return per_head(lambda hh: jnp.concatenate(
            [q_head(g, hh), block_rows, tail_ref[...] * slope(g, hh)], axis=0)).astype(BF16)

    def scores(ka, qa_t, kt):
        return _dot(ka[pl.ds(pl.multiple_of(kt * TK, TK), TK), :], qa_t)

    def stage_scores(buf, ka, qa_t, kt, mask_add=None):
        s = scores(ka, qa_t, kt)
        if mask_add is not None:
            s = s + mask_add
        buf[...] = s
        return jnp.max(s, axis=0, keepdims=True)

    def stage_probs(sbuf, pbuf, tile_max, m):
        m_new = jnp.maximum(m, tile_max)
        pbuf[...] = jnp.exp2(sbuf[...] - m_new).astype(BF16)
        return m_new, jnp.exp2(m - m_new)

    def stage_values(pbuf, va, kt, alpha, acc):
        return alpha * acc + _dot(va[:, pl.ds(pl.multiple_of(kt * TK, TK), TK)], pbuf[...])

    def normalised(acc):
        return acc[0:d] * (1.0 / acc[d:d + 1])

    m0, acc0 = jnp.full((1, R), NEG_BIG, F32), jnp.zeros((vsa_s.shape[1], R), F32)

    qc_t = [per_head(lambda hh: jnp.concatenate([q_head(g, hh), ctail_ref[...] * slope(g, hh)], axis=0)).astype(BF16)
            for g in groups]
    qw_t = [q_aug(g, jnp.zeros((n_sb, TQ), F32)) for g in groups]
    n_r = lax.broadcasted_iota(jnp.int32, (n_cb, TQ), 0)
    t_c = q0 + lax.broadcasted_iota(jnp.int32, (n_cb, TQ), 1)
    ok_c = (n_r * CMP_STRIDE + (CMP_LEN - 1) <= t_c) & (n_r < n_cmp)
    add_c, keep_c = tile_heads(jnp.where(ok_c, 0.0, NEG_BIG)), tile_heads(jnp.where(ok_c, 1.0, 0.0))
    s_c = [_dot(kca_s[g], qc_t[g]) + add_c for g in groups]
    n_win = (WINDOW - 1 + TK - 1) // TK + 1
    kt_win, max_win = [], []
    for back in range(n_win):
        kt_raw = kt_d - back
        kt_win.append(jnp.maximum(kt_raw, 0))
        if back == 0:
            mask_add = tile_heads(wmask_ref[0])
        elif (back + 1) * TK <= WINDOW:
            mask_add = jnp.where(kt_raw >= 0, 0.0, NEG_BIG)
        else:
            mask_add = tile_heads(wmask_ref[back] + jnp.where(kt_raw >= 0, 0.0, NEG_BIG))
        max_win.append([stage_scores(sw_s.at[g, back], kwa_s.at[g], qw_t[g], kt_win[back], mask_add) for g in groups])

    e_c = [jnp.exp2(s_c[g] - jnp.max(s_c[g], axis=0, keepdims=True)) * keep_c for g in groups]
    p_c = [e_c[g] * (1.0 / jnp.maximum(jnp.sum(e_c[g], axis=0, keepdims=True), 1.0)) for g in groups]
    o_c = [_dot(vct_ref[g], p_c[g].astype(BF16)) for g in groups]

    jb = lax.broadcasted_iota(jnp.int32, (n_sb, TQ), 0)
    cur = (q0 + lax.broadcasted_iota(jnp.int32, (n_sb, TQ), 1)) // SEL_LEN
    valid = jb <= cur
    forced = (jb == 0) | (jb == cur) | (jb == cur - 1)
    sub = lax.broadcasted_iota(jnp.int32, (8, TQ), 0)
    n_grp = n_sb // 8
    score, grp, rank = [], [], []
    for g in groups:
        p_sum = p_c[g][:, 0:TQ]
        for hh in range(1, HPG):
            p_sum = p_sum + p_c[g][:, hh * TQ:(hh + 1) * TQ]
        imp = jnp.dot(ovl_ref[...], p_sum, preferred_element_type=F32, precision=HIGHEST)
        score.append(jnp.where(forced, jnp.inf, jnp.where(valid, imp, -jnp.inf)))
        grp.append([score[g][8 * a:8 * a + 8] for a in range(n_grp)])
        rank.append([jnp.zeros((8, TQ), jnp.int32) for _ in range(n_grp)])
    for j in range(n_sb):
        for g in groups:
            r = score[g][j:j + 1, :]
            for a in range(n_grp):
                if a > j // 8:
                    ahead = (r >= grp[g][a]).astype(jnp.int32)
                elif a < j // 8:
                    ahead = (r > grp[g][a]).astype(jnp.int32)
                else:
                    ahead = jnp.where(sub > j % 8, (r >= grp[g][a]).astype(jnp.int32), (r > grp[g][a]).astype(jnp.int32))
                rank[g][a] = rank[g][a] + ahead
    picked = [valid & (jnp.concatenate(rank[g], axis=0) < SEL_TOPN) for g in groups]
    qs_t = [q_aug(g, jnp.where(picked[g], 0.0, NEG_BIG)) for g in groups]

    filler = 1 - kt_d % 2
    n_seq = kt_d + 1 + filler

    def sel_tile(i):
        return jnp.where(i == 0, kt_d, jnp.maximum(i - 1 - filler, 0))

    def a_stage(slot, kt, mask_add=None):
        return [stage_scores(ss_s.at[g, slot], ksa_s.at[g], qs_t[g], kt, mask_add) for g in groups]

    def b_stage(slot, tmax, m):
        out = [stage_probs(ss_s.at[g, slot], ps_s.at[g, slot], tmax[g], m[g]) for g in groups]
        return [o[0] for o in out], [o[1] for o in out]

    def c_stage(slot, kt, alpha, acc):
        return [stage_values(ps_s.at[g, slot], vsa_s.at[g], kt, alpha[g], acc[g]) for g in groups]

    tmax0 = a_stage(0, kt_d, tile_heads(wmask_ref[0]))
    m, alpha = b_stage(0, tmax0, [m0] * G)
    tmax1 = a_stage(1, sel_tile(1), jnp.where(filler == 1, NEG_BIG, 0.0))

    mw, accw = [m0] * G, [acc0] * G
    for back in range(n_win):
        outw = [stage_probs(sw_s.at[g, back], pw_s.at[g, back], max_win[back][g], mw[g]) for g in groups]
        mw = [o[0] for o in outw]
        accw = [stage_values(pw_s.at[g, back], vwa_s.at[g], kt_win[back], outw[g][1], accw[g]) for g in groups]
    o_w = [normalised(accw[g]) for g in groups]

    def sel_body(k, carry):
        m, alpha, acc, tmax1 = carry
        i = 2 * k
        tmax0 = a_stage(0, sel_tile(i + 2))
        m, alpha1 = b_stage(1, tmax1, m)
        tmax1 = a_stage(1, sel_tile(i + 3))
        acc = c_stage(0, sel_tile(i), alpha, acc)
        m, alpha2 = b_stage(0, tmax0, m)
        acc = c_stage(1, sel_tile(i + 1), alpha1, acc)
        return m, alpha2, acc, tmax1

    m, alpha, acc, tmax1 = lax.fori_loop(0, (n_seq - 2) // 2, sel_body, (m, alpha, [acc0] * G, tmax1))
    m, alpha1 = b_stage(1, tmax1, m)
    acc = c_stage(0, sel_tile(n_seq - 2), alpha, acc)
    acc = c_stage(1, sel_tile(n_seq - 1), alpha1, acc)
    o_s = [normalised(acc[g]) for g in groups]

    gates = _sigmoid(gt_ref[...] + gb_ref[...]).T
    outs = []
    for g in groups:
        for hh in range(HPG):
            cols = slice(hh * TQ, (hh + 1) * TQ)
            row = 16 + 3 * (g * HPG + hh)
            outs.append(gates[row:row + 1] * o_c[g][:, cols] + gates[row + 1:row + 2] * o_s[g][:, cols]
                        + gates[row + 2:row + 3] * o_w[g][:, cols])
    o_ref[...] = jnp.concatenate(outs, axis=0).T.astype(o_ref.dtype)


def nsa_block(zt, zb, zf, gate_bias, kcmp, vcmp_t, q_col_block, k_col_blocks, v_row_blocks):
    B, S, _ = zb.shape
    G, d = NSA_KV_HEADS, HEAD_DIM
    TQ = NSA_TQ
    n_cb = kcmp.shape[2]
    n_cmp = (S - CMP_LEN) // CMP_STRIDE + 1
    n_sb = S // SEL_LEN
    n_terms = len(LOG2E_TERMS)
    kaug = -(-(d + n_sb + 2 * n_terms) // LANES) * LANES
    R = NSA_HPG * TQ
    n_win = (WINDOW - 1 + NSA_TK - 1) // NSA_TK + 1
    assert S % NSA_TK == 0 and NSA_TK == TQ and TQ % SEL_LEN == 0

    cidx = np.arange(n_cb)[None, :] * CMP_STRIDE
    sstart = np.arange(n_sb)[:, None] * SEL_LEN
    ovl = ((cidx < sstart + SEL_LEN) & (cidx + CMP_LEN - 1 >= sstart) & (np.arange(n_cb)[None, :] < n_cmp))
    ovl = jnp.asarray(ovl.astype(np.float32))
    pos = np.arange(S)
    aug = np.zeros((S, kaug - d), np.float32)
    aug[pos, pos // SEL_LEN] = 1.0
    tail = np.zeros((kaug - d - n_sb, TQ), np.float32)
    caug = np.zeros((n_cb, d), np.float32)
    ctail = np.zeros((d, TQ), np.float32)
    for i, term in enumerate(LOG2E_TERMS):
        aug[:, n_sb + i] = pos // SEL_LEN
        aug[:, n_sb + n_terms + i] = pos % SEL_LEN
        tail[i] = term * SEL_LEN
        tail[n_terms + i] = term
        caug[:, i] = np.arange(n_cb)
        ctail[i] = term * CMP_STRIDE
    aug, caug, tail, ctail = jnp.asarray(aug, BF16), jnp.asarray(caug, BF16), jnp.asarray(tail), jnp.asarray(ctail)
    dist = np.arange(n_win)[:, None, None] * NSA_TK + np.arange(TQ)[None, None, :] - np.arange(NSA_TK)[None, :, None]
    wmask = jnp.asarray(np.where((dist >= 0) & (dist < WINDOW), 0.0, NEG_BIG).astype(np.float32))

    def kspec(j):
        return pl.BlockSpec((None, S, G * d), lambda b, i: (b, 0, k_col_blocks[j]))

    def vspec(j):
        return pl.BlockSpec((None, G * d, S), lambda b, i: (b, v_row_blocks[j], 0))

    def const(a):
        return pl.BlockSpec(a.shape, lambda b, i: (0,) * a.ndim)

    W = G * NSA_HPG * d
    return pl.pallas_call(
        functools.partial(_nsa_kernel, n_cmp=n_cmp), grid=(B, S // TQ),
        in_specs=[pl.BlockSpec((None, TQ, W), lambda b, i: (b, i, q_col_block)),
                  pl.BlockSpec((None, TQ, LANES), lambda b, i: (b, i, 6)),
                  const(gate_bias),
                  pl.BlockSpec((None, G, n_cb, d), lambda b, i: (b, 0, 0, 0)),
                  pl.BlockSpec((None, G, d, n_cb), lambda b, i: (b, 0, 0, 0)),
                  kspec(0), vspec(0), kspec(1), vspec(1), const(ovl), const(aug), const(caug), const(tail),
                  const(ctail), const(wmask)],
        out_specs=pl.BlockSpec((None, TQ, W), lambda b, i: (b, i, 0)),
        out_shape=jax.ShapeDtypeStruct((B, S, NSA_W), BF16),
        scratch_shapes=[pltpu.VMEM((G, S, kaug), BF16), pltpu.VMEM((G, S, kaug), BF16),
                        pltpu.VMEM((G, n_cb, 2 * d), BF16),
                        pltpu.VMEM((G, d + NSA_ONES, S), BF16), pltpu.VMEM((G, d + NSA_ONES, S), BF16),
                        pltpu.VMEM((G, 2, NSA_TK, R), F32), pltpu.VMEM((G, 2, NSA_TK, R), BF16),
                        pltpu.VMEM((G, n_win, NSA_TK, R), F32), pltpu.VMEM((G, n_win, NSA_TK, R), BF16)],
        compiler_params=_cparams(("parallel", "arbitrary")), name="nsa",
    )(zb, zf, gate_bias, kcmp, vcmp_t, zb, zt, zb, zt, ovl, aug, caug, tail, ctail, wmask)


def _rglru_kernel(h_ref, hh_ref, g_ref, wg_ref, wi_ref, cw_ref, cb_ref, wa_ref, wx_ref, ba_ref, bx_ref, lam_ref,
                  o_ref, h_s, au_s):
    NB, T, W = o_ref.shape
    first = pl.program_id(1) == 0

    @pl.when(first)
    def _():
        h_s[...] = jnp.zeros_like(h_s)

    def project(nb):
        xn = _rms(h_ref[nb], g_ref[...]).astype(BF16)
        x_halo = _dot(_rms(hh_ref[nb], g_ref[...]).astype(BF16), wi_ref[...])
        xe = jnp.concatenate([jnp.where(first, 0.0, x_halo), _dot(xn, wi_ref[...])], axis=0)
        return _dot(xn, wg_ref[...]), xe

    def recur(nb, gate, xe):
        xc = cb_ref[...] + cw_ref[3:4, :] * xe
        for k in range(1, LRU_CONV):
            xc = xc + cw_ref[3 - k:4 - k, :] * pltpu.roll(xe, k, 0)
        xc = xc[LRU_HALO:, :]
        xcb = xc.astype(BF16)
        half = W // 2

        def blockdiag(w_ref):
            return jnp.concatenate([_dot(xcb[:, :half], w_ref[0]), _dot(xcb[:, half:], w_ref[1])], axis=1)

        r = _sigmoid(blockdiag(wa_ref) + ba_ref[...])
        i = _sigmoid(blockdiag(wx_ref) + bx_ref[...])
        nl = -lam_ref[...]
        softplus = jnp.maximum(nl, 0.0) + jnp.log1p(jnp.exp(-jnp.abs(nl)))
        log_a = -LRU_C * r * softplus
        a = jnp.exp(log_a)
        one_m_a2 = -jnp.tanh(log_a) * (a * a + 1.0)
        u = jnp.where(one_m_a2 > 0.0, one_m_a2 * lax.rsqrt(one_m_a2), 0.0) * (i * xc)

        n_grp, n_slab = T // 8, W // LANES

        def phases(x, slab):
            for c in range(n_slab):
                au_s[nb, slab, c] = x[:, c * LANES:(c + 1) * LANES]
            return [jnp.concatenate([au_s[nb, slab, c, pl.ds(j, n_grp, stride=8), :] for c in range(n_slab)], axis=1)
                    for j in range(8)]

        a_ph, u_ph = phases(a, 0), phases(u, 1)
        prod, part = [a_ph[0]], [u_ph[0]]
        for j in range(1, 8):
            part.append(a_ph[j] * part[-1] + u_ph[j])
            prod.append(a_ph[j] * prod[-1])
        grp = lax.broadcasted_iota(jnp.int32, (n_grp, W), 0)
        ag, ug = prod[7], part[7]
        sft = 1
        while sft < n_grp:
            keep = grp >= sft
            ug = ag * jnp.where(keep, pltpu.roll(ug, sft, 0), 0.0) + ug
            ag = ag * jnp.where(keep, pltpu.roll(ag, sft, 0), 1.0)
            sft *= 2
        h_prev = h_s[nb, 0:1, :]
        hg = ug + ag * h_prev
        carry_in = jnp.where(grp >= 1, pltpu.roll(hg, 1, 0), h_prev)
        for j in range(8):
            hj = part[j] + prod[j] * carry_in
            for c in range(n_slab):
                au_s[nb, 0, c, pl.ds(j, n_grp, stride=8), :] = hj[:, c * LANES:(c + 1) * LANES]
        h = jnp.concatenate([au_s[nb, 0, c] for c in range(n_slab)], axis=1)
        h_s[nb] = jnp.broadcast_to(hg[n_grp - 1:n_grp, :], h_s.shape[1:])
        o_ref[nb] = (_gelu(gate) * h).astype(o_ref.dtype)

    nxt = project(0)
    for nb in range(NB):
        cur = nxt
        if nb + 1 < NB:
            nxt = project(nb + 1)
        recur(nb, *cur)


def rglru_block(h3, pre_g, w_gate, w_in, conv_w, conv_b, wa, ba, wx, bx, lam, T=LRU_T, NB=LRU_NB):
    B, S, D = h3.shape
    W = LRU_W
    half = W // 2
    hb = T // LRU_HALO

    def bd(w):
        blocks = [jax.scipy.linalg.block_diag(*[w[h] for h in range(4 * j, 4 * j + 4)]) for j in range(2)]
        return jnp.stack(blocks).astype(BF16)

    vec = lambda a: a.reshape(1, W)
    assert B % NB == 0 and S % T == 0
    vspec = pl.BlockSpec((1, W), lambda b, t: (0, 0))
    wspec = pl.BlockSpec((2, half, half), lambda b, t: (0, 0, 0))
    return pl.pallas_call(
        _rglru_kernel, grid=(B // NB, S // T),
        in_specs=[pl.BlockSpec((NB, T, D), lambda b, t: (b, t, 0)),
                  pl.BlockSpec((NB, LRU_HALO, D), lambda b, t: (b, jnp.maximum(t * hb - 1, 0), 0)),
                  pl.BlockSpec((1, D), lambda b, t: (0, 0)),
                  pl.BlockSpec((D, W), lambda b, t: (0, 0)), pl.BlockSpec((D, W), lambda b, t: (0, 0)),
                  pl.BlockSpec((LRU_CONV, W), lambda b, t: (0, 0)), vspec, wspec, wspec, vspec, vspec, vspec],
        out_specs=pl.BlockSpec((NB, T, W), lambda b, t: (b, t, 0)),
        out_shape=jax.ShapeDtypeStruct((B, S, W), BF16),
        scratch_shapes=[pltpu.VMEM((NB, 8, W), F32), pltpu.VMEM((NB, 2, W // LANES, T, LANES), F32)],
        compiler_params=_cparams(("parallel", "arbitrary")), name="rglru",
    )(h3, h3, pre_g.reshape(1, D), w_gate, w_in, conv_w, vec(conv_b), bd(wa), bd(wx), vec(ba), vec(bx), vec(lam))


def _sgu_kernel(h_ref, gpre_ref, wu_ref, wv_ref, g_ref, b_ref, w_ref, bias_ref, o_ref):
    C, W = SG_CHUNK, o_ref.shape[1]
    dg = W // SG_GROUPS
    n_sub = o_ref.shape[0] // SG_SUB
    row = lax.broadcasted_iota(jnp.int32, (C, C), 0)
    col = lax.broadcasted_iota(jnp.int32, (C, C), 1)
    wc = [jnp.where(col <= row, w_ref[gi], 0.0).astype(BF16) for gi in range(SG_GROUPS)]

    def project(j):
        xn = _rms(h_ref[j * SG_SUB:(j + 1) * SG_SUB, :], gpre_ref[...]).astype(BF16)
        return _dot(xn, wu_ref[...]), _dot(xn, wv_ref[...])

    def gate(j, u, v):
        u, v = _gelu(u), _gelu(v)
        mu = jnp.mean(v, axis=-1, keepdims=True)
        vc = v - mu
        vn = (vc * lax.rsqrt(jnp.mean(vc * vc, axis=-1, keepdims=True) + EPS) * g_ref[...] + b_ref[...]).astype(BF16)
        for gi in range(SG_GROUPS):
            sl = slice(gi * dg, (gi + 1) * dg)
            for c in range(SG_SUB // C):
                rows = slice(c * C, (c + 1) * C)
                mixed = _dot(wc[gi], vn[rows, sl]) + bias_ref[:, sl]
                o_ref[j * SG_SUB + c * C:j * SG_SUB + (c + 1) * C, sl] = (u[rows, sl] * mixed).astype(o_ref.dtype)

    nxt = project(0)
    for j in range(n_sub):
        cur = nxt
        if j + 1 < n_sub:
            nxt = project(j + 1)
        gate(j, *cur)


def sgu_block(h3, pre_g, w_u, w_v, ln_g, ln_b, w, b):
    B, S, D = h3.shape
    W, C, T = SG_W, SG_CHUNK, SG_ROWS
    assert S % T == 0 and T % SG_SUB == 0 and SG_SUB % C == 0
    bias = jnp.repeat(b.T, W // SG_GROUPS, axis=1)
    vspec = pl.BlockSpec((1, W), lambda bb, c: (0, 0))
    return pl.pallas_call(
        _sgu_kernel, grid=(B, S // T),
        in_specs=[pl.BlockSpec((None, T, D), lambda bb, c: (bb, c, 0)),
                  pl.BlockSpec((1, D), lambda bb, c: (0, 0)),
                  pl.BlockSpec((D, W), lambda bb, c: (0, 0)), pl.BlockSpec((D, W), lambda bb, c: (0, 0)),
                  vspec, vspec,
                  pl.BlockSpec((SG_GROUPS, C, C), lambda bb, c: (0, 0, 0)),
                  pl.BlockSpec((C, W), lambda bb, c: (0, 0))],
        out_specs=pl.BlockSpec((None, T, W), lambda bb, c: (bb, c, 0)),
        out_shape=jax.ShapeDtypeStruct((B, S, W), BF16),
        compiler_params=_cparams(("parallel", "parallel")), name="sgu",
    )(h3, pre_g.reshape(1, D), w_u, w_v, ln_g.reshape(1, W), ln_b.reshape(1, W), w, bias)


def _mixer_ab(h2, B, S, pre_g, w_in, ml_gate_b, ml_norm_g, nsa_gate_b, k_pe, k_w1, k_w2, v_pe, v_w1, v_w2):
    D = h2.shape[1]
    G, d = NSA_KV_HEADS, HEAD_DIM
    offs = np.cumsum([0, ML_W, ML_W, ML_W, ML_W, 2 * ML_HEADS, NSA_W] + [G * d] * 6 + [3 * NSA_HEADS])
    mq, mk, mv, mo, mif, nq, kc, vc, ks, vs, kw, vw, ng = [w_in[:, offs[i]:offs[i + 1]] for i in range(13)]
    w_b = jnp.concatenate([mk, nq, ks, kw], axis=1).astype(BF16)
    w_t = jnp.concatenate([mq, mv, vs, vw], axis=1).T.astype(BF16)
    gpad = LANES - 2 * ML_HEADS - 3 * NSA_HEADS
    w_f = jnp.concatenate([mo, kc, vc, mif, ng, jnp.zeros((D, gpad), w_in.dtype)], axis=1).astype(BF16)
    gate_bias = jnp.concatenate([ml_gate_b, nsa_gate_b, jnp.zeros((gpad,), F32)]).reshape(1, LANES)
    zb, zf, zt = norm_proj(h2, pre_g, [w_b, w_f], [BF16, F32], wts=[w_t], batch=B)
    zb = zb.reshape(B, S, -1)
    zf = zf.reshape(B, S, -1)
    h_ml = mlstm_block(zt, zb, zf, gate_bias, ml_norm_g)
    kcmp, vcmp_t = compress_block(zf, ML_W // (G * d), ML_W // (G * d) + 1, k_pe, k_w1, k_w2, v_pe, v_w1, v_w2)
    h_nsa = nsa_block(zt, zb, zf, gate_bias, kcmp, vcmp_t, q_col_block=ML_W // NSA_W,
                      k_col_blocks=((ML_W + NSA_W) // (G * d), (ML_W + NSA_W) // (G * d) + 1),
                      v_row_blocks=(2 * ML_W // (G * d), 2 * ML_W // (G * d) + 1))
    return h_ml.reshape(B * S, ML_W), h_nsa.reshape(B * S, NSA_W)


def _mixer_cd(h2, B, S, pre_g, w_in, conv_w, conv_b, wa, ba, wx, bx, lam, sg_g, sg_bn, sg_w, sg_b):
    h3 = h2.reshape(B, S, -1)
    w_gate, w_x, w_u, w_v = (w_in[:, j * LRU_W:(j + 1) * LRU_W].astype(BF16) for j in range(4))
    y_lru = rglru_block(h3, pre_g, w_gate, w_x, conv_w, conv_b, wa, ba, wx, bx, lam)
    y_sg = sgu_block(h3, pre_g, w_u, w_v, sg_g, sg_bn, sg_w, sg_b)
    return y_lru.reshape(B * S, LRU_W), y_sg.reshape(B * S, SG_W)


def kernel(x, pre_mix_g, post_mix_g, pre_ffn_g, post_ffn_g, ab_w_in, ab_w_out, ml_gate_b, ml_norm_g, nsa_gate_b, cmp_k_pe, cmp_k_w1, cmp_k_w2, cmp_v_pe, cmp_v_w1, cmp_v_w2, cd_w_in, cd_w_out, lru_conv_w, lru_conv_b, lru_wa, lru_ba, lru_wx, lru_bx, lru_lambda, sg_norm_g, sg_norm_b, sg_w, sg_b, ffn_w_up, ffn_conv_w, ffn_conv_b, ffn_w_down):
    B, S, D = x.shape
    depth = pre_mix_g.shape[0]
    h2 = x.reshape(B * S, D)
    for layer in range(depth):
        if layer % 2 == 0:
            e = layer // 2
            a1, a2 = _mixer_ab(h2, B, S, pre_mix_g[layer], ab_w_in[e], ml_gate_b[e], ml_norm_g[e], nsa_gate_b[e],
                               cmp_k_pe[e], cmp_k_w1[e], cmp_k_w2[e], cmp_v_pe[e], cmp_v_w1[e], cmp_v_w2[e])
            w_out = ab_w_out[e]
        else:
            o = layer // 2
            a1, a2 = _mixer_cd(h2, B, S, pre_mix_g[layer], cd_w_in[o], lru_conv_w[o], lru_conv_b[o], lru_wa[o],
                               lru_ba[o], lru_wx[o], lru_bx[o], lru_lambda[o], sg_norm_g[o], sg_norm_b[o], sg_w[o],
                               sg_b[o])
            w_out = cd_w_out[o]
        h2 = mix_ffn_block(h2, a1, a2, S, w_out, post_mix_g[layer], pre_ffn_g[layer], ffn_w_up[layer],
                           ffn_conv_w[layer], ffn_conv_b[layer], ffn_w_down[layer], post_ffn_g[layer])
    return h2.reshape(B, S, D)
```

```python
import functools

import numpy as np
import jax
import jax.numpy as jnp
from jax import lax
from jax.experimental import pallas as pl
from jax.experimental.pallas import tpu as pltpu

F32 = jnp.float32
BF16 = jnp.bfloat16

EPS = 1e-6
HEAD_DIM = 64
ML_HEADS = 8
ML_W = 512
GATE_SOFTCAP = 15.0
NSA_HEADS = 8
NSA_KV_HEADS = 2
NSA_HPG = NSA_HEADS // NSA_KV_HEADS
NSA_W = 512
CMP_LEN = 32
CMP_STRIDE = 16
CMP_HID = 128
SEL_LEN = 64
SEL_TOPN = 16
WINDOW = 512
LRU_W = 512
LRU_C = 8.0
LRU_CONV = 4
SG_GROUPS = 8
SG_W = 512
SG_CHUNK = 128
FFN_CONV = 3
GATE_COL_BLOCK = (ML_W + 2 * NSA_KV_HEADS * HEAD_DIM) // 128
NSA_GATE_LANE0 = 2 * ML_HEADS

LANES = 128
VMEM_LIMIT = 56 * 1024 * 1024
NEG_BIG = -1e30
HIGHEST = lax.Precision.HIGHEST
LOG2E = 1.4426950408889634


def _bf16_terms(x, n):
    terms = []
    for _ in range(n):
        bits = int(np.array(x, np.float32).view(np.uint32))
        t = float(np.array((bits + 0x7FFF + ((bits >> 16) & 1)) & 0xFFFF0000, np.uint32).view(np.float32))
        terms.append(t)
        x -= t
    return tuple(terms)


LOG2E_TERMS = _bf16_terms(LOG2E, 3)

ML_CHUNK = 128
ML_NB = 8
NSA_TQ = 256
NSA_TK = 256
NSA_ONES = 16
ROW_TILE = 512
FFN_TM = 512
FFN_SUB = 256
FFN_CK = 256
FFN_HALO = 16
SG_ROWS = 1024
SG_SUB = 256
LRU_T = 512
LRU_HALO = 8
LRU_NB = 4


def _cparams(sem):
    return pltpu.CompilerParams(dimension_semantics=sem, vmem_limit_bytes=VMEM_LIMIT)


def _rms(x, g):
    return x * lax.rsqrt(jnp.mean(x * x, axis=-1, keepdims=True) + EPS) * g


def _gelu(x):
    return 0.5 * x * (1.0 + jnp.tanh(0.7978845608028654 * (x + 0.044715 * (x * x * x))))


def _sigmoid(x):
    return 1.0 / (1.0 + jnp.exp(-x))


def _dot(a, b):
    return jnp.dot(a, b, preferred_element_type=F32)


def _dot_nt(a, b, precision=None):
    return lax.dot_general(a, b, (((1,), (1,)), ((), ())), preferred_element_type=F32, precision=precision)


def _norm_proj_kernel(h_ref, g_ref, *refs, n_row, n_t, cn):
    w_refs, wt_refs = refs[:n_row], refs[n_row:n_row + n_t]
    o_refs, ot_refs = refs[n_row + n_t:2 * n_row + n_t], refs[2 * n_row + n_t:]
    xn = _rms(h_ref[...], g_ref[...]).astype(BF16)
    for w_ref, o_ref in zip(w_refs, o_refs):
        n = w_ref.shape[1]
        for c in range(0, n, cn):
            ce = min(c + cn, n)
            o_ref[:, c:ce] = _dot(xn, w_ref[:, c:ce]).astype(o_ref.dtype)
    for wt_ref, ot_ref in zip(wt_refs, ot_refs):
        n = wt_ref.shape[0]
        for c in range(0, n, cn):
            ce = min(c + cn, n)
            ot_ref[c:ce, :] = _dot_nt(wt_ref[c:ce, :], xn).astype(ot_ref.dtype)


def norm_proj(h2, g, ws, dtypes, wts=(), batch=1, tm=ROW_TILE):
    M, D = h2.shape
    tps = M // batch // tm
    in_specs = [pl.BlockSpec((tm, D), lambda i: (i, 0)), pl.BlockSpec((1, D), lambda i: (0, 0))]
    in_specs += [pl.BlockSpec(w.shape, lambda i: (0, 0)) for w in (*ws, *wts)]
    out_specs = [pl.BlockSpec((tm, w.shape[1]), lambda i: (i, 0)) for w in ws]
    out_specs += [pl.BlockSpec((None, w.shape[0], tm), lambda i: (i // tps, 0, i % tps)) for w in wts]
    out_shape = [jax.ShapeDtypeStruct((M, w.shape[1]), dt) for w, dt in zip(ws, dtypes)]
    out_shape += [jax.ShapeDtypeStruct((batch, w.shape[0], M // batch), BF16) for w in wts]
    return pl.pallas_call(
        functools.partial(_norm_proj_kernel, n_row=len(ws), n_t=len(wts), cn=512),
        grid=(M // tm,), in_specs=in_specs, out_specs=out_specs, out_shape=out_shape,
        compiler_params=_cparams(("parallel",)), name="norm_proj",
    )(h2, g.reshape(1, D), *ws, *wts)


def _mix_ffn_kernel(h_ref, hh_ref, a1_ref, a1h_ref, a2_ref, a2h_ref, wo1_ref, wo2_ref, gmix_ref, gpre_ref, wu_ref,
                    cw_ref, cb_ref, wd_ref, gpost_ref, o_ref, xn_s, acc_s, *, tiles_per_seq, ck):
    F = wd_ref.shape[0]
    n_chunks = F // ck
    n_sub = h_ref.shape[0] // FFN_SUB
    first = (pl.program_id(0) % tiles_per_seq) == 0
    g = gpre_ref[...]

    def mixed(h, a1, a2):
        return h + _rms(_dot(a1, wo1_ref[...]) + _dot(a2, wo2_ref[...]), gmix_ref[...])

    def conv(u, cols):
        y = (cw_ref[2:3, cols] * u + cw_ref[1:2, cols] * pltpu.roll(u, 1, 0) + cw_ref[0:1, cols] * pltpu.roll(u, 2, 0)
             + cb_ref[:, cols])
        return y[FFN_HALO:, :]

    def cols_of(c, half):
        return slice(half * F + c * ck, half * F + (c + 1) * ck)

    x = {}

    def head(j):
        rows = slice(j * FFN_SUB, (j + 1) * FFN_SUB)
        x[j] = mixed(h_ref[rows, :], a1_ref[rows, :], a2_ref[rows, :])
        if j == 0:
            halo = jnp.where(first, 0.0, _rms(mixed(hh_ref[...], a1h_ref[...], a2h_ref[...]), g))
            xn_s[0:FFN_HALO, :] = halo.astype(BF16)
        xn_s[FFN_HALO + j * FFN_SUB:FFN_HALO + (j + 1) * FFN_SUB, :] = _rms(x[j], g).astype(BF16)
        acc_s[rows, :] = jnp.zeros((FFN_SUB, acc_s.shape[1]), F32)

    def up(j, c):
        xn = xn_s[j * FFN_SUB:(j + 1) * FFN_SUB + FFN_HALO, :]
        return _dot(xn, wu_ref[:, cols_of(c, 0)]), _dot(xn, wu_ref[:, cols_of(c, 1)])

    def tail(j):
        rows = slice(j * FFN_SUB, (j + 1) * FFN_SUB)
        o_ref[rows, :] = x[j] + _rms(acc_s[rows, :], gpost_ref[...])

    head(0)
    for j in range(n_sub):
        u = up(j, 0)
        for c in range(n_chunks):
            u_next = up(j, c + 1) if c + 1 < n_chunks else None
            if c == 0 and j + 1 < n_sub:
                head(j + 1)
            if c == n_chunks // 3 and j >= 1:
                tail(j - 1)
            act = (_gelu(conv(u[0], cols_of(c, 0))) * conv(u[1], cols_of(c, 1))).astype(BF16)
            acc_s[j * FFN_SUB:(j + 1) * FFN_SUB, :] += _dot(act, wd_ref[c * ck:(c + 1) * ck, :])
            u = u_next
    tail(n_sub - 1)


def mix_ffn_block(h2, a1, a2, seq_len, w_out, g_mix, g_pre, w_up, conv_w, conv_b, w_down, g_post, tm=FFN_TM, ck=FFN_CK):
    M, D = h2.shape
    F = w_down.shape[0]
    K1, K2 = a1.shape[1], a2.shape[1]
    assert F % ck == 0 and seq_len % tm == 0 and tm % FFN_SUB == 0 and FFN_SUB % FFN_HALO == 0
    assert conv_w.shape[0] == FFN_CONV and FFN_CONV - 1 <= FFN_HALO
    hb = tm // FFN_HALO
    consts = (w_out[:K1].astype(BF16), w_out[K1:].astype(BF16), g_mix.reshape(1, D), g_pre.reshape(1, D),
              w_up.astype(BF16), conv_w, conv_b.reshape(1, 2 * F), w_down.astype(BF16), g_post.reshape(1, D))

    def tile(width):
        return pl.BlockSpec((tm, width), lambda i: (i, 0))

    def halo(width):
        return pl.BlockSpec((FFN_HALO, width), lambda i: (jnp.maximum(i * hb - 1, 0), 0))

    def full(a):
        return pl.BlockSpec(a.shape, lambda i: (0,) * a.ndim)

    return pl.pallas_call(
        functools.partial(_mix_ffn_kernel, tiles_per_seq=seq_len // tm, ck=ck),
        grid=(M // tm,),
        in_specs=[tile(D), halo(D), tile(K1), halo(K1), tile(K2), halo(K2)] + [full(a) for a in consts],
        out_specs=tile(D),
        out_shape=jax.ShapeDtypeStruct((M, D), F32),
        scratch_shapes=[pltpu.VMEM((tm + FFN_HALO, D), BF16), pltpu.VMEM((tm, D), F32)],
        compiler_params=_cparams(("parallel",)), name="mix_ffn",
    )(h2, h2, a1, a1, a2, a2, *consts)


def _mlstm_kernel(qt_ref, vt_ref, k_ref, mo_ref, gt_ref, gb_ref, ng_ref, o_ref, cn_s, m_s):
    NB, _, L = qt_ref.shape
    d = HEAD_DIM

    @pl.when(pl.program_id(1) == 0)
    def _():
        cn_s[...] = jnp.zeros_like(cn_s)
        m_s[...] = jnp.zeros_like(m_s)

    src = lax.broadcasted_iota(jnp.int32, (L, L), 0)
    tgt = lax.broadcasted_iota(jnp.int32, (L, L), 1)
    causal = src <= tgt
    tri = (tgt <= src).astype(F32)

    gcap = [GATE_SOFTCAP * jnp.tanh((gt_ref[nb] + gb_ref[...]) * (1.0 / GATE_SOFTCAP)) for nb in range(NB)]
    lf = [jnp.minimum(x, 0.0) - jnp.log1p(jnp.exp(-jnp.abs(x))) for x in gcap]
    b_col = [jnp.dot(tri, x, preferred_element_type=F32, precision=HIGHEST) for x in lf]
    b_row = [_dot_nt(x.T, tri, precision=HIGHEST) for x in lf]
    i_row = [x.T for x in gcap]
    c_col = [b_col[nb] - pltpu.roll(gcap[nb], ML_HEADS, 1) for nb in range(NB)]

    chains = [(nb, h) for nb in range(NB) for h in range(ML_HEADS)]
    n = range(len(chains))

    def rows(h):
        return slice(h * d, (h + 1) * d)

    q_t = [qt_ref[nb, rows(h), :] for nb, h in chains]
    v_t = [vt_ref[nb, rows(h), :] for nb, h in chains]
    k = [k_ref[nb, :, rows(h)] * 0.125 for nb, h in chains]
    br = [b_row[nb][ML_HEADS + h:ML_HEADS + h + 1, :] for nb, h in chains]
    ir = [i_row[nb][h:h + 1, :] for nb, h in chains]
    g = [x[:, L - 1:L] for x in br]
    m_prev = [m_s[nb, h:h + 1, 0:1] for nb, h in chains]
    cn_prev = [cn_s[nb, h] for nb, h in chains]
    kq = [_dot(k[c], q_t[c]) for c in n]
    carry = [_dot(cn_prev[c].astype(BF16), q_t[c]) for c in n]
    dlog = [jnp.where(causal, br[c] - c_col[nb][:, ML_HEADS + h:ML_HEADS + h + 1], -jnp.inf)
            for c, (nb, h) in enumerate(chains)]
    inter = [br[c] + m_prev[c] for c in n]
    m_row = [jnp.maximum(inter[c], jnp.max(dlog[c], axis=0, keepdims=True)) for c in n]
    s = [kq[c] * jnp.exp(dlog[c] - m_row[c]) for c in n]
    w_inter = [jnp.exp(inter[c] - m_row[c]) for c in n]
    num = [_dot(v_t[c], s[c].astype(BF16)) + w_inter[c] * carry[c][0:d] for c in n]
    den = [jnp.sum(s[c], axis=0, keepdims=True) + w_inter[c] * carry[c][d:d + 1] for c in n]
    hh = [num[c] * (1.0 / jnp.maximum(jnp.abs(den[c]), jnp.exp(-m_row[c]))) for c in n]
    wlog = [g[c] - br[c] + ir[c] for c in n]
    m_new = [jnp.maximum(g[c] + m_prev[c], jnp.max(wlog[c], axis=-1, keepdims=True)) for c in n]
    w_row = [jnp.exp(wlog[c] - m_new[c]) for c in n]
    decay = [jnp.exp(g[c] + m_prev[c] - m_new[c]) for c in n]
    for c, (nb, h) in enumerate(chains):
        vw = jnp.concatenate([v_t[c].astype(F32) * w_row[c], jnp.broadcast_to(w_row[c], (8, L))], axis=0).astype(BF16)
        cn_s[nb, h] = decay[c] * cn_prev[c] + _dot(vw, k[c])
        m_s[nb, h:h + 1, :] = jnp.broadcast_to(m_new[c], (1, m_s.shape[2]))
    outs = [hh[c] * lax.rsqrt(jnp.mean(hh[c] * hh[c], axis=0, keepdims=True) + EPS) * ng_ref[rows(h), :]
            for c, (nb, h) in enumerate(chains)]
    for nb in range(NB):
        out_t = jnp.concatenate(outs[nb * ML_HEADS:(nb + 1) * ML_HEADS], axis=0)
        o_ref[nb] = (out_t.T * _sigmoid(mo_ref[nb])).astype(o_ref.dtype)


def mlstm_block(zt, zb, zf, gate_bias, norm_g, L=ML_CHUNK, NB=ML_NB):
    B, S, _ = zb.shape
    W, H, d = ML_W, ML_HEADS, HEAD_DIM
    assert B % NB == 0 and S % L == 0
    ng = jnp.broadcast_to(norm_g.reshape(W, 1), (W, L))
    return pl.pallas_call(
        _mlstm_kernel, grid=(B // NB, S // L),
        in_specs=[pl.BlockSpec((NB, W, L), lambda b, c: (b, 0, c)),
                  pl.BlockSpec((NB, W, L), lambda b, c: (b, 1, c)),
                  pl.BlockSpec((NB, L, W), lambda b, c: (b, c, 0)),
                  pl.BlockSpec((NB, L, W), lambda b, c: (b, c, 0)),
                  pl.BlockSpec((NB, L, LANES), lambda b, c: (b, c, GATE_COL_BLOCK)),
                  pl.BlockSpec((1, LANES), lambda b, c: (0, 0)),
                  pl.BlockSpec((W, L), lambda b, c: (0, 0))],
        out_specs=pl.BlockSpec((NB, L, W), lambda b, c: (b, c, 0)),
        out_shape=jax.ShapeDtypeStruct((B, S, W), BF16),
        scratch_shapes=[pltpu.VMEM((NB, H, d + 8, d), F32), pltpu.VMEM((NB, H, LANES), F32)],
        compiler_params=_cparams(("parallel", "arbitrary")), name="mlstm",
    )(zt, zt, zb, zf, zf, gate_bias, ng)


def _compress_kernel(kc_ref, vc_ref, pek_ref, pev_ref, w1k_ref, w1v_ref, w2k_ref, w2v_ref, ok_ref, ov_ref):
    G = NSA_KV_HEADS
    nh = kc_ref.shape[0] // CMP_STRIDE

    def one(x_ref, pe_ref, w1_ref, w2_ref):
        hid = None
        for l in range(CMP_STRIDE):
            y = x_ref[pl.ds(l, nh, stride=CMP_STRIDE), :]
            ya = (y + pe_ref[l:l + 1, :]).astype(BF16)
            yb = (pltpu.roll(y, nh - 1, 0) + pe_ref[CMP_STRIDE + l:CMP_STRIDE + l + 1, :]).astype(BF16)
            t = _dot(ya, w1_ref[l]) + _dot(yb, w1_ref[CMP_STRIDE + l])
            hid = t if hid is None else hid + t
        hid = _gelu(hid).astype(BF16)
        return [_dot(hid[:, gi * CMP_HID:(gi + 1) * CMP_HID], w2_ref[...]) for gi in range(G)]

    for gi, (ko, vo) in enumerate(zip(one(kc_ref, pek_ref, w1k_ref, w2k_ref), one(vc_ref, pev_ref, w1v_ref, w2v_ref))):
        ok_ref[gi] = ko.astype(ok_ref.dtype)
        ov_ref[gi] = vo.T.astype(ov_ref.dtype)


def compress_block(zf, kc_block, vc_block, k_pe, k_w1, k_w2, v_pe, v_w1, v_w2):
    B, S, _ = zf.shape
    G, d = NSA_KV_HEADS, HEAD_DIM
    nh = S // CMP_STRIDE

    def prep(pe, w1):
        w1bd = jnp.zeros((CMP_LEN, G * d, G * CMP_HID), w1.dtype)
        for gi in range(G):
            w1bd = w1bd.at[:, gi * d:(gi + 1) * d, gi * CMP_HID:(gi + 1) * CMP_HID].set(w1)
        return jnp.tile(pe, (1, G)), w1bd.astype(BF16)

    pek, w1k = prep(k_pe, k_w1)
    pev, w1v = prep(v_pe, v_w1)

    def full(a):
        return pl.BlockSpec(a.shape, lambda b: (0,) * a.ndim)

    w2k, w2v = k_w2.astype(BF16), v_w2.astype(BF16)
    return pl.pallas_call(
        _compress_kernel, grid=(B,),
        in_specs=[pl.BlockSpec((None, S, G * d), lambda b: (b, 0, kc_block)),
                  pl.BlockSpec((None, S, G * d), lambda b: (b, 0, vc_block)),
                  full(pek), full(pev), full(w1k), full(w1v), full(w2k), full(w2v)],
        out_specs=[pl.BlockSpec((None, G, nh, d), lambda b: (b, 0, 0, 0)),
                   pl.BlockSpec((None, G, d, nh), lambda b: (b, 0, 0, 0))],
        out_shape=[jax.ShapeDtypeStruct((B, G, nh, d), BF16), jax.ShapeDtypeStruct((B, G, d, nh), BF16)],
        compiler_params=_cparams(("parallel",)), name="nsa_compress",
    )(zf, zf, pek, pev, w1k, w1v, w2k, w2v)


def _nsa_kernel(q_ref, gt_ref, gb_ref, kc_ref, vct_ref, ks_ref, vst_ref, kw_ref, vwt_ref, ovl_ref, aug_ref, caug_ref,
                tail_ref, ctail_ref, wmask_ref, o_ref, ksa_s, kwa_s, kca_s, vsa_s, vwa_s, ss_s, ps_s, sw_s, pw_s, *, n_cmp):
    TQ, TK, d, HPG, G = NSA_TQ, NSA_TK, HEAD_DIM, NSA_HPG, NSA_KV_HEADS
    R = HPG * TQ
    groups = range(G)
    qi = pl.program_id(1)
    q0 = qi * TQ
    kt_d = q0 // TK
    n_cb = kc_ref.shape[1]
    n_sb = ovl_ref.shape[0]

    @pl.when(qi == 0)
    def _():
        for g in groups:
            ksa_s[g, :, 0:d] = ks_ref[:, g * d:(g + 1) * d]
            ksa_s[g, :, d:] = aug_ref[...]
            kwa_s[g, :, 0:d] = kw_ref[:, g * d:(g + 1) * d]
            kwa_s[g, :, d:] = aug_ref[...]
            kca_s[g, :, 0:d] = kc_ref[g]
            kca_s[g, :, d:] = caug_ref[...]
            vsa_s[g, 0:d, :] = vst_ref[g * d:(g + 1) * d, :]
            vsa_s[g, d:, :] = jnp.ones((vsa_s.shape[1] - d, vsa_s.shape[2]), BF16)
            vwa_s[g, 0:d, :] = vwt_ref[g * d:(g + 1) * d, :]
            vwa_s[g, d:, :] = jnp.ones((vwa_s.shape[1] - d, vwa_s.shape[2]), BF16)

    def slope(g, hh):
        return 2.0 ** (-(g * HPG + hh + 1))

    def per_head(fn):
        return jnp.concatenate([fn(hh) for hh in range(HPG)], axis=1)

    def tile_heads(x):
        return jnp.concatenate([x] * HPG, axis=1)

    q_t = [(q_ref[:, g * HPG * d:(g + 1) * HPG * d].astype(F32) * (LOG2E * 0.125)).T for g in groups]

    def q_head(g, hh):
        return q_t[g][hh * d:(hh + 1) * d]

    def q_aug(g, block_rows):
        return per_head(lambda hh: jnp.concatenate(
            [q_head(g, hh), block_rows, tail_ref[...] * slope(g, hh)], axis=0)).astype(BF16)

    def scores(ka, qa_t, kt):
        return _dot(ka[pl.ds(pl.multiple_of(kt * TK, TK), TK), :], qa_t)

    def stage_scores(buf, ka, qa_t, kt, mask_add=None):
        s = scores(ka, qa_t, kt)
        if mask_add is not None:
            s = s + mask_add
        buf[...] = s
        return jnp.max(s, axis=0, keepdims=True)

    def stage_probs(sbuf, pbuf, tile_max, m):
        m_new = jnp.maximum(m, tile_max)
        pbuf[...] = jnp.exp2(sbuf[...] - m_new).astype(BF16)
        return m_new, jnp.exp2(m - m_new)

    def stage_values(pbuf, va, kt, alpha, acc):
        return alpha * acc + _dot(va[:, pl.ds(pl.multiple_of(kt * TK, TK), TK)], pbuf[...])

    def normalised(acc):
        return acc[0:d] * (1.0 / acc[d:d + 1])

    m0, acc0 = jnp.full((1, R), NEG_BIG, F32), jnp.zeros((vsa_s.shape[1], R), F32)

    qc_t = [per_head(lambda hh: jnp.concatenate([q_head(g, hh), ctail_ref[...] * slope(g, hh)], axis=0)).astype(BF16)
            for g in groups]
    qw_t = [q_aug(g, jnp.zeros((n_sb, TQ), F32)) for g in groups]
    n_r = lax.broadcasted_iota(jnp.int32, (n_cb, TQ), 0)
    t_c = q0 + lax.broadcasted_iota(jnp.int32, (n_cb, TQ), 1)
    ok_c = (n_r * CMP_STRIDE + (CMP_LEN - 1) <= t_c) & (n_r < n_cmp)
    add_c = tile_heads(jnp.where(ok_c, 0.0, NEG_BIG))
    s_c = [_dot(kca_s[g], qc_t[g]) + add_c for g in groups]
    n_win = (WINDOW - 1 + TK - 1) // TK + 1
    kt_win, max_win = [], []
    for back in range(n_win):
        kt_raw = kt_d - back
        kt_win.append(jnp.maximum(kt_raw, 0))
        if back == 0:
            mask_add = tile_heads(wmask_ref[0])
        elif (back + 1) * TK <= WINDOW:
            mask_add = jnp.where(kt_raw >= 0, 0.0, NEG_BIG)
        else:
            mask_add = tile_heads(wmask_ref[back] + jnp.where(kt_raw >= 0, 0.0, NEG_BIG))
        max_win.append([stage_scores(sw_s.at[g, back], kwa_s.at[g], qw_t[g], kt_win[back], mask_add) for g in groups])

    e_c = [jnp.exp2(s_c[g] - jnp.maximum(jnp.max(s_c[g], axis=0, keepdims=True), 0.1 * NEG_BIG)) for g in groups]
    inv_c = [1.0 / jnp.maximum(jnp.sum(e_c[g], axis=0, keepdims=True), 1.0) for g in groups]
    o_c = [_dot(vct_ref[g], e_c[g].astype(BF16)) * inv_c[g] for g in groups]

    jb = lax.broadcasted_iota(jnp.int32, (n_sb, TQ), 0)
    cur = (q0 + lax.broadcasted_iota(jnp.int32, (n_sb, TQ), 1)) // SEL_LEN
    valid = jb <= cur
    forced = (jb == 0) | (jb == cur) | (jb == cur - 1)
    sub = lax.broadcasted_iota(jnp.int32, (8, TQ), 0)
    n_grp = n_sb // 8
    score, grp, rank = [], [], []
    for g in groups:
        p_sum = e_c[g][:, 0:TQ] * inv_c[g][:, 0:TQ]
        for hh in range(1, HPG):
            p_sum = p_sum + e_c[g][:, hh * TQ:(hh + 1) * TQ] * inv_c[g][:, hh * TQ:(hh + 1) * TQ]
        imp = jnp.dot(ovl_ref[...], p_sum, preferred_element_type=F32, precision=HIGHEST)
        score.append(jnp.where(forced, jnp.inf, jnp.where(valid, imp, -jnp.inf)))
        grp.append([score[g][8 * a:8 * a + 8] for a in range(n_grp)])
        rank.append([jnp.zeros((8, TQ), jnp.int32) for _ in range(n_grp)])
    for j in range(n_sb):
        for g in groups:
            r = score[g][j:j + 1, :]
            for a in range(n_grp):
                if a > j // 8:
                    ahead = (r >= grp[g][a]).astype(jnp.int32)
                elif a < j // 8:
                    ahead = (r > grp[g][a]).astype(jnp.int32)
                else:
                    ahead = jnp.where(sub > j % 8, (r >= grp[g][a]).astype(jnp.int32), (r > grp[g][a]).astype(jnp.int32))
                rank[g][a] = rank[g][a] + ahead
    picked = [valid & (jnp.concatenate(rank[g], axis=0) < SEL_TOPN) for g in groups]
    qs_t = [q_aug(g, jnp.where(picked[g], 0.0, NEG_BIG)) for g in groups]

    filler = 1 - kt_d % 2
    n_seq = kt_d + 1 + filler

    def sel_tile(i):
        return jnp.where(i == 0, kt_d, jnp.maximum(i - 1 - filler, 0))

    def a_stage(slot, kt, mask_add=None):
        return [stage_scores(ss_s.at[g, slot], ksa_s.at[g], qs_t[g], kt, mask_add) for g in groups]

    def b_stage(slot, tmax, m):
        out = [stage_probs(ss_s.at[g, slot], ps_s.at[g, slot], tmax[g], m[g]) for g in groups]
        return [o[0] for o in out], [o[1] for o in out]

    def c_stage(slot, kt, alpha, acc):
        return [stage_values(ps_s.at[g, slot], vsa_s.at[g], kt, alpha[g], acc[g]) for g in groups]

    tmax0 = a_stage(0, kt_d, tile_heads(wmask_ref[0]))
    m, alpha = b_stage(0, tmax0, [m0] * G)
    tmax1 = a_stage(1, sel_tile(1), jnp.where(filler == 1, NEG_BIG, 0.0))

    mw, accw = [m0] * G, [acc0] * G
    for back in range(n_win):
        outw = [stage_probs(sw_s.at[g, back], pw_s.at[g, back], max_win[back][g], mw[g]) for g in groups]
        mw = [o[0] for o in outw]
        accw = [stage_values(pw_s.at[g, back], vwa_s.at[g], kt_win[back], outw[g][1], accw[g]) for g in groups]
    o_w = [normalised(accw[g]) for g in groups]

    def sel_body(k, carry):
        m, alpha, acc, tmax1 = carry
        i = 2 * k
        tmax0 = a_stage(0, sel_tile(i + 2))
        m, alpha1 = b_stage(1, tmax1, m)
        tmax1 = a_stage(1, sel_tile(i + 3))
        acc = c_stage(0, sel_tile(i), alpha, acc)
        m, alpha2 = b_stage(0, tmax0, m)
        acc = c_stage(1, sel_tile(i + 1), alpha1, acc)
        return m, alpha2, acc, tmax1

    m, alpha, acc, tmax1 = lax.fori_loop(0, (n_seq - 2) // 2, sel_body, (m, alpha, [acc0] * G, tmax1))
    m, alpha1 = b_stage(1, tmax1, m)
    acc = c_stage(0, sel_tile(n_seq - 2), alpha, acc)
    acc = c_stage(1, sel_tile(n_seq - 1), alpha1, acc)
    o_s = [normalised(acc[g]) for g in groups]

    gates = _sigmoid(gt_ref[...] + gb_ref[...]).T
    outs = []
    for g in groups:
        for hh in range(HPG):
            cols = slice(hh * TQ, (hh + 1) * TQ)
            row = NSA_GATE_LANE0 + 3 * (g * HPG + hh)
            outs.append(gates[row:row + 1] * o_c[g][:, cols] + gates[row + 1:row + 2] * o_s[g][:, cols]
                        + gates[row + 2:row + 3] * o_w[g][:, cols])
    o_ref[...] = jnp.concatenate(outs, axis=0).T.astype(o_ref.dtype)


def nsa_block(zt, zb, zf, gate_bias, kcmp, vcmp_t, q_col_block, k_col_blocks, v_row_blocks):
    B, S, _ = zb.shape
    G, d = NSA_KV_HEADS, HEAD_DIM
    TQ = NSA_TQ
    n_cb = kcmp.shape[2]
    n_cmp = (S - CMP_LEN) // CMP_STRIDE + 1
    n_sb = S // SEL_LEN
    n_terms = len(LOG2E_TERMS)
    kaug = -(-(d + n_sb + 2 * n_terms) // LANES) * LANES
    R = NSA_HPG * TQ
    n_win = (WINDOW - 1 + NSA_TK - 1) // NSA_TK + 1
    assert S % NSA_TK == 0 and NSA_TK == TQ and TQ % SEL_LEN == 0

    cidx = np.arange(n_cb)[None, :] * CMP_STRIDE
    sstart = np.arange(n_sb)[:, None] * SEL_LEN
    ovl = ((cidx < sstart + SEL_LEN) & (cidx + CMP_LEN - 1 >= sstart) & (np.arange(n_cb)[None, :] < n_cmp))
    ovl = jnp.asarray(ovl.astype(np.float32))
    pos = np.arange(S)
    aug = np.zeros((S, kaug - d), np.float32)
    aug[pos, pos // SEL_LEN] = 1.0
    tail = np.zeros((kaug - d - n_sb, TQ), np.float32)
    caug = np.zeros((n_cb, d), np.float32)
    ctail = np.zeros((d, TQ), np.float32)
    for i, term in enumerate(LOG2E_TERMS):
        aug[:, n_sb + i] = pos // SEL_LEN
        aug[:, n_sb + n_terms + i] = pos % SEL_LEN
        tail[i] = term * SEL_LEN
        tail[n_terms + i] = term
        caug[:, i] = np.arange(n_cb)
        ctail[i] = term * CMP_STRIDE
    aug, caug, tail, ctail = jnp.asarray(aug, BF16), jnp.asarray(caug, BF16), jnp.asarray(tail), jnp.asarray(ctail)
    dist = np.arange(n_win)[:, None, None] * NSA_TK + np.arange(TQ)[None, None, :] - np.arange(NSA_TK)[None, :, None]
    wmask = jnp.asarray(np.where((dist >= 0) & (dist < WINDOW), 0.0, NEG_BIG).astype(np.float32))

    def kspec(j):
        return pl.BlockSpec((None, S, G * d), lambda b, i: (b, 0, k_col_blocks[j]))

    def vspec(j):
        return pl.BlockSpec((None, G * d, S), lambda b, i: (b, v_row_blocks[j], 0))

    def const(a):
        return pl.BlockSpec(a.shape, lambda b, i: (0,) * a.ndim)

    W = G * NSA_HPG * d
    return pl.pallas_call(
        functools.partial(_nsa_kernel, n_cmp=n_cmp), grid=(B, S // TQ),
        in_specs=[pl.BlockSpec((None, TQ, W), lambda b, i: (b, i, q_col_block)),
                  pl.BlockSpec((None, TQ, LANES), lambda b, i: (b, i, GATE_COL_BLOCK)),
                  const(gate_bias),
                  pl.BlockSpec((None, G, n_cb, d), lambda b, i: (b, 0, 0, 0)),
                  pl.BlockSpec((None, G, d, n_cb), lambda b, i: (b, 0, 0, 0)),
                  kspec(0), vspec(0), kspec(1), vspec(1), const(ovl), const(aug), const(caug), const(tail),
                  const(ctail), const(wmask)],
        out_specs=pl.BlockSpec((None, TQ, W), lambda b, i: (b, i, 0)),
        out_shape=jax.ShapeDtypeStruct((B, S, NSA_W), BF16),
        scratch_shapes=[pltpu.VMEM((G, S, kaug), BF16), pltpu.VMEM((G, S, kaug), BF16),
                        pltpu.VMEM((G, n_cb, 2 * d), BF16),
                        pltpu.VMEM((G, d + NSA_ONES, S), BF16), pltpu.VMEM((G, d + NSA_ONES, S), BF16),
                        pltpu.VMEM((G, 2, NSA_TK, R), F32), pltpu.VMEM((G, 2, NSA_TK, R), BF16),
                        pltpu.VMEM((G, n_win, NSA_TK, R), F32), pltpu.VMEM((G, n_win, NSA_TK, R), BF16)],
        compiler_params=_cparams(("parallel", "arbitrary")), name="nsa",
    )(zb, zf, gate_bias, kcmp, vcmp_t, zb, zt, zb, zt, ovl, aug, caug, tail, ctail, wmask)


def _rglru_kernel(h_ref, hh_ref, g_ref, wg_ref, wi_ref, cw_ref, cb_ref, wa_ref, wx_ref, ba_ref, bx_ref, lam_ref,
                  o_ref, h_s, au_s):
    NB, T, W = o_ref.shape
    first = pl.program_id(1) == 0

    @pl.when(first)
    def _():
        h_s[...] = jnp.zeros_like(h_s)

    def project(nb):
        xn = _rms(h_ref[nb], g_ref[...]).astype(BF16)
        x_halo = _dot(_rms(hh_ref[nb], g_ref[...]).astype(BF16), wi_ref[...])
        xe = jnp.concatenate([jnp.where(first, 0.0, x_halo), _dot(xn, wi_ref[...])], axis=0)
        return _dot(xn, wg_ref[...]), xe

    def recur(nb, gate, xe):
        xc = cb_ref[...] + cw_ref[3:4, :] * xe
        for k in range(1, LRU_CONV):
            xc = xc + cw_ref[3 - k:4 - k, :] * pltpu.roll(xe, k, 0)
        xc = xc[LRU_HALO:, :]
        xcb = xc.astype(BF16)
        half = W // 2

        def blockdiag(w_ref):
            return jnp.concatenate([_dot(xcb[:, :half], w_ref[0]), _dot(xcb[:, half:], w_ref[1])], axis=1)

        r = _sigmoid(blockdiag(wa_ref) + ba_ref[...])
        i = _sigmoid(blockdiag(wx_ref) + bx_ref[...])
        nl = -lam_ref[...]
        softplus = jnp.maximum(nl, 0.0) + jnp.log1p(jnp.exp(-jnp.abs(nl)))
        log_a = -LRU_C * r * softplus
        a = jnp.exp(log_a)
        one_m_a2 = -jnp.tanh(log_a) * (a * a + 1.0)
        u = jnp.where(one_m_a2 > 0.0, one_m_a2 * lax.rsqrt(one_m_a2), 0.0) * (i * xc)

        n_grp, n_slab = T // 8, W // LANES

        def phases(x, slab):
            for c in range(n_slab):
                au_s[nb, slab, c] = x[:, c * LANES:(c + 1) * LANES]
            return [jnp.concatenate([au_s[nb, slab, c, pl.ds(j, n_grp, stride=8), :] for c in range(n_slab)], axis=1)
                    for j in range(8)]

        a_ph, u_ph = phases(a, 0), phases(u, 1)
        prod, part = [a_ph[0]], [u_ph[0]]
        for j in range(1, 8):
            part.append(a_ph[j] * part[-1] + u_ph[j])
            prod.append(a_ph[j] * prod[-1])
        grp = lax.broadcasted_iota(jnp.int32, (n_grp, W), 0)
        ag, ug = prod[7], part[7]
        sft = 1
        while sft < n_grp:
            keep = grp >= sft
            ug = ag * jnp.where(keep, pltpu.roll(ug, sft, 0), 0.0) + ug
            ag = ag * jnp.where(keep, pltpu.roll(ag, sft, 0), 1.0)
            sft *= 2
        h_prev = h_s[nb, 0:1, :]
        hg = ug + ag * h_prev
        carry_in = jnp.where(grp >= 1, pltpu.roll(hg, 1, 0), h_prev)
        for j in range(8):
            hj = part[j] + prod[j] * carry_in
            for c in range(n_slab):
                au_s[nb, 0, c, pl.ds(j, n_grp, stride=8), :] = hj[:, c * LANES:(c + 1) * LANES]
        h = jnp.concatenate([au_s[nb, 0, c] for c in range(n_slab)], axis=1)
        h_s[nb] = jnp.broadcast_to(hg[n_grp - 1:n_grp, :], h_s.shape[1:])
        o_ref[nb] = (_gelu(gate) * h).astype(o_ref.dtype)

    nxt = project(0)
    for nb in range(NB):
        cur = nxt
        if nb + 1 < NB:
            nxt = project(nb + 1)
        recur(nb, *cur)


def rglru_block(h3, pre_g, w_gate, w_in, conv_w, conv_b, wa, ba, wx, bx, lam, T=LRU_T, NB=LRU_NB):
    B, S, D = h3.shape
    W = LRU_W
    half = W // 2
    hb = T // LRU_HALO

    def bd(w):
        blocks = [jax.scipy.linalg.block_diag(*[w[h] for h in range(4 * j, 4 * j + 4)]) for j in range(2)]
        return jnp.stack(blocks).astype(BF16)

    vec = lambda a: a.reshape(1, W)
    assert B % NB == 0 and S % T == 0
    vspec = pl.BlockSpec((1, W), lambda b, t: (0, 0))
    wspec = pl.BlockSpec((2, half, half), lambda b, t: (0, 0, 0))
    return pl.pallas_call(
        _rglru_kernel, grid=(B // NB, S // T),
        in_specs=[pl.BlockSpec((NB, T, D), lambda b, t: (b, t, 0)),
                  pl.BlockSpec((NB, LRU_HALO, D), lambda b, t: (b, jnp.maximum(t * hb - 1, 0), 0)),
                  pl.BlockSpec((1, D), lambda b, t: (0, 0)),
                  pl.BlockSpec((D, W), lambda b, t: (0, 0)), pl.BlockSpec((D, W), lambda b, t: (0, 0)),
                  pl.BlockSpec((LRU_CONV, W), lambda b, t: (0, 0)), vspec, wspec, wspec, vspec, vspec, vspec],
        out_specs=pl.BlockSpec((NB, T, W), lambda b, t: (b, t, 0)),
        out_shape=jax.ShapeDtypeStruct((B, S, W), BF16),
        scratch_shapes=[pltpu.VMEM((NB, 8, W), F32), pltpu.VMEM((NB, 2, W // LANES, T, LANES), F32)],
        compiler_params=_cparams(("parallel", "arbitrary")), name="rglru",
    )(h3, h3, pre_g.reshape(1, D), w_gate, w_in, conv_w, vec(conv_b), bd(wa), bd(wx), vec(ba), vec(bx), vec(lam))


def _sgu_kernel(h_ref, gpre_ref, wu_ref, wv_ref, g_ref, b_ref, w_ref, bias_ref, o_ref):
    C, W = SG_CHUNK, o_ref.shape[1]
    dg = W // SG_GROUPS
    n_sub = o_ref.shape[0] // SG_SUB
    row = lax.broadcasted_iota(jnp.int32, (C, C), 0)
    col = lax.broadcasted_iota(jnp.int32, (C, C), 1)
    wc = [jnp.where(col <= row, w_ref[gi], 0.0).astype(BF16) for gi in range(SG_GROUPS)]

    def project(j):
        xn = _rms(h_ref[j * SG_SUB:(j + 1) * SG_SUB, :], gpre_ref[...]).astype(BF16)
        return _dot(xn, wu_ref[...]), _dot(xn, wv_ref[...])

    def gate(j, u, v):
        u, v = _gelu(u), _gelu(v)
        mu = jnp.mean(v, axis=-1, keepdims=True)
        vc = v - mu
        vn = (vc * lax.rsqrt(jnp.mean(vc * vc, axis=-1, keepdims=True) + EPS) * g_ref[...] + b_ref[...]).astype(BF16)
        for gi in range(SG_GROUPS):
            sl = slice(gi * dg, (gi + 1) * dg)
            for c in range(SG_SUB // C):
                rows = slice(c * C, (c + 1) * C)
                mixed = _dot(wc[gi], vn[rows, sl]) + bias_ref[:, sl]
                o_ref[j * SG_SUB + c * C:j * SG_SUB + (c + 1) * C, sl] = (u[rows, sl] * mixed).astype(o_ref.dtype)

    nxt = project(0)
    for j in range(n_sub):
        cur = nxt
        if j + 1 < n_sub:
            nxt = project(j + 1)
        gate(j, *cur)


def sgu_block(h3, pre_g, w_u, w_v, ln_g, ln_b, w, b):
    B, S, D = h3.shape
    W, C, T = SG_W, SG_CHUNK, SG_ROWS
    assert S % T == 0 and T % SG_SUB == 0 and SG_SUB % C == 0
    bias = jnp.repeat(b.T, W // SG_GROUPS, axis=1)
    vspec = pl.BlockSpec((1, W), lambda bb, c: (0, 0))
    return pl.pallas_call(
        _sgu_kernel, grid=(B, S // T),
        in_specs=[pl.BlockSpec((None, T, D), lambda bb, c: (bb, c, 0)),
                  pl.BlockSpec((1, D), lambda bb, c: (0, 0)),
                  pl.BlockSpec((D, W), lambda bb, c: (0, 0)), pl.BlockSpec((D, W), lambda bb, c: (0, 0)),
                  vspec, vspec,
                  pl.BlockSpec((SG_GROUPS, C, C), lambda bb, c: (0, 0, 0)),
                  pl.BlockSpec((C, W), lambda bb, c: (0, 0))],
        out_specs=pl.BlockSpec((None, T, W), lambda bb, c: (bb, c, 0)),
        out_shape=jax.ShapeDtypeStruct((B, S, W), BF16),
        compiler_params=_cparams(("parallel", "parallel")), name="sgu",
    )(h3, pre_g.reshape(1, D), w_u, w_v, ln_g.reshape(1, W), ln_b.reshape(1, W), w, bias)


def _mixer_ab(h2, B, S, pre_g, w_in, ml_gate_b, ml_norm_g, nsa_gate_b, k_pe, k_w1, k_w2, v_pe, v_w1, v_w2):
    D = h2.shape[1]
    G, d = NSA_KV_HEADS, HEAD_DIM
    offs = np.cumsum([0, ML_W, ML_W, ML_W, ML_W, 2 * ML_HEADS, NSA_W] + [G * d] * 6 + [3 * NSA_HEADS])
    mq, mk, mv, mo, mif, nq, kc, vc, ks, vs, kw, vw, ng = [w_in[:, offs[i]:offs[i + 1]] for i in range(13)]
    w_b = jnp.concatenate([mk, nq, ks, kw], axis=1).astype(BF16)
    w_t = jnp.concatenate([mq, mv, vs, vw], axis=1).T.astype(BF16)
    gpad = LANES - 2 * ML_HEADS - 3 * NSA_HEADS
    w_f = jnp.concatenate([mo, kc, vc, mif, ng, jnp.zeros((D, gpad), w_in.dtype)], axis=1).astype(BF16)
    gate_bias = jnp.concatenate([ml_gate_b, nsa_gate_b, jnp.zeros((gpad,), F32)]).reshape(1, LANES)
    zb, zf, zt = norm_proj(h2, pre_g, [w_b, w_f], [BF16, F32], wts=[w_t], batch=B)
    zb = zb.reshape(B, S, -1)
    zf = zf.reshape(B, S, -1)
    h_ml = mlstm_block(zt, zb, zf, gate_bias, ml_norm_g)
    kcmp, vcmp_t = compress_block(zf, ML_W // (G * d), ML_W // (G * d) + 1, k_pe, k_w1, k_w2, v_pe, v_w1, v_w2)
    h_nsa = nsa_block(zt, zb, zf, gate_bias, kcmp, vcmp_t, q_col_block=ML_W // NSA_W,
                      k_col_blocks=((ML_W + NSA_W) // (G * d), (ML_W + NSA_W) // (G * d) + 1),
                      v_row_blocks=(2 * ML_W // (G * d), 2 * ML_W // (G * d) + 1))
    return h_ml.reshape(B * S, ML_W), h_nsa.reshape(B * S, NSA_W)


def _mixer_cd(h2, B, S, pre_g, w_in, conv_w, conv_b, wa, ba, wx, bx, lam, sg_g, sg_bn, sg_w, sg_b):
    h3 = h2.reshape(B, S, -1)
    w_gate, w_x, w_u, w_v = (w_in[:, j * LRU_W:(j + 1) * LRU_W].astype(BF16) for j in range(4))
    y_lru = rglru_block(h3, pre_g, w_gate, w_x, conv_w, conv_b, wa, ba, wx, bx, lam)
    y_sg = sgu_block(h3, pre_g, w_u, w_v, sg_g, sg_bn, sg_w, sg_b)
    return y_lru.reshape(B * S, LRU_W), y_sg.reshape(B * S, SG_W)


def kernel(x, pre_mix_g, post_mix_g, pre_ffn_g, post_ffn_g, ab_w_in, ab_w_out, ml_gate_b, ml_norm_g, nsa_gate_b, cmp_k_pe, cmp_k_w1, cmp_k_w2, cmp_v_pe, cmp_v_w1, cmp_v_w2, cd_w_in, cd_w_out, lru_conv_w, lru_conv_b, lru_wa, lru_ba, lru_wx, lru_bx, lru_lambda, sg_norm_g, sg_norm_b, sg_w, sg_b, ffn_w_up, ffn_conv_w, ffn_conv_b, ffn_w_down):
    B, S, D = x.shape
    depth = pre_mix_g.shape[0]
    h2 = x.reshape(B * S, D)
    for layer in range(depth):
        if layer % 2 == 0:
            e = layer // 2
            a1, a2 = _mixer_ab(h2, B, S, pre_mix_g[layer], ab_w_in[e], ml_gate_b[e], ml_norm_g[e], nsa_gate_b[e],
                               cmp_k_pe[e], cmp_k_w1[e], cmp_k_w2[e], cmp_v_pe[e], cmp_v_w1[e], cmp_v_w2[e])
            w_out = ab_w_out[e]
        else:
            o = layer // 2
            a1, a2 = _mixer_cd(h2, B, S, pre_mix_g[layer], cd_w_in[o], lru_conv_w[o], lru_conv_b[o], lru_wa[o],
                               lru_ba[o], lru_wx[o], lru_bx[o], lru_lambda[o], sg_norm_g[o], sg_norm_b[o], sg_w[o],
                               sg_b[o])
            w_out = cd_w_out[o]
        h2 = mix_ffn_block(h2, a1, a2, S, w_out, post_mix_g[layer], pre_ffn_g[layer], ffn_w_up[layer],
                           ffn_conv_w[layer], ffn_conv_b[layer], ffn_w_down[layer], post_ffn_g[layer])
    return h2.reshape(B, S, D)
```

```python
import functools

import numpy as np
import jax
import jax.numpy as jnp
from jax import lax
from jax.experimental import pallas as pl
from jax.experimental.pallas import tpu as pltpu

F32 = jnp.float32
BF16 = jnp.bfloat16

EPS = 1e-6
HEAD_DIM = 64
ML_HEADS = 8
ML_W = 512
GATE_SOFTCAP = 15.0
NSA_HEADS = 8
NSA_KV_HEADS = 2
NSA_HPG = NSA_HEADS // NSA_KV_HEADS
NSA_W = 512
CMP_LEN = 32
CMP_STRIDE = 16
CMP_HID = 128
SEL_LEN = 64
SEL_TOPN = 16
WINDOW = 512
LRU_W = 512
LRU_C = 8.0
LRU_CONV = 4
SG_GROUPS = 8
SG_W = 512
SG_CHUNK = 128
FFN_CONV = 3
GATE_COL_BLOCK = (ML_W + 2 * NSA_KV_HEADS * HEAD_DIM) // 128
NSA_GATE_LANE0 = 2 * ML_HEADS

LANES = 128
VMEM_LIMIT = 56 * 1024 * 1024
NEG_BIG = -1e30
HIGHEST = lax.Precision.HIGHEST
LOG2E = 1.4426950408889634


def _bf16_terms(x, n):
    terms = []
    for _ in range(n):
        bits = int(np.array(x, np.float32).view(np.uint32))
        t = float(np.array((bits + 0x7FFF + ((bits >> 16) & 1)) & 0xFFFF0000, np.uint32).view(np.float32))
        terms.append(t)
        x -= t
    return tuple(terms)


LOG2E_TERMS = _bf16_terms(LOG2E, 3)

ML_CHUNK = 128
ML_NB = 8
NSA_TQ = 256
NSA_TK = 256
RANK_STEP = 16
NSA_ONES = 16
ROW_TILE = 512
FFN_TM = 512
FFN_SUB = 256
FFN_CK = 256
FFN_HALO = 16
SG_ROWS = 1024
SG_SUB = 256
LRU_T = 512
LRU_HALO = 8
LRU_NB = 4


def _cparams(sem):
    return pltpu.CompilerParams(dimension_semantics=sem, vmem_limit_bytes=VMEM_LIMIT)


def _rms(x, g):
    return x * lax.rsqrt(jnp.mean(x * x, axis=-1, keepdims=True) + EPS) * g


def _gelu(x):
    return 0.5 * x * (1.0 + jnp.tanh(0.7978845608028654 * (x + 0.044715 * (x * x * x))))


def _sigmoid(x):
    return 1.0 / (1.0 + jnp.exp(-x))


def _dot(a, b):
    return jnp.dot(a, b, preferred_element_type=F32)


def _dot_nt(a, b, precision=None):
    return lax.dot_general(a, b, (((1,), (1,)), ((), ())), preferred_element_type=F32, precision=precision)


def _norm_proj_kernel(h_ref, g_ref, *refs, n_row, n_t, cn):
    w_refs, wt_refs = refs[:n_row], refs[n_row:n_row + n_t]
    o_refs, ot_refs = refs[n_row + n_t:2 * n_row + n_t], refs[2 * n_row + n_t:]
    xn = _rms(h_ref[...], g_ref[...]).astype(BF16)
    for w_ref, o_ref in zip(w_refs, o_refs):
        n = w_ref.shape[1]
        for c in range(0, n, cn):
            ce = min(c + cn, n)
            o_ref[:, c:ce] = _dot(xn, w_ref[:, c:ce]).astype(o_ref.dtype)
    for wt_ref, ot_ref in zip(wt_refs, ot_refs):
        n = wt_ref.shape[0]
        for c in range(0, n, cn):
            ce = min(c + cn, n)
            ot_ref[c:ce, :] = _dot_nt(wt_ref[c:ce, :], xn).astype(ot_ref.dtype)


def norm_proj(h2, g, ws, dtypes, wts=(), batch=1, tm=ROW_TILE):
    M, D = h2.shape
    tps = M // batch // tm
    in_specs = [pl.BlockSpec((tm, D), lambda i: (i, 0)), pl.BlockSpec((1, D), lambda i: (0, 0))]
    in_specs += [pl.BlockSpec(w.shape, lambda i: (0, 0)) for w in (*ws, *wts)]
    out_specs = [pl.BlockSpec((tm, w.shape[1]), lambda i: (i, 0)) for w in ws]
    out_specs += [pl.BlockSpec((None, w.shape[0], tm), lambda i: (i // tps, 0, i % tps)) for w in wts]
    out_shape = [jax.ShapeDtypeStruct((M, w.shape[1]), dt) for w, dt in zip(ws, dtypes)]
    out_shape += [jax.ShapeDtypeStruct((batch, w.shape[0], M // batch), BF16) for w in wts]
    return pl.pallas_call(
        functools.partial(_norm_proj_kernel, n_row=len(ws), n_t=len(wts), cn=512),
        grid=(M // tm,), in_specs=in_specs, out_specs=out_specs, out_shape=out_shape,
        compiler_params=_cparams(("parallel",)), name="norm_proj",
    )(h2, g.reshape(1, D), *ws, *wts)


def _mix_ffn_kernel(h_ref, hh_ref, a1_ref, a1h_ref, a2_ref, a2h_ref, wo1_ref, wo2_ref, gmix_ref, gpre_ref, wu_ref,
                    cw_ref, cb_ref, wd_ref, gpost_ref, o_ref, xn_s, acc_s, *, tiles_per_seq, ck):
    F = wd_ref.shape[0]
    n_chunks = F // ck
    n_sub = h_ref.shape[0] // FFN_SUB
    first = (pl.program_id(0) % tiles_per_seq) == 0
    g = gpre_ref[...]

    def mixed(h, a1, a2):
        return h + _rms(_dot(a1, wo1_ref[...]) + _dot(a2, wo2_ref[...]), gmix_ref[...])

    def conv(u, cols):
        y = (cw_ref[2:3, cols] * u + cw_ref[1:2, cols] * pltpu.roll(u, 1, 0) + cw_ref[0:1, cols] * pltpu.roll(u, 2, 0)
             + cb_ref[:, cols])
        return y[FFN_HALO:, :]

    def cols_of(c, half):
        return slice(half * F + c * ck, half * F + (c + 1) * ck)

    x = {}

    def head(j):
        rows = slice(j * FFN_SUB, (j + 1) * FFN_SUB)
        x[j] = mixed(h_ref[rows, :], a1_ref[rows, :], a2_ref[rows, :])
        if j == 0:
            halo = jnp.where(first, 0.0, _rms(mixed(hh_ref[...], a1h_ref[...], a2h_ref[...]), g))
            xn_s[0:FFN_HALO, :] = halo.astype(BF16)
        xn_s[FFN_HALO + j * FFN_SUB:FFN_HALO + (j + 1) * FFN_SUB, :] = _rms(x[j], g).astype(BF16)
        acc_s[rows, :] = jnp.zeros((FFN_SUB, acc_s.shape[1]), F32)

    def up(j, c):
        xn = xn_s[j * FFN_SUB:(j + 1) * FFN_SUB + FFN_HALO, :]
        return _dot(xn, wu_ref[:, cols_of(c, 0)]), _dot(xn, wu_ref[:, cols_of(c, 1)])

    def tail(j):
        rows = slice(j * FFN_SUB, (j + 1) * FFN_SUB)
        o_ref[rows, :] = x[j] + _rms(acc_s[rows, :], gpost_ref[...])

    head(0)
    for j in range(n_sub):
        u = up(j, 0)
        for c in range(n_chunks):
            u_next = up(j, c + 1) if c + 1 < n_chunks else None
            if c == 0 and j + 1 < n_sub:
                head(j + 1)
            if c == n_chunks // 3 and j >= 1:
                tail(j - 1)
            act = (_gelu(conv(u[0], cols_of(c, 0))) * conv(u[1], cols_of(c, 1))).astype(BF16)
            acc_s[j * FFN_SUB:(j + 1) * FFN_SUB, :] += _dot(act, wd_ref[c * ck:(c + 1) * ck, :])
            u = u_next
    tail(n_sub - 1)


def mix_ffn_block(h2, a1, a2, seq_len, w_out, g_mix, g_pre, w_up, conv_w, conv_b, w_down, g_post, tm=FFN_TM, ck=FFN_CK):
    M, D = h2.shape
    F = w_down.shape[0]
    K1, K2 = a1.shape[1], a2.shape[1]
    assert F % ck == 0 and seq_len % tm == 0 and tm % FFN_SUB == 0 and FFN_SUB % FFN_HALO == 0
    assert conv_w.shape[0] == FFN_CONV and FFN_CONV - 1 <= FFN_HALO
    hb = tm // FFN_HALO
    consts = (w_out[:K1].astype(BF16), w_out[K1:].astype(BF16), g_mix.reshape(1, D), g_pre.reshape(1, D),
              w_up.astype(BF16), conv_w, conv_b.reshape(1, 2 * F), w_down.astype(BF16), g_post.reshape(1, D))

    def tile(width):
        return pl.BlockSpec((tm, width), lambda i: (i, 0))

    def halo(width):
        return pl.BlockSpec((FFN_HALO, width), lambda i: (jnp.maximum(i * hb - 1, 0), 0))

    def full(a):
        return pl.BlockSpec(a.shape, lambda i: (0,) * a.ndim)

    return pl.pallas_call(
        functools.partial(_mix_ffn_kernel, tiles_per_seq=seq_len // tm, ck=ck),
        grid=(M // tm,),
        in_specs=[tile(D), halo(D), tile(K1), halo(K1), tile(K2), halo(K2)] + [full(a) for a in consts],
        out_specs=tile(D),
        out_shape=jax.ShapeDtypeStruct((M, D), F32),
        scratch_shapes=[pltpu.VMEM((tm + FFN_HALO, D), BF16), pltpu.VMEM((tm, D), F32)],
        compiler_params=_cparams(("parallel",)), name="mix_ffn",
    )(h2, h2, a1, a1, a2, a2, *consts)


def _mlstm_kernel(qt_ref, vt_ref, k_ref, mo_ref, gt_ref, gb_ref, ng_ref, o_ref, cn_s, m_s):
    NB, _, L = qt_ref.shape
    d = HEAD_DIM

    @pl.when(pl.program_id(1) == 0)
    def _():
        cn_s[...] = jnp.zeros_like(cn_s)
        m_s[...] = jnp.zeros_like(m_s)

    src = lax.broadcasted_iota(jnp.int32, (L, L), 0)
    tgt = lax.broadcasted_iota(jnp.int32, (L, L), 1)
    causal = src <= tgt
    tri = (tgt <= src).astype(F32)

    gcap = [GATE_SOFTCAP * jnp.tanh((gt_ref[nb] + gb_ref[...]) * (1.0 / GATE_SOFTCAP)) for nb in range(NB)]
    lf = [jnp.minimum(x, 0.0) - jnp.log1p(jnp.exp(-jnp.abs(x))) for x in gcap]
    b_col = [jnp.dot(tri, x, preferred_element_type=F32, precision=HIGHEST) for x in lf]
    b_row = [_dot_nt(x.T, tri, precision=HIGHEST) for x in lf]
    i_row = [x.T for x in gcap]
    c_col = [b_col[nb] - pltpu.roll(gcap[nb], ML_HEADS, 1) for nb in range(NB)]

    chains = [(nb, h) for nb in range(NB) for h in range(ML_HEADS)]
    n = range(len(chains))

    def rows(h):
        return slice(h * d, (h + 1) * d)

    q_t = [qt_ref[nb, rows(h), :] for nb, h in chains]
    v_t = [vt_ref[nb, rows(h), :] for nb, h in chains]
    k = [k_ref[nb, :, rows(h)] * 0.125 for nb, h in chains]
    br = [b_row[nb][ML_HEADS + h:ML_HEADS + h + 1, :] for nb, h in chains]
    ir = [i_row[nb][h:h + 1, :] for nb, h in chains]
    g = [x[:, L - 1:L] for x in br]
    m_prev = [m_s[nb, h:h + 1, 0:1] for nb, h in chains]
    cn_prev = [cn_s[nb, h] for nb, h in chains]
    kq = [_dot(k[c], q_t[c]) for c in n]
    carry = [_dot(cn_prev[c].astype(BF16), q_t[c]) for c in n]
    dlog = [jnp.where(causal, br[c] - c_col[nb][:, ML_HEADS + h:ML_HEADS + h + 1], -jnp.inf)
            for c, (nb, h) in enumerate(chains)]
    inter = [br[c] + m_prev[c] for c in n]
    m_row = [jnp.maximum(inter[c], jnp.max(dlog[c], axis=0, keepdims=True)) for c in n]
    s = [kq[c] * jnp.exp(dlog[c] - m_row[c]) for c in n]
    w_inter = [jnp.exp(inter[c] - m_row[c]) for c in n]
    num = [_dot(v_t[c], s[c].astype(BF16)) + w_inter[c] * carry[c][0:d] for c in n]
    den = [jnp.sum(s[c], axis=0, keepdims=True) + w_inter[c] * carry[c][d:d + 1] for c in n]
    hh = [num[c] * (1.0 / jnp.maximum(jnp.abs(den[c]), jnp.exp(-m_row[c]))) for c in n]
    wlog = [g[c] - br[c] + ir[c] for c in n]
    m_new = [jnp.maximum(g[c] + m_prev[c], jnp.max(wlog[c], axis=-1, keepdims=True)) for c in n]
    w_row = [jnp.exp(wlog[c] - m_new[c]) for c in n]
    decay = [jnp.exp(g[c] + m_prev[c] - m_new[c]) for c in n]
    for c, (nb, h) in enumerate(chains):
        vw = jnp.concatenate([v_t[c].astype(F32) * w_row[c], jnp.broadcast_to(w_row[c], (8, L))], axis=0).astype(BF16)
        cn_s[nb, h] = decay[c] * cn_prev[c] + _dot(vw, k[c])
        m_s[nb, h:h + 1, :] = jnp.broadcast_to(m_new[c], (1, m_s.shape[2]))
    outs = [hh[c] * lax.rsqrt(jnp.mean(hh[c] * hh[c], axis=0, keepdims=True) + EPS) * ng_ref[rows(h), :]
            for c, (nb, h) in enumerate(chains)]
    for nb in range(NB):
        out_t = jnp.concatenate(outs[nb * ML_HEADS:(nb + 1) * ML_HEADS], axis=0)
        o_ref[nb] = (out_t.T * _sigmoid(mo_ref[nb])).astype(o_ref.dtype)


def mlstm_block(zt, zb, zf, gate_bias, norm_g, L=ML_CHUNK, NB=ML_NB):
    B, S, _ = zb.shape
    W, H, d = ML_W, ML_HEADS, HEAD_DIM
    assert B % NB == 0 and S % L == 0
    ng = jnp.broadcast_to(norm_g.reshape(W, 1), (W, L))
    return pl.pallas_call(
        _mlstm_kernel, grid=(B // NB, S // L),
        in_specs=[pl.BlockSpec((NB, W, L), lambda b, c: (b, 0, c)),
                  pl.BlockSpec((NB, W, L), lambda b, c: (b, 1, c)),
                  pl.BlockSpec((NB, L, W), lambda b, c: (b, c, 0)),
                  pl.BlockSpec((NB, L, W), lambda b, c: (b, c, 0)),
                  pl.BlockSpec((NB, L, LANES), lambda b, c: (b, c, GATE_COL_BLOCK)),
                  pl.BlockSpec((1, LANES), lambda b, c: (0, 0)),
                  pl.BlockSpec((W, L), lambda b, c: (0, 0))],
        out_specs=pl.BlockSpec((NB, L, W), lambda b, c: (b, c, 0)),
        out_shape=jax.ShapeDtypeStruct((B, S, W), BF16),
        scratch_shapes=[pltpu.VMEM((NB, H, d + 8, d), F32), pltpu.VMEM((NB, H, LANES), F32)],
        compiler_params=_cparams(("parallel", "arbitrary")), name="mlstm",
    )(zt, zt, zb, zf, zf, gate_bias, ng)


def _compress_kernel(kc_ref, vc_ref, pek_ref, pev_ref, w1k_ref, w1v_ref, w2k_ref, w2v_ref, ok_ref, ov_ref):
    G = NSA_KV_HEADS
    nh = kc_ref.shape[0] // CMP_STRIDE

    def one(x_ref, pe_ref, w1_ref, w2_ref):
        hid = None
        for l in range(CMP_STRIDE):
            y = x_ref[pl.ds(l, nh, stride=CMP_STRIDE), :]
            ya = (y + pe_ref[l:l + 1, :]).astype(BF16)
            yb = (pltpu.roll(y, nh - 1, 0) + pe_ref[CMP_STRIDE + l:CMP_STRIDE + l + 1, :]).astype(BF16)
            t = _dot(ya, w1_ref[l]) + _dot(yb, w1_ref[CMP_STRIDE + l])
            hid = t if hid is None else hid + t
        hid = _gelu(hid).astype(BF16)
        return [_dot(hid[:, gi * CMP_HID:(gi + 1) * CMP_HID], w2_ref[...]) for gi in range(G)]

    for gi, (ko, vo) in enumerate(zip(one(kc_ref, pek_ref, w1k_ref, w2k_ref), one(vc_ref, pev_ref, w1v_ref, w2v_ref))):
        ok_ref[gi] = ko.astype(ok_ref.dtype)
        ov_ref[gi] = vo.T.astype(ov_ref.dtype)


def compress_block(zf, kc_block, vc_block, k_pe, k_w1, k_w2, v_pe, v_w1, v_w2):
    B, S, _ = zf.shape
    G, d = NSA_KV_HEADS, HEAD_DIM
    nh = S // CMP_STRIDE

    def prep(pe, w1):
        w1bd = jnp.zeros((CMP_LEN, G * d, G * CMP_HID), w1.dtype)
        for gi in range(G):
            w1bd = w1bd.at[:, gi * d:(gi + 1) * d, gi * CMP_HID:(gi + 1) * CMP_HID].set(w1)
        return jnp.tile(pe, (1, G)), w1bd.astype(BF16)

    pek, w1k = prep(k_pe, k_w1)
    pev, w1v = prep(v_pe, v_w1)

    def full(a):
        return pl.BlockSpec(a.shape, lambda b: (0,) * a.ndim)

    w2k, w2v = k_w2.astype(BF16), v_w2.astype(BF16)
    return pl.pallas_call(
        _compress_kernel, grid=(B,),
        in_specs=[pl.BlockSpec((None, S, G * d), lambda b: (b, 0, kc_block)),
                  pl.BlockSpec((None, S, G * d), lambda b: (b, 0, vc_block)),
                  full(pek), full(pev), full(w1k), full(w1v), full(w2k), full(w2v)],
        out_specs=[pl.BlockSpec((None, G, nh, d), lambda b: (b, 0, 0, 0)),
                   pl.BlockSpec((None, G, d, nh), lambda b: (b, 0, 0, 0))],
        out_shape=[jax.ShapeDtypeStruct((B, G, nh, d), BF16), jax.ShapeDtypeStruct((B, G, d, nh), BF16)],
        compiler_params=_cparams(("parallel",)), name="nsa_compress",
    )(zf, zf, pek, pev, w1k, w1v, w2k, w2v)


def _nsa_kernel(q_ref, gt_ref, gb_ref, kc_ref, vct_ref, ks_ref, vst_ref, kw_ref, vwt_ref, ovl_ref, aug_ref, caug_ref,
                tail_ref, ctail_ref, wmask_ref, o_ref, ksa_s, kwa_s, kca_s, vsa_s, vwa_s, ss_s, ps_s, sw_s, pw_s, *, n_cmp):
    TQ, TK, d, HPG, G = NSA_TQ, NSA_TK, HEAD_DIM, NSA_HPG, NSA_KV_HEADS
    R = HPG * TQ
    groups = range(G)
    qi = pl.program_id(1)
    q0 = qi * TQ
    kt_d = q0 // TK
    n_cb = kc_ref.shape[1]
    n_sb = ovl_ref.shape[0]

    @pl.when(qi == 0)
    def _():
        for g in groups:
            ksa_s[g, :, 0:d] = ks_ref[:, g * d:(g + 1) * d]
            ksa_s[g, :, d:] = aug_ref[...]
            kwa_s[g, :, 0:d] = kw_ref[:, g * d:(g + 1) * d]
            kwa_s[g, :, d:] = aug_ref[...]
            kca_s[g, :, 0:d] = kc_ref[g]
            kca_s[g, :, d:] = caug_ref[...]
            vsa_s[g, 0:d, :] = vst_ref[g * d:(g + 1) * d, :]
            vsa_s[g, d:, :] = jnp.ones((vsa_s.shape[1] - d, vsa_s.shape[2]), BF16)
            vwa_s[g, 0:d, :] = vwt_ref[g * d:(g + 1) * d, :]
            vwa_s[g, d:, :] = jnp.ones((vwa_s.shape[1] - d, vwa_s.shape[2]), BF16)

    def slope(g, hh):
        return 2.0 ** (-(g * HPG + hh + 1))

    def per_head(fn):
        return jnp.concatenate([fn(hh) for hh in range(HPG)], axis=1)

    def tile_heads(x):
        return jnp.concatenate([x] * HPG, axis=1)

    q_t = [(q_ref[:, g * HPG * d:(g + 1) * HPG * d].astype(F32) * (LOG2E * 0.125)).T for g in groups]

    def q_head(g, hh):
        return q_t[g][hh * d:(hh + 1) * d]

    def q_aug(g, block_rows):
        return per_head(lambda hh: jnp.concatenate(
            [q_head(g, hh), block_rows, tail_ref[...] * slope(g, hh)], axis=0)).astype(BF16)

    def scores(ka, qa_t, kt):
        return _dot(ka[pl.ds(pl.multiple_of(kt * TK, TK), TK), :], qa_t)

    def stage_scores(buf, ka, qa_t, kt, mask_add=None):
        s = scores(ka, qa_t, kt)
        if mask_add is not None:
            s = s + mask_add
        buf[...] = s
        return jnp.max(s, axis=0, keepdims=True)

    def stage_probs(sbuf, pbuf, tile_max, m):
        m_new = jnp.maximum(m, tile_max)
        pbuf[...] = jnp.exp2(sbuf[...] - m_new).astype(BF16)
        return m_new, jnp.exp2(m - m_new)

    def stage_values(pbuf, va, kt, alpha, acc):
        return alpha * acc + _dot(va[:, pl.ds(pl.multiple_of(kt * TK, TK), TK)], pbuf[...])

    def normalised(acc):
        return acc[0:d] * (1.0 / acc[d:d + 1])

    m0, acc0 = jnp.full((1, R), NEG_BIG, F32), jnp.zeros((vsa_s.shape[1], R), F32)

    qc_t = [per_head(lambda hh: jnp.concatenate([q_head(g, hh), ctail_ref[...] * slope(g, hh)], axis=0)).astype(BF16)
            for g in groups]
    qw_t = [q_aug(g, jnp.zeros((n_sb, TQ), F32)) for g in groups]
    n_r = lax.broadcasted_iota(jnp.int32, (n_cb, TQ), 0)
    t_c = q0 + lax.broadcasted_iota(jnp.int32, (n_cb, TQ), 1)
    ok_c = (n_r * CMP_STRIDE + (CMP_LEN - 1) <= t_c) & (n_r < n_cmp)
    add_c = tile_heads(jnp.where(ok_c, 0.0, NEG_BIG))
    s_c = [_dot(kca_s[g], qc_t[g]) + add_c for g in groups]
    n_win = (WINDOW - 1 + TK - 1) // TK + 1
    kt_win, max_win = [], []
    for back in range(n_win):
        kt_raw = kt_d - back
        kt_win.append(jnp.maximum(kt_raw, 0))
        if back == 0:
            mask_add = tile_heads(wmask_ref[0])
        elif (back + 1) * TK <= WINDOW:
            mask_add = jnp.where(kt_raw >= 0, 0.0, NEG_BIG)
        else:
            mask_add = tile_heads(wmask_ref[back] + jnp.where(kt_raw >= 0, 0.0, NEG_BIG))
        max_win.append([stage_scores(sw_s.at[g, back], kwa_s.at[g], qw_t[g], kt_win[back], mask_add) for g in groups])

    e_c = [jnp.exp2(s_c[g] - jnp.maximum(jnp.max(s_c[g], axis=0, keepdims=True), 0.1 * NEG_BIG)) for g in groups]
    inv_c = [1.0 / jnp.maximum(jnp.sum(e_c[g], axis=0, keepdims=True), 1.0) for g in groups]
    o_c = [_dot(vct_ref[g], e_c[g].astype(BF16)) * inv_c[g] for g in groups]

    jb = lax.broadcasted_iota(jnp.int32, (n_sb, TQ), 0)
    cur = (q0 + lax.broadcasted_iota(jnp.int32, (n_sb, TQ), 1)) // SEL_LEN
    valid = jb <= cur
    forced = (jb == 0) | (jb == cur) | (jb == cur - 1)
    sub = lax.broadcasted_iota(jnp.int32, (8, TQ), 0)
    score = []
    for g in groups:
        p_sum = e_c[g][:, 0:TQ] * inv_c[g][:, 0:TQ]
        for hh in range(1, HPG):
            p_sum = p_sum + e_c[g][:, hh * TQ:(hh + 1) * TQ] * inv_c[g][:, hh * TQ:(hh + 1) * TQ]
        imp = jnp.dot(ovl_ref[...], p_sum, preferred_element_type=F32, precision=HIGHEST)
        score.append(jnp.where(forced, jnp.inf, jnp.where(valid, imp, -jnp.inf)))

    def ranked(n_rows):
        def fn(*score):
            n_grp = n_rows // 8
            outs = []
            grp = [[s[8 * a:8 * a + 8] for a in range(n_grp)] for s in score]
            rank = [[jnp.zeros((8, TQ), jnp.int32) for _ in range(n_grp)] for _ in score]
            for j in range(n_rows):
                for g in groups:
                    r = score[g][j:j + 1, :]
                    for a in range(n_grp):
                        if a > j // 8:
                            ahead = (r >= grp[g][a]).astype(jnp.int32)
                        elif a < j // 8:
                            ahead = (r > grp[g][a]).astype(jnp.int32)
                        else:
                            ahead = jnp.where(sub > j % 8, (r >= grp[g][a]).astype(jnp.int32),
                                              (r > grp[g][a]).astype(jnp.int32))
                        rank[g][a] = rank[g][a] + ahead
            for g in groups:
                top = jnp.where(jnp.concatenate(rank[g], axis=0) < SEL_TOPN, 1.0, 0.0)
                outs.append(jnp.concatenate([top, jnp.zeros((n_sb - n_rows, TQ), F32)], axis=0) if n_rows < n_sb else top)
            return tuple(outs)
        return fn

    sizes = [n for n in range(RANK_STEP, n_sb + 1, RANK_STEP)]
    rows_needed = (q0 + TQ + SEL_LEN - 1) // SEL_LEN
    in_top = lax.switch((rows_needed + RANK_STEP - 1) // RANK_STEP - 1, [ranked(n) for n in sizes], *score)
    picked = [valid & (in_top[g] > 0.5) for g in groups]
    qs_t = [q_aug(g, jnp.where(picked[g], 0.0, NEG_BIG)) for g in groups]

    filler = 1 - kt_d % 2
    n_seq = kt_d + 1 + filler

    def sel_tile(i):
        return jnp.where(i == 0, kt_d, jnp.maximum(i - 1 - filler, 0))

    def a_stage(slot, kt, mask_add=None):
        return [stage_scores(ss_s.at[g, slot], ksa_s.at[g], qs_t[g], kt, mask_add) for g in groups]

    def b_stage(slot, tmax, m):
        out = [stage_probs(ss_s.at[g, slot], ps_s.at[g, slot], tmax[g], m[g]) for g in groups]
        return [o[0] for o in out], [o[1] for o in out]

    def c_stage(slot, kt, alpha, acc):
        return [stage_values(ps_s.at[g, slot], vsa_s.at[g], kt, alpha[g], acc[g]) for g in groups]

    tmax0 = a_stage(0, kt_d, tile_heads(wmask_ref[0]))
    m, alpha = b_stage(0, tmax0, [m0] * G)
    tmax1 = a_stage(1, sel_tile(1), jnp.where(filler == 1, NEG_BIG, 0.0))

    mw, accw = [m0] * G, [acc0] * G
    for back in range(n_win):
        outw = [stage_probs(sw_s.at[g, back], pw_s.at[g, back], max_win[back][g], mw[g]) for g in groups]
        mw = [o[0] for o in outw]
        accw = [stage_values(pw_s.at[g, back], vwa_s.at[g], kt_win[back], outw[g][1], accw[g]) for g in groups]
    o_w = [normalised(accw[g]) for g in groups]

    def sel_body(k, carry):
        m, alpha, acc, tmax1 = carry
        i = 2 * k
        tmax0 = a_stage(0, sel_tile(i + 2))
        m, alpha1 = b_stage(1, tmax1, m)
        tmax1 = a_stage(1, sel_tile(i + 3))
        acc = c_stage(0, sel_tile(i), alpha, acc)
        m, alpha2 = b_stage(0, tmax0, m)
        acc = c_stage(1, sel_tile(i + 1), alpha1, acc)
        return m, alpha2, acc, tmax1

    m, alpha, acc, tmax1 = lax.fori_loop(0, (n_seq - 2) // 2, sel_body, (m, alpha, [acc0] * G, tmax1))
    m, alpha1 = b_stage(1, tmax1, m)
    acc = c_stage(0, sel_tile(n_seq - 2), alpha, acc)
    acc = c_stage(1, sel_tile(n_seq - 1), alpha1, acc)
    o_s = [normalised(acc[g]) for g in groups]

    gates = _sigmoid(gt_ref[...] + gb_ref[...]).T
    outs = []
    for g in groups:
        for hh in range(HPG):
            cols = slice(hh * TQ, (hh + 1) * TQ)
            row = NSA_GATE_LANE0 + 3 * (g * HPG + hh)
            outs.append(gates[row:row + 1] * o_c[g][:, cols] + gates[row + 1:row + 2] * o_s[g][:, cols]
                        + gates[row + 2:row + 3] * o_w[g][:, cols])
    o_ref[...] = jnp.concatenate(outs, axis=0).T.astype(o_ref.dtype)


def nsa_block(zt, zb, zf, gate_bias, kcmp, vcmp_t, q_col_block, k_col_blocks, v_row_blocks):
    B, S, _ = zb.shape
    G, d = NSA_KV_HEADS, HEAD_DIM
    TQ = NSA_TQ
    n_cb = kcmp.shape[2]
    n_cmp = (S - CMP_LEN) // CMP_STRIDE + 1
    n_sb = S // SEL_LEN
    n_terms = len(LOG2E_TERMS)
    kaug = -(-(d + n_sb + 2 * n_terms) // LANES) * LANES
    R = NSA_HPG * TQ
    n_win = (WINDOW - 1 + NSA_TK - 1) // NSA_TK + 1
    assert S % NSA_TK == 0 and NSA_TK == TQ and TQ % SEL_LEN == 0

    cidx = np.arange(n_cb)[None, :] * CMP_STRIDE
    sstart = np.arange(n_sb)[:, None] * SEL_LEN
    ovl = ((cidx < sstart + SEL_LEN) & (cidx + CMP_LEN - 1 >= sstart) & (np.arange(n_cb)[None, :] < n_cmp))
    ovl = jnp.asarray(ovl.astype(np.float32))
    pos = np.arange(S)
    aug = np.zeros((S, kaug - d), np.float32)
    aug[pos, pos // SEL_LEN] = 1.0
    tail = np.zeros((kaug - d - n_sb, TQ), np.float32)
    caug = np.zeros((n_cb, d), np.float32)
    ctail = np.zeros((d, TQ), np.float32)
    for i, term in enumerate(LOG2E_TERMS):
        aug[:, n_sb + i] = pos // SEL_LEN
        aug[:, n_sb + n_terms + i] = pos % SEL_LEN
        tail[i] = term * SEL_LEN
        tail[n_terms + i] = term
        caug[:, i] = np.arange(n_cb)
        ctail[i] = term * CMP_STRIDE
    aug, caug, tail, ctail = jnp.asarray(aug, BF16), jnp.asarray(caug, BF16), jnp.asarray(tail), jnp.asarray(ctail)
    dist = np.arange(n_win)[:, None, None] * NSA_TK + np.arange(TQ)[None, None, :] - np.arange(NSA_TK)[None, :, None]
    wmask = jnp.asarray(np.where((dist >= 0) & (dist < WINDOW), 0.0, NEG_BIG).astype(np.float32))

    def kspec(j):
        return pl.BlockSpec((None, S, G * d), lambda b, i: (b, 0, k_col_blocks[j]))

    def vspec(j):
        return pl.BlockSpec((None, G * d, S), lambda b, i: (b, v_row_blocks[j], 0))

    def const(a):
        return pl.BlockSpec(a.shape, lambda b, i: (0,) * a.ndim)

    W = G * NSA_HPG * d
    return pl.pallas_call(
        functools.partial(_nsa_kernel, n_cmp=n_cmp), grid=(B, S // TQ),
        in_specs=[pl.BlockSpec((None, TQ, W), lambda b, i: (b, i, q_col_block)),
                  pl.BlockSpec((None, TQ, LANES), lambda b, i: (b, i, GATE_COL_BLOCK)),
                  const(gate_bias),
                  pl.BlockSpec((None, G, n_cb, d), lambda b, i: (b, 0, 0, 0)),
                  pl.BlockSpec((None, G, d, n_cb), lambda b, i: (b, 0, 0, 0)),
                  kspec(0), vspec(0), kspec(1), vspec(1), const(ovl), const(aug), const(caug), const(tail),
                  const(ctail), const(wmask)],
        out_specs=pl.BlockSpec((None, TQ, W), lambda b, i: (b, i, 0)),
        out_shape=jax.ShapeDtypeStruct((B, S, NSA_W), BF16),
        scratch_shapes=[pltpu.VMEM((G, S, kaug), BF16), pltpu.VMEM((G, S, kaug), BF16),
                        pltpu.VMEM((G, n_cb, 2 * d), BF16),
                        pltpu.VMEM((G, d + NSA_ONES, S), BF16), pltpu.VMEM((G, d + NSA_ONES, S), BF16),
                        pltpu.VMEM((G, 2, NSA_TK, R), F32), pltpu.VMEM((G, 2, NSA_TK, R), BF16),
                        pltpu.VMEM((G, n_win, NSA_TK, R), F32), pltpu.VMEM((G, n_win, NSA_TK, R), BF16)],
        compiler_params=_cparams(("parallel", "arbitrary")), name="nsa",
    )(zb, zf, gate_bias, kcmp, vcmp_t, zb, zt, zb, zt, ovl, aug, caug, tail, ctail, wmask)


def _rglru_kernel(h_ref, hh_ref, g_ref, wg_ref, wi_ref, cw_ref, cb_ref, wa_ref, wx_ref, ba_ref, bx_ref, lam_ref,
                  o_ref, h_s, au_s):
    NB, T, W = o_ref.shape
    first = pl.program_id(1) == 0

    @pl.when(first)
    def _():
        h_s[...] = jnp.zeros_like(h_s)

    def project(nb):
        xn = _rms(h_ref[nb], g_ref[...]).astype(BF16)
        x_halo = _dot(_rms(hh_ref[nb], g_ref[...]).astype(BF16), wi_ref[...])
        xe = jnp.concatenate([jnp.where(first, 0.0, x_halo), _dot(xn, wi_ref[...])], axis=0)
        return _dot(xn, wg_ref[...]), xe

    def recur(nb, gate, xe):
        xc = cb_ref[...] + cw_ref[3:4, :] * xe
        for k in range(1, LRU_CONV):
            xc = xc + cw_ref[3 - k:4 - k, :] * pltpu.roll(xe, k, 0)
        xc = xc[LRU_HALO:, :]
        xcb = xc.astype(BF16)
        half = W // 2

        def blockdiag(w_ref):
            return jnp.concatenate([_dot(xcb[:, :half], w_ref[0]), _dot(xcb[:, half:], w_ref[1])], axis=1)

        r = _sigmoid(blockdiag(wa_ref) + ba_ref[...])
        i = _sigmoid(blockdiag(wx_ref) + bx_ref[...])
        nl = -lam_ref[...]
        softplus = jnp.maximum(nl, 0.0) + jnp.log1p(jnp.exp(-jnp.abs(nl)))
        log_a = -LRU_C * r * softplus
        a = jnp.exp(log_a)
        one_m_a2 = -jnp.tanh(log_a) * (a * a + 1.0)
        u = jnp.where(one_m_a2 > 0.0, one_m_a2 * lax.rsqrt(one_m_a2), 0.0) * (i * xc)

        n_grp, n_slab = T // 8, W // LANES

        def phases(x, slab):
            for c in range(n_slab):
                au_s[nb, slab, c] = x[:, c * LANES:(c + 1) * LANES]
            return [jnp.concatenate([au_s[nb, slab, c, pl.ds(j, n_grp, stride=8), :] for c in range(n_slab)], axis=1)
                    for j in range(8)]

        a_ph, u_ph = phases(a, 0), phases(u, 1)
        prod, part = [a_ph[0]], [u_ph[0]]
        for j in range(1, 8):
            part.append(a_ph[j] * part[-1] + u_ph[j])
            prod.append(a_ph[j] * prod[-1])
        grp = lax.broadcasted_iota(jnp.int32, (n_grp, W), 0)
        ag, ug = prod[7], part[7]
        sft = 1
        while sft < n_grp:
            keep = grp >= sft
            ug = ag * jnp.where(keep, pltpu.roll(ug, sft, 0), 0.0) + ug
            ag = ag * jnp.where(keep, pltpu.roll(ag, sft, 0), 1.0)
            sft *= 2
        h_prev = h_s[nb, 0:1, :]
        hg = ug + ag * h_prev
        carry_in = jnp.where(grp >= 1, pltpu.roll(hg, 1, 0), h_prev)
        for j in range(8):
            hj = part[j] + prod[j] * carry_in
            for c in range(n_slab):
                au_s[nb, 0, c, pl.ds(j, n_grp, stride=8), :] = hj[:, c * LANES:(c + 1) * LANES]
        h = jnp.concatenate([au_s[nb, 0, c] for c in range(n_slab)], axis=1)
        h_s[nb] = jnp.broadcast_to(hg[n_grp - 1:n_grp, :], h_s.shape[1:])
        o_ref[nb] = (_gelu(gate) * h).astype(o_ref.dtype)

    nxt = project(0)
    for nb in range(NB):
        cur = nxt
        if nb + 1 < NB:
            nxt = project(nb + 1)
        recur(nb, *cur)


def rglru_block(h3, pre_g, w_gate, w_in, conv_w, conv_b, wa, ba, wx, bx, lam, T=LRU_T, NB=LRU_NB):
    B, S, D = h3.shape
    W = LRU_W
    half = W // 2
    hb = T // LRU_HALO

    def bd(w):
        blocks = [jax.scipy.linalg.block_diag(*[w[h] for h in range(4 * j, 4 * j + 4)]) for j in range(2)]
        return jnp.stack(blocks).astype(BF16)

    vec = lambda a: a.reshape(1, W)
    assert B % NB == 0 and S % T == 0
    vspec = pl.BlockSpec((1, W), lambda b, t: (0, 0))
    wspec = pl.BlockSpec((2, half, half), lambda b, t: (0, 0, 0))
    return pl.pallas_call(
        _rglru_kernel, grid=(B // NB, S // T),
        in_specs=[pl.BlockSpec((NB, T, D), lambda b, t: (b, t, 0)),
                  pl.BlockSpec((NB, LRU_HALO, D), lambda b, t: (b, jnp.maximum(t * hb - 1, 0), 0)),
                  pl.BlockSpec((1, D), lambda b, t: (0, 0)),
                  pl.BlockSpec((D, W), lambda b, t: (0, 0)), pl.BlockSpec((D, W), lambda b, t: (0, 0)),
                  pl.BlockSpec((LRU_CONV, W), lambda b, t: (0, 0)), vspec, wspec, wspec, vspec, vspec, vspec],
        out_specs=pl.BlockSpec((NB, T, W), lambda b, t: (b, t, 0)),
        out_shape=jax.ShapeDtypeStruct((B, S, W), BF16),
        scratch_shapes=[pltpu.VMEM((NB, 8, W), F32), pltpu.VMEM((NB, 2, W // LANES, T, LANES), F32)],
        compiler_params=_cparams(("parallel", "arbitrary")), name="rglru",
    )(h3, h3, pre_g.reshape(1, D), w_gate, w_in, conv_w, vec(conv_b), bd(wa), bd(wx), vec(ba), vec(bx), vec(lam))


def _sgu_kernel(h_ref, gpre_ref, wu_ref, wv_ref, g_ref, b_ref, w_ref, bias_ref, o_ref):
    C, W = SG_CHUNK, o_ref.shape[1]
    dg = W // SG_GROUPS
    n_sub = o_ref.shape[0] // SG_SUB
    row = lax.broadcasted_iota(jnp.int32, (C, C), 0)
    col = lax.broadcasted_iota(jnp.int32, (C, C), 1)
    wc = [jnp.where(col <= row, w_ref[gi], 0.0).astype(BF16) for gi in range(SG_GROUPS)]

    def project(j):
        xn = _rms(h_ref[j * SG_SUB:(j + 1) * SG_SUB, :], gpre_ref[...]).astype(BF16)
        return _dot(xn, wu_ref[...]), _dot(xn, wv_ref[...])

    def gate(j, u, v):
        u, v = _gelu(u), _gelu(v)
        mu = jnp.mean(v, axis=-1, keepdims=True)
        vc = v - mu
        vn = (vc * lax.rsqrt(jnp.mean(vc * vc, axis=-1, keepdims=True) + EPS) * g_ref[...] + b_ref[...]).astype(BF16)
        for gi in range(SG_GROUPS):
            sl = slice(gi * dg, (gi + 1) * dg)
            for c in range(SG_SUB // C):
                rows = slice(c * C, (c + 1) * C)
                mixed = _dot(wc[gi], vn[rows, sl]) + bias_ref[:, sl]
                o_ref[j * SG_SUB + c * C:j * SG_SUB + (c + 1) * C, sl] = (u[rows, sl] * mixed).astype(o_ref.dtype)

    nxt = project(0)
    for j in range(n_sub):
        cur = nxt
        if j + 1 < n_sub:
            nxt = project(j + 1)
        gate(j, *cur)


def sgu_block(h3, pre_g, w_u, w_v, ln_g, ln_b, w, b):
    B, S, D = h3.shape
    W, C, T = SG_W, SG_CHUNK, SG_ROWS
    assert S % T == 0 and T % SG_SUB == 0 and SG_SUB % C == 0
    bias = jnp.repeat(b.T, W // SG_GROUPS, axis=1)
    vspec = pl.BlockSpec((1, W), lambda bb, c: (0, 0))
    return pl.pallas_call(
        _sgu_kernel, grid=(B, S // T),
        in_specs=[pl.BlockSpec((None, T, D), lambda bb, c: (bb, c, 0)),
                  pl.BlockSpec((1, D), lambda bb, c: (0, 0)),
                  pl.BlockSpec((D, W), lambda bb, c: (0, 0)), pl.BlockSpec((D, W), lambda bb, c: (0, 0)),
                  vspec, vspec,
                  pl.BlockSpec((SG_GROUPS, C, C), lambda bb, c: (0, 0, 0)),
                  pl.BlockSpec((C, W), lambda bb, c: (0, 0))],
        out_specs=pl.BlockSpec((None, T, W), lambda bb, c: (bb, c, 0)),
        out_shape=jax.ShapeDtypeStruct((B, S, W), BF16),
        compiler_params=_cparams(("parallel", "parallel")), name="sgu",
    )(h3, pre_g.reshape(1, D), w_u, w_v, ln_g.reshape(1, W), ln_b.reshape(1, W), w, bias)


def _mixer_ab(h2, B, S, pre_g, w_in, ml_gate_b, ml_norm_g, nsa_gate_b, k_pe, k_w1, k_w2, v_pe, v_w1, v_w2):
    D = h2.shape[1]
    G, d = NSA_KV_HEADS, HEAD_DIM
    offs = np.cumsum([0, ML_W, ML_W, ML_W, ML_W, 2 * ML_HEADS, NSA_W] + [G * d] * 6 + [3 * NSA_HEADS])
    mq, mk, mv, mo, mif, nq, kc, vc, ks, vs, kw, vw, ng = [w_in[:, offs[i]:offs[i + 1]] for i in range(13)]
    w_b = jnp.concatenate([mk, nq, ks, kw], axis=1).astype(BF16)
    w_t = jnp.concatenate([mq, mv, vs, vw], axis=1).T.astype(BF16)
    gpad = LANES - 2 * ML_HEADS - 3 * NSA_HEADS
    w_f = jnp.concatenate([mo, kc, vc, mif, ng, jnp.zeros((D, gpad), w_in.dtype)], axis=1).astype(BF16)
    gate_bias = jnp.concatenate([ml_gate_b, nsa_gate_b, jnp.zeros((gpad,), F32)]).reshape(1, LANES)
    zb, zf, zt = norm_proj(h2, pre_g, [w_b, w_f], [BF16, F32], wts=[w_t], batch=B)
    zb = zb.reshape(B, S, -1)
    zf = zf.reshape(B, S, -1)
    h_ml = mlstm_block(zt, zb, zf, gate_bias, ml_norm_g)
    kcmp, vcmp_t = compress_block(zf, ML_W // (G * d), ML_W // (G * d) + 1, k_pe, k_w1, k_w2, v_pe, v_w1, v_w2)
    h_nsa = nsa_block(zt, zb, zf, gate_bias, kcmp, vcmp_t, q_col_block=ML_W // NSA_W,
                      k_col_blocks=((ML_W + NSA_W) // (G * d), (ML_W + NSA_W) // (G * d) + 1),
                      v_row_blocks=(2 * ML_W // (G * d), 2 * ML_W // (G * d) + 1))
    return h_ml.reshape(B * S, ML_W), h_nsa.reshape(B * S, NSA_W)


def _mixer_cd(h2, B, S, pre_g, w_in, conv_w, conv_b, wa, ba, wx, bx, lam, sg_g, sg_bn, sg_w, sg_b):
    h3 = h2.reshape(B, S, -1)
    w_gate, w_x, w_u, w_v = (w_in[:, j * LRU_W:(j + 1) * LRU_W].astype(BF16) for j in range(4))
    y_lru = rglru_block(h3, pre_g, w_gate, w_x, conv_w, conv_b, wa, ba, wx, bx, lam)
    y_sg = sgu_block(h3, pre_g, w_u, w_v, sg_g, sg_bn, sg_w, sg_b)
    return y_lru.reshape(B * S, LRU_W), y_sg.reshape(B * S, SG_W)


def kernel(x, pre_mix_g, post_mix_g, pre_ffn_g, post_ffn_g, ab_w_in, ab_w_out, ml_gate_b, ml_norm_g, nsa_gate_b, cmp_k_pe, cmp_k_w1, cmp_k_w2, cmp_v_pe, cmp_v_w1, cmp_v_w2, cd_w_in, cd_w_out, lru_conv_w, lru_conv_b, lru_wa, lru_ba, lru_wx, lru_bx, lru_lambda, sg_norm_g, sg_norm_b, sg_w, sg_b, ffn_w_up, ffn_conv_w, ffn_conv_b, ffn_w_down):
    B, S, D = x.shape
    depth = pre_mix_g.shape[0]
    h2 = x.reshape(B * S, D)
    for layer in range(depth):
        if layer % 2 == 0:
            e = layer // 2
            a1, a2 = _mixer_ab(h2, B, S, pre_mix_g[layer], ab_w_in[e], ml_gate_b[e], ml_norm_g[e], nsa_gate_b[e],
                               cmp_k_pe[e], cmp_k_w1[e], cmp_k_w2[e], cmp_v_pe[e], cmp_v_w1[e], cmp_v_w2[e])
            w_out = ab_w_out[e]
        else:
            o = layer // 2
            a1, a2 = _mixer_cd(h2, B, S, pre_mix_g[layer], cd_w_in[o], lru_conv_w[o], lru_conv_b[o], lru_wa[o],
                               lru_ba[o], lru_wx[o], lru_bx[o], lru_lambda[o], sg_norm_g[o], sg_norm_b[o], sg_w[o],
                               sg_b[o])
            w_out = cd_w_out[o]
        h2 = mix_ffn_block(h2, a1, a2, S, w_out, post_mix_g[layer], pre_ffn_g[layer], ffn_w_up[layer],
                           ffn_conv_w[layer], ffn_conv_b[layer], ffn_w_down[layer], post_ffn_g[layer])
    return h2.reshape(B, S, D)
```

```python
import functools

import numpy as np
import jax
import jax.numpy as jnp
from jax import lax
from jax.experimental import pallas as pl
from jax.experimental.pallas import tpu as pltpu

F32 = jnp.float32
BF16 = jnp.bfloat16

EPS = 1e-6
HEAD_DIM = 64
ML_HEADS = 8
ML_W = 512
GATE_SOFTCAP = 15.0
NSA_HEADS = 8
NSA_KV_HEADS = 2
NSA_HPG = NSA_HEADS // NSA_KV_HEADS
NSA_W = 512
CMP_LEN = 32
CMP_STRIDE = 16
CMP_HID = 128
SEL_LEN = 64
SEL_TOPN = 16
WINDOW = 512
LRU_W = 512
LRU_C = 8.0
LRU_CONV = 4
SG_GROUPS = 8
SG_W = 512
SG_CHUNK = 128
FFN_CONV = 3
GATE_COL_BLOCK = (ML_W + 2 * NSA_KV_HEADS * HEAD_DIM) // 128
NSA_GATE_LANE0 = 2 * ML_HEADS

LANES = 128
VMEM_LIMIT = 56 * 1024 * 1024
NEG_BIG = -1e30
HIGHEST = lax.Precision.HIGHEST
LOG2E = 1.4426950408889634


def _bf16_terms(x, n):
    terms = []
    for _ in range(n):
        bits = int(np.array(x, np.float32).view(np.uint32))
        t = float(np.array((bits + 0x7FFF + ((bits >> 16) & 1)) & 0xFFFF0000, np.uint32).view(np.float32))
        terms.append(t)
        x -= t
    return tuple(terms)


LOG2E_TERMS = _bf16_terms(LOG2E, 3)

ML_CHUNK = 128
ML_NB = 8
NSA_TQ = 256
NSA_TK = 256
RANK_STEP = 16
NSA_ONES = 16
ROW_TILE = 512
FFN_TM = 512
FFN_SUB = 256
FFN_CK = 256
FFN_HALO = 16
SG_ROWS = 1024
SG_SUB = 256
LRU_T = 512
LRU_HALO = 8
LRU_NB = 4


def _cparams(sem):
    return pltpu.CompilerParams(dimension_semantics=sem, vmem_limit_bytes=VMEM_LIMIT)


def _rms(x, g):
    return x * lax.rsqrt(jnp.mean(x * x, axis=-1, keepdims=True) + EPS) * g


def _gelu(x):
    return 0.5 * x * (1.0 + jnp.tanh(0.7978845608028654 * (x + 0.044715 * (x * x * x))))


def _sigmoid(x):
    return 1.0 / (1.0 + jnp.exp(-x))


def _dot(a, b):
    return jnp.dot(a, b, preferred_element_type=F32)


def _dot_nt(a, b, precision=None):
    return lax.dot_general(a, b, (((1,), (1,)), ((), ())), preferred_element_type=F32, precision=precision)


def _norm_proj_kernel(h_ref, g_ref, *refs, n_row, n_t, cn):
    w_refs, wt_refs = refs[:n_row], refs[n_row:n_row + n_t]
    o_refs, ot_refs = refs[n_row + n_t:2 * n_row + n_t], refs[2 * n_row + n_t:]
    xn = _rms(h_ref[...], g_ref[...]).astype(BF16)
    for w_ref, o_ref in zip(w_refs, o_refs):
        n = w_ref.shape[1]
        for c in range(0, n, cn):
            ce = min(c + cn, n)
            o_ref[:, c:ce] = _dot(xn, w_ref[:, c:ce]).astype(o_ref.dtype)
    for wt_ref, ot_ref in zip(wt_refs, ot_refs):
        n = wt_ref.shape[0]
        for c in range(0, n, cn):
            ce = min(c + cn, n)
            ot_ref[c:ce, :] = _dot_nt(wt_ref[c:ce, :], xn).astype(ot_ref.dtype)


def norm_proj(h2, g, ws, dtypes, wts=(), batch=1, tm=ROW_TILE):
    M, D = h2.shape
    tps = M // batch // tm
    in_specs = [pl.BlockSpec((tm, D), lambda i: (i, 0)), pl.BlockSpec((1, D), lambda i: (0, 0))]
    in_specs += [pl.BlockSpec(w.shape, lambda i: (0, 0)) for w in (*ws, *wts)]
    out_specs = [pl.BlockSpec((tm, w.shape[1]), lambda i: (i, 0)) for w in ws]
    out_specs += [pl.BlockSpec((None, w.shape[0], tm), lambda i: (i // tps, 0, i % tps)) for w in wts]
    out_shape = [jax.ShapeDtypeStruct((M, w.shape[1]), dt) for w, dt in zip(ws, dtypes)]
    out_shape += [jax.ShapeDtypeStruct((batch, w.shape[0], M // batch), BF16) for w in wts]
    return pl.pallas_call(
        functools.partial(_norm_proj_kernel, n_row=len(ws), n_t=len(wts), cn=512),
        grid=(M // tm,), in_specs=in_specs, out_specs=out_specs, out_shape=out_shape,
        compiler_params=_cparams(("parallel",)), name="norm_proj",
    )(h2, g.reshape(1, D), *ws, *wts)


def _mix_ffn_kernel(h_ref, hh_ref, a1_ref, a1h_ref, a2_ref, a2h_ref, wo1_ref, wo2_ref, gmix_ref, gpre_ref, wu_ref,
                    cw_ref, cb_ref, wd_ref, gpost_ref, o_ref, xn_s, acc_s, *, tiles_per_seq, ck):
    F = wd_ref.shape[0]
    n_chunks = F // ck
    n_sub = h_ref.shape[0] // FFN_SUB
    first = (pl.program_id(0) % tiles_per_seq) == 0
    g = gpre_ref[...]

    def mixed(h, a1, a2):
        return h + _rms(_dot(a1, wo1_ref[...]) + _dot(a2, wo2_ref[...]), gmix_ref[...])

    def conv(u, cols):
        y = (cw_ref[2:3, cols] * u + cw_ref[1:2, cols] * pltpu.roll(u, 1, 0) + cw_ref[0:1, cols] * pltpu.roll(u, 2, 0)
             + cb_ref[:, cols])
        return y[FFN_HALO:, :]

    def cols_of(c, half):
        return slice(half * F + c * ck, half * F + (c + 1) * ck)

    x = {}

    def head(j):
        rows = slice(j * FFN_SUB, (j + 1) * FFN_SUB)
        x[j] = mixed(h_ref[rows, :], a1_ref[rows, :], a2_ref[rows, :])
        if j == 0:
            halo = jnp.where(first, 0.0, _rms(mixed(hh_ref[...], a1h_ref[...], a2h_ref[...]), g))
            xn_s[0:FFN_HALO, :] = halo.astype(BF16)
        xn_s[FFN_HALO + j * FFN_SUB:FFN_HALO + (j + 1) * FFN_SUB, :] = _rms(x[j], g).astype(BF16)
        acc_s[rows, :] = jnp.zeros((FFN_SUB, acc_s.shape[1]), F32)

    def up(j, c):
        xn = xn_s[j * FFN_SUB:(j + 1) * FFN_SUB + FFN_HALO, :]
        return _dot(xn, wu_ref[:, cols_of(c, 0)]), _dot(xn, wu_ref[:, cols_of(c, 1)])

    def tail(j):
        rows = slice(j * FFN_SUB, (j + 1) * FFN_SUB)
        o_ref[rows, :] = x[j] + _rms(acc_s[rows, :], gpost_ref[...])

    head(0)
    for j in range(n_sub):
        u = up(j, 0)
        for c in range(n_chunks):
            u_next = up(j, c + 1) if c + 1 < n_chunks else None
            if c == 0 and j + 1 < n_sub:
                head(j + 1)
            if c == n_chunks // 3 and j >= 1:
                tail(j - 1)
            act = (_gelu(conv(u[0], cols_of(c, 0))) * conv(u[1], cols_of(c, 1))).astype(BF16)
            acc_s[j * FFN_SUB:(j + 1) * FFN_SUB, :] += _dot(act, wd_ref[c * ck:(c + 1) * ck, :])
            u = u_next
    tail(n_sub - 1)


def mix_ffn_block(h2, a1, a2, seq_len, w_out, g_mix, g_pre, w_up, conv_w, conv_b, w_down, g_post, tm=FFN_TM, ck=FFN_CK):
    M, D = h2.shape
    F = w_down.shape[0]
    K1, K2 = a1.shape[1], a2.shape[1]
    assert F % ck == 0 and seq_len % tm == 0 and tm % FFN_SUB == 0 and FFN_SUB % FFN_HALO == 0
    assert conv_w.shape[0] == FFN_CONV and FFN_CONV - 1 <= FFN_HALO
    hb = tm // FFN_HALO
    consts = (w_out[:K1].astype(BF16), w_out[K1:].astype(BF16), g_mix.reshape(1, D), g_pre.reshape(1, D),
              w_up.astype(BF16), conv_w, conv_b.reshape(1, 2 * F), w_down.astype(BF16), g_post.reshape(1, D))

    def tile(width):
        return pl.BlockSpec((tm, width), lambda i: (i, 0))

    def halo(width):
        return pl.BlockSpec((FFN_HALO, width), lambda i: (jnp.maximum(i * hb - 1, 0), 0))

    def full(a):
        return pl.BlockSpec(a.shape, lambda i: (0,) * a.ndim)

    return pl.pallas_call(
        functools.partial(_mix_ffn_kernel, tiles_per_seq=seq_len // tm, ck=ck),
        grid=(M // tm,),
        in_specs=[tile(D), halo(D), tile(K1), halo(K1), tile(K2), halo(K2)] + [full(a) for a in consts],
        out_specs=tile(D),
        out_shape=jax.ShapeDtypeStruct((M, D), F32),
        scratch_shapes=[pltpu.VMEM((tm + FFN_HALO, D), BF16), pltpu.VMEM((tm, D), F32)],
        compiler_params=_cparams(("parallel",)), name="mix_ffn",
    )(h2, h2, a1, a1, a2, a2, *consts)


def _mlstm_kernel(qt_ref, vt_ref, k_ref, mo_ref, gt_ref, gb_ref, ng_ref, o_ref, cn_s, m_s):
    NB, _, L = qt_ref.shape
    d = HEAD_DIM

    @pl.when(pl.program_id(1) == 0)
    def _():
        cn_s[...] = jnp.zeros_like(cn_s)
        m_s[...] = jnp.zeros_like(m_s)

    src = lax.broadcasted_iota(jnp.int32, (L, L), 0)
    tgt = lax.broadcasted_iota(jnp.int32, (L, L), 1)
    causal = src <= tgt
    tri = (tgt <= src).astype(F32)

    gcap = [GATE_SOFTCAP * jnp.tanh((gt_ref[nb] + gb_ref[...]) * (1.0 / GATE_SOFTCAP)) for nb in range(NB)]
    lf = [jnp.minimum(x, 0.0) - jnp.log1p(jnp.exp(-jnp.abs(x))) for x in gcap]
    b_col = [jnp.dot(tri, x, preferred_element_type=F32, precision=HIGHEST) for x in lf]
    b_row = [x.T for x in b_col]
    i_row = [x.T for x in gcap]
    c_col = [b_col[nb] - pltpu.roll(gcap[nb], ML_HEADS, 1) for nb in range(NB)]

    chains = [(nb, h) for nb in range(NB) for h in range(ML_HEADS)]
    n = range(len(chains))

    def rows(h):
        return slice(h * d, (h + 1) * d)

    q_t = [qt_ref[nb, rows(h), :] for nb, h in chains]
    v_t = [vt_ref[nb, rows(h), :] for nb, h in chains]
    k = [k_ref[nb, :, rows(h)] * 0.125 for nb, h in chains]
    br = [b_row[nb][ML_HEADS + h:ML_HEADS + h + 1, :] for nb, h in chains]
    ir = [i_row[nb][h:h + 1, :] for nb, h in chains]
    g = [x[:, L - 1:L] for x in br]
    m_prev = [m_s[nb, h:h + 1, 0:1] for nb, h in chains]
    cn_prev = [cn_s[nb, h] for nb, h in chains]
    kq = [_dot(k[c], q_t[c]) for c in n]
    carry = [_dot(cn_prev[c].astype(BF16), q_t[c]) for c in n]
    dlog = [jnp.where(causal, br[c] - c_col[nb][:, ML_HEADS + h:ML_HEADS + h + 1], -jnp.inf)
            for c, (nb, h) in enumerate(chains)]
    inter = [br[c] + m_prev[c] for c in n]
    m_row = [jnp.maximum(inter[c], jnp.max(dlog[c], axis=0, keepdims=True)) for c in n]
    s = [kq[c] * jnp.exp(dlog[c] - m_row[c]) for c in n]
    w_inter = [jnp.exp(inter[c] - m_row[c]) for c in n]
    num = [_dot(v_t[c], s[c].astype(BF16)) + w_inter[c] * carry[c][0:d] for c in n]
    den = [jnp.sum(s[c], axis=0, keepdims=True) + w_inter[c] * carry[c][d:d + 1] for c in n]
    hh = [num[c] * (1.0 / jnp.maximum(jnp.abs(den[c]), jnp.exp(-m_row[c]))) for c in n]
    wlog = [g[c] - br[c] + ir[c] for c in n]
    m_new = [jnp.maximum(g[c] + m_prev[c], jnp.max(wlog[c], axis=-1, keepdims=True)) for c in n]
    w_row = [jnp.exp(wlog[c] - m_new[c]) for c in n]
    decay = [jnp.exp(g[c] + m_prev[c] - m_new[c]) for c in n]
    for c, (nb, h) in enumerate(chains):
        vw = jnp.concatenate([v_t[c].astype(F32) * w_row[c], jnp.broadcast_to(w_row[c], (8, L))], axis=0).astype(BF16)
        cn_s[nb, h] = decay[c] * cn_prev[c] + _dot(vw, k[c])
        m_s[nb, h:h + 1, :] = jnp.broadcast_to(m_new[c], (1, m_s.shape[2]))
    outs = [hh[c] * lax.rsqrt(jnp.mean(hh[c] * hh[c], axis=0, keepdims=True) + EPS) * ng_ref[rows(h), :]
            for c, (nb, h) in enumerate(chains)]
    for nb in range(NB):
        out_t = jnp.concatenate(outs[nb * ML_HEADS:(nb + 1) * ML_HEADS], axis=0)
        o_ref[nb] = (out_t.T * _sigmoid(mo_ref[nb])).astype(o_ref.dtype)


def mlstm_block(zt, zb, zf, gate_bias, norm_g, L=ML_CHUNK, NB=ML_NB):
    B, S, _ = zb.shape
    W, H, d = ML_W, ML_HEADS, HEAD_DIM
    assert B % NB == 0 and S % L == 0
    ng = jnp.broadcast_to(norm_g.reshape(W, 1), (W, L))
    return pl.pallas_call(
        _mlstm_kernel, grid=(B // NB, S // L),
        in_specs=[pl.BlockSpec((NB, W, L), lambda b, c: (b, 0, c)),
                  pl.BlockSpec((NB, W, L), lambda b, c: (b, 1, c)),
                  pl.BlockSpec((NB, L, W), lambda b, c: (b, c, 0)),
                  pl.BlockSpec((NB, L, W), lambda b, c: (b, c, 0)),
                  pl.BlockSpec((NB, L, LANES), lambda b, c: (b, c, GATE_COL_BLOCK)),
                  pl.BlockSpec((1, LANES), lambda b, c: (0, 0)),
                  pl.BlockSpec((W, L), lambda b, c: (0, 0))],
        out_specs=pl.BlockSpec((NB, L, W), lambda b, c: (b, c, 0)),
        out_shape=jax.ShapeDtypeStruct((B, S, W), BF16),
        scratch_shapes=[pltpu.VMEM((NB, H, d + 8, d), F32), pltpu.VMEM((NB, H, LANES), F32)],
        compiler_params=_cparams(("parallel", "arbitrary")), name="mlstm",
    )(zt, zt, zb, zf, zf, gate_bias, ng)


def _compress_kernel(kc_ref, vc_ref, pek_ref, pev_ref, w1k_ref, w1v_ref, w2k_ref, w2v_ref, ok_ref, ov_ref):
    G = NSA_KV_HEADS
    nh = kc_ref.shape[0] // CMP_STRIDE

    def one(x_ref, pe_ref, w1_ref, w2_ref):
        hid = None
        for l in range(CMP_STRIDE):
            y = x_ref[pl.ds(l, nh, stride=CMP_STRIDE), :]
            ya = (y + pe_ref[l:l + 1, :]).astype(BF16)
            yb = (pltpu.roll(y, nh - 1, 0) + pe_ref[CMP_STRIDE + l:CMP_STRIDE + l + 1, :]).astype(BF16)
            t = _dot(ya, w1_ref[l]) + _dot(yb, w1_ref[CMP_STRIDE + l])
            hid = t if hid is None else hid + t
        hid = _gelu(hid).astype(BF16)
        return [_dot(hid[:, gi * CMP_HID:(gi + 1) * CMP_HID], w2_ref[...]) for gi in range(G)]

    for gi, (ko, vo) in enumerate(zip(one(kc_ref, pek_ref, w1k_ref, w2k_ref), one(vc_ref, pev_ref, w1v_ref, w2v_ref))):
        ok_ref[gi] = ko.astype(ok_ref.dtype)
        ov_ref[gi] = vo.T.astype(ov_ref.dtype)


def compress_block(zf, kc_block, vc_block, k_pe, k_w1, k_w2, v_pe, v_w1, v_w2):
    B, S, _ = zf.shape
    G, d = NSA_KV_HEADS, HEAD_DIM
    nh = S // CMP_STRIDE

    def prep(pe, w1):
        w1bd = jnp.zeros((CMP_LEN, G * d, G * CMP_HID), w1.dtype)
        for gi in range(G):
            w1bd = w1bd.at[:, gi * d:(gi + 1) * d, gi * CMP_HID:(gi + 1) * CMP_HID].set(w1)
        return jnp.tile(pe, (1, G)), w1bd.astype(BF16)

    pek, w1k = prep(k_pe, k_w1)
    pev, w1v = prep(v_pe, v_w1)

    def full(a):
        return pl.BlockSpec(a.shape, lambda b: (0,) * a.ndim)

    w2k, w2v = k_w2.astype(BF16), v_w2.astype(BF16)
    return pl.pallas_call(
        _compress_kernel, grid=(B,),
        in_specs=[pl.BlockSpec((None, S, G * d), lambda b: (b, 0, kc_block)),
                  pl.BlockSpec((None, S, G * d), lambda b: (b, 0, vc_block)),
                  full(pek), full(pev), full(w1k), full(w1v), full(w2k), full(w2v)],
        out_specs=[pl.BlockSpec((None, G, nh, d), lambda b: (b, 0, 0, 0)),
                   pl.BlockSpec((None, G, d, nh), lambda b: (b, 0, 0, 0))],
        out_shape=[jax.ShapeDtypeStruct((B, G, nh, d), BF16), jax.ShapeDtypeStruct((B, G, d, nh), BF16)],
        compiler_params=_cparams(("parallel",)), name="nsa_compress",
    )(zf, zf, pek, pev, w1k, w1v, w2k, w2v)


def _nsa_kernel(q_ref, gt_ref, gb_ref, kc_ref, vct_ref, ks_ref, vst_ref, kw_ref, vwt_ref, ovl_ref, aug_ref, caug_ref,
                tail_ref, ctail_ref, wmask_ref, o_ref, ksa_s, kwa_s, kca_s, vsa_s, vwa_s, ss_s, ps_s, sw_s, pw_s, *, n_cmp):
    TQ, TK, d, HPG, G = NSA_TQ, NSA_TK, HEAD_DIM, NSA_HPG, NSA_KV_HEADS
    R = HPG * TQ
    groups = range(G)
    qi = pl.program_id(1)
    q0 = qi * TQ
    kt_d = q0 // TK
    n_cb = kc_ref.shape[1]
    n_sb = ovl_ref.shape[0]

    @pl.when(qi == 0)
    def _():
        for g in groups:
            ksa_s[g, :, 0:d] = ks_ref[:, g * d:(g + 1) * d]
            ksa_s[g, :, d:] = aug_ref[...]
            kwa_s[g, :, 0:d] = kw_ref[:, g * d:(g + 1) * d]
            kwa_s[g, :, d:] = aug_ref[...]
            kca_s[g, :, 0:d] = kc_ref[g]
            kca_s[g, :, d:] = caug_ref[...]
            vsa_s[g, 0:d, :] = vst_ref[g * d:(g + 1) * d, :]
            vsa_s[g, d:, :] = jnp.ones((vsa_s.shape[1] - d, vsa_s.shape[2]), BF16)
            vwa_s[g, 0:d, :] = vwt_ref[g * d:(g + 1) * d, :]
            vwa_s[g, d:, :] = jnp.ones((vwa_s.shape[1] - d, vwa_s.shape[2]), BF16)

    def slope(g, hh):
        return 2.0 ** (-(g * HPG + hh + 1))

    def per_head(fn):
        return jnp.concatenate([fn(hh) for hh in range(HPG)], axis=1)

    def tile_heads(x):
        return jnp.concatenate([x] * HPG, axis=1)

    q_t = [(q_ref[:, g * HPG * d:(g + 1) * HPG * d].astype(F32) * (LOG2E * 0.125)).T for g in groups]

    def q_head(g, hh):
        return q_t[g][hh * d:(hh + 1) * d]

    def q_aug(g, block_rows):
        return per_head(lambda hh: jnp.concatenate(
            [q_head(g, hh), block_rows, tail_ref[...] * slope(g, hh)], axis=0)).astype(BF16)

    def scores(ka, qa_t, kt):
        return _dot(ka[pl.ds(pl.multiple_of(kt * TK, TK), TK), :], qa_t)

    def stage_scores(buf, ka, qa_t, kt, mask_add=None):
        s = scores(ka, qa_t, kt)
        if mask_add is not None:
            s = s + mask_add
        buf[...] = s
        return jnp.max(s, axis=0, keepdims=True)

    def stage_probs(sbuf, pbuf, tile_max, m):
        m_new = jnp.maximum(m, tile_max)
        pbuf[...] = jnp.exp2(sbuf[...] - m_new).astype(BF16)
        return m_new, jnp.exp2(m - m_new)

    def stage_values(pbuf, va, kt, alpha, acc):
        return alpha * acc + _dot(va[:, pl.ds(pl.multiple_of(kt * TK, TK), TK)], pbuf[...])

    def normalised(acc):
        return acc[0:d] * (1.0 / acc[d:d + 1])

    m0, acc0 = jnp.full((1, R), NEG_BIG, F32), jnp.zeros((vsa_s.shape[1], R), F32)

    qc_t = [per_head(lambda hh: jnp.concatenate([q_head(g, hh), ctail_ref[...] * slope(g, hh)], axis=0)).astype(BF16)
            for g in groups]
    qw_t = [q_aug(g, jnp.zeros((n_sb, TQ), F32)) for g in groups]
    n_r = lax.broadcasted_iota(jnp.int32, (n_cb, TQ), 0)
    t_c = q0 + lax.broadcasted_iota(jnp.int32, (n_cb, TQ), 1)
    ok_c = (n_r * CMP_STRIDE + (CMP_LEN - 1) <= t_c) & (n_r < n_cmp)
    add_c = tile_heads(jnp.where(ok_c, 0.0, NEG_BIG))
    s_c = [_dot(kca_s[g], qc_t[g]) + add_c for g in groups]
    n_win = (WINDOW - 1 + TK - 1) // TK + 1
    kt_win, max_win = [], []
    for back in range(n_win):
        kt_raw = kt_d - back
        kt_win.append(jnp.maximum(kt_raw, 0))
        if back == 0:
            mask_add = tile_heads(wmask_ref[0])
        elif (back + 1) * TK <= WINDOW:
            mask_add = jnp.where(kt_raw >= 0, 0.0, NEG_BIG)
        else:
            mask_add = tile_heads(wmask_ref[back] + jnp.where(kt_raw >= 0, 0.0, NEG_BIG))
        max_win.append([stage_scores(sw_s.at[g, back], kwa_s.at[g], qw_t[g], kt_win[back], mask_add) for g in groups])

    e_c = [jnp.exp2(s_c[g] - jnp.maximum(jnp.max(s_c[g], axis=0, keepdims=True), 0.1 * NEG_BIG)) for g in groups]
    inv_c = [1.0 / jnp.maximum(jnp.sum(e_c[g], axis=0, keepdims=True), 1.0) for g in groups]
    o_c = [_dot(vct_ref[g], e_c[g].astype(BF16)) * inv_c[g] for g in groups]

    jb = lax.broadcasted_iota(jnp.int32, (n_sb, TQ), 0)
    cur = (q0 + lax.broadcasted_iota(jnp.int32, (n_sb, TQ), 1)) // SEL_LEN
    valid = jb <= cur
    forced = (jb == 0) | (jb == cur) | (jb == cur - 1)
    sub = lax.broadcasted_iota(jnp.int32, (8, TQ), 0)
    score = []
    for g in groups:
        p_sum = e_c[g][:, 0:TQ] * inv_c[g][:, 0:TQ]
        for hh in range(1, HPG):
            p_sum = p_sum + e_c[g][:, hh * TQ:(hh + 1) * TQ] * inv_c[g][:, hh * TQ:(hh + 1) * TQ]
        imp = jnp.dot(ovl_ref[...], p_sum, preferred_element_type=F32, precision=HIGHEST)
        score.append(jnp.where(forced, jnp.inf, jnp.where(valid, imp, -jnp.inf)))

    def ranked(n_rows):
        def fn(*score):
            n_grp = n_rows // 8
            outs = []
            grp = [[s[8 * a:8 * a + 8] for a in range(n_grp)] for s in score]
            rank = [[jnp.zeros((8, TQ), jnp.int32) for _ in range(n_grp)] for _ in score]
            for j in range(n_rows):
                for g in groups:
                    r = score[g][j:j + 1, :]
                    for a in range(n_grp):
                        if a > j // 8:
                            ahead = (r >= grp[g][a]).astype(jnp.int32)
                        elif a < j // 8:
                            ahead = (r > grp[g][a]).astype(jnp.int32)
                        else:
                            ahead = jnp.where(sub > j % 8, (r >= grp[g][a]).astype(jnp.int32),
                                              (r > grp[g][a]).astype(jnp.int32))
                        rank[g][a] = rank[g][a] + ahead
            for g in groups:
                top = jnp.where(jnp.concatenate(rank[g], axis=0) < SEL_TOPN, 1.0, 0.0)
                outs.append(jnp.concatenate([top, jnp.zeros((n_sb - n_rows, TQ), F32)], axis=0) if n_rows < n_sb else top)
            return tuple(outs)
        return fn

    sizes = [n for n in range(RANK_STEP, n_sb + 1, RANK_STEP)]
    rows_needed = (q0 + TQ + SEL_LEN - 1) // SEL_LEN
    in_top = lax.switch((rows_needed + RANK_STEP - 1) // RANK_STEP - 1, [ranked(n) for n in sizes], *score)
    picked = [valid & (in_top[g] > 0.5) for g in groups]
    qs_t = [q_aug(g, jnp.where(picked[g], 0.0, NEG_BIG)) for g in groups]

    filler = 1 - kt_d % 2
    n_seq = kt_d + 1 + filler

    def sel_tile(i):
        return jnp.where(i == 0, kt_d, jnp.maximum(i - 1 - filler, 0))

    def a_stage(slot, kt, mask_add=None):
        return [stage_scores(ss_s.at[g, slot], ksa_s.at[g], qs_t[g], kt, mask_add) for g in groups]

    def b_stage(slot, tmax, m):
        out = [stage_probs(ss_s.at[g, slot], ps_s.at[g, slot], tmax[g], m[g]) for g in groups]
        return [o[0] for o in out], [o[1] for o in out]

    def c_stage(slot, kt, alpha, acc):
        return [stage_values(ps_s.at[g, slot], vsa_s.at[g], kt, alpha[g], acc[g]) for g in groups]

    tmax0 = a_stage(0, kt_d, tile_heads(wmask_ref[0]))
    m, alpha = b_stage(0, tmax0, [m0] * G)
    tmax1 = a_stage(1, sel_tile(1), jnp.where(filler == 1, NEG_BIG, 0.0))

    m_w = [functools.reduce(jnp.maximum, [max_win[back][g] for back in range(n_win)]) for g in groups]
    accw = [None] * G
    for back in range(n_win):
        for g in groups:
            pw_s[g, back] = jnp.exp2(sw_s[g, back] - m_w[g]).astype(BF16)
        for g in groups:
            pv = _dot(vwa_s[g, :, pl.ds(pl.multiple_of(kt_win[back] * TK, TK), TK)], pw_s[g, back])
            accw[g] = pv if accw[g] is None else accw[g] + pv
    o_w = [normalised(accw[g]) for g in groups]

    def sel_body(k, carry):
        m, alpha, acc, tmax1 = carry
        i = 2 * k
        tmax0 = a_stage(0, sel_tile(i + 2))
        m, alpha1 = b_stage(1, tmax1, m)
        tmax1 = a_stage(1, sel_tile(i + 3))
        acc = c_stage(0, sel_tile(i), alpha, acc)
        m, alpha2 = b_stage(0, tmax0, m)
        acc = c_stage(1, sel_tile(i + 1), alpha1, acc)
        return m, alpha2, acc, tmax1

    m, alpha, acc, tmax1 = lax.fori_loop(0, (n_seq - 2) // 2, sel_body, (m, alpha, [acc0] * G, tmax1))
    m, alpha1 = b_stage(1, tmax1, m)
    acc = c_stage(0, sel_tile(n_seq - 2), alpha, acc)
    acc = c_stage(1, sel_tile(n_seq - 1), alpha1, acc)
    o_s = [normalised(acc[g]) for g in groups]

    gates = _sigmoid(gt_ref[...] + gb_ref[...]).T
    outs = []
    for g in groups:
        for hh in range(HPG):
            cols = slice(hh * TQ, (hh + 1) * TQ)
            row = NSA_GATE_LANE0 + 3 * (g * HPG + hh)
            outs.append(gates[row:row + 1] * o_c[g][:, cols] + gates[row + 1:row + 2] * o_s[g][:, cols]
                        + gates[row + 2:row + 3] * o_w[g][:, cols])
    o_ref[...] = jnp.concatenate(outs, axis=0).T.astype(o_ref.dtype)


def nsa_block(zt, zb, zf, gate_bias, kcmp, vcmp_t, q_col_block, k_col_blocks, v_row_blocks):
    B, S, _ = zb.shape
    G, d = NSA_KV_HEADS, HEAD_DIM
    TQ = NSA_TQ
    n_cb = kcmp.shape[2]
    n_cmp = (S - CMP_LEN) // CMP_STRIDE + 1
    n_sb = S // SEL_LEN
    n_terms = len(LOG2E_TERMS)
    kaug = -(-(d + n_sb + 2 * n_terms) // LANES) * LANES
    R = NSA_HPG * TQ
    n_win = (WINDOW - 1 + NSA_TK - 1) // NSA_TK + 1
    assert S % NSA_TK == 0 and NSA_TK == TQ and TQ % SEL_LEN == 0

    cidx = np.arange(n_cb)[None, :] * CMP_STRIDE
    sstart = np.arange(n_sb)[:, None] * SEL_LEN
    ovl = ((cidx < sstart + SEL_LEN) & (cidx + CMP_LEN - 1 >= sstart) & (np.arange(n_cb)[None, :] < n_cmp))
    ovl = jnp.asarray(ovl.astype(np.float32))
    pos = np.arange(S)
    aug = np.zeros((S, kaug - d), np.float32)
    aug[pos, pos // SEL_LEN] = 1.0
    tail = np.zeros((kaug - d - n_sb, TQ), np.float32)
    caug = np.zeros((n_cb, d), np.float32)
    ctail = np.zeros((d, TQ), np.float32)
    for i, term in enumerate(LOG2E_TERMS):
        aug[:, n_sb + i] = pos // SEL_LEN
        aug[:, n_sb + n_terms + i] = pos % SEL_LEN
        tail[i] = term * SEL_LEN
        tail[n_terms + i] = term
        caug[:, i] = np.arange(n_cb)
        ctail[i] = term * CMP_STRIDE
    aug, caug, tail, ctail = jnp.asarray(aug, BF16), jnp.asarray(caug, BF16), jnp.asarray(tail), jnp.asarray(ctail)
    dist = np.arange(n_win)[:, None, None] * NSA_TK + np.arange(TQ)[None, None, :] - np.arange(NSA_TK)[None, :, None]
    wmask = jnp.asarray(np.where((dist >= 0) & (dist < WINDOW), 0.0, NEG_BIG).astype(np.float32))

    def kspec(j):
        return pl.BlockSpec((None, S, G * d), lambda b, i: (b, 0, k_col_blocks[j]))

    def vspec(j):
        return pl.BlockSpec((None, G * d, S), lambda b, i: (b, v_row_blocks[j], 0))

    def const(a):
        return pl.BlockSpec(a.shape, lambda b, i: (0,) * a.ndim)

    W = G * NSA_HPG * d
    return pl.pallas_call(
        functools.partial(_nsa_kernel, n_cmp=n_cmp), grid=(B, S // TQ),
        in_specs=[pl.BlockSpec((None, TQ, W), lambda b, i: (b, i, q_col_block)),
                  pl.BlockSpec((None, TQ, LANES), lambda b, i: (b, i, GATE_COL_BLOCK)),
                  const(gate_bias),
                  pl.BlockSpec((None, G, n_cb, d), lambda b, i: (b, 0, 0, 0)),
                  pl.BlockSpec((None, G, d, n_cb), lambda b, i: (b, 0, 0, 0)),
                  kspec(0), vspec(0), kspec(1), vspec(1), const(ovl), const(aug), const(caug), const(tail),
                  const(ctail), const(wmask)],
        out_specs=pl.BlockSpec((None, TQ, W), lambda b, i: (b, i, 0)),
        out_shape=jax.ShapeDtypeStruct((B, S, NSA_W), BF16),
        scratch_shapes=[pltpu.VMEM((G, S, kaug), BF16), pltpu.VMEM((G, S, kaug), BF16),
                        pltpu.VMEM((G, n_cb, 2 * d), BF16),
                        pltpu.VMEM((G, d + NSA_ONES, S), BF16), pltpu.VMEM((G, d + NSA_ONES, S), BF16),
                        pltpu.VMEM((G, 2, NSA_TK, R), F32), pltpu.VMEM((G, 2, NSA_TK, R), BF16),
                        pltpu.VMEM((G, n_win, NSA_TK, R), F32), pltpu.VMEM((G, n_win, NSA_TK, R), BF16)],
        compiler_params=_cparams(("parallel", "arbitrary")), name="nsa",
    )(zb, zf, gate_bias, kcmp, vcmp_t, zb, zt, zb, zt, ovl, aug, caug, tail, ctail, wmask)


def _rglru_kernel(h_ref, hh_ref, g_ref, wg_ref, wi_ref, cw_ref, cb_ref, wa_ref, wx_ref, ba_ref, bx_ref, lam_ref,
                  o_ref, h_s, au_s):
    NB, T, W = o_ref.shape
    first = pl.program_id(1) == 0

    @pl.when(first)
    def _():
        h_s[...] = jnp.zeros_like(h_s)

    def project(nb):
        xn = _rms(h_ref[nb], g_ref[...]).astype(BF16)
        x_halo = _dot(_rms(hh_ref[nb], g_ref[...]).astype(BF16), wi_ref[...])
        xe = jnp.concatenate([jnp.where(first, 0.0, x_halo), _dot(xn, wi_ref[...])], axis=0)
        return _dot(xn, wg_ref[...]), xe

    def recur(nb, gate, xe):
        xc = cb_ref[...] + cw_ref[3:4, :] * xe
        for k in range(1, LRU_CONV):
            xc = xc + cw_ref[3 - k:4 - k, :] * pltpu.roll(xe, k, 0)
        xc = xc[LRU_HALO:, :]
        xcb = xc.astype(BF16)
        half = W // 2

        def blockdiag(w_ref):
            return jnp.concatenate([_dot(xcb[:, :half], w_ref[0]), _dot(xcb[:, half:], w_ref[1])], axis=1)

        r = _sigmoid(blockdiag(wa_ref) + ba_ref[...])
        i = _sigmoid(blockdiag(wx_ref) + bx_ref[...])
        nl = -lam_ref[...]
        softplus = jnp.maximum(nl, 0.0) + jnp.log1p(jnp.exp(-jnp.abs(nl)))
        log_a = -LRU_C * r * softplus
        a = jnp.exp(log_a)
        one_m_a2 = -jnp.tanh(log_a) * (a * a + 1.0)
        u = jnp.where(one_m_a2 > 0.0, one_m_a2 * lax.rsqrt(one_m_a2), 0.0) * (i * xc)

        n_grp, n_slab = T // 8, W // LANES

        def phases(x, slab):
            for c in range(n_slab):
                au_s[nb, slab, c] = x[:, c * LANES:(c + 1) * LANES]
            return [jnp.concatenate([au_s[nb, slab, c, pl.ds(j, n_grp, stride=8), :] for c in range(n_slab)], axis=1)
                    for j in range(8)]

        a_ph, u_ph = phases(a, 0), phases(u, 1)
        prod, part = [a_ph[0]], [u_ph[0]]
        for j in range(1, 8):
            part.append(a_ph[j] * part[-1] + u_ph[j])
            prod.append(a_ph[j] * prod[-1])
        grp = lax.broadcasted_iota(jnp.int32, (n_grp, W), 0)
        ag, ug = prod[7], part[7]
        sft = 1
        while sft < n_grp:
            keep = grp >= sft
            ug = ag * jnp.where(keep, pltpu.roll(ug, sft, 0), 0.0) + ug
            ag = ag * jnp.where(keep, pltpu.roll(ag, sft, 0), 1.0)
            sft *= 2
        h_prev = h_s[nb, 0:1, :]
        hg = ug + ag * h_prev
        carry_in = jnp.where(grp >= 1, pltpu.roll(hg, 1, 0), h_prev)
        for j in range(8):
            hj = part[j] + prod[j] * carry_in
            for c in range(n_slab):
                au_s[nb, 0, c, pl.ds(j, n_grp, stride=8), :] = hj[:, c * LANES:(c + 1) * LANES]
        h = jnp.concatenate([au_s[nb, 0, c] for c in range(n_slab)], axis=1)
        h_s[nb] = jnp.broadcast_to(hg[n_grp - 1:n_grp, :], h_s.shape[1:])
        o_ref[nb] = (_gelu(gate) * h).astype(o_ref.dtype)

    nxt = project(0)
    for nb in range(NB):
        cur = nxt
        if nb + 1 < NB:
            nxt = project(nb + 1)
        recur(nb, *cur)


def rglru_block(h3, pre_g, w_gate, w_in, conv_w, conv_b, wa, ba, wx, bx, lam, T=LRU_T, NB=LRU_NB):
    B, S, D = h3.shape
    W = LRU_W
    half = W // 2
    hb = T // LRU_HALO

    def bd(w):
        blocks = [jax.scipy.linalg.block_diag(*[w[h] for h in range(4 * j, 4 * j + 4)]) for j in range(2)]
        return jnp.stack(blocks).astype(BF16)

    vec = lambda a: a.reshape(1, W)
    assert B % NB == 0 and S % T == 0
    vspec = pl.BlockSpec((1, W), lambda b, t: (0, 0))
    wspec = pl.BlockSpec((2, half, half), lambda b, t: (0, 0, 0))
    return pl.pallas_call(
        _rglru_kernel, grid=(B // NB, S // T),
        in_specs=[pl.BlockSpec((NB, T, D), lambda b, t: (b, t, 0)),
                  pl.BlockSpec((NB, LRU_HALO, D), lambda b, t: (b, jnp.maximum(t * hb - 1, 0), 0)),
                  pl.BlockSpec((1, D), lambda b, t: (0, 0)),
                  pl.BlockSpec((D, W), lambda b, t: (0, 0)), pl.BlockSpec((D, W), lambda b, t: (0, 0)),
                  pl.BlockSpec((LRU_CONV, W), lambda b, t: (0, 0)), vspec, wspec, wspec, vspec, vspec, vspec],
        out_specs=pl.BlockSpec((NB, T, W), lambda b, t: (b, t, 0)),
        out_shape=jax.ShapeDtypeStruct((B, S, W), BF16),
        scratch_shapes=[pltpu.VMEM((NB, 8, W), F32), pltpu.VMEM((NB, 2, W // LANES, T, LANES), F32)],
        compiler_params=_cparams(("parallel", "arbitrary")), name="rglru",
    )(h3, h3, pre_g.reshape(1, D), w_gate, w_in, conv_w, vec(conv_b), bd(wa), bd(wx), vec(ba), vec(bx), vec(lam))


def _sgu_kernel(h_ref, gpre_ref, wu_ref, wv_ref, g_ref, b_ref, w_ref, bias_ref, o_ref):
    C, W = SG_CHUNK, o_ref.shape[1]
    dg = W // SG_GROUPS
    n_sub = o_ref.shape[0] // SG_SUB
    row = lax.broadcasted_iota(jnp.int32, (C, C), 0)
    col = lax.broadcasted_iota(jnp.int32, (C, C), 1)
    wc = [jnp.where(col <= row, w_ref[gi], 0.0).astype(BF16) for gi in range(SG_GROUPS)]

    def project(j):
        xn = _rms(h_ref[j * SG_SUB:(j + 1) * SG_SUB, :], gpre_ref[...]).astype(BF16)
        return _dot(xn, wu_ref[...]), _dot(xn, wv_ref[...])

    def gate(j, u, v):
        u, v = _gelu(u), _gelu(v)
        mu = jnp.mean(v, axis=-1, keepdims=True)
        vc = v - mu
        vn = (vc * lax.rsqrt(jnp.mean(vc * vc, axis=-1, keepdims=True) + EPS) * g_ref[...] + b_ref[...]).astype(BF16)
        for gi in range(SG_GROUPS):
            sl = slice(gi * dg, (gi + 1) * dg)
            for c in range(SG_SUB // C):
                rows = slice(c * C, (c + 1) * C)
                mixed = _dot(wc[gi], vn[rows, sl]) + bias_ref[:, sl]
                o_ref[j * SG_SUB + c * C:j * SG_SUB + (c + 1) * C, sl] = (u[rows, sl] * mixed).astype(o_ref.dtype)

    nxt = project(0)
    for j in range(n_sub):
        cur = nxt
        if j + 1 < n_sub:
            nxt = project(j + 1)
        gate(j, *cur)


def sgu_block(h3, pre_g, w_u, w_v, ln_g, ln_b, w, b):
    B, S, D = h3.shape
    W, C, T = SG_W, SG_CHUNK, SG_ROWS
    assert S % T == 0 and T % SG_SUB == 0 and SG_SUB % C == 0
    bias = jnp.repeat(b.T, W // SG_GROUPS, axis=1)
    vspec = pl.BlockSpec((1, W), lambda bb, c: (0, 0))
    return pl.pallas_call(
        _sgu_kernel, grid=(B, S // T),
        in_specs=[pl.BlockSpec((None, T, D), lambda bb, c: (bb, c, 0)),
                  pl.BlockSpec((1, D), lambda bb, c: (0, 0)),
                  pl.BlockSpec((D, W), lambda bb, c: (0, 0)), pl.BlockSpec((D, W), lambda bb, c: (0, 0)),
                  vspec, vspec,
                  pl.BlockSpec((SG_GROUPS, C, C), lambda bb, c: (0, 0, 0)),
                  pl.BlockSpec((C, W), lambda bb, c: (0, 0))],
        out_specs=pl.BlockSpec((None, T, W), lambda bb, c: (bb, c, 0)),
        out_shape=jax.ShapeDtypeStruct((B, S, W), BF16),
        compiler_params=_cparams(("parallel", "parallel")), name="sgu",
    )(h3, pre_g.reshape(1, D), w_u, w_v, ln_g.reshape(1, W), ln_b.reshape(1, W), w, bias)


def _mixer_ab(h2, B, S, pre_g, w_in, ml_gate_b, ml_norm_g, nsa_gate_b, k_pe, k_w1, k_w2, v_pe, v_w1, v_w2):
    D = h2.shape[1]
    G, d = NSA_KV_HEADS, HEAD_DIM
    offs = np.cumsum([0, ML_W, ML_W, ML_W, ML_W, 2 * ML_HEADS, NSA_W] + [G * d] * 6 + [3 * NSA_HEADS])
    mq, mk, mv, mo, mif, nq, kc, vc, ks, vs, kw, vw, ng = [w_in[:, offs[i]:offs[i + 1]] for i in range(13)]
    w_b = jnp.concatenate([mk, nq, ks, kw], axis=1).astype(BF16)
    w_t = jnp.concatenate([mq, mv, vs, vw], axis=1).T.astype(BF16)
    gpad = LANES - 2 * ML_HEADS - 3 * NSA_HEADS
    w_f = jnp.concatenate([mo, kc, vc, mif, ng, jnp.zeros((D, gpad), w_in.dtype)], axis=1).astype(BF16)
    gate_bias = jnp.concatenate([ml_gate_b, nsa_gate_b, jnp.zeros((gpad,), F32)]).reshape(1, LANES)
    zb, zf, zt = norm_proj(h2, pre_g, [w_b, w_f], [BF16, F32], wts=[w_t], batch=B)
    zb = zb.reshape(B, S, -1)
    zf = zf.reshape(B, S, -1)
    h_ml = mlstm_block(zt, zb, zf, gate_bias, ml_norm_g)
    kcmp, vcmp_t = compress_block(zf, ML_W // (G * d), ML_W // (G * d) + 1, k_pe, k_w1, k_w2, v_pe, v_w1, v_w2)
    h_nsa = nsa_block(zt, zb, zf, gate_bias, kcmp, vcmp_t, q_col_block=ML_W // NSA_W,
                      k_col_blocks=((ML_W + NSA_W) // (G * d), (ML_W + NSA_W) // (G * d) + 1),
                      v_row_blocks=(2 * ML_W // (G * d), 2 * ML_W // (G * d) + 1))
    return h_ml.reshape(B * S, ML_W), h_nsa.reshape(B * S, NSA_W)


def _mixer_cd(h2, B, S, pre_g, w_in, conv_w, conv_b, wa, ba, wx, bx, lam, sg_g, sg_bn, sg_w, sg_b):
    h3 = h2.reshape(B, S, -1)
    w_gate, w_x, w_u, w_v = (w_in[:, j * LRU_W:(j + 1) * LRU_W].astype(BF16) for j in range(4))
    y_lru = rglru_block(h3, pre_g, w_gate, w_x, conv_w, conv_b, wa, ba, wx, bx, lam)
    y_sg = sgu_block(h3, pre_g, w_u, w_v, sg_g, sg_bn, sg_w, sg_b)
    return y_lru.reshape(B * S, LRU_W), y_sg.reshape(B * S, SG_W)


def kernel(x, pre_mix_g, post_mix_g, pre_ffn_g, post_ffn_g, ab_w_in, ab_w_out, ml_gate_b, ml_norm_g, nsa_gate_b, cmp_k_pe, cmp_k_w1, cmp_k_w2, cmp_v_pe, cmp_v_w1, cmp_v_w2, cd_w_in, cd_w_out, lru_conv_w, lru_conv_b, lru_wa, lru_ba, lru_wx, lru_bx, lru_lambda, sg_norm_g, sg_norm_b, sg_w, sg_b, ffn_w_up, ffn_conv_w, ffn_conv_b, ffn_w_down):
    B, S, D = x.shape
    depth = pre_mix_g.shape[0]
    h2 = x.reshape(B * S, D)
    for layer in range(depth):
        if layer % 2 == 0:
            e = layer // 2
            a1, a2 = _mixer_ab(h2, B, S, pre_mix_g[layer], ab_w_in[e], ml_gate_b[e], ml_norm_g[e], nsa_gate_b[e],
                               cmp_k_pe[e], cmp_k_w1[e], cmp_k_w2[e], cmp_v_pe[e], cmp_v_w1[e], cmp_v_w2[e])
            w_out = ab_w_out[e]
        else:
            o = layer // 2
            a1, a2 = _mixer_cd(h2, B, S, pre_mix_g[layer], cd_w_in[o], lru_conv_w[o], lru_conv_b[o], lru_wa[o],
                               lru_ba[o], lru_wx[o], lru_bx[o], lru_lambda[o], sg_norm_g[o], sg_norm_b[o], sg_w[o],
                               sg_b[o])
            w_out = cd_w_out[o]
        h2 = mix_ffn_block(h2, a1, a2, S, w_out, post_mix_g[layer], pre_ffn_g[layer], ffn_w_up[layer],
                           ffn_conv_w[layer], ffn_conv_b[layer], ffn_w_down[layer], post_ffn_g[layer])
    return h2.reshape(B, S, D)
```

```python
import functools

import numpy as np
import jax
import jax.numpy as jnp
from jax import lax
from jax.experimental import pallas as pl
from jax.experimental.pallas import tpu as pltpu

F32 = jnp.float32
BF16 = jnp.bfloat16

EPS = 1e-6
HEAD_DIM = 64
ML_HEADS = 8
ML_W = 512
GATE_SOFTCAP = 15.0
NSA_HEADS = 8
NSA_KV_HEADS = 2
NSA_HPG = NSA_HEADS // NSA_KV_HEADS
NSA_W = 512
CMP_LEN = 32
CMP_STRIDE = 16
CMP_HID = 128
SEL_LEN = 64
SEL_TOPN = 16
WINDOW = 512
LRU_W = 512
LRU_C = 8.0
LRU_CONV = 4
SG_GROUPS = 8
SG_W = 512
SG_CHUNK = 128
FFN_CONV = 3
GATE_COL_BLOCK = (ML_W + 2 * NSA_KV_HEADS * HEAD_DIM) // 128
NSA_GATE_LANE0 = 2 * ML_HEADS

LANES = 128
VMEM_LIMIT = 56 * 1024 * 1024
NEG_BIG = -1e30
HIGHEST = lax.Precision.HIGHEST
LOG2E = 1.4426950408889634


def _bf16_terms(x, n):
    terms = []
    for _ in range(n):
        bits = int(np.array(x, np.float32).view(np.uint32))
        t = float(np.array((bits + 0x7FFF + ((bits >> 16) & 1)) & 0xFFFF0000, np.uint32).view(np.float32))
        terms.append(t)
        x -= t
    return tuple(terms)


LOG2E_TERMS = _bf16_terms(LOG2E, 3)

ML_CHUNK = 128
ML_NB = 8
NSA_TQ = 256
NSA_TK = 256
RANK_STEP = 16
NSA_ONES = 16
ROW_TILE = 512
FFN_TM = 512
FFN_SUB = 256
FFN_CK = 256
FFN_HALO = 16
SG_ROWS = 1024
SG_SUB = 256
LRU_T = 512
LRU_HALO = 8
LRU_NB = 4


def _cparams(sem):
    return pltpu.CompilerParams(dimension_semantics=sem, vmem_limit_bytes=VMEM_LIMIT)


def _rms(x, g):
    return x * lax.rsqrt(jnp.mean(x * x, axis=-1, keepdims=True) + EPS) * g


def _gelu(x):
    return 0.5 * x * (1.0 + jnp.tanh(0.7978845608028654 * (x + 0.044715 * (x * x * x))))


def _sigmoid(x):
    return 1.0 / (1.0 + jnp.exp(-x))


def _dot(a, b):
    return jnp.dot(a, b, preferred_element_type=F32)


def _dot_nt(a, b, precision=None):
    return lax.dot_general(a, b, (((1,), (1,)), ((), ())), preferred_element_type=F32, precision=precision)


def _norm_proj_kernel(h_ref, g_ref, *refs, n_row, n_t, cn):
    w_refs, wt_refs = refs[:n_row], refs[n_row:n_row + n_t]
    o_refs, ot_refs = refs[n_row + n_t:2 * n_row + n_t], refs[2 * n_row + n_t:]
    xn = _rms(h_ref[...], g_ref[...]).astype(BF16)
    for w_ref, o_ref in zip(w_refs, o_refs):
        n = w_ref.shape[1]
        for c in range(0, n, cn):
            ce = min(c + cn, n)
            o_ref[:, c:ce] = _dot(xn, w_ref[:, c:ce]).astype(o_ref.dtype)
    for wt_ref, ot_ref in zip(wt_refs, ot_refs):
        n = wt_ref.shape[0]
        for c in range(0, n, cn):
            ce = min(c + cn, n)
            ot_ref[c:ce, :] = _dot_nt(wt_ref[c:ce, :], xn).astype(ot_ref.dtype)


def norm_proj(h2, g, ws, dtypes, wts=(), batch=1, tm=ROW_TILE):
    M, D = h2.shape
    tps = M // batch // tm
    in_specs = [pl.BlockSpec((tm, D), lambda i: (i, 0)), pl.BlockSpec((1, D), lambda i: (0, 0))]
    in_specs += [pl.BlockSpec(w.shape, lambda i: (0, 0)) for w in (*ws, *wts)]
    out_specs = [pl.BlockSpec((tm, w.shape[1]), lambda i: (i, 0)) for w in ws]
    out_specs += [pl.BlockSpec((None, w.shape[0], tm), lambda i: (i // tps, 0, i % tps)) for w in wts]
    out_shape = [jax.ShapeDtypeStruct((M, w.shape[1]), dt) for w, dt in zip(ws, dtypes)]
    out_shape += [jax.ShapeDtypeStruct((batch, w.shape[0], M // batch), BF16) for w in wts]
    return pl.pallas_call(
        functools.partial(_norm_proj_kernel, n_row=len(ws), n_t=len(wts), cn=512),
        grid=(M // tm,), in_specs=in_specs, out_specs=out_specs, out_shape=out_shape,
        compiler_params=_cparams(("parallel",)), name="norm_proj",
    )(h2, g.reshape(1, D), *ws, *wts)


def _mix_ffn_kernel(h_ref, hh_ref, a1_ref, a1h_ref, a2_ref, a2h_ref, wo1_ref, wo2_ref, gmix_ref, gpre_ref, wu_ref,
                    cw_ref, cb_ref, wd_ref, gpost_ref, o_ref, xn_s, acc_s, *, tiles_per_seq, ck):
    F = wd_ref.shape[0]
    n_chunks = F // ck
    n_sub = h_ref.shape[0] // FFN_SUB
    first = (pl.program_id(0) % tiles_per_seq) == 0
    g = gpre_ref[...]

    def mixed(h, a1, a2):
        return h + _rms(_dot(a1, wo1_ref[...]) + _dot(a2, wo2_ref[...]), gmix_ref[...])

    def conv(u, cols):
        y = (cw_ref[2:3, cols] * u + cw_ref[1:2, cols] * pltpu.roll(u, 1, 0) + cw_ref[0:1, cols] * pltpu.roll(u, 2, 0)
             + cb_ref[:, cols])
        return y[FFN_HALO:, :]

    def cols_of(c, half):
        return slice(half * F + c * ck, half * F + (c + 1) * ck)

    x = {}

    def head(j):
        rows = slice(j * FFN_SUB, (j + 1) * FFN_SUB)
        x[j] = mixed(h_ref[rows, :], a1_ref[rows, :], a2_ref[rows, :])
        if j == 0:
            halo = jnp.where(first, 0.0, _rms(mixed(hh_ref[...], a1h_ref[...], a2h_ref[...]), g))
            xn_s[0:FFN_HALO, :] = halo.astype(BF16)
        xn_s[FFN_HALO + j * FFN_SUB:FFN_HALO + (j + 1) * FFN_SUB, :] = _rms(x[j], g).astype(BF16)
        acc_s[rows, :] = jnp.zeros((FFN_SUB, acc_s.shape[1]), F32)

    def up(j, c):
        xn = xn_s[j * FFN_SUB:(j + 1) * FFN_SUB + FFN_HALO, :]
        return _dot(xn, wu_ref[:, cols_of(c, 0)]), _dot(xn, wu_ref[:, cols_of(c, 1)])

    def tail(j):
        rows = slice(j * FFN_SUB, (j + 1) * FFN_SUB)
        o_ref[rows, :] = x[j] + _rms(acc_s[rows, :], gpost_ref[...])

    head(0)
    for j in range(n_sub):
        u = up(j, 0)
        for c in range(n_chunks):
            u_next = up(j, c + 1) if c + 1 < n_chunks else None
            if c == 0 and j + 1 < n_sub:
                head(j + 1)
            if c == n_chunks // 3 and j >= 1:
                tail(j - 1)
            act = (_gelu(conv(u[0], cols_of(c, 0))) * conv(u[1], cols_of(c, 1))).astype(BF16)
            acc_s[j * FFN_SUB:(j + 1) * FFN_SUB, :] += _dot(act, wd_ref[c * ck:(c + 1) * ck, :])
            u = u_next
    tail(n_sub - 1)


def mix_ffn_block(h2, a1, a2, seq_len, w_out, g_mix, g_pre, w_up, conv_w, conv_b, w_down, g_post, tm=FFN_TM, ck=FFN_CK):
    M, D = h2.shape
    F = w_down.shape[0]
    K1, K2 = a1.shape[1], a2.shape[1]
    assert F % ck == 0 and seq_len % tm == 0 and tm % FFN_SUB == 0 and FFN_SUB % FFN_HALO == 0
    assert conv_w.shape[0] == FFN_CONV and FFN_CONV - 1 <= FFN_HALO
    hb = tm // FFN_HALO
    consts = (w_out[:K1].astype(BF16), w_out[K1:].astype(BF16), g_mix.reshape(1, D), g_pre.reshape(1, D),
              w_up.astype(BF16), conv_w, conv_b.reshape(1, 2 * F), w_down.astype(BF16), g_post.reshape(1, D))

    def tile(width):
        return pl.BlockSpec((tm, width), lambda i: (i, 0))

    def halo(width):
        return pl.BlockSpec((FFN_HALO, width), lambda i: (jnp.maximum(i * hb - 1, 0), 0))

    def full(a):
        return pl.BlockSpec(a.shape, lambda i: (0,) * a.ndim)

    return pl.pallas_call(
        functools.partial(_mix_ffn_kernel, tiles_per_seq=seq_len // tm, ck=ck),
        grid=(M // tm,),
        in_specs=[tile(D), halo(D), tile(K1), halo(K1), tile(K2), halo(K2)] + [full(a) for a in consts],
        out_specs=tile(D),
        out_shape=jax.ShapeDtypeStruct((M, D), F32),
        scratch_shapes=[pltpu.VMEM((tm + FFN_HALO, D), BF16), pltpu.VMEM((tm, D), F32)],
        compiler_params=_cparams(("parallel",)), name="mix_ffn",
    )(h2, h2, a1, a1, a2, a2, *consts)


def _mlstm_kernel(qt_ref, vt_ref, k_ref, mo_ref, gt_ref, gb_ref, ng_ref, o_ref, cn_s, m_s):
    NB, _, L = qt_ref.shape
    d = HEAD_DIM

    @pl.when(pl.program_id(1) == 0)
    def _():
        cn_s[...] = jnp.zeros_like(cn_s)
        m_s[...] = jnp.zeros_like(m_s)

    src = lax.broadcasted_iota(jnp.int32, (L, L), 0)
    tgt = lax.broadcasted_iota(jnp.int32, (L, L), 1)
    causal = src <= tgt
    tri = (tgt <= src).astype(F32)

    gcap = [GATE_SOFTCAP * jnp.tanh((gt_ref[nb] + gb_ref[...]) * (1.0 / GATE_SOFTCAP)) for nb in range(NB)]
    lf = [jnp.minimum(x, 0.0) - jnp.log1p(jnp.exp(-jnp.abs(x))) for x in gcap]
    b_col = [jnp.dot(tri, x, preferred_element_type=F32, precision=HIGHEST) for x in lf]
    b_row = [x.T for x in b_col]
    i_row = [x.T for x in gcap]
    c_col = [b_col[nb] - pltpu.roll(gcap[nb], ML_HEADS, 1) for nb in range(NB)]

    chains = [(nb, h) for nb in range(NB) for h in range(ML_HEADS)]
    n = range(len(chains))

    def rows(h):
        return slice(h * d, (h + 1) * d)

    q_t = [qt_ref[nb, rows(h), :] for nb, h in chains]
    v_t = [vt_ref[nb, rows(h), :] for nb, h in chains]
    k = [k_ref[nb, :, rows(h)] * 0.125 for nb, h in chains]
    br = [b_row[nb][ML_HEADS + h:ML_HEADS + h + 1, :] for nb, h in chains]
    ir = [i_row[nb][h:h + 1, :] for nb, h in chains]
    g = [x[:, L - 1:L] for x in br]
    m_prev = [m_s[nb, h:h + 1, 0:1] for nb, h in chains]
    cn_prev = [cn_s[nb, h] for nb, h in chains]
    kq = [_dot(k[c], q_t[c]) for c in n]
    carry = [_dot(cn_prev[c].astype(BF16), q_t[c]) for c in n]
    dlog = [jnp.where(causal, br[c] - c_col[nb][:, ML_HEADS + h:ML_HEADS + h + 1], -jnp.inf)
            for c, (nb, h) in enumerate(chains)]
    inter = [br[c] + m_prev[c] for c in n]
    m_row = [jnp.maximum(inter[c], jnp.max(dlog[c], axis=0, keepdims=True)) for c in n]
    s = [kq[c] * jnp.exp(dlog[c] - m_row[c]) for c in n]
    w_inter = [jnp.exp(inter[c] - m_row[c]) for c in n]
    num = [_dot(v_t[c], s[c].astype(BF16)) + w_inter[c] * carry[c][0:d] for c in n]
    den = [jnp.sum(s[c], axis=0, keepdims=True) + w_inter[c] * carry[c][d:d + 1] for c in n]
    hh = [num[c] * (1.0 / jnp.maximum(jnp.abs(den[c]), jnp.exp(-m_row[c]))) for c in n]
    wlog = [g[c] - br[c] + ir[c] for c in n]
    m_new = [jnp.maximum(g[c] + m_prev[c], jnp.max(wlog[c], axis=-1, keepdims=True)) for c in n]
    w_row = [jnp.exp(wlog[c] - m_new[c]) for c in n]
    decay = [jnp.exp(g[c] + m_prev[c] - m_new[c]) for c in n]
    for c, (nb, h) in enumerate(chains):
        vw = jnp.concatenate([v_t[c].astype(F32) * w_row[c], jnp.broadcast_to(w_row[c], (8, L))], axis=0).astype(BF16)
        cn_s[nb, h] = decay[c] * cn_prev[c] + _dot(vw, k[c])
        m_s[nb, h:h + 1, :] = jnp.broadcast_to(m_new[c], (1, m_s.shape[2]))
    outs = [hh[c] * lax.rsqrt(jnp.mean(hh[c] * hh[c], axis=0, keepdims=True) + EPS) * ng_ref[rows(h), :]
            for c, (nb, h) in enumerate(chains)]
    for nb in range(NB):
        out_t = jnp.concatenate(outs[nb * ML_HEADS:(nb + 1) * ML_HEADS], axis=0)
        o_ref[nb] = (out_t.T * _sigmoid(mo_ref[nb])).astype(o_ref.dtype)


def mlstm_block(zt, zb, zf, gate_bias, norm_g, L=ML_CHUNK, NB=ML_NB):
    B, S, _ = zb.shape
    W, H, d = ML_W, ML_HEADS, HEAD_DIM
    assert B % NB == 0 and S % L == 0
    ng = jnp.broadcast_to(norm_g.reshape(W, 1), (W, L))
    return pl.pallas_call(
        _mlstm_kernel, grid=(B // NB, S // L),
        in_specs=[pl.BlockSpec((NB, W, L), lambda b, c: (b, 0, c)),
                  pl.BlockSpec((NB, W, L), lambda b, c: (b, 1, c)),
                  pl.BlockSpec((NB, L, W), lambda b, c: (b, c, 0)),
                  pl.BlockSpec((NB, L, W), lambda b, c: (b, c, 0)),
                  pl.BlockSpec((NB, L, LANES), lambda b, c: (b, c, GATE_COL_BLOCK)),
                  pl.BlockSpec((1, LANES), lambda b, c: (0, 0)),
                  pl.BlockSpec((W, L), lambda b, c: (0, 0))],
        out_specs=pl.BlockSpec((NB, L, W), lambda b, c: (b, c, 0)),
        out_shape=jax.ShapeDtypeStruct((B, S, W), BF16),
        scratch_shapes=[pltpu.VMEM((NB, H, d + 8, d), F32), pltpu.VMEM((NB, H, LANES), F32)],
        compiler_params=_cparams(("parallel", "arbitrary")), name="mlstm",
    )(zt, zt, zb, zf, zf, gate_bias, ng)


def _compress_kernel(kc_ref, vc_ref, pek_ref, pev_ref, w1k_ref, w1v_ref, w2k_ref, w2v_ref, ok_ref, ov_ref):
    G = NSA_KV_HEADS
    nh = kc_ref.shape[0] // CMP_STRIDE

    def one(x_ref, pe_ref, w1_ref, w2_ref):
        hid = None
        for l in range(CMP_STRIDE):
            y = x_ref[pl.ds(l, nh, stride=CMP_STRIDE), :]
            ya = (y + pe_ref[l:l + 1, :]).astype(BF16)
            yb = (pltpu.roll(y, nh - 1, 0) + pe_ref[CMP_STRIDE + l:CMP_STRIDE + l + 1, :]).astype(BF16)
            t = _dot(ya, w1_ref[l]) + _dot(yb, w1_ref[CMP_STRIDE + l])
            hid = t if hid is None else hid + t
        hid = _gelu(hid).astype(BF16)
        return [_dot(hid[:, gi * CMP_HID:(gi + 1) * CMP_HID], w2_ref[...]) for gi in range(G)]

    for gi, (ko, vo) in enumerate(zip(one(kc_ref, pek_ref, w1k_ref, w2k_ref), one(vc_ref, pev_ref, w1v_ref, w2v_ref))):
        ok_ref[gi] = ko.astype(ok_ref.dtype)
        ov_ref[gi] = vo.T.astype(ov_ref.dtype)


def compress_block(zf, kc_block, vc_block, k_pe, k_w1, k_w2, v_pe, v_w1, v_w2):
    B, S, _ = zf.shape
    G, d = NSA_KV_HEADS, HEAD_DIM
    nh = S // CMP_STRIDE

    def prep(pe, w1):
        w1bd = jnp.zeros((CMP_LEN, G * d, G * CMP_HID), w1.dtype)
        for gi in range(G):
            w1bd = w1bd.at[:, gi * d:(gi + 1) * d, gi * CMP_HID:(gi + 1) * CMP_HID].set(w1)
        return jnp.tile(pe, (1, G)), w1bd.astype(BF16)

    pek, w1k = prep(k_pe, k_w1)
    pev, w1v = prep(v_pe, v_w1)

    def full(a):
        return pl.BlockSpec(a.shape, lambda b: (0,) * a.ndim)

    w2k, w2v = k_w2.astype(BF16), v_w2.astype(BF16)
    return pl.pallas_call(
        _compress_kernel, grid=(B,),
        in_specs=[pl.BlockSpec((None, S, G * d), lambda b: (b, 0, kc_block)),
                  pl.BlockSpec((None, S, G * d), lambda b: (b, 0, vc_block)),
                  full(pek), full(pev), full(w1k), full(w1v), full(w2k), full(w2v)],
        out_specs=[pl.BlockSpec((None, G, nh, d), lambda b: (b, 0, 0, 0)),
                   pl.BlockSpec((None, G, d, nh), lambda b: (b, 0, 0, 0))],
        out_shape=[jax.ShapeDtypeStruct((B, G, nh, d), BF16), jax.ShapeDtypeStruct((B, G, d, nh), BF16)],
        compiler_params=_cparams(("parallel",)), name="nsa_compress",
    )(zf, zf, pek, pev, w1k, w1v, w2k, w2v)


def _nsa_kernel(q_ref, gt_ref, gb_ref, kc_ref, vct_ref, ks_ref, vst_ref, kw_ref, vwt_ref, ovl_ref, aug_ref, caug_ref,
                tail_ref, ctail_ref, wmask_ref, o_ref, ksa_s, kwa_s, kca_s, vsa_s, vwa_s, ss_s, ps_s, sw_s, pw_s, *, n_cmp):
    TQ, TK, d, HPG, G = NSA_TQ, NSA_TK, HEAD_DIM, NSA_HPG, NSA_KV_HEADS
    R = HPG * TQ
    groups = range(G)
    qi = pl.program_id(1)
    q0 = qi * TQ
    kt_d = q0 // TK
    n_cb = kc_ref.shape[1]
    n_sb = ovl_ref.shape[0]

    @pl.when((qi == 0) & (pl.program_id(0) == 0))
    def _():
        for g in groups:
            ksa_s[g, :, d:] = aug_ref[...]
            kwa_s[g, :, d:] = aug_ref[...]
            kca_s[g, :, d:] = caug_ref[...]
            vsa_s[g, d:, :] = jnp.ones((vsa_s.shape[1] - d, vsa_s.shape[2]), BF16)
            vwa_s[g, d:, :] = jnp.ones((vwa_s.shape[1] - d, vwa_s.shape[2]), BF16)

    @pl.when(qi == 0)
    def _():
        for g in groups:
            ksa_s[g, :, 0:d] = ks_ref[:, g * d:(g + 1) * d]
            kwa_s[g, :, 0:d] = kw_ref[:, g * d:(g + 1) * d]
            kca_s[g, :, 0:d] = kc_ref[g]
            vsa_s[g, 0:d, :] = vst_ref[g * d:(g + 1) * d, :]
            vwa_s[g, 0:d, :] = vwt_ref[g * d:(g + 1) * d, :]

    def slope(g, hh):
        return 2.0 ** (-(g * HPG + hh + 1))

    def per_head(fn):
        return jnp.concatenate([fn(hh) for hh in range(HPG)], axis=1)

    def tile_heads(x):
        return jnp.concatenate([x] * HPG, axis=1)

    q_t = [(q_ref[:, g * HPG * d:(g + 1) * HPG * d].astype(F32) * (LOG2E * 0.125)).T for g in groups]

    def q_head(g, hh):
        return q_t[g][hh * d:(hh + 1) * d]

    def q_aug(g, block_rows):
        return per_head(lambda hh: jnp.concatenate(
            [q_head(g, hh), block_rows, tail_ref[...] * slope(g, hh)], axis=0)).astype(BF16)

    def scores(ka, qa_t, kt):
        return _dot(ka[pl.ds(pl.multiple_of(kt * TK, TK), TK), :], qa_t)

    def stage_scores(buf, ka, qa_t, kt, mask_add=None):
        s = scores(ka, qa_t, kt)
        if mask_add is not None:
            s = s + mask_add
        buf[...] = s
        return jnp.max(s, axis=0, keepdims=True)

    def stage_probs(sbuf, pbuf, tile_max, m):
        m_new = jnp.maximum(m, tile_max)
        pbuf[...] = jnp.exp2(sbuf[...] - m_new).astype(BF16)
        return m_new, jnp.exp2(m - m_new)

    def stage_values(pbuf, va, kt, alpha, acc):
        return alpha * acc + _dot(va[:, pl.ds(pl.multiple_of(kt * TK, TK), TK)], pbuf[...])

    def normalised(acc):
        return acc[0:d] * (1.0 / acc[d:d + 1])

    m0, acc0 = jnp.full((1, R), NEG_BIG, F32), jnp.zeros((vsa_s.shape[1], R), F32)

    qc_t = [per_head(lambda hh: jnp.concatenate([q_head(g, hh), ctail_ref[...] * slope(g, hh)], axis=0)).astype(BF16)
            for g in groups]
    qw_t = [q_aug(g, jnp.zeros((n_sb, TQ), F32)) for g in groups]
    n_r = lax.broadcasted_iota(jnp.int32, (n_cb, TQ), 0)
    t_c = q0 + lax.broadcasted_iota(jnp.int32, (n_cb, TQ), 1)
    ok_c = (n_r * CMP_STRIDE + (CMP_LEN - 1) <= t_c) & (n_r < n_cmp)
    add_c = tile_heads(jnp.where(ok_c, 0.0, NEG_BIG))
    s_c = [_dot(kca_s[g], qc_t[g]) + add_c for g in groups]
    n_win = (WINDOW - 1 + TK - 1) // TK + 1
    kt_win, max_win = [], []
    for back in range(n_win):
        kt_raw = kt_d - back
        kt_win.append(jnp.maximum(kt_raw, 0))
        if back == 0:
            mask_add = tile_heads(wmask_ref[0])
        elif (back + 1) * TK <= WINDOW:
            mask_add = jnp.where(kt_raw >= 0, 0.0, NEG_BIG)
        else:
            mask_add = tile_heads(wmask_ref[back] + jnp.where(kt_raw >= 0, 0.0, NEG_BIG))
        max_win.append([stage_scores(sw_s.at[g, back], kwa_s.at[g], qw_t[g], kt_win[back], mask_add) for g in groups])

    e_c = [jnp.exp2(s_c[g] - jnp.maximum(jnp.max(s_c[g], axis=0, keepdims=True), 0.1 * NEG_BIG)) for g in groups]
    inv_c = [1.0 / jnp.maximum(jnp.sum(e_c[g], axis=0, keepdims=True), 1.0) for g in groups]
    o_c = [_dot(vct_ref[g], e_c[g].astype(BF16)) * inv_c[g] for g in groups]

    jb = lax.broadcasted_iota(jnp.int32, (n_sb, TQ), 0)
    cur = (q0 + lax.broadcasted_iota(jnp.int32, (n_sb, TQ), 1)) // SEL_LEN
    valid = jb <= cur
    forced = (jb == 0) | (jb == cur) | (jb == cur - 1)
    sub = lax.broadcasted_iota(jnp.int32, (8, TQ), 0)
    score = []
    for g in groups:
        p_sum = e_c[g][:, 0:TQ] * inv_c[g][:, 0:TQ]
        for hh in range(1, HPG):
            p_sum = p_sum + e_c[g][:, hh * TQ:(hh + 1) * TQ] * inv_c[g][:, hh * TQ:(hh + 1) * TQ]
        imp = jnp.dot(ovl_ref[...], p_sum, preferred_element_type=F32, precision=HIGHEST)
        score.append(jnp.where(forced, jnp.inf, jnp.where(valid, imp, -jnp.inf)))

    def ranked(n_rows):
        def fn(*score):
            n_grp = n_rows // 8
            outs = []
            grp = [[s[8 * a:8 * a + 8] for a in range(n_grp)] for s in score]
            rank = [[jnp.zeros((8, TQ), jnp.int32) for _ in range(n_grp)] for _ in score]
            for j in range(n_rows):
                for g in groups:
                    r = score[g][j:j + 1, :]
                    for a in range(n_grp):
                        if a > j // 8:
                            ahead = (r >= grp[g][a]).astype(jnp.int32)
                        elif a < j // 8:
                            ahead = (r > grp[g][a]).astype(jnp.int32)
                        else:
                            ahead = jnp.where(sub > j % 8, (r >= grp[g][a]).astype(jnp.int32),
                                              (r > grp[g][a]).astype(jnp.int32))
                        rank[g][a] = rank[g][a] + ahead
            for g in groups:
                top = jnp.where(jnp.concatenate(rank[g], axis=0) < SEL_TOPN, 1.0, 0.0)
                outs.append(jnp.concatenate([top, jnp.zeros((n_sb - n_rows, TQ), F32)], axis=0) if n_rows < n_sb else top)
            return tuple(outs)
        return fn

    sizes = [n for n in range(RANK_STEP, n_sb + 1, RANK_STEP)]
    rows_needed = (q0 + TQ + SEL_LEN - 1) // SEL_LEN
    in_top = lax.switch((rows_needed + RANK_STEP - 1) // RANK_STEP - 1, [ranked(n) for n in sizes], *score)
    picked = [valid & (in_top[g] > 0.5) for g in groups]
    qs_t = [q_aug(g, jnp.where(picked[g], 0.0, NEG_BIG)) for g in groups]

    filler = 1 - kt_d % 2
    n_seq = kt_d + 1 + filler

    def sel_tile(i):
        return jnp.where(i == 0, kt_d, jnp.maximum(i - 1 - filler, 0))

    def a_stage(slot, kt, mask_add=None):
        return [stage_scores(ss_s.at[g, slot], ksa_s.at[g], qs_t[g], kt, mask_add) for g in groups]

    def b_stage(slot, tmax, m):
        out = [stage_probs(ss_s.at[g, slot], ps_s.at[g, slot], tmax[g], m[g]) for g in groups]
        return [o[0] for o in out], [o[1] for o in out]

    def c_stage(slot, kt, alpha, acc):
        return [stage_values(ps_s.at[g, slot], vsa_s.at[g], kt, alpha[g], acc[g]) for g in groups]

    tmax0 = a_stage(0, kt_d, tile_heads(wmask_ref[0]))
    m, alpha = b_stage(0, tmax0, [m0] * G)
    tmax1 = a_stage(1, sel_tile(1), jnp.where(filler == 1, NEG_BIG, 0.0))

    m_w = [functools.reduce(jnp.maximum, [max_win[back][g] for back in range(n_win)]) for g in groups]
    accw = [None] * G
    for back in range(n_win):
        for g in groups:
            pw_s[g, back] = jnp.exp2(sw_s[g, back] - m_w[g]).astype(BF16)
        for g in groups:
            pv = _dot(vwa_s[g, :, pl.ds(pl.multiple_of(kt_win[back] * TK, TK), TK)], pw_s[g, back])
            accw[g] = pv if accw[g] is None else accw[g] + pv
    o_w = [normalised(accw[g]) for g in groups]

    def sel_body(k, carry):
        m, alpha, acc, tmax1 = carry
        i = 2 * k
        tmax0 = a_stage(0, sel_tile(i + 2))
        m, alpha1 = b_stage(1, tmax1, m)
        tmax1 = a_stage(1, sel_tile(i + 3))
        acc = c_stage(0, sel_tile(i), alpha, acc)
        m, alpha2 = b_stage(0, tmax0, m)
        acc = c_stage(1, sel_tile(i + 1), alpha1, acc)
        return m, alpha2, acc, tmax1

    m, alpha, acc, tmax1 = lax.fori_loop(0, (n_seq - 2) // 2, sel_body, (m, alpha, [acc0] * G, tmax1))
    m, alpha1 = b_stage(1, tmax1, m)
    acc = c_stage(0, sel_tile(n_seq - 2), alpha, acc)
    acc = c_stage(1, sel_tile(n_seq - 1), alpha1, acc)
    o_s = [normalised(acc[g]) for g in groups]

    gates = _sigmoid(gt_ref[...] + gb_ref[...]).T
    outs = []
    for g in groups:
        for hh in range(HPG):
            cols = slice(hh * TQ, (hh + 1) * TQ)
            row = NSA_GATE_LANE0 + 3 * (g * HPG + hh)
            outs.append(gates[row:row + 1] * o_c[g][:, cols] + gates[row + 1:row + 2] * o_s[g][:, cols]
                        + gates[row + 2:row + 3] * o_w[g][:, cols])
    o_ref[...] = jnp.concatenate(outs, axis=0).T.astype(o_ref.dtype)


def nsa_block(zt, zb, zf, gate_bias, kcmp, vcmp_t, q_col_block, k_col_blocks, v_row_blocks):
    B, S, _ = zb.shape
    G, d = NSA_KV_HEADS, HEAD_DIM
    TQ = NSA_TQ
    n_cb = kcmp.shape[2]
    n_cmp = (S - CMP_LEN) // CMP_STRIDE + 1
    n_sb = S // SEL_LEN
    n_terms = len(LOG2E_TERMS)
    kaug = -(-(d + n_sb + 2 * n_terms) // LANES) * LANES
    R = NSA_HPG * TQ
    n_win = (WINDOW - 1 + NSA_TK - 1) // NSA_TK + 1
    assert S % NSA_TK == 0 and NSA_TK == TQ and TQ % SEL_LEN == 0

    cidx = np.arange(n_cb)[None, :] * CMP_STRIDE
    sstart = np.arange(n_sb)[:, None] * SEL_LEN
    ovl = ((cidx < sstart + SEL_LEN) & (cidx + CMP_LEN - 1 >= sstart) & (np.arange(n_cb)[None, :] < n_cmp))
    ovl = jnp.asarray(ovl.astype(np.float32))
    pos = np.arange(S)
    aug = np.zeros((S, kaug - d), np.float32)
    aug[pos, pos // SEL_LEN] = 1.0
    tail = np.zeros((kaug - d - n_sb, TQ), np.float32)
    caug = np.zeros((n_cb, d), np.float32)
    ctail = np.zeros((d, TQ), np.float32)
    for i, term in enumerate(LOG2E_TERMS):
        aug[:, n_sb + i] = pos // SEL_LEN
        aug[:, n_sb + n_terms + i] = pos % SEL_LEN
        tail[i] = term * SEL_LEN
        tail[n_terms + i] = term
        caug[:, i] = np.arange(n_cb)
        ctail[i] = term * CMP_STRIDE
    aug, caug, tail, ctail = jnp.asarray(aug, BF16), jnp.asarray(caug, BF16), jnp.asarray(tail), jnp.asarray(ctail)
    dist = np.arange(n_win)[:, None, None] * NSA_TK + np.arange(TQ)[None, None, :] - np.arange(NSA_TK)[None, :, None]
    wmask = jnp.asarray(np.where((dist >= 0) & (dist < WINDOW), 0.0, NEG_BIG).astype(np.float32))

    def kspec(j):
        return pl.BlockSpec((None, S, G * d), lambda b, i: (b, 0, k_col_blocks[j]))

    def vspec(j):
        return pl.BlockSpec((None, G * d, S), lambda b, i: (b, v_row_blocks[j], 0))

    def const(a):
        return pl.BlockSpec(a.shape, lambda b, i: (0,) * a.ndim)

    W = G * NSA_HPG * d
    return pl.pallas_call(
        functools.partial(_nsa_kernel, n_cmp=n_cmp), grid=(B, S // TQ),
        in_specs=[pl.BlockSpec((None, TQ, W), lambda b, i: (b, i, q_col_block)),
                  pl.BlockSpec((None, TQ, LANES), lambda b, i: (b, i, GATE_COL_BLOCK)),
                  const(gate_bias),
                  pl.BlockSpec((None, G, n_cb, d), lambda b, i: (b, 0, 0, 0)),
                  pl.BlockSpec((None, G, d, n_cb), lambda b, i: (b, 0, 0, 0)),
                  kspec(0), vspec(0), kspec(1), vspec(1), const(ovl), const(aug), const(caug), const(tail),
                  const(ctail), const(wmask)],
        out_specs=pl.BlockSpec((None, TQ, W), lambda b, i: (b, i, 0)),
        out_shape=jax.ShapeDtypeStruct((B, S, NSA_W), BF16),
        scratch_shapes=[pltpu.VMEM((G, S, kaug), BF16), pltpu.VMEM((G, S, kaug), BF16),
                        pltpu.VMEM((G, n_cb, 2 * d), BF16),
                        pltpu.VMEM((G, d + NSA_ONES, S), BF16), pltpu.VMEM((G, d + NSA_ONES, S), BF16),
                        pltpu.VMEM((G, 2, NSA_TK, R), F32), pltpu.VMEM((G, 2, NSA_TK, R), BF16),
                        pltpu.VMEM((G, n_win, NSA_TK, R), F32), pltpu.VMEM((G, n_win, NSA_TK, R), BF16)],
        compiler_params=_cparams(("arbitrary", "arbitrary")), name="nsa",
    )(zb, zf, gate_bias, kcmp, vcmp_t, zb, zt, zb, zt, ovl, aug, caug, tail, ctail, wmask)


def _rglru_kernel(h_ref, hh_ref, g_ref, wg_ref, wi_ref, cw_ref, cb_ref, wa_ref, wx_ref, ba_ref, bx_ref, lam_ref,
                  o_ref, h_s, au_s):
    NB, T, W = o_ref.shape
    first = pl.program_id(1) == 0

    @pl.when(first)
    def _():
        h_s[...] = jnp.zeros_like(h_s)

    def project(nb):
        xn = _rms(h_ref[nb], g_ref[...]).astype(BF16)
        x_halo = _dot(_rms(hh_ref[nb], g_ref[...]).astype(BF16), wi_ref[...])
        xe = jnp.concatenate([jnp.where(first, 0.0, x_halo), _dot(xn, wi_ref[...])], axis=0)
        return _dot(xn, wg_ref[...]), xe

    def recur(nb, gate, xe):
        xc = cb_ref[...] + cw_ref[3:4, :] * xe
        for k in range(1, LRU_CONV):
            xc = xc + cw_ref[3 - k:4 - k, :] * pltpu.roll(xe, k, 0)
        xc = xc[LRU_HALO:, :]
        xcb = xc.astype(BF16)
        half = W // 2

        def blockdiag(w_ref):
            return jnp.concatenate([_dot(xcb[:, :half], w_ref[0]), _dot(xcb[:, half:], w_ref[1])], axis=1)

        r = _sigmoid(blockdiag(wa_ref) + ba_ref[...])
        i = _sigmoid(blockdiag(wx_ref) + bx_ref[...])
        nl = -lam_ref[...]
        softplus = jnp.maximum(nl, 0.0) + jnp.log1p(jnp.exp(-jnp.abs(nl)))
        log_a = -LRU_C * r * softplus
        a = jnp.exp(log_a)
        one_m_a2 = -jnp.tanh(log_a) * (a * a + 1.0)
        u = jnp.where(one_m_a2 > 0.0, one_m_a2 * lax.rsqrt(one_m_a2), 0.0) * (i * xc)

        n_grp, n_slab = T // 8, W // LANES

        def phases(x, slab):
            for c in range(n_slab):
                au_s[nb, slab, c] = x[:, c * LANES:(c + 1) * LANES]
            return [jnp.concatenate([au_s[nb, slab, c, pl.ds(j, n_grp, stride=8), :] for c in range(n_slab)], axis=1)
                    for j in range(8)]

        a_ph, u_ph = phases(a, 0), phases(u, 1)
        prod, part = [a_ph[0]], [u_ph[0]]
        for j in range(1, 8):
            part.append(a_ph[j] * part[-1] + u_ph[j])
            prod.append(a_ph[j] * prod[-1])
        grp = lax.broadcasted_iota(jnp.int32, (n_grp, W), 0)
        ag, ug = prod[7], part[7]
        sft = 1
        while sft < n_grp:
            keep = grp >= sft
            ug = ag * jnp.where(keep, pltpu.roll(ug, sft, 0), 0.0) + ug
            ag = ag * jnp.where(keep, pltpu.roll(ag, sft, 0), 1.0)
            sft *= 2
        h_prev = h_s[nb, 0:1, :]
        hg = ug + ag * h_prev
        carry_in = jnp.where(grp >= 1, pltpu.roll(hg, 1, 0), h_prev)
        for j in range(8):
            hj = part[j] + prod[j] * carry_in
            for c in range(n_slab):
                au_s[nb, 0, c, pl.ds(j, n_grp, stride=8), :] = hj[:, c * LANES:(c + 1) * LANES]
        h = jnp.concatenate([au_s[nb, 0, c] for c in range(n_slab)], axis=1)
        h_s[nb] = jnp.broadcast_to(hg[n_grp - 1:n_grp, :], h_s.shape[1:])
        o_ref[nb] = (_gelu(gate) * h).astype(o_ref.dtype)

    nxt = project(0)
    for nb in range(NB):
        cur = nxt
        if nb + 1 < NB:
            nxt = project(nb + 1)
        recur(nb, *cur)


def rglru_block(h3, pre_g, w_gate, w_in, conv_w, conv_b, wa, ba, wx, bx, lam, T=LRU_T, NB=LRU_NB):
    B, S, D = h3.shape
    W = LRU_W
    half = W // 2
    hb = T // LRU_HALO

    def bd(w):
        blocks = [jax.scipy.linalg.block_diag(*[w[h] for h in range(4 * j, 4 * j + 4)]) for j in range(2)]
        return jnp.stack(blocks).astype(BF16)

    vec = lambda a: a.reshape(1, W)
    assert B % NB == 0 and S % T == 0
    vspec = pl.BlockSpec((1, W), lambda b, t: (0, 0))
    wspec = pl.BlockSpec((2, half, half), lambda b, t: (0, 0, 0))
    return pl.pallas_call(
        _rglru_kernel, grid=(B // NB, S // T),
        in_specs=[pl.BlockSpec((NB, T, D), lambda b, t: (b, t, 0)),
                  pl.BlockSpec((NB, LRU_HALO, D), lambda b, t: (b, jnp.maximum(t * hb - 1, 0), 0)),
                  pl.BlockSpec((1, D), lambda b, t: (0, 0)),
                  pl.BlockSpec((D, W), lambda b, t: (0, 0)), pl.BlockSpec((D, W), lambda b, t: (0, 0)),
                  pl.BlockSpec((LRU_CONV, W), lambda b, t: (0, 0)), vspec, wspec, wspec, vspec, vspec, vspec],
        out_specs=pl.BlockSpec((NB, T, W), lambda b, t: (b, t, 0)),
        out_shape=jax.ShapeDtypeStruct((B, S, W), BF16),
        scratch_shapes=[pltpu.VMEM((NB, 8, W), F32), pltpu.VMEM((NB, 2, W // LANES, T, LANES), F32)],
        compiler_params=_cparams(("parallel", "arbitrary")), name="rglru",
    )(h3, h3, pre_g.reshape(1, D), w_gate, w_in, conv_w, vec(conv_b), bd(wa), bd(wx), vec(ba), vec(bx), vec(lam))


def _sgu_kernel(h_ref, gpre_ref, wu_ref, wv_ref, g_ref, b_ref, w_ref, bias_ref, o_ref):
    C, W = SG_CHUNK, o_ref.shape[1]
    dg = W // SG_GROUPS
    n_sub = o_ref.shape[0] // SG_SUB
    row = lax.broadcasted_iota(jnp.int32, (C, C), 0)
    col = lax.broadcasted_iota(jnp.int32, (C, C), 1)
    wc = [jnp.where(col <= row, w_ref[gi], 0.0).astype(BF16) for gi in range(SG_GROUPS)]

    def project(j):
        xn = _rms(h_ref[j * SG_SUB:(j + 1) * SG_SUB, :], gpre_ref[...]).astype(BF16)
        return _dot(xn, wu_ref[...]), _dot(xn, wv_ref[...])

    def gate(j, u, v):
        u, v = _gelu(u), _gelu(v)
        mu = jnp.mean(v, axis=-1, keepdims=True)
        vc = v - mu
        vn = (vc * lax.rsqrt(jnp.mean(vc * vc, axis=-1, keepdims=True) + EPS) * g_ref[...] + b_ref[...]).astype(BF16)
        for gi in range(SG_GROUPS):
            sl = slice(gi * dg, (gi + 1) * dg)
            for c in range(SG_SUB // C):
                rows = slice(c * C, (c + 1) * C)
                mixed = _dot(wc[gi], vn[rows, sl]) + bias_ref[:, sl]
                o_ref[j * SG_SUB + c * C:j * SG_SUB + (c + 1) * C, sl] = (u[rows, sl] * mixed).astype(o_ref.dtype)

    nxt = project(0)
    for j in range(n_sub):
        cur = nxt
        if j + 1 < n_sub:
            nxt = project(j + 1)
        gate(j, *cur)


def sgu_block(h3, pre_g, w_u, w_v, ln_g, ln_b, w, b):
    B, S, D = h3.shape
    W, C, T = SG_W, SG_CHUNK, SG_ROWS
    assert S % T == 0 and T % SG_SUB == 0 and SG_SUB % C == 0
    bias = jnp.repeat(b.T, W // SG_GROUPS, axis=1)
    vspec = pl.BlockSpec((1, W), lambda bb, c: (0, 0))
    return pl.pallas_call(
        _sgu_kernel, grid=(B, S // T),
        in_specs=[pl.BlockSpec((None, T, D), lambda bb, c: (bb, c, 0)),
                  pl.BlockSpec((1, D), lambda bb, c: (0, 0)),
                  pl.BlockSpec((D, W), lambda bb, c: (0, 0)), pl.BlockSpec((D, W), lambda bb, c: (0, 0)),
                  vspec, vspec,
                  pl.BlockSpec((SG_GROUPS, C, C), lambda bb, c: (0, 0, 0)),
                  pl.BlockSpec((C, W), lambda bb, c: (0, 0))],
        out_specs=pl.BlockSpec((None, T, W), lambda bb, c: (bb, c, 0)),
        out_shape=jax.ShapeDtypeStruct((B, S, W), BF16),
        compiler_params=_cparams(("parallel", "parallel")), name="sgu",
    )(h3, pre_g.reshape(1, D), w_u, w_v, ln_g.reshape(1, W), ln_b.reshape(1, W), w, bias)


def _mixer_ab(h2, B, S, pre_g, w_in, ml_gate_b, ml_norm_g, nsa_gate_b, k_pe, k_w1, k_w2, v_pe, v_w1, v_w2):
    D = h2.shape[1]
    G, d = NSA_KV_HEADS, HEAD_DIM
    offs = np.cumsum([0, ML_W, ML_W, ML_W, ML_W, 2 * ML_HEADS, NSA_W] + [G * d] * 6 + [3 * NSA_HEADS])
    mq, mk, mv, mo, mif, nq, kc, vc, ks, vs, kw, vw, ng = [w_in[:, offs[i]:offs[i + 1]] for i in range(13)]
    w_b = jnp.concatenate([mk, nq, ks, kw], axis=1).astype(BF16)
    w_t = jnp.concatenate([mq, mv, vs, vw], axis=1).T.astype(BF16)
    gpad = LANES - 2 * ML_HEADS - 3 * NSA_HEADS
    w_f = jnp.concatenate([mo, kc, vc, mif, ng, jnp.zeros((D, gpad), w_in.dtype)], axis=1).astype(BF16)
    gate_bias = jnp.concatenate([ml_gate_b, nsa_gate_b, jnp.zeros((gpad,), F32)]).reshape(1, LANES)
    zb, zf, zt = norm_proj(h2, pre_g, [w_b, w_f], [BF16, F32], wts=[w_t], batch=B)
    zb = zb.reshape(B, S, -1)
    zf = zf.reshape(B, S, -1)
    h_ml = mlstm_block(zt, zb, zf, gate_bias, ml_norm_g)
    kcmp, vcmp_t = compress_block(zf, ML_W // (G * d), ML_W // (G * d) + 1, k_pe, k_w1, k_w2, v_pe, v_w1, v_w2)
    h_nsa = nsa_block(zt, zb, zf, gate_bias, kcmp, vcmp_t, q_col_block=ML_W // NSA_W,
                      k_col_blocks=((ML_W + NSA_W) // (G * d), (ML_W + NSA_W) // (G * d) + 1),
                      v_row_blocks=(2 * ML_W // (G * d), 2 * ML_W // (G * d) + 1))
    return h_ml.reshape(B * S, ML_W), h_nsa.reshape(B * S, NSA_W)


def _mixer_cd(h2, B, S, pre_g, w_in, conv_w, conv_b, wa, ba, wx, bx, lam, sg_g, sg_bn, sg_w, sg_b):
    h3 = h2.reshape(B, S, -1)
    w_gate, w_x, w_u, w_v = (w_in[:, j * LRU_W:(j + 1) * LRU_W].astype(BF16) for j in range(4))
    y_lru = rglru_block(h3, pre_g, w_gate, w_x, conv_w, conv_b, wa, ba, wx, bx, lam)
    y_sg = sgu_block(h3, pre_g, w_u, w_v, sg_g, sg_bn, sg_w, sg_b)
    return y_lru.reshape(B * S, LRU_W), y_sg.reshape(B * S, SG_W)


def kernel(x, pre_mix_g, post_mix_g, pre_ffn_g, post_ffn_g, ab_w_in, ab_w_out, ml_gate_b, ml_norm_g, nsa_gate_b, cmp_k_pe, cmp_k_w1, cmp_k_w2, cmp_v_pe, cmp_v_w1, cmp_v_w2, cd_w_in, cd_w_out, lru_conv_w, lru_conv_b, lru_wa, lru_ba, lru_wx, lru_bx, lru_lambda, sg_norm_g, sg_norm_b, sg_w, sg_b, ffn_w_up, ffn_conv_w, ffn_conv_b, ffn_w_down):
    B, S, D = x.shape
    depth = pre_mix_g.shape[0]
    h2 = x.reshape(B * S, D)
    for layer in range(depth):
        if layer % 2 == 0:
            e = layer // 2
            a1, a2 = _mixer_ab(h2, B, S, pre_mix_g[layer], ab_w_in[e], ml_gate_b[e], ml_norm_g[e], nsa_gate_b[e],
                               cmp_k_pe[e], cmp_k_w1[e], cmp_k_w2[e], cmp_v_pe[e], cmp_v_w1[e], cmp_v_w2[e])
            w_out = ab_w_out[e]
        else:
            o = layer // 2
            a1, a2 = _mixer_cd(h2, B, S, pre_mix_g[layer], cd_w_in[o], lru_conv_w[o], lru_conv_b[o], lru_wa[o],
                               lru_ba[o], lru_wx[o], lru_bx[o], lru_lambda[o], sg_norm_g[o], sg_norm_b[o], sg_w[o],
                               sg_b[o])
            w_out = cd_w_out[o]
        h2 = mix_ffn_block(h2, a1, a2, S, w_out, post_mix_g[layer], pre_ffn_g[layer], ffn_w_up[layer],
                           ffn_conv_w[layer], ffn_conv_b[layer], ffn_w_down[layer], post_ffn_g[layer])
    return h2.reshape(B, S, D)
```

```python
import functools

import numpy as np
import jax
import jax.numpy as jnp
from jax import lax
from jax.experimental import pallas as pl
from jax.experimental.pallas import tpu as pltpu

F32 = jnp.float32
BF16 = jnp.bfloat16

EPS = 1e-6
HEAD_DIM = 64
ML_HEADS = 8
ML_W = 512
GATE_SOFTCAP = 15.0
NSA_HEADS = 8
NSA_KV_HEADS = 2
NSA_HPG = NSA_HEADS // NSA_KV_HEADS
NSA_W = 512
CMP_LEN = 32
CMP_STRIDE = 16
CMP_HID = 128
SEL_LEN = 64
SEL_TOPN = 16
WINDOW = 512
LRU_W = 512
LRU_C = 8.0
LRU_CONV = 4
SG_GROUPS = 8
SG_W = 512
SG_CHUNK = 128
FFN_CONV = 3
GATE_COL_BLOCK = (ML_W + 2 * NSA_KV_HEADS * HEAD_DIM) // 128
NSA_GATE_LANE0 = 2 * ML_HEADS

LANES = 128
VMEM_LIMIT = 56 * 1024 * 1024
NEG_BIG = -1e30
HIGHEST = lax.Precision.HIGHEST
LOG2E = 1.4426950408889634


def _bf16_terms(x, n):
    terms = []
    for _ in range(n):
        bits = int(np.array(x, np.float32).view(np.uint32))
        t = float(np.array((bits + 0x7FFF + ((bits >> 16) & 1)) & 0xFFFF0000, np.uint32).view(np.float32))
        terms.append(t)
        x -= t
    return tuple(terms)


LOG2E_TERMS = _bf16_terms(LOG2E, 3)

ML_CHUNK = 128
ML_NB = 8
NSA_TQ = 256
NSA_TK = 256
RANK_STEP = 16
NSA_ONES = 16
ROW_TILE = 512
FFN_TM = 512
FFN_SUB = 256
FFN_CK = 256
FFN_HALO = 16
SG_ROWS = 1024
SG_SUB = 256
LRU_T = 512
LRU_HALO = 8
LRU_NB = 4


def _cparams(sem):
    return pltpu.CompilerParams(dimension_semantics=sem, vmem_limit_bytes=VMEM_LIMIT)


def _rms(x, g):
    return x * lax.rsqrt(jnp.mean(x * x, axis=-1, keepdims=True) + EPS) * g


def _gelu(x):
    return 0.5 * x * (1.0 + jnp.tanh(0.7978845608028654 * (x + 0.044715 * (x * x * x))))


def _sigmoid(x):
    return 1.0 / (1.0 + jnp.exp(-x))


def _dot(a, b):
    return jnp.dot(a, b, preferred_element_type=F32)


def _dot_nt(a, b, precision=None):
    return lax.dot_general(a, b, (((1,), (1,)), ((), ())), preferred_element_type=F32, precision=precision)


def _norm_proj_kernel(h_ref, g_ref, *refs, n_row, n_t, cn):
    w_refs, wt_refs = refs[:n_row], refs[n_row:n_row + n_t]
    o_refs, ot_refs = refs[n_row + n_t:2 * n_row + n_t], refs[2 * n_row + n_t:]
    xn = _rms(h_ref[...], g_ref[...]).astype(BF16)
    for w_ref, o_ref in zip(w_refs, o_refs):
        n = w_ref.shape[1]
        for c in range(0, n, cn):
            ce = min(c + cn, n)
            o_ref[:, c:ce] = _dot(xn, w_ref[:, c:ce]).astype(o_ref.dtype)
    for wt_ref, ot_ref in zip(wt_refs, ot_refs):
        n = wt_ref.shape[0]
        for c in range(0, n, cn):
            ce = min(c + cn, n)
            ot_ref[c:ce, :] = _dot_nt(wt_ref[c:ce, :], xn).astype(ot_ref.dtype)


def norm_proj(h2, g, ws, dtypes, wts=(), batch=1, tm=ROW_TILE):
    M, D = h2.shape
    tps = M // batch // tm
    in_specs = [pl.BlockSpec((tm, D), lambda i: (i, 0)), pl.BlockSpec((1, D), lambda i: (0, 0))]
    in_specs += [pl.BlockSpec(w.shape, lambda i: (0, 0)) for w in (*ws, *wts)]
    out_specs = [pl.BlockSpec((tm, w.shape[1]), lambda i: (i, 0)) for w in ws]
    out_specs += [pl.BlockSpec((None, w.shape[0], tm), lambda i: (i // tps, 0, i % tps)) for w in wts]
    out_shape = [jax.ShapeDtypeStruct((M, w.shape[1]), dt) for w, dt in zip(ws, dtypes)]
    out_shape += [jax.ShapeDtypeStruct((batch, w.shape[0], M // batch), BF16) for w in wts]
    return pl.pallas_call(
        functools.partial(_norm_proj_kernel, n_row=len(ws), n_t=len(wts), cn=512),
        grid=(M // tm,), in_specs=in_specs, out_specs=out_specs, out_shape=out_shape,
        compiler_params=_cparams(("parallel",)), name="norm_proj",
    )(h2, g.reshape(1, D), *ws, *wts)


def _mix_ffn_kernel(h_ref, a1_ref, a2_ref, wo1_ref, wo2_ref, gmix_ref, gpre_ref, wu_ref, cw_ref, cb_ref, wd_ref,
                    gpost_ref, o_ref, xn_s, acc_s, ut_s, *, tiles_per_seq, ck):
    F = wd_ref.shape[0]
    n_chunks = F // ck
    n_sub = h_ref.shape[0] // FFN_SUB
    g = gpre_ref[...]

    @pl.when(pl.program_id(0) % tiles_per_seq == 0)
    def _():
        ut_s[...] = jnp.zeros_like(ut_s)

    def mixed(h, a1, a2):
        return h + _rms(_dot(a1, wo1_ref[...]) + _dot(a2, wo2_ref[...]), gmix_ref[...])

    def conv(u, cols):
        y = (cw_ref[2:3, cols] * u + cw_ref[1:2, cols] * pltpu.roll(u, 1, 0) + cw_ref[0:1, cols] * pltpu.roll(u, 2, 0)
             + cb_ref[:, cols])
        return y[FFN_HALO:, :]

    def cols_of(c, half):
        return slice(half * F + c * ck, half * F + (c + 1) * ck)

    x = {}

    def head(j):
        rows = slice(j * FFN_SUB, (j + 1) * FFN_SUB)
        x[j] = mixed(h_ref[rows, :], a1_ref[rows, :], a2_ref[rows, :])
        xn_s[rows, :] = _rms(x[j], g).astype(BF16)
        acc_s[rows, :] = jnp.zeros((FFN_SUB, acc_s.shape[1]), F32)

    def up(j, c):
        xn = xn_s[j * FFN_SUB:(j + 1) * FFN_SUB, :]
        u = [jnp.concatenate([ut_s[c, half], _dot(xn, wu_ref[:, cols_of(c, half)])], axis=0) for half in range(2)]
        for half in range(2):
            ut_s[c, half] = u[half][FFN_SUB:, :]
        return u

    def tail(j):
        rows = slice(j * FFN_SUB, (j + 1) * FFN_SUB)
        o_ref[rows, :] = x[j] + _rms(acc_s[rows, :], gpost_ref[...])

    head(0)
    for j in range(n_sub):
        u = up(j, 0)
        for c in range(n_chunks):
            u_next = up(j, c + 1) if c + 1 < n_chunks else None
            if c == 0 and j + 1 < n_sub:
                head(j + 1)
            if c == n_chunks // 3 and j >= 1:
                tail(j - 1)
            act = (_gelu(conv(u[0], cols_of(c, 0))) * conv(u[1], cols_of(c, 1))).astype(BF16)
            acc_s[j * FFN_SUB:(j + 1) * FFN_SUB, :] += _dot(act, wd_ref[c * ck:(c + 1) * ck, :])
            u = u_next
    tail(n_sub - 1)


def mix_ffn_block(h2, a1, a2, seq_len, w_out, g_mix, g_pre, w_up, conv_w, conv_b, w_down, g_post, tm=FFN_TM, ck=FFN_CK):
    M, D = h2.shape
    F = w_down.shape[0]
    K1, K2 = a1.shape[1], a2.shape[1]
    assert F % ck == 0 and seq_len % tm == 0 and tm % FFN_SUB == 0 and FFN_SUB % FFN_HALO == 0
    assert conv_w.shape[0] == FFN_CONV and FFN_CONV - 1 <= FFN_HALO
    consts = (w_out[:K1].astype(BF16), w_out[K1:].astype(BF16), g_mix.reshape(1, D), g_pre.reshape(1, D),
              w_up.astype(BF16), conv_w, conv_b.reshape(1, 2 * F), w_down.astype(BF16), g_post.reshape(1, D))

    def tile(width):
        return pl.BlockSpec((tm, width), lambda i: (i, 0))

    def full(a):
        return pl.BlockSpec(a.shape, lambda i: (0,) * a.ndim)

    return pl.pallas_call(
        functools.partial(_mix_ffn_kernel, tiles_per_seq=seq_len // tm, ck=ck),
        grid=(M // tm,),
        in_specs=[tile(D), tile(K1), tile(K2)] + [full(a) for a in consts],
        out_specs=tile(D),
        out_shape=jax.ShapeDtypeStruct((M, D), F32),
        scratch_shapes=[pltpu.VMEM((tm, D), BF16), pltpu.VMEM((tm, D), F32),
                        pltpu.VMEM((F // ck, 2, FFN_HALO, ck), F32)],
        compiler_params=_cparams(("arbitrary",)), name="mix_ffn",
    )(h2, a1, a2, *consts)


def _mlstm_kernel(qt_ref, vt_ref, k_ref, mo_ref, gt_ref, gb_ref, ng_ref, o_ref, cn_s, m_s):
    NB, _, L = qt_ref.shape
    d = HEAD_DIM

    @pl.when(pl.program_id(1) == 0)
    def _():
        cn_s[...] = jnp.zeros_like(cn_s)
        m_s[...] = jnp.zeros_like(m_s)

    src = lax.broadcasted_iota(jnp.int32, (L, L), 0)
    tgt = lax.broadcasted_iota(jnp.int32, (L, L), 1)
    causal = src <= tgt
    tri = (tgt <= src).astype(F32)

    gcap = [GATE_SOFTCAP * jnp.tanh((gt_ref[nb] + gb_ref[...]) * (1.0 / GATE_SOFTCAP)) for nb in range(NB)]
    lf = [jnp.minimum(x, 0.0) - jnp.log1p(jnp.exp(-jnp.abs(x))) for x in gcap]
    b_col = [jnp.dot(tri, x, preferred_element_type=F32, precision=HIGHEST) for x in lf]
    b_row = [x.T for x in b_col]
    i_row = [x.T for x in gcap]
    c_col = [b_col[nb] - pltpu.roll(gcap[nb], ML_HEADS, 1) for nb in range(NB)]

    chains = [(nb, h) for nb in range(NB) for h in range(ML_HEADS)]
    n = range(len(chains))

    def rows(h):
        return slice(h * d, (h + 1) * d)

    q_t = [qt_ref[nb, rows(h), :] for nb, h in chains]
    v_t = [vt_ref[nb, rows(h), :] for nb, h in chains]
    k = [k_ref[nb, :, rows(h)] * 0.125 for nb, h in chains]
    br = [b_row[nb][ML_HEADS + h:ML_HEADS + h + 1, :] for nb, h in chains]
    ir = [i_row[nb][h:h + 1, :] for nb, h in chains]
    g = [x[:, L - 1:L] for x in br]
    m_prev = [m_s[nb, h:h + 1, 0:1] for nb, h in chains]
    cn_prev = [cn_s[nb, h] for nb, h in chains]
    kq = [_dot(k[c], q_t[c]) for c in n]
    carry = [_dot(cn_prev[c].astype(BF16), q_t[c]) for c in n]
    dlog = [jnp.where(causal, br[c] - c_col[nb][:, ML_HEADS + h:ML_HEADS + h + 1], -jnp.inf)
            for c, (nb, h) in enumerate(chains)]
    inter = [br[c] + m_prev[c] for c in n]
    m_row = [jnp.maximum(inter[c], jnp.max(dlog[c], axis=0, keepdims=True)) for c in n]
    s = [kq[c] * jnp.exp(dlog[c] - m_row[c]) for c in n]
    w_inter = [jnp.exp(inter[c] - m_row[c]) for c in n]
    num = [_dot(v_t[c], s[c].astype(BF16)) + w_inter[c] * carry[c][0:d] for c in n]
    den = [jnp.sum(s[c], axis=0, keepdims=True) + w_inter[c] * carry[c][d:d + 1] for c in n]
    hh = [num[c] * (1.0 / jnp.maximum(jnp.abs(den[c]), jnp.exp(-m_row[c]))) for c in n]
    wlog = [g[c] - br[c] + ir[c] for c in n]
    m_new = [jnp.maximum(g[c] + m_prev[c], jnp.max(wlog[c], axis=-1, keepdims=True)) for c in n]
    w_row = [jnp.exp(wlog[c] - m_new[c]) for c in n]
    decay = [jnp.exp(g[c] + m_prev[c] - m_new[c]) for c in n]
    for c, (nb, h) in enumerate(chains):
        vw = jnp.concatenate([v_t[c].astype(F32) * w_row[c], jnp.broadcast_to(w_row[c], (8, L))], axis=0).astype(BF16)
        cn_s[nb, h] = decay[c] * cn_prev[c] + _dot(vw, k[c])
        m_s[nb, h:h + 1, :] = jnp.broadcast_to(m_new[c], (1, m_s.shape[2]))
    outs = [hh[c] * lax.rsqrt(jnp.mean(hh[c] * hh[c], axis=0, keepdims=True) + EPS) * ng_ref[rows(h), :]
            for c, (nb, h) in enumerate(chains)]
    for nb in range(NB):
        out_t = jnp.concatenate(outs[nb * ML_HEADS:(nb + 1) * ML_HEADS], axis=0)
        o_ref[nb] = (out_t.T * _sigmoid(mo_ref[nb])).astype(o_ref.dtype)


def mlstm_block(zt, zb, zf, gate_bias, norm_g, L=ML_CHUNK, NB=ML_NB):
    B, S, _ = zb.shape
    W, H, d = ML_W, ML_HEADS, HEAD_DIM
    assert B % NB == 0 and S % L == 0
    ng = jnp.broadcast_to(norm_g.reshape(W, 1), (W, L))
    return pl.pallas_call(
        _mlstm_kernel, grid=(B // NB, S // L),
        in_specs=[pl.BlockSpec((NB, W, L), lambda b, c: (b, 0, c)),
                  pl.BlockSpec((NB, W, L), lambda b, c: (b, 1, c)),
                  pl.BlockSpec((NB, L, W), lambda b, c: (b, c, 0)),
                  pl.BlockSpec((NB, L, W), lambda b, c: (b, c, 0)),
                  pl.BlockSpec((NB, L, LANES), lambda b, c: (b, c, GATE_COL_BLOCK)),
                  pl.BlockSpec((1, LANES), lambda b, c: (0, 0)),
                  pl.BlockSpec((W, L), lambda b, c: (0, 0))],
        out_specs=pl.BlockSpec((NB, L, W), lambda b, c: (b, c, 0)),
        out_shape=jax.ShapeDtypeStruct((B, S, W), BF16),
        scratch_shapes=[pltpu.VMEM((NB, H, d + 8, d), F32), pltpu.VMEM((NB, H, LANES), F32)],
        compiler_params=_cparams(("parallel", "arbitrary")), name="mlstm",
    )(zt, zt, zb, zf, zf, gate_bias, ng)


def _compress_kernel(kc_ref, vc_ref, pek_ref, pev_ref, w1k_ref, w1v_ref, w2k_ref, w2v_ref, ok_ref, ov_ref):
    G = NSA_KV_HEADS
    nh = kc_ref.shape[0] // CMP_STRIDE

    def one(x_ref, pe_ref, w1_ref, w2_ref):
        hid = None
        for l in range(CMP_STRIDE):
            y = x_ref[pl.ds(l, nh, stride=CMP_STRIDE), :]
            ya = (y + pe_ref[l:l + 1, :]).astype(BF16)
            yb = (pltpu.roll(y, nh - 1, 0) + pe_ref[CMP_STRIDE + l:CMP_STRIDE + l + 1, :]).astype(BF16)
            t = _dot(ya, w1_ref[l]) + _dot(yb, w1_ref[CMP_STRIDE + l])
            hid = t if hid is None else hid + t
        hid = _gelu(hid).astype(BF16)
        return [_dot(hid[:, gi * CMP_HID:(gi + 1) * CMP_HID], w2_ref[...]) for gi in range(G)]

    for gi, (ko, vo) in enumerate(zip(one(kc_ref, pek_ref, w1k_ref, w2k_ref), one(vc_ref, pev_ref, w1v_ref, w2v_ref))):
        ok_ref[gi] = ko.astype(ok_ref.dtype)
        ov_ref[gi] = vo.T.astype(ov_ref.dtype)


def compress_block(zf, kc_block, vc_block, k_pe, k_w1, k_w2, v_pe, v_w1, v_w2):
    B, S, _ = zf.shape
    G, d = NSA_KV_HEADS, HEAD_DIM
    nh = S // CMP_STRIDE

    def prep(pe, w1):
        w1bd = jnp.zeros((CMP_LEN, G * d, G * CMP_HID), w1.dtype)
        for gi in range(G):
            w1bd = w1bd.at[:, gi * d:(gi + 1) * d, gi * CMP_HID:(gi + 1) * CMP_HID].set(w1)
        return jnp.tile(pe, (1, G)), w1bd.astype(BF16)

    pek, w1k = prep(k_pe, k_w1)
    pev, w1v = prep(v_pe, v_w1)

    def full(a):
        return pl.BlockSpec(a.shape, lambda b: (0,) * a.ndim)

    w2k, w2v = k_w2.astype(BF16), v_w2.astype(BF16)
    return pl.pallas_call(
        _compress_kernel, grid=(B,),
        in_specs=[pl.BlockSpec((None, S, G * d), lambda b: (b, 0, kc_block)),
                  pl.BlockSpec((None, S, G * d), lambda b: (b, 0, vc_block)),
                  full(pek), full(pev), full(w1k), full(w1v), full(w2k), full(w2v)],
        out_specs=[pl.BlockSpec((None, G, nh, d), lambda b: (b, 0, 0, 0)),
                   pl.BlockSpec((None, G, d, nh), lambda b: (b, 0, 0, 0))],
        out_shape=[jax.ShapeDtypeStruct((B, G, nh, d), BF16), jax.ShapeDtypeStruct((B, G, d, nh), BF16)],
        compiler_params=_cparams(("parallel",)), name="nsa_compress",
    )(zf, zf, pek, pev, w1k, w1v, w2k, w2v)


def _nsa_kernel(q_ref, gt_ref, gb_ref, kc_ref, vct_ref, ks_ref, vst_ref, kw_ref, vwt_ref, ovl_ref, aug_ref, caug_ref,
                tail_ref, ctail_ref, wmask_ref, o_ref, ksa_s, kwa_s, kca_s, vsa_s, vwa_s, ss_s, ps_s, sw_s, pw_s, *, n_cmp):
    TQ, TK, d, HPG, G = NSA_TQ, NSA_TK, HEAD_DIM, NSA_HPG, NSA_KV_HEADS
    R = HPG * TQ
    groups = range(G)
    qi = pl.program_id(1)
    q0 = qi * TQ
    kt_d = q0 // TK
    n_cb = kc_ref.shape[1]
    n_sb = ovl_ref.shape[0]

    @pl.when(qi == 0)
    def _():
        for g in groups:
            ksa_s[g, :, 0:d] = ks_ref[:, g * d:(g + 1) * d]
            ksa_s[g, :, d:] = aug_ref[...]
            kwa_s[g, :, 0:d] = kw_ref[:, g * d:(g + 1) * d]
            kwa_s[g, :, d:] = aug_ref[...]
            kca_s[g, :, 0:d] = kc_ref[g]
            kca_s[g, :, d:] = caug_ref[...]
            vsa_s[g, 0:d, :] = vst_ref[g * d:(g + 1) * d, :]
            vsa_s[g, d:, :] = jnp.ones((vsa_s.shape[1] - d, vsa_s.shape[2]), BF16)
            vwa_s[g, 0:d, :] = vwt_ref[g * d:(g + 1) * d, :]
            vwa_s[g, d:, :] = jnp.ones((vwa_s.shape[1] - d, vwa_s.shape[2]), BF16)

    def slope(g, hh):
        return 2.0 ** (-(g * HPG + hh + 1))

    def per_head(fn):
        return jnp.concatenate([fn(hh) for hh in range(HPG)], axis=1)

    def tile_heads(x):
        return jnp.concatenate([x] * HPG, axis=1)

    q_t = [(q_ref[:, g * HPG * d:(g + 1) * HPG * d].astype(F32) * (LOG2E * 0.125)).T for g in groups]

    def q_head(g, hh):
        return q_t[g][hh * d:(hh + 1) * d]

    def q_aug(g, block_rows):
        return per_head(lambda hh: jnp.concatenate(
            [q_head(g, hh), block_rows, tail_ref[...] * slope(g, hh)], axis=0)).astype(BF16)

    def scores(ka, qa_t, kt):
        return _dot(ka[pl.ds(pl.multiple_of(kt * TK, TK), TK), :], qa_t)

    def stage_scores(buf, ka, qa_t, kt, mask_add=None):
        s = scores(ka, qa_t, kt)
        if mask_add is not None:
            s = s + mask_add
        buf[...] = s
        return jnp.max(s, axis=0, keepdims=True)

    def stage_probs(sbuf, pbuf, tile_max, m):
        m_new = jnp.maximum(m, tile_max)
        pbuf[...] = jnp.exp2(sbuf[...] - m_new).astype(BF16)
        return m_new, jnp.exp2(m - m_new)

    def stage_values(pbuf, va, kt, alpha, acc):
        return alpha * acc + _dot(va[:, pl.ds(pl.multiple_of(kt * TK, TK), TK)], pbuf[...])

    def normalised(acc):
        return acc[0:d] * (1.0 / acc[d:d + 1])

    m0, acc0 = jnp.full((1, R), NEG_BIG, F32), jnp.zeros((vsa_s.shape[1], R), F32)

    qc_t = [per_head(lambda hh: jnp.concatenate([q_head(g, hh), ctail_ref[...] * slope(g, hh)], axis=0)).astype(BF16)
            for g in groups]
    qw_t = [q_aug(g, jnp.zeros((n_sb, TQ), F32)) for g in groups]
    n_r = lax.broadcasted_iota(jnp.int32, (n_cb, TQ), 0)
    t_c = q0 + lax.broadcasted_iota(jnp.int32, (n_cb, TQ), 1)
    ok_c = (n_r * CMP_STRIDE + (CMP_LEN - 1) <= t_c) & (n_r < n_cmp)
    add_c = tile_heads(jnp.where(ok_c, 0.0, NEG_BIG))
    s_c = [_dot(kca_s[g], qc_t[g]) + add_c for g in groups]
    n_win = (WINDOW - 1 + TK - 1) // TK + 1
    kt_win, max_win = [], []
    for back in range(n_win):
        kt_raw = kt_d - back
        kt_win.append(jnp.maximum(kt_raw, 0))
        if back == 0:
            mask_add = tile_heads(wmask_ref[0])
        elif (back + 1) * TK <= WINDOW:
            mask_add = jnp.where(kt_raw >= 0, 0.0, NEG_BIG)
        else:
            mask_add = tile_heads(wmask_ref[back] + jnp.where(kt_raw >= 0, 0.0, NEG_BIG))
        max_win.append([stage_scores(sw_s.at[g, back], kwa_s.at[g], qw_t[g], kt_win[back], mask_add) for g in groups])

    e_c = [jnp.exp2(s_c[g] - jnp.maximum(jnp.max(s_c[g], axis=0, keepdims=True), 0.1 * NEG_BIG)) for g in groups]
    inv_c = [1.0 / jnp.maximum(jnp.sum(e_c[g], axis=0, keepdims=True), 1.0) for g in groups]
    o_c = [_dot(vct_ref[g], e_c[g].astype(BF16)) * inv_c[g] for g in groups]

    jb = lax.broadcasted_iota(jnp.int32, (n_sb, TQ), 0)
    cur = (q0 + lax.broadcasted_iota(jnp.int32, (n_sb, TQ), 1)) // SEL_LEN
    valid = jb <= cur
    forced = (jb == 0) | (jb == cur) | (jb == cur - 1)
    sub = lax.broadcasted_iota(jnp.int32, (8, TQ), 0)
    score = []
    for g in groups:
        p_sum = e_c[g][:, 0:TQ] * inv_c[g][:, 0:TQ]
        for hh in range(1, HPG):
            p_sum = p_sum + e_c[g][:, hh * TQ:(hh + 1) * TQ] * inv_c[g][:, hh * TQ:(hh + 1) * TQ]
        imp = jnp.dot(ovl_ref[...], p_sum, preferred_element_type=F32, precision=HIGHEST)
        score.append(jnp.where(forced, jnp.inf, jnp.where(valid, imp, -jnp.inf)))

    def ranked(n_rows):
        def fn(*score):
            n_grp = n_rows // 8
            outs = []
            grp = [[s[8 * a:8 * a + 8] for a in range(n_grp)] for s in score]
            rank = [[jnp.zeros((8, TQ), jnp.int32) for _ in range(n_grp)] for _ in score]
            for j in range(n_rows):
                for g in groups:
                    r = score[g][j:j + 1, :]
                    for a in range(n_grp):
                        if a > j // 8:
                            ahead = (r >= grp[g][a]).astype(jnp.int32)
                        elif a < j // 8:
                            ahead = (r > grp[g][a]).astype(jnp.int32)
                        else:
                            ahead = jnp.where(sub > j % 8, (r >= grp[g][a]).astype(jnp.int32),
                                              (r > grp[g][a]).astype(jnp.int32))
                        rank[g][a] = rank[g][a] + ahead
            for g in groups:
                top = jnp.where(jnp.concatenate(rank[g], axis=0) < SEL_TOPN, 1.0, 0.0)
                outs.append(jnp.concatenate([top, jnp.zeros((n_sb - n_rows, TQ), F32)], axis=0) if n_rows < n_sb else top)
            return tuple(outs)
        return fn

    sizes = [n for n in range(RANK_STEP, n_sb + 1, RANK_STEP)]
    rows_needed = (q0 + TQ + SEL_LEN - 1) // SEL_LEN
    in_top = lax.switch((rows_needed + RANK_STEP - 1) // RANK_STEP - 1, [ranked(n) for n in sizes], *score)
    picked = [valid & (in_top[g] > 0.5) for g in groups]
    qs_t = [q_aug(g, jnp.where(picked[g], 0.0, NEG_BIG)) for g in groups]

    filler = 1 - kt_d % 2
    n_seq = kt_d + 1 + filler

    def sel_tile(i):
        return jnp.where(i == 0, kt_d, jnp.maximum(i - 1 - filler, 0))

    def a_stage(slot, kt, mask_add=None):
        return [stage_scores(ss_s.at[g, slot], ksa_s.at[g], qs_t[g], kt, mask_add) for g in groups]

    def b_stage(slot, tmax, m):
        out = [stage_probs(ss_s.at[g, slot], ps_s.at[g, slot], tmax[g], m[g]) for g in groups]
        return [o[0] for o in out], [o[1] for o in out]

    def c_stage(slot, kt, alpha, acc):
        return [stage_values(ps_s.at[g, slot], vsa_s.at[g], kt, alpha[g], acc[g]) for g in groups]

    tmax0 = a_stage(0, kt_d, tile_heads(wmask_ref[0]))
    m, alpha = b_stage(0, tmax0, [m0] * G)
    tmax1 = a_stage(1, sel_tile(1), jnp.where(filler == 1, NEG_BIG, 0.0))

    m_w = [functools.reduce(jnp.maximum, [max_win[back][g] for back in range(n_win)]) for g in groups]
    accw = [None] * G
    for back in range(n_win):
        for g in groups:
            pw_s[g, back] = jnp.exp2(sw_s[g, back] - m_w[g]).astype(BF16)
        for g in groups:
            pv = _dot(vwa_s[g, :, pl.ds(pl.multiple_of(kt_win[back] * TK, TK), TK)], pw_s[g, back])
            accw[g] = pv if accw[g] is None else accw[g] + pv
    o_w = [normalised(accw[g]) for g in groups]

    def sel_body(k, carry):
        m, alpha, acc, tmax1 = carry
        i = 2 * k
        tmax0 = a_stage(0, sel_tile(i + 2))
        m, alpha1 = b_stage(1, tmax1, m)
        tmax1 = a_stage(1, sel_tile(i + 3))
        acc = c_stage(0, sel_tile(i), alpha, acc)
        m, alpha2 = b_stage(0, tmax0, m)
        acc = c_stage(1, sel_tile(i + 1), alpha1, acc)
        return m, alpha2, acc, tmax1

    m, alpha, acc, tmax1 = lax.fori_loop(0, (n_seq - 2) // 2, sel_body, (m, alpha, [acc0] * G, tmax1))
    m, alpha1 = b_stage(1, tmax1, m)
    acc = c_stage(0, sel_tile(n_seq - 2), alpha, acc)
    acc = c_stage(1, sel_tile(n_seq - 1), alpha1, acc)
    o_s = [normalised(acc[g]) for g in groups]

    gates = _sigmoid(gt_ref[...] + gb_ref[...]).T
    outs = []
    for g in groups:
        for hh in range(HPG):
            cols = slice(hh * TQ, (hh + 1) * TQ)
            row = NSA_GATE_LANE0 + 3 * (g * HPG + hh)
            outs.append(gates[row:row + 1] * o_c[g][:, cols] + gates[row + 1:row + 2] * o_s[g][:, cols]
                        + gates[row + 2:row + 3] * o_w[g][:, cols])
    o_ref[...] = jnp.concatenate(outs, axis=0).T.astype(o_ref.dtype)


def nsa_block(zt, zb, zf, gate_bias, kcmp, vcmp_t, q_col_block, k_col_blocks, v_row_blocks):
    B, S, _ = zb.shape
    G, d = NSA_KV_HEADS, HEAD_DIM
    TQ = NSA_TQ
    n_cb = kcmp.shape[2]
    n_cmp = (S - CMP_LEN) // CMP_STRIDE + 1
    n_sb = S // SEL_LEN
    n_terms = len(LOG2E_TERMS)
    kaug = -(-(d + n_sb + 2 * n_terms) // LANES) * LANES
    R = NSA_HPG * TQ
    n_win = (WINDOW - 1 + NSA_TK - 1) // NSA_TK + 1
    assert S % NSA_TK == 0 and NSA_TK == TQ and TQ % SEL_LEN == 0

    cidx = np.arange(n_cb)[None, :] * CMP_STRIDE
    sstart = np.arange(n_sb)[:, None] * SEL_LEN
    ovl = ((cidx < sstart + SEL_LEN) & (cidx + CMP_LEN - 1 >= sstart) & (np.arange(n_cb)[None, :] < n_cmp))
    ovl = jnp.asarray(ovl.astype(np.float32))
    pos = np.arange(S)
    aug = np.zeros((S, kaug - d), np.float32)
    aug[pos, pos // SEL_LEN] = 1.0
    tail = np.zeros((kaug - d - n_sb, TQ), np.float32)
    caug = np.zeros((n_cb, d), np.float32)
    ctail = np.zeros((d, TQ), np.float32)
    for i, term in enumerate(LOG2E_TERMS):
        aug[:, n_sb + i] = pos // SEL_LEN
        aug[:, n_sb + n_terms + i] = pos % SEL_LEN
        tail[i] = term * SEL_LEN
        tail[n_terms + i] = term
        caug[:, i] = np.arange(n_cb)
        ctail[i] = term * CMP_STRIDE
    aug, caug, tail, ctail = jnp.asarray(aug, BF16), jnp.asarray(caug, BF16), jnp.asarray(tail), jnp.asarray(ctail)
    dist = np.arange(n_win)[:, None, None] * NSA_TK + np.arange(TQ)[None, None, :] - np.arange(NSA_TK)[None, :, None]
    wmask = jnp.asarray(np.where((dist >= 0) & (dist < WINDOW), 0.0, NEG_BIG).astype(np.float32))

    def kspec(j):
        return pl.BlockSpec((None, S, G * d), lambda b, i: (b, 0, k_col_blocks[j]))

    def vspec(j):
        return pl.BlockSpec((None, G * d, S), lambda b, i: (b, v_row_blocks[j], 0))

    def const(a):
        return pl.BlockSpec(a.shape, lambda b, i: (0,) * a.ndim)

    W = G * NSA_HPG * d
    return pl.pallas_call(
        functools.partial(_nsa_kernel, n_cmp=n_cmp), grid=(B, S // TQ),
        in_specs=[pl.BlockSpec((None, TQ, W), lambda b, i: (b, i, q_col_block)),
                  pl.BlockSpec((None, TQ, LANES), lambda b, i: (b, i, GATE_COL_BLOCK)),
                  const(gate_bias),
                  pl.BlockSpec((None, G, n_cb, d), lambda b, i: (b, 0, 0, 0)),
                  pl.BlockSpec((None, G, d, n_cb), lambda b, i: (b, 0, 0, 0)),
                  kspec(0), vspec(0), kspec(1), vspec(1), const(ovl), const(aug), const(caug), const(tail),
                  const(ctail), const(wmask)],
        out_specs=pl.BlockSpec((None, TQ, W), lambda b, i: (b, i, 0)),
        out_shape=jax.ShapeDtypeStruct((B, S, NSA_W), BF16),
        scratch_shapes=[pltpu.VMEM((G, S, kaug), BF16), pltpu.VMEM((G, S, kaug), BF16),
                        pltpu.VMEM((G, n_cb, 2 * d), BF16),
                        pltpu.VMEM((G, d + NSA_ONES, S), BF16), pltpu.VMEM((G, d + NSA_ONES, S), BF16),
                        pltpu.VMEM((G, 2, NSA_TK, R), F32), pltpu.VMEM((G, 2, NSA_TK, R), BF16),
                        pltpu.VMEM((G, n_win, NSA_TK, R), F32), pltpu.VMEM((G, n_win, NSA_TK, R), BF16)],
        compiler_params=_cparams(("parallel", "arbitrary")), name="nsa",
    )(zb, zf, gate_bias, kcmp, vcmp_t, zb, zt, zb, zt, ovl, aug, caug, tail, ctail, wmask)


def _rglru_kernel(h_ref, hh_ref, g_ref, wg_ref, wi_ref, cw_ref, cb_ref, wa_ref, wx_ref, ba_ref, bx_ref, lam_ref,
                  o_ref, h_s, au_s):
    NB, T, W = o_ref.shape
    first = pl.program_id(1) == 0

    @pl.when(first)
    def _():
        h_s[...] = jnp.zeros_like(h_s)

    def project(nb):
        xn = _rms(h_ref[nb], g_ref[...]).astype(BF16)
        x_halo = _dot(_rms(hh_ref[nb], g_ref[...]).astype(BF16), wi_ref[...])
        xe = jnp.concatenate([jnp.where(first, 0.0, x_halo), _dot(xn, wi_ref[...])], axis=0)
        return _dot(xn, wg_ref[...]), xe

    def recur(nb, gate, xe):
        xc = cb_ref[...] + cw_ref[3:4, :] * xe
        for k in range(1, LRU_CONV):
            xc = xc + cw_ref[3 - k:4 - k, :] * pltpu.roll(xe, k, 0)
        xc = xc[LRU_HALO:, :]
        xcb = xc.astype(BF16)
        half = W // 2

        def blockdiag(w_ref):
            return jnp.concatenate([_dot(xcb[:, :half], w_ref[0]), _dot(xcb[:, half:], w_ref[1])], axis=1)

        r = _sigmoid(blockdiag(wa_ref) + ba_ref[...])
        i = _sigmoid(blockdiag(wx_ref) + bx_ref[...])
        nl = -lam_ref[...]
        softplus = jnp.maximum(nl, 0.0) + jnp.log1p(jnp.exp(-jnp.abs(nl)))
        log_a = -LRU_C * r * softplus
        a = jnp.exp(log_a)
        one_m_a2 = -jnp.tanh(log_a) * (a * a + 1.0)
        u = jnp.where(one_m_a2 > 0.0, one_m_a2 * lax.rsqrt(one_m_a2), 0.0) * (i * xc)

        n_grp, n_slab = T // 8, W // LANES

        def phases(x, slab):
            for c in range(n_slab):
                au_s[nb, slab, c] = x[:, c * LANES:(c + 1) * LANES]
            return [jnp.concatenate([au_s[nb, slab, c, pl.ds(j, n_grp, stride=8), :] for c in range(n_slab)], axis=1)
                    for j in range(8)]

        a_ph, u_ph = phases(a, 0), phases(u, 1)
        prod, part = [a_ph[0]], [u_ph[0]]
        for j in range(1, 8):
            part.append(a_ph[j] * part[-1] + u_ph[j])
            prod.append(a_ph[j] * prod[-1])
        grp = lax.broadcasted_iota(jnp.int32, (n_grp, W), 0)
        ag, ug = prod[7], part[7]
        sft = 1
        while sft < n_grp:
            keep = grp >= sft
            ug = ag * jnp.where(keep, pltpu.roll(ug, sft, 0), 0.0) + ug
            ag = ag * jnp.where(keep, pltpu.roll(ag, sft, 0), 1.0)
            sft *= 2
        h_prev = h_s[nb, 0:1, :]
        hg = ug + ag * h_prev
        carry_in = jnp.where(grp >= 1, pltpu.roll(hg, 1, 0), h_prev)
        for j in range(8):
            hj = part[j] + prod[j] * carry_in
            for c in range(n_slab):
                au_s[nb, 0, c, pl.ds(j, n_grp, stride=8), :] = hj[:, c * LANES:(c + 1) * LANES]
        h = jnp.concatenate([au_s[nb, 0, c] for c in range(n_slab)], axis=1)
        h_s[nb] = jnp.broadcast_to(hg[n_grp - 1:n_grp, :], h_s.shape[1:])
        o_ref[nb] = (_gelu(gate) * h).astype(o_ref.dtype)

    nxt = project(0)
    for nb in range(NB):
        cur = nxt
        if nb + 1 < NB:
            nxt = project(nb + 1)
        recur(nb, *cur)


def rglru_block(h3, pre_g, w_gate, w_in, conv_w, conv_b, wa, ba, wx, bx, lam, T=LRU_T, NB=LRU_NB):
    B, S, D = h3.shape
    W = LRU_W
    half = W // 2
    hb = T // LRU_HALO

    def bd(w):
        blocks = [jax.scipy.linalg.block_diag(*[w[h] for h in range(4 * j, 4 * j + 4)]) for j in range(2)]
        return jnp.stack(blocks).astype(BF16)

    vec = lambda a: a.reshape(1, W)
    assert B % NB == 0 and S % T == 0
    vspec = pl.BlockSpec((1, W), lambda b, t: (0, 0))
    wspec = pl.BlockSpec((2, half, half), lambda b, t: (0, 0, 0))
    return pl.pallas_call(
        _rglru_kernel, grid=(B // NB, S // T),
        in_specs=[pl.BlockSpec((NB, T, D), lambda b, t: (b, t, 0)),
                  pl.BlockSpec((NB, LRU_HALO, D), lambda b, t: (b, jnp.maximum(t * hb - 1, 0), 0)),
                  pl.BlockSpec((1, D), lambda b, t: (0, 0)),
                  pl.BlockSpec((D, W), lambda b, t: (0, 0)), pl.BlockSpec((D, W), lambda b, t: (0, 0)),
                  pl.BlockSpec((LRU_CONV, W), lambda b, t: (0, 0)), vspec, wspec, wspec, vspec, vspec, vspec],
        out_specs=pl.BlockSpec((NB, T, W), lambda b, t: (b, t, 0)),
        out_shape=jax.ShapeDtypeStruct((B, S, W), BF16),
        scratch_shapes=[pltpu.VMEM((NB, 8, W), F32), pltpu.VMEM((NB, 2, W // LANES, T, LANES), F32)],
        compiler_params=_cparams(("parallel", "arbitrary")), name="rglru",
    )(h3, h3, pre_g.reshape(1, D), w_gate, w_in, conv_w, vec(conv_b), bd(wa), bd(wx), vec(ba), vec(bx), vec(lam))


def _sgu_kernel(h_ref, gpre_ref, wu_ref, wv_ref, g_ref, b_ref, w_ref, bias_ref, o_ref):
    C, W = SG_CHUNK, o_ref.shape[1]
    dg = W // SG_GROUPS
    n_sub = o_ref.shape[0] // SG_SUB
    row = lax.broadcasted_iota(jnp.int32, (C, C), 0)
    col = lax.broadcasted_iota(jnp.int32, (C, C), 1)
    wc = [jnp.where(col <= row, w_ref[gi], 0.0).astype(BF16) for gi in range(SG_GROUPS)]

    def project(j):
        xn = _rms(h_ref[j * SG_SUB:(j + 1) * SG_SUB, :], gpre_ref[...]).astype(BF16)
        return _dot(xn, wu_ref[...]), _dot(xn, wv_ref[...])

    def gate(j, u, v):
        u, v = _gelu(u), _gelu(v)
        mu = jnp.mean(v, axis=-1, keepdims=True)
        vc = v - mu
        vn = (vc * lax.rsqrt(jnp.mean(vc * vc, axis=-1, keepdims=True) + EPS) * g_ref[...] + b_ref[...]).astype(BF16)
        for gi in range(SG_GROUPS):
            sl = slice(gi * dg, (gi + 1) * dg)
            for c in range(SG_SUB // C):
                rows = slice(c * C, (c + 1) * C)
                mixed = _dot(wc[gi], vn[rows, sl]) + bias_ref[:, sl]
                o_ref[j * SG_SUB + c * C:j * SG_SUB + (c + 1) * C, sl] = (u[rows, sl] * mixed).astype(o_ref.dtype)

    nxt = project(0)
    for j in range(n_sub):
        cur = nxt
        if j + 1 < n_sub:
            nxt = project(j + 1)
        gate(j, *cur)


def sgu_block(h3, pre_g, w_u, w_v, ln_g, ln_b, w, b):
    B, S, D = h3.shape
    W, C, T = SG_W, SG_CHUNK, SG_ROWS
    assert S % T == 0 and T % SG_SUB == 0 and SG_SUB % C == 0
    bias = jnp.repeat(b.T, W // SG_GROUPS, axis=1)
    vspec = pl.BlockSpec((1, W), lambda bb, c: (0, 0))
    return pl.pallas_call(
        _sgu_kernel, grid=(B, S // T),
        in_specs=[pl.BlockSpec((None, T, D), lambda bb, c: (bb, c, 0)),
                  pl.BlockSpec((1, D), lambda bb, c: (0, 0)),
                  pl.BlockSpec((D, W), lambda bb, c: (0, 0)), pl.BlockSpec((D, W), lambda bb, c: (0, 0)),
                  vspec, vspec,
                  pl.BlockSpec((SG_GROUPS, C, C), lambda bb, c: (0, 0, 0)),
                  pl.BlockSpec((C, W), lambda bb, c: (0, 0))],
        out_specs=pl.BlockSpec((None, T, W), lambda bb, c: (bb, c, 0)),
        out_shape=jax.ShapeDtypeStruct((B, S, W), BF16),
        compiler_params=_cparams(("parallel", "parallel")), name="sgu",
    )(h3, pre_g.reshape(1, D), w_u, w_v, ln_g.reshape(1, W), ln_b.reshape(1, W), w, bias)


def _mixer_ab(h2, B, S, pre_g, w_in, ml_gate_b, ml_norm_g, nsa_gate_b, k_pe, k_w1, k_w2, v_pe, v_w1, v_w2):
    D = h2.shape[1]
    G, d = NSA_KV_HEADS, HEAD_DIM
    offs = np.cumsum([0, ML_W, ML_W, ML_W, ML_W, 2 * ML_HEADS, NSA_W] + [G * d] * 6 + [3 * NSA_HEADS])
    mq, mk, mv, mo, mif, nq, kc, vc, ks, vs, kw, vw, ng = [w_in[:, offs[i]:offs[i + 1]] for i in range(13)]
    w_b = jnp.concatenate([mk, nq, ks, kw], axis=1).astype(BF16)
    w_t = jnp.concatenate([mq, mv, vs, vw], axis=1).T.astype(BF16)
    gpad = LANES - 2 * ML_HEADS - 3 * NSA_HEADS
    w_f = jnp.concatenate([mo, kc, vc, mif, ng, jnp.zeros((D, gpad), w_in.dtype)], axis=1).astype(BF16)
    gate_bias = jnp.concatenate([ml_gate_b, nsa_gate_b, jnp.zeros((gpad,), F32)]).reshape(1, LANES)
    zb, zf, zt = norm_proj(h2, pre_g, [w_b, w_f], [BF16, F32], wts=[w_t], batch=B)
    zb = zb.reshape(B, S, -1)
    zf = zf.reshape(B, S, -1)
    h_ml = mlstm_block(zt, zb, zf, gate_bias, ml_norm_g)
    kcmp, vcmp_t = compress_block(zf, ML_W // (G * d), ML_W // (G * d) + 1, k_pe, k_w1, k_w2, v_pe, v_w1, v_w2)
    h_nsa = nsa_block(zt, zb, zf, gate_bias, kcmp, vcmp_t, q_col_block=ML_W // NSA_W,
                      k_col_blocks=((ML_W + NSA_W) // (G * d), (ML_W + NSA_W) // (G * d) + 1),
                      v_row_blocks=(2 * ML_W // (G * d), 2 * ML_W // (G * d) + 1))
    return h_ml.reshape(B * S, ML_W), h_nsa.reshape(B * S, NSA_W)


def _mixer_cd(h2, B, S, pre_g, w_in, conv_w, conv_b, wa, ba, wx, bx, lam, sg_g, sg_bn, sg_w, sg_b):
    h3 = h2.reshape(B, S, -1)
    w_gate, w_x, w_u, w_v = (w_in[:, j * LRU_W:(j + 1) * LRU_W].astype(BF16) for j in range(4))
    y_lru = rglru_block(h3, pre_g, w_gate, w_x, conv_w, conv_b, wa, ba, wx, bx, lam)
    y_sg = sgu_block(h3, pre_g, w_u, w_v, sg_g, sg_bn, sg_w, sg_b)
    return y_lru.reshape(B * S, LRU_W), y_sg.reshape(B * S, SG_W)


def kernel(x, pre_mix_g, post_mix_g, pre_ffn_g, post_ffn_g, ab_w_in, ab_w_out, ml_gate_b, ml_norm_g, nsa_gate_b, cmp_k_pe, cmp_k_w1, cmp_k_w2, cmp_v_pe, cmp_v_w1, cmp_v_w2, cd_w_in, cd_w_out, lru_conv_w, lru_conv_b, lru_wa, lru_ba, lru_wx, lru_bx, lru_lambda, sg_norm_g, sg_norm_b, sg_w, sg_b, ffn_w_up, ffn_conv_w, ffn_conv_b, ffn_w_down):
    B, S, D = x.shape
    depth = pre_mix_g.shape[0]
    h2 = x.reshape(B * S, D)
    for layer in range(depth):
        if layer % 2 == 0:
            e = layer // 2
            a1, a2 = _mixer_ab(h2, B, S, pre_mix_g[layer], ab_w_in[e], ml_gate_b[e], ml_norm_g[e], nsa_gate_b[e],
                               cmp_k_pe[e], cmp_k_w1[e], cmp_k_w2[e], cmp_v_pe[e], cmp_v_w1[e], cmp_v_w2[e])
            w_out = ab_w_out[e]
        else:
            o = layer // 2
            a1, a2 = _mixer_cd(h2, B, S, pre_mix_g[layer], cd_w_in[o], lru_conv_w[o], lru_conv_b[o], lru_wa[o],
                               lru_ba[o], lru_wx[o], lru_bx[o], lru_lambda[o], sg_norm_g[o], sg_norm_b[o], sg_w[o],
                               sg_b[o])
            w_out = cd_w_out[o]
        h2 = mix_ffn_block(h2, a1, a2, S, w_out, post_mix_g[layer], pre_ffn_g[layer], ffn_w_up[layer],
                           ffn_conv_w[layer], ffn_conv_b[layer], ffn_w_down[layer], post_ffn_g[layer])
    return h2.reshape(B, S, D)
```

```python
import functools

import numpy as np
import jax
import jax.numpy as jnp
from jax import lax
from jax.experimental import pallas as pl
from jax.experimental.pallas import tpu as pltpu

F32 = jnp.float32
BF16 = jnp.bfloat16

EPS = 1e-6
HEAD_DIM = 64
ML_HEADS = 8
ML_W = 512
GATE_SOFTCAP = 15.0
NSA_HEADS = 8
NSA_KV_HEADS = 2
NSA_HPG = NSA_HEADS // NSA_KV_HEADS
NSA_W = 512
CMP_LEN = 32
CMP_STRIDE = 16
CMP_HID = 128
SEL_LEN = 64
SEL_TOPN = 16
WINDOW = 512
LRU_W = 512
LRU_C = 8.0
LRU_CONV = 4
SG_GROUPS = 8
SG_W = 512
SG_CHUNK = 128
FFN_CONV = 3
GATE_COL_BLOCK = (ML_W + 2 * NSA_KV_HEADS * HEAD_DIM) // 128
NSA_GATE_LANE0 = 2 * ML_HEADS

LANES = 128
VMEM_LIMIT = 56 * 1024 * 1024
NEG_BIG = -1e30
HIGHEST = lax.Precision.HIGHEST
LOG2E = 1.4426950408889634


def _bf16_terms(x, n):
    terms = []
    for _ in range(n):
        bits = int(np.array(x, np.float32).view(np.uint32))
        t = float(np.array((bits + 0x7FFF + ((bits >> 16) & 1)) & 0xFFFF0000, np.uint32).view(np.float32))
        terms.append(t)
        x -= t
    return tuple(terms)


LOG2E_TERMS = _bf16_terms(LOG2E, 3)

ML_CHUNK = 128
ML_NB = 8
NSA_TQ = 256
NSA_TK = 256
RANK_STEP = 16
NSA_ONES = 16
ROW_TILE = 512
FFN_TM = 512
FFN_SUB = 256
FFN_CK = 256
FFN_HALO = 16
FFN_STAGE_UP = 128
FFN_STAGE_DOWN = 256
SG_ROWS = 1024
SG_SUB = 256
LRU_T = 512
LRU_HALO = 8
LRU_NB = 4


def _cparams(sem):
    return pltpu.CompilerParams(dimension_semantics=sem, vmem_limit_bytes=VMEM_LIMIT)


def _rms(x, g):
    return x * lax.rsqrt(jnp.mean(x * x, axis=-1, keepdims=True) + EPS) * g


def _gelu(x):
    return 0.5 * x * (1.0 + jnp.tanh(0.7978845608028654 * (x + 0.044715 * (x * x * x))))


def _sigmoid(x):
    return 1.0 / (1.0 + jnp.exp(-x))


def _dot(a, b):
    return jnp.dot(a, b, preferred_element_type=F32)


def _dot_nt(a, b, precision=None):
    return lax.dot_general(a, b, (((1,), (1,)), ((), ())), preferred_element_type=F32, precision=precision)


def _norm_proj_kernel(h_ref, g_ref, *refs, n_row, n_t, cn):
    w_refs, wt_refs = refs[:n_row], refs[n_row:n_row + n_t]
    o_refs, ot_refs = refs[n_row + n_t:2 * n_row + n_t], refs[2 * n_row + n_t:]
    xn = _rms(h_ref[...], g_ref[...]).astype(BF16)
    for w_ref, o_ref in zip(w_refs, o_refs):
        n = w_ref.shape[1]
        for c in range(0, n, cn):
            ce = min(c + cn, n)
            o_ref[:, c:ce] = _dot(xn, w_ref[:, c:ce]).astype(o_ref.dtype)
    for wt_ref, ot_ref in zip(wt_refs, ot_refs):
        n = wt_ref.shape[0]
        for c in range(0, n, cn):
            ce = min(c + cn, n)
            ot_ref[c:ce, :] = _dot_nt(wt_ref[c:ce, :], xn).astype(ot_ref.dtype)


def norm_proj(h2, g, ws, dtypes, wts=(), batch=1, tm=ROW_TILE):
    M, D = h2.shape
    tps = M // batch // tm
    in_specs = [pl.BlockSpec((tm, D), lambda i: (i, 0)), pl.BlockSpec((1, D), lambda i: (0, 0))]
    in_specs += [pl.BlockSpec(w.shape, lambda i: (0, 0)) for w in (*ws, *wts)]
    out_specs = [pl.BlockSpec((tm, w.shape[1]), lambda i: (i, 0)) for w in ws]
    out_specs += [pl.BlockSpec((None, w.shape[0], tm), lambda i: (i // tps, 0, i % tps)) for w in wts]
    out_shape = [jax.ShapeDtypeStruct((M, w.shape[1]), dt) for w, dt in zip(ws, dtypes)]
    out_shape += [jax.ShapeDtypeStruct((batch, w.shape[0], M // batch), BF16) for w in wts]
    return pl.pallas_call(
        functools.partial(_norm_proj_kernel, n_row=len(ws), n_t=len(wts), cn=512),
        grid=(M // tm,), in_specs=in_specs, out_specs=out_specs, out_shape=out_shape,
        compiler_params=_cparams(("parallel",)), name="norm_proj",
    )(h2, g.reshape(1, D), *ws, *wts)


def _stage_bf16(src_hbm, dst_s, stg_s, sem, rows):
    n = src_hbm.shape[0] // rows

    def copy(i):
        return pltpu.make_async_copy(src_hbm.at[pl.ds(i * rows, rows), :], stg_s.at[i % 2], sem.at[i % 2])

    copy(0).start()
    for i in range(n):
        if i + 1 < n:
            copy(i + 1).start()
        copy(i).wait()
        dst_s[i * rows:(i + 1) * rows, :] = stg_s[i % 2].astype(BF16)


def _mix_ffn_kernel(h_ref, a1_ref, a2_ref, wo1_ref, wo2_ref, gmix_ref, gpre_ref, wu_hbm, cw_ref, cb_ref, wd_hbm,
                    gpost_ref, o_ref, xn_s, acc_s, ut_s, wu_ref, wd_ref, stgu_s, stgd_s, sem_u, sem_d,
                    *, tiles_per_seq, ck):
    F = wd_ref.shape[0]
    n_chunks = F // ck
    n_sub = h_ref.shape[0] // FFN_SUB
    g = gpre_ref[...]

    @pl.when(pl.program_id(0) == 0)
    def _():
        _stage_bf16(wu_hbm, wu_ref, stgu_s, sem_u, FFN_STAGE_UP)
        _stage_bf16(wd_hbm, wd_ref, stgd_s, sem_d, FFN_STAGE_DOWN)

    @pl.when(pl.program_id(0) % tiles_per_seq == 0)
    def _():
        ut_s[...] = jnp.zeros_like(ut_s)

    def mixed(h, a1, a2):
        return h + _rms(_dot(a1, wo1_ref[...]) + _dot(a2, wo2_ref[...]), gmix_ref[...])

    def conv(u, cols):
        y = (cw_ref[2:3, cols] * u + cw_ref[1:2, cols] * pltpu.roll(u, 1, 0) + cw_ref[0:1, cols] * pltpu.roll(u, 2, 0)
             + cb_ref[:, cols])
        return y[FFN_HALO:, :]

    def cols_of(c, half):
        return slice(half * F + c * ck, half * F + (c + 1) * ck)

    x = {}

    def head(j):
        rows = slice(j * FFN_SUB, (j + 1) * FFN_SUB)
        x[j] = mixed(h_ref[rows, :], a1_ref[rows, :], a2_ref[rows, :])
        xn_s[rows, :] = _rms(x[j], g).astype(BF16)
        acc_s[rows, :] = jnp.zeros((FFN_SUB, acc_s.shape[1]), F32)

    def up(j, c):
        xn = xn_s[j * FFN_SUB:(j + 1) * FFN_SUB, :]
        u = [jnp.concatenate([ut_s[c, half], _dot(xn, wu_ref[:, cols_of(c, half)])], axis=0) for half in range(2)]
        for half in range(2):
            ut_s[c, half] = u[half][FFN_SUB:, :]
        return u

    def tail(j):
        rows = slice(j * FFN_SUB, (j + 1) * FFN_SUB)
        o_ref[rows, :] = x[j] + _rms(acc_s[rows, :], gpost_ref[...])

    head(0)
    for j in range(n_sub):
        u = up(j, 0)
        for c in range(n_chunks):
            u_next = up(j, c + 1) if c + 1 < n_chunks else None
            if c == 0 and j + 1 < n_sub:
                head(j + 1)
            if c == n_chunks // 3 and j >= 1:
                tail(j - 1)
            act = (_gelu(conv(u[0], cols_of(c, 0))) * conv(u[1], cols_of(c, 1))).astype(BF16)
            acc_s[j * FFN_SUB:(j + 1) * FFN_SUB, :] += _dot(act, wd_ref[c * ck:(c + 1) * ck, :])
            u = u_next
    tail(n_sub - 1)


def mix_ffn_block(h2, a1, a2, seq_len, w_out, g_mix, g_pre, w_up, conv_w, conv_b, w_down, g_post, tm=FFN_TM, ck=FFN_CK):
    M, D = h2.shape
    F = w_down.shape[0]
    K1, K2 = a1.shape[1], a2.shape[1]
    assert F % ck == 0 and seq_len % tm == 0 and tm % FFN_SUB == 0 and FFN_SUB % FFN_HALO == 0
    assert conv_w.shape[0] == FFN_CONV and FFN_CONV - 1 <= FFN_HALO
    assert D % FFN_STAGE_UP == 0 and F % FFN_STAGE_DOWN == 0
    consts = (w_out[:K1].astype(BF16), w_out[K1:].astype(BF16), g_mix.reshape(1, D), g_pre.reshape(1, D),
              w_up, conv_w, conv_b.reshape(1, 2 * F), w_down, g_post.reshape(1, D))

    def tile(width):
        return pl.BlockSpec((tm, width), lambda i: (i, 0))

    def full(a):
        if a is w_up or a is w_down:
            return pl.BlockSpec(memory_space=pl.ANY)
        return pl.BlockSpec(a.shape, lambda i: (0,) * a.ndim)

    return pl.pallas_call(
        functools.partial(_mix_ffn_kernel, tiles_per_seq=seq_len // tm, ck=ck),
        grid=(M // tm,),
        in_specs=[tile(D), tile(K1), tile(K2)] + [full(a) for a in consts],
        out_specs=tile(D),
        out_shape=jax.ShapeDtypeStruct((M, D), F32),
        scratch_shapes=[pltpu.VMEM((tm, D), BF16), pltpu.VMEM((tm, D), F32),
                        pltpu.VMEM((F // ck, 2, FFN_HALO, ck), F32),
                        pltpu.VMEM((D, 2 * F), BF16), pltpu.VMEM((F, D), BF16),
                        pltpu.VMEM((2, FFN_STAGE_UP, 2 * F), F32), pltpu.VMEM((2, FFN_STAGE_DOWN, D), F32),
                        pltpu.SemaphoreType.DMA((2,)), pltpu.SemaphoreType.DMA((2,))],
        compiler_params=_cparams(("arbitrary",)), name="mix_ffn",
    )(h2, a1, a2, *consts)


def _mlstm_kernel(qt_ref, vt_ref, k_ref, mo_ref, gt_ref, gb_ref, ng_ref, o_ref, cn_s, m_s):
    NB, _, L = qt_ref.shape
    d = HEAD_DIM

    @pl.when(pl.program_id(1) == 0)
    def _():
        cn_s[...] = jnp.zeros_like(cn_s)
        m_s[...] = jnp.zeros_like(m_s)

    src = lax.broadcasted_iota(jnp.int32, (L, L), 0)
    tgt = lax.broadcasted_iota(jnp.int32, (L, L), 1)
    causal = src <= tgt
    tri = (tgt <= src).astype(F32)

    gcap = [GATE_SOFTCAP * jnp.tanh((gt_ref[nb] + gb_ref[...]) * (1.0 / GATE_SOFTCAP)) for nb in range(NB)]
    lf = [jnp.minimum(x, 0.0) - jnp.log1p(jnp.exp(-jnp.abs(x))) for x in gcap]
    b_col = [jnp.dot(tri, x, preferred_element_type=F32, precision=HIGHEST) for x in lf]
    b_row = [x.T for x in b_col]
    i_row = [x.T for x in gcap]
    c_col = [b_col[nb] - pltpu.roll(gcap[nb], ML_HEADS, 1) for nb in range(NB)]

    chains = [(nb, h) for nb in range(NB) for h in range(ML_HEADS)]
    n = range(len(chains))

    def rows(h):
        return slice(h * d, (h + 1) * d)

    q_t = [qt_ref[nb, rows(h), :] for nb, h in chains]
    v_t = [vt_ref[nb, rows(h), :] for nb, h in chains]
    k = [k_ref[nb, :, rows(h)] * 0.125 for nb, h in chains]
    br = [b_row[nb][ML_HEADS + h:ML_HEADS + h + 1, :] for nb, h in chains]
    ir = [i_row[nb][h:h + 1, :] for nb, h in chains]
    g = [x[:, L - 1:L] for x in br]
    m_prev = [m_s[nb, h:h + 1, 0:1] for nb, h in chains]
    cn_prev = [cn_s[nb, h] for nb, h in chains]
    kq = [_dot(k[c], q_t[c]) for c in n]
    carry = [_dot(cn_prev[c].astype(BF16), q_t[c]) for c in n]
    dlog = [jnp.where(causal, br[c] - c_col[nb][:, ML_HEADS + h:ML_HEADS + h + 1], -jnp.inf)
            for c, (nb, h) in enumerate(chains)]
    inter = [br[c] + m_prev[c] for c in n]
    m_row = [jnp.maximum(inter[c], jnp.max(dlog[c], axis=0, keepdims=True)) for c in n]
    s = [kq[c] * jnp.exp(dlog[c] - m_row[c]) for c in n]
    w_inter = [jnp.exp(inter[c] - m_row[c]) for c in n]
    num = [_dot(v_t[c], s[c].astype(BF16)) + w_inter[c] * carry[c][0:d] for c in n]
    den = [jnp.sum(s[c], axis=0, keepdims=True) + w_inter[c] * carry[c][d:d + 1] for c in n]
    hh = [num[c] * (1.0 / jnp.maximum(jnp.abs(den[c]), jnp.exp(-m_row[c]))) for c in n]
    wlog = [g[c] - br[c] + ir[c] for c in n]
    m_new = [jnp.maximum(g[c] + m_prev[c], jnp.max(wlog[c], axis=-1, keepdims=True)) for c in n]
    w_row = [jnp.exp(wlog[c] - m_new[c]) for c in n]
    decay = [jnp.exp(g[c] + m_prev[c] - m_new[c]) for c in n]
    for c, (nb, h) in enumerate(chains):
        vw = jnp.concatenate([v_t[c].astype(F32) * w_row[c], jnp.broadcast_to(w_row[c], (8, L))], axis=0).astype(BF16)
        cn_s[nb, h] = decay[c] * cn_prev[c] + _dot(vw, k[c])
        m_s[nb, h:h + 1, :] = jnp.broadcast_to(m_new[c], (1, m_s.shape[2]))
    outs = [hh[c] * lax.rsqrt(jnp.mean(hh[c] * hh[c], axis=0, keepdims=True) + EPS) * ng_ref[rows(h), :]
            for c, (nb, h) in enumerate(chains)]
    for nb in range(NB):
        out_t = jnp.concatenate(outs[nb * ML_HEADS:(nb + 1) * ML_HEADS], axis=0)
        o_ref[nb] = (out_t.T * _sigmoid(mo_ref[nb])).astype(o_ref.dtype)


def mlstm_block(zt, zb, zf, gate_bias, norm_g, L=ML_CHUNK, NB=ML_NB):
    B, S, _ = zb.shape
    W, H, d = ML_W, ML_HEADS, HEAD_DIM
    assert B % NB == 0 and S % L == 0
    ng = jnp.broadcast_to(norm_g.reshape(W, 1), (W, L))
    return pl.pallas_call(
        _mlstm_kernel, grid=(B // NB, S // L),
        in_specs=[pl.BlockSpec((NB, W, L), lambda b, c: (b, 0, c)),
                  pl.BlockSpec((NB, W, L), lambda b, c: (b, 1, c)),
                  pl.BlockSpec((NB, L, W), lambda b, c: (b, c, 0)),
                  pl.BlockSpec((NB, L, W), lambda b, c: (b, c, 0)),
                  pl.BlockSpec((NB, L, LANES), lambda b, c: (b, c, GATE_COL_BLOCK)),
                  pl.BlockSpec((1, LANES), lambda b, c: (0, 0)),
                  pl.BlockSpec((W, L), lambda b, c: (0, 0))],
        out_specs=pl.BlockSpec((NB, L, W), lambda b, c: (b, c, 0)),
        out_shape=jax.ShapeDtypeStruct((B, S, W), BF16),
        scratch_shapes=[pltpu.VMEM((NB, H, d + 8, d), F32), pltpu.VMEM((NB, H, LANES), F32)],
        compiler_params=_cparams(("parallel", "arbitrary")), name="mlstm",
    )(zt, zt, zb, zf, zf, gate_bias, ng)


def _compress_kernel(kc_ref, vc_ref, pek_ref, pev_ref, w1k_ref, w1v_ref, w2k_ref, w2v_ref, ok_ref, ov_ref):
    G = NSA_KV_HEADS
    nh = kc_ref.shape[0] // CMP_STRIDE

    def one(x_ref, pe_ref, w1_ref, w2_ref):
        hid = None
        for l in range(CMP_STRIDE):
            y = x_ref[pl.ds(l, nh, stride=CMP_STRIDE), :]
            ya = (y + pe_ref[l:l + 1, :]).astype(BF16)
            yb = (pltpu.roll(y, nh - 1, 0) + pe_ref[CMP_STRIDE + l:CMP_STRIDE + l + 1, :]).astype(BF16)
            t = _dot(ya, w1_ref[l]) + _dot(yb, w1_ref[CMP_STRIDE + l])
            hid = t if hid is None else hid + t
        hid = _gelu(hid).astype(BF16)
        return [_dot(hid[:, gi * CMP_HID:(gi + 1) * CMP_HID], w2_ref[...]) for gi in range(G)]

    for gi, (ko, vo) in enumerate(zip(one(kc_ref, pek_ref, w1k_ref, w2k_ref), one(vc_ref, pev_ref, w1v_ref, w2v_ref))):
        ok_ref[gi] = ko.astype(ok_ref.dtype)
        ov_ref[gi] = vo.T.astype(ov_ref.dtype)


def compress_block(zf, kc_block, vc_block, k_pe, k_w1, k_w2, v_pe, v_w1, v_w2):
    B, S, _ = zf.shape
    G, d = NSA_KV_HEADS, HEAD_DIM
    nh = S // CMP_STRIDE

    def prep(pe, w1):
        w1bd = jnp.zeros((CMP_LEN, G * d, G * CMP_HID), w1.dtype)
        for gi in range(G):
            w1bd = w1bd.at[:, gi * d:(gi + 1) * d, gi * CMP_HID:(gi + 1) * CMP_HID].set(w1)
        return jnp.tile(pe, (1, G)), w1bd.astype(BF16)

    pek, w1k = prep(k_pe, k_w1)
    pev, w1v = prep(v_pe, v_w1)

    def full(a):
        return pl.BlockSpec(a.shape, lambda b: (0,) * a.ndim)

    w2k, w2v = k_w2.astype(BF16), v_w2.astype(BF16)
    return pl.pallas_call(
        _compress_kernel, grid=(B,),
        in_specs=[pl.BlockSpec((None, S, G * d), lambda b: (b, 0, kc_block)),
                  pl.BlockSpec((None, S, G * d), lambda b: (b, 0, vc_block)),
                  full(pek), full(pev), full(w1k), full(w1v), full(w2k), full(w2v)],
        out_specs=[pl.BlockSpec((None, G, nh, d), lambda b: (b, 0, 0, 0)),
                   pl.BlockSpec((None, G, d, nh), lambda b: (b, 0, 0, 0))],
        out_shape=[jax.ShapeDtypeStruct((B, G, nh, d), BF16), jax.ShapeDtypeStruct((B, G, d, nh), BF16)],
        compiler_params=_cparams(("parallel",)), name="nsa_compress",
    )(zf, zf, pek, pev, w1k, w1v, w2k, w2v)


def _nsa_kernel(q_ref, gt_ref, gb_ref, kc_ref, vct_ref, ks_ref, vst_ref, kw_ref, vwt_ref, ovl_ref, aug_ref, caug_ref,
                tail_ref, ctail_ref, wmask_ref, o_ref, ksa_s, kwa_s, kca_s, vsa_s, vwa_s, ss_s, ps_s, sw_s, pw_s, *, n_cmp):
    TQ, TK, d, HPG, G = NSA_TQ, NSA_TK, HEAD_DIM, NSA_HPG, NSA_KV_HEADS
    R = HPG * TQ
    groups = range(G)
    qi = pl.program_id(1)
    q0 = qi * TQ
    kt_d = q0 // TK
    n_cb = kc_ref.shape[1]
    n_sb = ovl_ref.shape[0]

    @pl.when(qi == 0)
    def _():
        for g in groups:
            ksa_s[g, :, 0:d] = ks_ref[:, g * d:(g + 1) * d]
            ksa_s[g, :, d:] = aug_ref[...]
            kwa_s[g, :, 0:d] = kw_ref[:, g * d:(g + 1) * d]
            kwa_s[g, :, d:] = aug_ref[...]
            kca_s[g, :, 0:d] = kc_ref[g]
            kca_s[g, :, d:] = caug_ref[...]
            vsa_s[g, 0:d, :] = vst_ref[g * d:(g + 1) * d, :]
            vsa_s[g, d:, :] = jnp.ones((vsa_s.shape[1] - d, vsa_s.shape[2]), BF16)
            vwa_s[g, 0:d, :] = vwt_ref[g * d:(g + 1) * d, :]
            vwa_s[g, d:, :] = jnp.ones((vwa_s.shape[1] - d, vwa_s.shape[2]), BF16)

    def slope(g, hh):
        return 2.0 ** (-(g * HPG + hh + 1))

    def per_head(fn):
        return jnp.concatenate([fn(hh) for hh in range(HPG)], axis=1)

    def tile_heads(x):
        return jnp.concatenate([x] * HPG, axis=1)

    q_t = [(q_ref[:, g * HPG * d:(g + 1) * HPG * d].astype(F32) * (LOG2E * 0.125)).T for g in groups]

    def q_head(g, hh):
        return q_t[g][hh * d:(hh + 1) * d]

    def q_aug(g, block_rows):
        return per_head(lambda hh: jnp.concatenate(
            [q_head(g, hh), block_rows, tail_ref[...] * slope(g, hh)], axis=0)).astype(BF16)

    def scores(ka, qa_t, kt):
        return _dot(ka[pl.ds(pl.multiple_of(kt * TK, TK), TK), :], qa_t)

    def stage_scores(buf, ka, qa_t, kt, mask_add=None):
        s = scores(ka, qa_t, kt)
        if mask_add is not None:
            s = s + mask_add
        buf[...] = s
        return jnp.max(s, axis=0, keepdims=True)

    def stage_probs(sbuf, pbuf, tile_max, m):
        m_new = jnp.maximum(m, tile_max)
        pbuf[...] = jnp.exp2(sbuf[...] - m_new).astype(BF16)
        return m_new, jnp.exp2(m - m_new)

    def stage_values(pbuf, va, kt, alpha, acc):
        return alpha * acc + _dot(va[:, pl.ds(pl.multiple_of(kt * TK, TK), TK)], pbuf[...])

    def normalised(acc):
        return acc[0:d] * (1.0 / acc[d:d + 1])

    m0, acc0 = jnp.full((1, R), NEG_BIG, F32), jnp.zeros((vsa_s.shape[1], R), F32)

    qc_t = [per_head(lambda hh: jnp.concatenate([q_head(g, hh), ctail_ref[...] * slope(g, hh)], axis=0)).astype(BF16)
            for g in groups]
    qw_t = [q_aug(g, jnp.zeros((n_sb, TQ), F32)) for g in groups]
    n_r = lax.broadcasted_iota(jnp.int32, (n_cb, TQ), 0)
    t_c = q0 + lax.broadcasted_iota(jnp.int32, (n_cb, TQ), 1)
    ok_c = (n_r * CMP_STRIDE + (CMP_LEN - 1) <= t_c) & (n_r < n_cmp)
    add_c = tile_heads(jnp.where(ok_c, 0.0, NEG_BIG))
    s_c = [_dot(kca_s[g], qc_t[g]) + add_c for g in groups]
    n_win = (WINDOW - 1 + TK - 1) // TK + 1
    kt_win, max_win = [], []
    for back in range(n_win):
        kt_raw = kt_d - back
        kt_win.append(jnp.maximum(kt_raw, 0))
        if back == 0:
            mask_add = tile_heads(wmask_ref[0])
        elif (back + 1) * TK <= WINDOW:
            mask_add = jnp.where(kt_raw >= 0, 0.0, NEG_BIG)
        else:
            mask_add = tile_heads(wmask_ref[back] + jnp.where(kt_raw >= 0, 0.0, NEG_BIG))
        max_win.append([stage_scores(sw_s.at[g, back], kwa_s.at[g], qw_t[g], kt_win[back], mask_add) for g in groups])

    e_c = [jnp.exp2(s_c[g] - jnp.maximum(jnp.max(s_c[g], axis=0, keepdims=True), 0.1 * NEG_BIG)) for g in groups]
    inv_c = [1.0 / jnp.maximum(jnp.sum(e_c[g], axis=0, keepdims=True), 1.0) for g in groups]
    o_c = [_dot(vct_ref[g], e_c[g].astype(BF16)) * inv_c[g] for g in groups]

    jb = lax.broadcasted_iota(jnp.int32, (n_sb, TQ), 0)
    cur = (q0 + lax.broadcasted_iota(jnp.int32, (n_sb, TQ), 1)) // SEL_LEN
    valid = jb <= cur
    forced = (jb == 0) | (jb == cur) | (jb == cur - 1)
    sub = lax.broadcasted_iota(jnp.int32, (8, TQ), 0)
    score = []
    for g in groups:
        p_sum = e_c[g][:, 0:TQ] * inv_c[g][:, 0:TQ]
        for hh in range(1, HPG):
            p_sum = p_sum + e_c[g][:, hh * TQ:(hh + 1) * TQ] * inv_c[g][:, hh * TQ:(hh + 1) * TQ]
        imp = jnp.dot(ovl_ref[...], p_sum, preferred_element_type=F32, precision=HIGHEST)
        score.append(jnp.where(forced, jnp.inf, jnp.where(valid, imp, -jnp.inf)))

    def ranked(n_rows):
        def fn(*score):
            n_grp = n_rows // 8
            outs = []
            grp = [[s[8 * a:8 * a + 8] for a in range(n_grp)] for s in score]
            rank = [[jnp.zeros((8, TQ), jnp.int32) for _ in range(n_grp)] for _ in score]
            for j in range(n_rows):
                for g in groups:
                    r = score[g][j:j + 1, :]
                    for a in range(n_grp):
                        if a > j // 8:
                            ahead = (r >= grp[g][a]).astype(jnp.int32)
                        elif a < j // 8:
                            ahead = (r > grp[g][a]).astype(jnp.int32)
                        else:
                            ahead = jnp.where(sub > j % 8, (r >= grp[g][a]).astype(jnp.int32),
                                              (r > grp[g][a]).astype(jnp.int32))
                        rank[g][a] = rank[g][a] + ahead
            for g in groups:
                top = jnp.where(jnp.concatenate(rank[g], axis=0) < SEL_TOPN, 1.0, 0.0)
                outs.append(jnp.concatenate([top, jnp.zeros((n_sb - n_rows, TQ), F32)], axis=0) if n_rows < n_sb else top)
            return tuple(outs)
        return fn

    sizes = [n for n in range(RANK_STEP, n_sb + 1, RANK_STEP)]
    rows_needed = (q0 + TQ + SEL_LEN - 1) // SEL_LEN
    in_top = lax.switch((rows_needed + RANK_STEP - 1) // RANK_STEP - 1, [ranked(n) for n in sizes], *score)
    picked = [valid & (in_top[g] > 0.5) for g in groups]
    qs_t = [q_aug(g, jnp.where(picked[g], 0.0, NEG_BIG)) for g in groups]

    filler = 1 - kt_d % 2
    n_seq = kt_d + 1 + filler

    def sel_tile(i):
        return jnp.where(i == 0, kt_d, jnp.maximum(i - 1 - filler, 0))

    def a_stage(slot, kt, mask_add=None):
        return [stage_scores(ss_s.at[g, slot], ksa_s.at[g], qs_t[g], kt, mask_add) for g in groups]

    def b_stage(slot, tmax, m):
        out = [stage_probs(ss_s.at[g, slot], ps_s.at[g, slot], tmax[g], m[g]) for g in groups]
        return [o[0] for o in out], [o[1] for o in out]

    def c_stage(slot, kt, alpha, acc):
        return [stage_values(ps_s.at[g, slot], vsa_s.at[g], kt, alpha[g], acc[g]) for g in groups]

    tmax0 = a_stage(0, kt_d, tile_heads(wmask_ref[0]))
    m, alpha = b_stage(0, tmax0, [m0] * G)
    tmax1 = a_stage(1, sel_tile(1), jnp.where(filler == 1, NEG_BIG, 0.0))

    m_w = [functools.reduce(jnp.maximum, [max_win[back][g] for back in range(n_win)]) for g in groups]
    accw = [None] * G
    for back in range(n_win):
        for g in groups:
            pw_s[g, back] = jnp.exp2(sw_s[g, back] - m_w[g]).astype(BF16)
        for g in groups:
            pv = _dot(vwa_s[g, :, pl.ds(pl.multiple_of(kt_win[back] * TK, TK), TK)], pw_s[g, back])
            accw[g] = pv if accw[g] is None else accw[g] + pv
    o_w = [normalised(accw[g]) for g in groups]

    def sel_body(k, carry):
        m, alpha, acc, tmax1 = carry
        i = 2 * k
        tmax0 = a_stage(0, sel_tile(i + 2))
        m, alpha1 = b_stage(1, tmax1, m)
        tmax1 = a_stage(1, sel_tile(i + 3))
        acc = c_stage(0, sel_tile(i), alpha, acc)
        m, alpha2 = b_stage(0, tmax0, m)
        acc = c_stage(1, sel_tile(i + 1), alpha1, acc)
        return m, alpha2, acc, tmax1

    m, alpha, acc, tmax1 = lax.fori_loop(0, (n_seq - 2) // 2, sel_body, (m, alpha, [acc0] * G, tmax1))
    m, alpha1 = b_stage(1, tmax1, m)
    acc = c_stage(0, sel_tile(n_seq - 2), alpha, acc)
    acc = c_stage(1, sel_tile(n_seq - 1), alpha1, acc)
    o_s = [normalised(acc[g]) for g in groups]

    gates = _sigmoid(gt_ref[...] + gb_ref[...]).T
    outs = []
    for g in groups:
        for hh in range(HPG):
            cols = slice(hh * TQ, (hh + 1) * TQ)
            row = NSA_GATE_LANE0 + 3 * (g * HPG + hh)
            outs.append(gates[row:row + 1] * o_c[g][:, cols] + gates[row + 1:row + 2] * o_s[g][:, cols]
                        + gates[row + 2:row + 3] * o_w[g][:, cols])
    o_ref[...] = jnp.concatenate(outs, axis=0).T.astype(o_ref.dtype)


def nsa_block(zt, zb, zf, gate_bias, kcmp, vcmp_t, q_col_block, k_col_blocks, v_row_blocks):
    B, S, _ = zb.shape
    G, d = NSA_KV_HEADS, HEAD_DIM
    TQ = NSA_TQ
    n_cb = kcmp.shape[2]
    n_cmp = (S - CMP_LEN) // CMP_STRIDE + 1
    n_sb = S // SEL_LEN
    n_terms = len(LOG2E_TERMS)
    kaug = -(-(d + n_sb + 2 * n_terms) // LANES) * LANES
    R = NSA_HPG * TQ
    n_win = (WINDOW - 1 + NSA_TK - 1) // NSA_TK + 1
    assert S % NSA_TK == 0 and NSA_TK == TQ and TQ % SEL_LEN == 0

    cidx = np.arange(n_cb)[None, :] * CMP_STRIDE
    sstart = np.arange(n_sb)[:, None] * SEL_LEN
    ovl = ((cidx < sstart + SEL_LEN) & (cidx + CMP_LEN - 1 >= sstart) & (np.arange(n_cb)[None, :] < n_cmp))
    ovl = jnp.asarray(ovl.astype(np.float32))
    pos = np.arange(S)
    aug = np.zeros((S, kaug - d), np.float32)
    aug[pos, pos // SEL_LEN] = 1.0
    tail = np.zeros((kaug - d - n_sb, TQ), np.float32)
    caug = np.zeros((n_cb, d), np.float32)
    ctail = np.zeros((d, TQ), np.float32)
    for i, term in enumerate(LOG2E_TERMS):
        aug[:, n_sb + i] = pos // SEL_LEN
        aug[:, n_sb + n_terms + i] = pos % SEL_LEN
        tail[i] = term * SEL_LEN
        tail[n_terms + i] = term
        caug[:, i] = np.arange(n_cb)
        ctail[i] = term * CMP_STRIDE
    aug, caug, tail, ctail = jnp.asarray(aug, BF16), jnp.asarray(caug, BF16), jnp.asarray(tail), jnp.asarray(ctail)
    dist = np.arange(n_win)[:, None, None] * NSA_TK + np.arange(TQ)[None, None, :] - np.arange(NSA_TK)[None, :, None]
    wmask = jnp.asarray(np.where((dist >= 0) & (dist < WINDOW), 0.0, NEG_BIG).astype(np.float32))

    def kspec(j):
        return pl.BlockSpec((None, S, G * d), lambda b, i: (b, 0, k_col_blocks[j]))

    def vspec(j):
        return pl.BlockSpec((None, G * d, S), lambda b, i: (b, v_row_blocks[j], 0))

    def const(a):
        return pl.BlockSpec(a.shape, lambda b, i: (0,) * a.ndim)

    W = G * NSA_HPG * d
    return pl.pallas_call(
        functools.partial(_nsa_kernel, n_cmp=n_cmp), grid=(B, S // TQ),
        in_specs=[pl.BlockSpec((None, TQ, W), lambda b, i: (b, i, q_col_block)),
                  pl.BlockSpec((None, TQ, LANES), lambda b, i: (b, i, GATE_COL_BLOCK)),
                  const(gate_bias),
                  pl.BlockSpec((None, G, n_cb, d), lambda b, i: (b, 0, 0, 0)),
                  pl.BlockSpec((None, G, d, n_cb), lambda b, i: (b, 0, 0, 0)),
                  kspec(0), vspec(0), kspec(1), vspec(1), const(ovl), const(aug), const(caug), const(tail),
                  const(ctail), const(wmask)],
        out_specs=pl.BlockSpec((None, TQ, W), lambda b, i: (b, i, 0)),
        out_shape=jax.ShapeDtypeStruct((B, S, NSA_W), BF16),
        scratch_shapes=[pltpu.VMEM((G, S, kaug), BF16), pltpu.VMEM((G, S, kaug), BF16),
                        pltpu.VMEM((G, n_cb, 2 * d), BF16),
                        pltpu.VMEM((G, d + NSA_ONES, S), BF16), pltpu.VMEM((G, d + NSA_ONES, S), BF16),
                        pltpu.VMEM((G, 2, NSA_TK, R), F32), pltpu.VMEM((G, 2, NSA_TK, R), BF16),
                        pltpu.VMEM((G, n_win, NSA_TK, R), F32), pltpu.VMEM((G, n_win, NSA_TK, R), BF16)],
        compiler_params=_cparams(("parallel", "arbitrary")), name="nsa",
    )(zb, zf, gate_bias, kcmp, vcmp_t, zb, zt, zb, zt, ovl, aug, caug, tail, ctail, wmask)


def _rglru_kernel(h_ref, hh_ref, g_ref, wg_ref, wi_ref, cw_ref, cb_ref, wa_ref, wx_ref, ba_ref, bx_ref, lam_ref,
                  o_ref, h_s, au_s):
    NB, T, W = o_ref.shape
    first = pl.program_id(1) == 0

    @pl.when(first)
    def _():
        h_s[...] = jnp.zeros_like(h_s)

    def project(nb):
        xn = _rms(h_ref[nb], g_ref[...]).astype(BF16)
        x_halo = _dot(_rms(hh_ref[nb], g_ref[...]).astype(BF16), wi_ref[...])
        xe = jnp.concatenate([jnp.where(first, 0.0, x_halo), _dot(xn, wi_ref[...])], axis=0)
        return _dot(xn, wg_ref[...]), xe

    def recur(nb, gate, xe):
        xc = cb_ref[...] + cw_ref[3:4, :] * xe
        for k in range(1, LRU_CONV):
            xc = xc + cw_ref[3 - k:4 - k, :] * pltpu.roll(xe, k, 0)
        xc = xc[LRU_HALO:, :]
        xcb = xc.astype(BF16)
        half = W // 2

        def blockdiag(w_ref):
            return jnp.concatenate([_dot(xcb[:, :half], w_ref[0]), _dot(xcb[:, half:], w_ref[1])], axis=1)

        r = _sigmoid(blockdiag(wa_ref) + ba_ref[...])
        i = _sigmoid(blockdiag(wx_ref) + bx_ref[...])
        nl = -lam_ref[...]
        softplus = jnp.maximum(nl, 0.0) + jnp.log1p(jnp.exp(-jnp.abs(nl)))
        log_a = -LRU_C * r * softplus
        a = jnp.exp(log_a)
        one_m_a2 = -jnp.tanh(log_a) * (a * a + 1.0)
        u = jnp.where(one_m_a2 > 0.0, one_m_a2 * lax.rsqrt(one_m_a2), 0.0) * (i * xc)

        n_grp, n_slab = T // 8, W // LANES

        def phases(x, slab):
            for c in range(n_slab):
                au_s[nb, slab, c] = x[:, c * LANES:(c + 1) * LANES]
            return [jnp.concatenate([au_s[nb, slab, c, pl.ds(j, n_grp, stride=8), :] for c in range(n_slab)], axis=1)
                    for j in range(8)]

        a_ph, u_ph = phases(a, 0), phases(u, 1)
        prod, part = [a_ph[0]], [u_ph[0]]
        for j in range(1, 8):
            part.append(a_ph[j] * part[-1] + u_ph[j])
            prod.append(a_ph[j] * prod[-1])
        grp = lax.broadcasted_iota(jnp.int32, (n_grp, W), 0)
        ag, ug = prod[7], part[7]
        sft = 1
        while sft < n_grp:
            keep = grp >= sft
            ug = ag * jnp.where(keep, pltpu.roll(ug, sft, 0), 0.0) + ug
            ag = ag * jnp.where(keep, pltpu.roll(ag, sft, 0), 1.0)
            sft *= 2
        h_prev = h_s[nb, 0:1, :]
        hg = ug + ag * h_prev
        carry_in = jnp.where(grp >= 1, pltpu.roll(hg, 1, 0), h_prev)
        for j in range(8):
            hj = part[j] + prod[j] * carry_in
            for c in range(n_slab):
                au_s[nb, 0, c, pl.ds(j, n_grp, stride=8), :] = hj[:, c * LANES:(c + 1) * LANES]
        h = jnp.concatenate([au_s[nb, 0, c] for c in range(n_slab)], axis=1)
        h_s[nb] = jnp.broadcast_to(hg[n_grp - 1:n_grp, :], h_s.shape[1:])
        o_ref[nb] = (_gelu(gate) * h).astype(o_ref.dtype)

    nxt = project(0)
    for nb in range(NB):
        cur = nxt
        if nb + 1 < NB:
            nxt = project(nb + 1)
        recur(nb, *cur)


def rglru_block(h3, pre_g, w_gate, w_in, conv_w, conv_b, wa, ba, wx, bx, lam, T=LRU_T, NB=LRU_NB):
    B, S, D = h3.shape
    W = LRU_W
    half = W // 2
    hb = T // LRU_HALO

    def bd(w):
        blocks = [jax.scipy.linalg.block_diag(*[w[h] for h in range(4 * j, 4 * j + 4)]) for j in range(2)]
        return jnp.stack(blocks).astype(BF16)

    vec = lambda a: a.reshape(1, W)
    assert B % NB == 0 and S % T == 0
    vspec = pl.BlockSpec((1, W), lambda b, t: (0, 0))
    wspec = pl.BlockSpec((2, half, half), lambda b, t: (0, 0, 0))
    return pl.pallas_call(
        _rglru_kernel, grid=(B // NB, S // T),
        in_specs=[pl.BlockSpec((NB, T, D), lambda b, t: (b, t, 0)),
                  pl.BlockSpec((NB, LRU_HALO, D), lambda b, t: (b, jnp.maximum(t * hb - 1, 0), 0)),
                  pl.BlockSpec((1, D), lambda b, t: (0, 0)),
                  pl.BlockSpec((D, W), lambda b, t: (0, 0)), pl.BlockSpec((D, W), lambda b, t: (0, 0)),
                  pl.BlockSpec((LRU_CONV, W), lambda b, t: (0, 0)), vspec, wspec, wspec, vspec, vspec, vspec],
        out_specs=pl.BlockSpec((NB, T, W), lambda b, t: (b, t, 0)),
        out_shape=jax.ShapeDtypeStruct((B, S, W), BF16),
        scratch_shapes=[pltpu.VMEM((NB, 8, W), F32), pltpu.VMEM((NB, 2, W // LANES, T, LANES), F32)],
        compiler_params=_cparams(("parallel", "arbitrary")), name="rglru",
    )(h3, h3, pre_g.reshape(1, D), w_gate, w_in, conv_w, vec(conv_b), bd(wa), bd(wx), vec(ba), vec(bx), vec(lam))


def _sgu_kernel(h_ref, gpre_ref, wu_ref, wv_ref, g_ref, b_ref, w_ref, bias_ref, o_ref):
    C, W = SG_CHUNK, o_ref.shape[1]
    dg = W // SG_GROUPS
    n_sub = o_ref.shape[0] // SG_SUB
    row = lax.broadcasted_iota(jnp.int32, (C, C), 0)
    col = lax.broadcasted_iota(jnp.int32, (C, C), 1)
    wc = [jnp.where(col <= row, w_ref[gi], 0.0).astype(BF16) for gi in range(SG_GROUPS)]

    def project(j):
        xn = _rms(h_ref[j * SG_SUB:(j + 1) * SG_SUB, :], gpre_ref[...]).astype(BF16)
        return _dot(xn, wu_ref[...]), _dot(xn, wv_ref[...])

    def gate(j, u, v):
        u, v = _gelu(u), _gelu(v)
        mu = jnp.mean(v, axis=-1, keepdims=True)
        vc = v - mu
        vn = (vc * lax.rsqrt(jnp.mean(vc * vc, axis=-1, keepdims=True) + EPS) * g_ref[...] + b_ref[...]).astype(BF16)
        for gi in range(SG_GROUPS):
            sl = slice(gi * dg, (gi + 1) * dg)
            for c in range(SG_SUB // C):
                rows = slice(c * C, (c + 1) * C)
                mixed = _dot(wc[gi], vn[rows, sl]) + bias_ref[:, sl]
                o_ref[j * SG_SUB + c * C:j * SG_SUB + (c + 1) * C, sl] = (u[rows, sl] * mixed).astype(o_ref.dtype)

    nxt = project(0)
    for j in range(n_sub):
        cur = nxt
        if j + 1 < n_sub:
            nxt = project(j + 1)
        gate(j, *cur)


def sgu_block(h3, pre_g, w_u, w_v, ln_g, ln_b, w, b):
    B, S, D = h3.shape
    W, C, T = SG_W, SG_CHUNK, SG_ROWS
    assert S % T == 0 and T % SG_SUB == 0 and SG_SUB % C == 0
    bias = jnp.repeat(b.T, W // SG_GROUPS, axis=1)
    vspec = pl.BlockSpec((1, W), lambda bb, c: (0, 0))
    return pl.pallas_call(
        _sgu_kernel, grid=(B, S // T),
        in_specs=[pl.BlockSpec((None, T, D), lambda bb, c: (bb, c, 0)),
                  pl.BlockSpec((1, D), lambda bb, c: (0, 0)),
                  pl.BlockSpec((D, W), lambda bb, c: (0, 0)), pl.BlockSpec((D, W), lambda bb, c: (0, 0)),
                  vspec, vspec,
                  pl.BlockSpec((SG_GROUPS, C, C), lambda bb, c: (0, 0, 0)),
                  pl.BlockSpec((C, W), lambda bb, c: (0, 0))],
        out_specs=pl.BlockSpec((None, T, W), lambda bb, c: (bb, c, 0)),
        out_shape=jax.ShapeDtypeStruct((B, S, W), BF16),
        compiler_params=_cparams(("parallel", "parallel")), name="sgu",
    )(h3, pre_g.reshape(1, D), w_u, w_v, ln_g.reshape(1, W), ln_b.reshape(1, W), w, bias)


def _mixer_ab(h2, B, S, pre_g, w_in, ml_gate_b, ml_norm_g, nsa_gate_b, k_pe, k_w1, k_w2, v_pe, v_w1, v_w2):
    D = h2.shape[1]
    G, d = NSA_KV_HEADS, HEAD_DIM
    offs = np.cumsum([0, ML_W, ML_W, ML_W, ML_W, 2 * ML_HEADS, NSA_W] + [G * d] * 6 + [3 * NSA_HEADS])
    mq, mk, mv, mo, mif, nq, kc, vc, ks, vs, kw, vw, ng = [w_in[:, offs[i]:offs[i + 1]] for i in range(13)]
    w_b = jnp.concatenate([mk, nq, ks, kw], axis=1).astype(BF16)
    w_t = jnp.concatenate([mq, mv, vs, vw], axis=1).T.astype(BF16)
    gpad = LANES - 2 * ML_HEADS - 3 * NSA_HEADS
    w_f = jnp.concatenate([mo, kc, vc, mif, ng, jnp.zeros((D, gpad), w_in.dtype)], axis=1).astype(BF16)
    gate_bias = jnp.concatenate([ml_gate_b, nsa_gate_b, jnp.zeros((gpad,), F32)]).reshape(1, LANES)
    zb, zf, zt = norm_proj(h2, pre_g, [w_b, w_f], [BF16, F32], wts=[w_t], batch=B)
    zb = zb.reshape(B, S, -1)
    zf = zf.reshape(B, S, -1)
    h_ml = mlstm_block(zt, zb, zf, gate_bias, ml_norm_g)
    kcmp, vcmp_t = compress_block(zf, ML_W // (G * d), ML_W // (G * d) + 1, k_pe, k_w1, k_w2, v_pe, v_w1, v_w2)
    h_nsa = nsa_block(zt, zb, zf, gate_bias, kcmp, vcmp_t, q_col_block=ML_W // NSA_W,
                      k_col_blocks=((ML_W + NSA_W) // (G * d), (ML_W + NSA_W) // (G * d) + 1),
                      v_row_blocks=(2 * ML_W // (G * d), 2 * ML_W // (G * d) + 1))
    return h_ml.reshape(B * S, ML_W), h_nsa.reshape(B * S, NSA_W)


def _mixer_cd(h2, B, S, pre_g, w_in, conv_w, conv_b, wa, ba, wx, bx, lam, sg_g, sg_bn, sg_w, sg_b):
    h3 = h2.reshape(B, S, -1)
    w_gate, w_x, w_u, w_v = (w_in[:, j * LRU_W:(j + 1) * LRU_W].astype(BF16) for j in range(4))
    y_lru = rglru_block(h3, pre_g, w_gate, w_x, conv_w, conv_b, wa, ba, wx, bx, lam)
    y_sg = sgu_block(h3, pre_g, w_u, w_v, sg_g, sg_bn, sg_w, sg_b)
    return y_lru.reshape(B * S, LRU_W), y_sg.reshape(B * S, SG_W)


def kernel(x, pre_mix_g, post_mix_g, pre_ffn_g, post_ffn_g, ab_w_in, ab_w_out, ml_gate_b, ml_norm_g, nsa_gate_b, cmp_k_pe, cmp_k_w1, cmp_k_w2, cmp_v_pe, cmp_v_w1, cmp_v_w2, cd_w_in, cd_w_out, lru_conv_w, lru_conv_b, lru_wa, lru_ba, lru_wx, lru_bx, lru_lambda, sg_norm_g, sg_norm_b, sg_w, sg_b, ffn_w_up, ffn_conv_w, ffn_conv_b, ffn_w_down):
    B, S, D = x.shape
    depth = pre_mix_g.shape[0]
    h2 = x.reshape(B * S, D)
    for layer in range(depth):
        if layer % 2 == 0:
            e = layer // 2
            a1, a2 = _mixer_ab(h2, B, S, pre_mix_g[layer], ab_w_in[e], ml_gate_b[e], ml_norm_g[e], nsa_gate_b[e],
                               cmp_k_pe[e], cmp_k_w1[e], cmp_k_w2[e], cmp_v_pe[e], cmp_v_w1[e], cmp_v_w2[e])
            w_out = ab_w_out[e]
        else:
            o = layer // 2
            a1, a2 = _mixer_cd(h2, B, S, pre_mix_g[layer], cd_w_in[o], lru_conv_w[o], lru_conv_b[o], lru_wa[o],
                               lru_ba[o], lru_wx[o], lru_bx[o], lru_lambda[o], sg_norm_g[o], sg_norm_b[o], sg_w[o],
                               sg_b[o])
            w_out = cd_w_out[o]
        h2 = mix_ffn_block(h2, a1, a2, S, w_out, post_mix_g[layer], pre_ffn_g[layer], ffn_w_up[layer],
                           ffn_conv_w[layer], ffn_conv_b[layer], ffn_w_down[layer], post_ffn_g[layer])
    return h2.reshape(B, S, D)
```

```python
import functools

import numpy as np
import jax
import jax.numpy as jnp
from jax import lax
from jax.experimental import pallas as pl
from jax.experimental.pallas import tpu as pltpu

F32 = jnp.float32
BF16 = jnp.bfloat16

EPS = 1e-6
HEAD_DIM = 64
ML_HEADS = 8
ML_W = 512
GATE_SOFTCAP = 15.0
NSA_HEADS = 8
NSA_KV_HEADS = 2
NSA_HPG = NSA_HEADS // NSA_KV_HEADS
NSA_W = 512
CMP_LEN = 32
CMP_STRIDE = 16
CMP_HID = 128
SEL_LEN = 64
SEL_TOPN = 16
WINDOW = 512
LRU_W = 512
LRU_C = 8.0
LRU_CONV = 4
SG_GROUPS = 8
SG_W = 512
SG_CHUNK = 128
FFN_CONV = 3
GATE_COL_BLOCK = (ML_W + 2 * NSA_KV_HEADS * HEAD_DIM) // 128
NSA_GATE_LANE0 = 2 * ML_HEADS

LANES = 128
VMEM_LIMIT = 56 * 1024 * 1024
NEG_BIG = -1e30
HIGHEST = lax.Precision.HIGHEST
LOG2E = 1.4426950408889634


def _bf16_terms(x, n):
    terms = []
    for _ in range(n):
        bits = int(np.array(x, np.float32).view(np.uint32))
        t = float(np.array((bits + 0x7FFF + ((bits >> 16) & 1)) & 0xFFFF0000, np.uint32).view(np.float32))
        terms.append(t)
        x -= t
    return tuple(terms)


LOG2E_TERMS = _bf16_terms(LOG2E, 3)

ML_CHUNK = 128
ML_NB = 8
NSA_TQ = 256
NSA_TK = 256
RANK_STEP = 16
NSA_ONES = 16
ROW_TILE = 512
FFN_TM = 512
FFN_SUB = 256
FFN_CK = 256
FFN_HALO = 16
SG_ROWS = 1024
SG_SUB = 256
LRU_T = 512
LRU_HALO = 8
LRU_NB = 4


def _cparams(sem, n_in=0, fuse=()):
    fusion = [i in fuse for i in range(n_in)] if fuse else None
    return pltpu.CompilerParams(dimension_semantics=sem, vmem_limit_bytes=VMEM_LIMIT, allow_input_fusion=fusion)


def _rms(x, g):
    return x * lax.rsqrt(jnp.mean(x * x, axis=-1, keepdims=True) + EPS) * g


def _gelu(x):
    return 0.5 * x * (1.0 + jnp.tanh(0.7978845608028654 * (x + 0.044715 * (x * x * x))))


def _sigmoid(x):
    return 1.0 / (1.0 + jnp.exp(-x))


def _dot(a, b):
    return jnp.dot(a, b, preferred_element_type=F32)


def _dot_nt(a, b, precision=None):
    return lax.dot_general(a, b, (((1,), (1,)), ((), ())), preferred_element_type=F32, precision=precision)


def _norm_proj_kernel(h_ref, g_ref, *refs, n_row, n_t, cn):
    w_refs, wt_refs = refs[:n_row], refs[n_row:n_row + n_t]
    o_refs, ot_refs = refs[n_row + n_t:2 * n_row + n_t], refs[2 * n_row + n_t:]
    xn = _rms(h_ref[...], g_ref[...]).astype(BF16)
    for w_ref, o_ref in zip(w_refs, o_refs):
        n = w_ref.shape[1]
        for c in range(0, n, cn):
            ce = min(c + cn, n)
            o_ref[:, c:ce] = _dot(xn, w_ref[:, c:ce]).astype(o_ref.dtype)
    for wt_ref, ot_ref in zip(wt_refs, ot_refs):
        n = wt_ref.shape[0]
        for c in range(0, n, cn):
            ce = min(c + cn, n)
            ot_ref[c:ce, :] = _dot_nt(wt_ref[c:ce, :], xn).astype(ot_ref.dtype)


def norm_proj(h2, g, ws, dtypes, wts=(), batch=1, tm=ROW_TILE):
    M, D = h2.shape
    tps = M // batch // tm
    in_specs = [pl.BlockSpec((tm, D), lambda i: (i, 0)), pl.BlockSpec((1, D), lambda i: (0, 0))]
    in_specs += [pl.BlockSpec(w.shape, lambda i: (0, 0)) for w in (*ws, *wts)]
    out_specs = [pl.BlockSpec((tm, w.shape[1]), lambda i: (i, 0)) for w in ws]
    out_specs += [pl.BlockSpec((None, w.shape[0], tm), lambda i: (i // tps, 0, i % tps)) for w in wts]
    out_shape = [jax.ShapeDtypeStruct((M, w.shape[1]), dt) for w, dt in zip(ws, dtypes)]
    out_shape += [jax.ShapeDtypeStruct((batch, w.shape[0], M // batch), BF16) for w in wts]
    return pl.pallas_call(
        functools.partial(_norm_proj_kernel, n_row=len(ws), n_t=len(wts), cn=512),
        grid=(M // tm,), in_specs=in_specs, out_specs=out_specs, out_shape=out_shape,
        compiler_params=_cparams(("parallel",), 2 + len(ws) + len(wts), range(2, 2 + len(ws) + len(wts))),
        name="norm_proj",
    )(h2, g.reshape(1, D), *ws, *wts)


def _mix_ffn_kernel(h_ref, a1_ref, a2_ref, wo1_ref, wo2_ref, gmix_ref, gpre_ref, wu_ref, cw_ref, cb_ref, wd_ref,
                    gpost_ref, o_ref, xn_s, acc_s, ut_s, *, tiles_per_seq, ck):
    F = wd_ref.shape[0]
    n_chunks = F // ck
    n_sub = h_ref.shape[0] // FFN_SUB
    g = gpre_ref[...]

    @pl.when(pl.program_id(0) % tiles_per_seq == 0)
    def _():
        ut_s[...] = jnp.zeros_like(ut_s)

    def mixed(h, a1, a2):
        return h + _rms(_dot(a1, wo1_ref[...]) + _dot(a2, wo2_ref[...]), gmix_ref[...])

    def conv(u, cols):
        y = (cw_ref[2:3, cols] * u + cw_ref[1:2, cols] * pltpu.roll(u, 1, 0) + cw_ref[0:1, cols] * pltpu.roll(u, 2, 0)
             + cb_ref[:, cols])
        return y[FFN_HALO:, :]

    def cols_of(c, half):
        return slice(half * F + c * ck, half * F + (c + 1) * ck)

    x = {}

    def head(j):
        rows = slice(j * FFN_SUB, (j + 1) * FFN_SUB)
        x[j] = mixed(h_ref[rows, :], a1_ref[rows, :], a2_ref[rows, :])
        xn_s[rows, :] = _rms(x[j], g).astype(BF16)
        acc_s[rows, :] = jnp.zeros((FFN_SUB, acc_s.shape[1]), F32)

    def up(j, c):
        xn = xn_s[j * FFN_SUB:(j + 1) * FFN_SUB, :]
        u = [jnp.concatenate([ut_s[c, half], _dot(xn, wu_ref[:, cols_of(c, half)])], axis=0) for half in range(2)]
        for half in range(2):
            ut_s[c, half] = u[half][FFN_SUB:, :]
        return u

    def tail(j):
        rows = slice(j * FFN_SUB, (j + 1) * FFN_SUB)
        o_ref[rows, :] = x[j] + _rms(acc_s[rows, :], gpost_ref[...])

    head(0)
    for j in range(n_sub):
        u = up(j, 0)
        for c in range(n_chunks):
            u_next = up(j, c + 1) if c + 1 < n_chunks else None
            if c == 0 and j + 1 < n_sub:
                head(j + 1)
            if c == n_chunks // 3 and j >= 1:
                tail(j - 1)
            act = (_gelu(conv(u[0], cols_of(c, 0))) * conv(u[1], cols_of(c, 1))).astype(BF16)
            acc_s[j * FFN_SUB:(j + 1) * FFN_SUB, :] += _dot(act, wd_ref[c * ck:(c + 1) * ck, :])
            u = u_next
    tail(n_sub - 1)


def mix_ffn_block(h2, a1, a2, seq_len, w_out, g_mix, g_pre, w_up, conv_w, conv_b, w_down, g_post, tm=FFN_TM, ck=FFN_CK):
    M, D = h2.shape
    F = w_down.shape[0]
    K1, K2 = a1.shape[1], a2.shape[1]
    assert F % ck == 0 and seq_len % tm == 0 and tm % FFN_SUB == 0 and FFN_SUB % FFN_HALO == 0
    assert conv_w.shape[0] == FFN_CONV and FFN_CONV - 1 <= FFN_HALO
    consts = (w_out[:K1].astype(BF16), w_out[K1:].astype(BF16), g_mix.reshape(1, D), g_pre.reshape(1, D),
              w_up.astype(BF16), conv_w, conv_b.reshape(1, 2 * F), w_down.astype(BF16), g_post.reshape(1, D))

    def tile(width):
        return pl.BlockSpec((tm, width), lambda i: (i, 0))

    def full(a):
        return pl.BlockSpec(a.shape, lambda i: (0,) * a.ndim)

    return pl.pallas_call(
        functools.partial(_mix_ffn_kernel, tiles_per_seq=seq_len // tm, ck=ck),
        grid=(M // tm,),
        in_specs=[tile(D), tile(K1), tile(K2)] + [full(a) for a in consts],
        out_specs=tile(D),
        out_shape=jax.ShapeDtypeStruct((M, D), F32),
        scratch_shapes=[pltpu.VMEM((tm, D), BF16), pltpu.VMEM((tm, D), F32),
                        pltpu.VMEM((F // ck, 2, FFN_HALO, ck), F32)],
        compiler_params=_cparams(("arbitrary",), 3 + len(consts), (3, 4)), name="mix_ffn",
    )(h2, a1, a2, *consts)


def _mlstm_kernel(qt_ref, vt_ref, k_ref, mo_ref, gt_ref, gb_ref, ng_ref, o_ref, cn_s, m_s):
    NB, _, L = qt_ref.shape
    d = HEAD_DIM

    @pl.when(pl.program_id(1) == 0)
    def _():
        cn_s[...] = jnp.zeros_like(cn_s)
        m_s[...] = jnp.zeros_like(m_s)

    src = lax.broadcasted_iota(jnp.int32, (L, L), 0)
    tgt = lax.broadcasted_iota(jnp.int32, (L, L), 1)
    causal = src <= tgt
    tri = (tgt <= src).astype(F32)

    gcap = [GATE_SOFTCAP * jnp.tanh((gt_ref[nb] + gb_ref[...]) * (1.0 / GATE_SOFTCAP)) for nb in range(NB)]
    lf = [jnp.minimum(x, 0.0) - jnp.log1p(jnp.exp(-jnp.abs(x))) for x in gcap]
    b_col = [jnp.dot(tri, x, preferred_element_type=F32, precision=HIGHEST) for x in lf]
    b_row = [x.T for x in b_col]
    i_row = [x.T for x in gcap]
    c_col = [b_col[nb] - pltpu.roll(gcap[nb], ML_HEADS, 1) for nb in range(NB)]

    chains = [(nb, h) for nb in range(NB) for h in range(ML_HEADS)]
    n = range(len(chains))

    def rows(h):
        return slice(h * d, (h + 1) * d)

    q_t = [qt_ref[nb, rows(h), :] for nb, h in chains]
    v_t = [vt_ref[nb, rows(h), :] for nb, h in chains]
    k = [k_ref[nb, :, rows(h)] * 0.125 for nb, h in chains]
    br = [b_row[nb][ML_HEADS + h:ML_HEADS + h + 1, :] for nb, h in chains]
    ir = [i_row[nb][h:h + 1, :] for nb, h in chains]
    g = [x[:, L - 1:L] for x in br]
    m_prev = [m_s[nb, h:h + 1, 0:1] for nb, h in chains]
    cn_prev = [cn_s[nb, h] for nb, h in chains]
    kq = [_dot(k[c], q_t[c]) for c in n]
    carry = [_dot(cn_prev[c].astype(BF16), q_t[c]) for c in n]
    dlog = [jnp.where(causal, br[c] - c_col[nb][:, ML_HEADS + h:ML_HEADS + h + 1], -jnp.inf)
            for c, (nb, h) in enumerate(chains)]
    inter = [br[c] + m_prev[c] for c in n]
    m_row = [jnp.maximum(inter[c], jnp.max(dlog[c], axis=0, keepdims=True)) for c in n]
    s = [kq[c] * jnp.exp(dlog[c] - m_row[c]) for c in n]
    w_inter = [jnp.exp(inter[c] - m_row[c]) for c in n]
    num = [_dot(v_t[c], s[c].astype(BF16)) + w_inter[c] * carry[c][0:d] for c in n]
    den = [jnp.sum(s[c], axis=0, keepdims=True) + w_inter[c] * carry[c][d:d + 1] for c in n]
    hh = [num[c] * (1.0 / jnp.maximum(jnp.abs(den[c]), jnp.exp(-m_row[c]))) for c in n]
    wlog = [g[c] - br[c] + ir[c] for c in n]
    m_new = [jnp.maximum(g[c] + m_prev[c], jnp.max(wlog[c], axis=-1, keepdims=True)) for c in n]
    w_row = [jnp.exp(wlog[c] - m_new[c]) for c in n]
    decay = [jnp.exp(g[c] + m_prev[c] - m_new[c]) for c in n]
    for c, (nb, h) in enumerate(chains):
        vw = jnp.concatenate([v_t[c].astype(F32) * w_row[c], jnp.broadcast_to(w_row[c], (8, L))], axis=0).astype(BF16)
        cn_s[nb, h] = decay[c] * cn_prev[c] + _dot(vw, k[c])
        m_s[nb, h:h + 1, :] = jnp.broadcast_to(m_new[c], (1, m_s.shape[2]))
    outs = [hh[c] * lax.rsqrt(jnp.mean(hh[c] * hh[c], axis=0, keepdims=True) + EPS) * ng_ref[rows(h), :]
            for c, (nb, h) in enumerate(chains)]
    for nb in range(NB):
        out_t = jnp.concatenate(outs[nb * ML_HEADS:(nb + 1) * ML_HEADS], axis=0)
        o_ref[nb] = (out_t.T * _sigmoid(mo_ref[nb])).astype(o_ref.dtype)


def mlstm_block(zt, zb, zf, gate_bias, norm_g, L=ML_CHUNK, NB=ML_NB):
    B, S, _ = zb.shape
    W, H, d = ML_W, ML_HEADS, HEAD_DIM
    assert B % NB == 0 and S % L == 0
    ng = jnp.broadcast_to(norm_g.reshape(W, 1), (W, L))
    return pl.pallas_call(
        _mlstm_kernel, grid=(B // NB, S // L),
        in_specs=[pl.BlockSpec((NB, W, L), lambda b, c: (b, 0, c)),
                  pl.BlockSpec((NB, W, L), lambda b, c: (b, 1, c)),
                  pl.BlockSpec((NB, L, W), lambda b, c: (b, c, 0)),
                  pl.BlockSpec((NB, L, W), lambda b, c: (b, c, 0)),
                  pl.BlockSpec((NB, L, LANES), lambda b, c: (b, c, GATE_COL_BLOCK)),
                  pl.BlockSpec((1, LANES), lambda b, c: (0, 0)),
                  pl.BlockSpec((W, L), lambda b, c: (0, 0))],
        out_specs=pl.BlockSpec((NB, L, W), lambda b, c: (b, c, 0)),
        out_shape=jax.ShapeDtypeStruct((B, S, W), BF16),
        scratch_shapes=[pltpu.VMEM((NB, H, d + 8, d), F32), pltpu.VMEM((NB, H, LANES), F32)],
        compiler_params=_cparams(("parallel", "arbitrary")), name="mlstm",
    )(zt, zt, zb, zf, zf, gate_bias, ng)


def _compress_kernel(kc_ref, vc_ref, pek_ref, pev_ref, w1k_ref, w1v_ref, w2k_ref, w2v_ref, ok_ref, ov_ref):
    G = NSA_KV_HEADS
    nh = kc_ref.shape[0] // CMP_STRIDE

    def one(x_ref, pe_ref, w1_ref, w2_ref):
        hid = None
        for l in range(CMP_STRIDE):
            y = x_ref[pl.ds(l, nh, stride=CMP_STRIDE), :]
            ya = (y + pe_ref[l:l + 1, :]).astype(BF16)
            yb = (pltpu.roll(y, nh - 1, 0) + pe_ref[CMP_STRIDE + l:CMP_STRIDE + l + 1, :]).astype(BF16)
            t = _dot(ya, w1_ref[l]) + _dot(yb, w1_ref[CMP_STRIDE + l])
            hid = t if hid is None else hid + t
        hid = _gelu(hid).astype(BF16)
        return [_dot(hid[:, gi * CMP_HID:(gi + 1) * CMP_HID], w2_ref[...]) for gi in range(G)]

    for gi, (ko, vo) in enumerate(zip(one(kc_ref, pek_ref, w1k_ref, w2k_ref), one(vc_ref, pev_ref, w1v_ref, w2v_ref))):
        ok_ref[gi] = ko.astype(ok_ref.dtype)
        ov_ref[gi] = vo.T.astype(ov_ref.dtype)


def compress_block(zf, kc_block, vc_block, k_pe, k_w1, k_w2, v_pe, v_w1, v_w2):
    B, S, _ = zf.shape
    G, d = NSA_KV_HEADS, HEAD_DIM
    nh = S // CMP_STRIDE

    def prep(pe, w1):
        w1bd = jnp.zeros((CMP_LEN, G * d, G * CMP_HID), w1.dtype)
        for gi in range(G):
            w1bd = w1bd.at[:, gi * d:(gi + 1) * d, gi * CMP_HID:(gi + 1) * CMP_HID].set(w1)
        return jnp.tile(pe, (1, G)), w1bd.astype(BF16)

    pek, w1k = prep(k_pe, k_w1)
    pev, w1v = prep(v_pe, v_w1)

    def full(a):
        return pl.BlockSpec(a.shape, lambda b: (0,) * a.ndim)

    w2k, w2v = k_w2.astype(BF16), v_w2.astype(BF16)
    return pl.pallas_call(
        _compress_kernel, grid=(B,),
        in_specs=[pl.BlockSpec((None, S, G * d), lambda b: (b, 0, kc_block)),
                  pl.BlockSpec((None, S, G * d), lambda b: (b, 0, vc_block)),
                  full(pek), full(pev), full(w1k), full(w1v), full(w2k), full(w2v)],
        out_specs=[pl.BlockSpec((None, G, nh, d), lambda b: (b, 0, 0, 0)),
                   pl.BlockSpec((None, G, d, nh), lambda b: (b, 0, 0, 0))],
        out_shape=[jax.ShapeDtypeStruct((B, G, nh, d), BF16), jax.ShapeDtypeStruct((B, G, d, nh), BF16)],
        compiler_params=_cparams(("parallel",)), name="nsa_compress",
    )(zf, zf, pek, pev, w1k, w1v, w2k, w2v)


def _nsa_kernel(q_ref, gt_ref, gb_ref, kc_ref, vct_ref, ks_ref, vst_ref, kw_ref, vwt_ref, ovl_ref, aug_ref, caug_ref,
                tail_ref, ctail_ref, wmask_ref, o_ref, ksa_s, kwa_s, kca_s, vsa_s, vwa_s, ss_s, ps_s, sw_s, pw_s, *, n_cmp):
    TQ, TK, d, HPG, G = NSA_TQ, NSA_TK, HEAD_DIM, NSA_HPG, NSA_KV_HEADS
    R = HPG * TQ
    groups = range(G)
    qi = pl.program_id(1)
    q0 = qi * TQ
    kt_d = q0 // TK
    n_cb = kc_ref.shape[1]
    n_sb = ovl_ref.shape[0]

    @pl.when(qi == 0)
    def _():
        for g in groups:
            ksa_s[g, :, 0:d] = ks_ref[:, g * d:(g + 1) * d]
            ksa_s[g, :, d:] = aug_ref[...]
            kwa_s[g, :, 0:d] = kw_ref[:, g * d:(g + 1) * d]
            kwa_s[g, :, d:] = aug_ref[...]
            kca_s[g, :, 0:d] = kc_ref[g]
            kca_s[g, :, d:] = caug_ref[...]
            vsa_s[g, 0:d, :] = vst_ref[g * d:(g + 1) * d, :]
            vsa_s[g, d:, :] = jnp.ones((vsa_s.shape[1] - d, vsa_s.shape[2]), BF16)
            vwa_s[g, 0:d, :] = vwt_ref[g * d:(g + 1) * d, :]
            vwa_s[g, d:, :] = jnp.ones((vwa_s.shape[1] - d, vwa_s.shape[2]), BF16)

    def slope(g, hh):
        return 2.0 ** (-(g * HPG + hh + 1))

    def per_head(fn):
        return jnp.concatenate([fn(hh) for hh in range(HPG)], axis=1)

    def tile_heads(x):
        return jnp.concatenate([x] * HPG, axis=1)

    q_t = [(q_ref[:, g * HPG * d:(g + 1) * HPG * d].astype(F32) * (LOG2E * 0.125)).T for g in groups]

    def q_head(g, hh):
        return q_t[g][hh * d:(hh + 1) * d]

    def q_aug(g, block_rows):
        return per_head(lambda hh: jnp.concatenate(
            [q_head(g, hh), block_rows, tail_ref[...] * slope(g, hh)], axis=0)).astype(BF16)

    def scores(ka, qa_t, kt):
        return _dot(ka[pl.ds(pl.multiple_of(kt * TK, TK), TK), :], qa_t)

    def stage_scores(buf, ka, qa_t, kt, mask_add=None):
        s = scores(ka, qa_t, kt)
        if mask_add is not None:
            s = s + mask_add
        buf[...] = s
        return jnp.max(s, axis=0, keepdims=True)

    def stage_probs(sbuf, pbuf, tile_max, m):
        m_new = jnp.maximum(m, tile_max)
        pbuf[...] = jnp.exp2(sbuf[...] - m_new).astype(BF16)
        return m_new, jnp.exp2(m - m_new)

    def stage_values(pbuf, va, kt, alpha, acc):
        return alpha * acc + _dot(va[:, pl.ds(pl.multiple_of(kt * TK, TK), TK)], pbuf[...])

    def normalised(acc):
        return acc[0:d] * (1.0 / acc[d:d + 1])

    m0, acc0 = jnp.full((1, R), NEG_BIG, F32), jnp.zeros((vsa_s.shape[1], R), F32)

    qc_t = [per_head(lambda hh: jnp.concatenate([q_head(g, hh), ctail_ref[...] * slope(g, hh)], axis=0)).astype(BF16)
            for g in groups]
    qw_t = [q_aug(g, jnp.zeros((n_sb, TQ), F32)) for g in groups]
    n_r = lax.broadcasted_iota(jnp.int32, (n_cb, TQ), 0)
    t_c = q0 + lax.broadcasted_iota(jnp.int32, (n_cb, TQ), 1)
    ok_c = (n_r * CMP_STRIDE + (CMP_LEN - 1) <= t_c) & (n_r < n_cmp)
    add_c = tile_heads(jnp.where(ok_c, 0.0, NEG_BIG))
    s_c = [_dot(kca_s[g], qc_t[g]) + add_c for g in groups]
    n_win = (WINDOW - 1 + TK - 1) // TK + 1
    kt_win, max_win = [], []
    for back in range(n_win):
        kt_raw = kt_d - back
        kt_win.append(jnp.maximum(kt_raw, 0))
        if back == 0:
            mask_add = tile_heads(wmask_ref[0])
        elif (back + 1) * TK <= WINDOW:
            mask_add = jnp.where(kt_raw >= 0, 0.0, NEG_BIG)
        else:
            mask_add = tile_heads(wmask_ref[back] + jnp.where(kt_raw >= 0, 0.0, NEG_BIG))
        max_win.append([stage_scores(sw_s.at[g, back], kwa_s.at[g], qw_t[g], kt_win[back], mask_add) for g in groups])

    e_c = [jnp.exp2(s_c[g] - jnp.maximum(jnp.max(s_c[g], axis=0, keepdims=True), 0.1 * NEG_BIG)) for g in groups]
    inv_c = [1.0 / jnp.maximum(jnp.sum(e_c[g], axis=0, keepdims=True), 1.0) for g in groups]
    o_c = [_dot(vct_ref[g], e_c[g].astype(BF16)) * inv_c[g] for g in groups]

    jb = lax.broadcasted_iota(jnp.int32, (n_sb, TQ), 0)
    cur = (q0 + lax.broadcasted_iota(jnp.int32, (n_sb, TQ), 1)) // SEL_LEN
    valid = jb <= cur
    forced = (jb == 0) | (jb == cur) | (jb == cur - 1)
    sub = lax.broadcasted_iota(jnp.int32, (8, TQ), 0)
    score = []
    for g in groups:
        p_sum = e_c[g][:, 0:TQ] * inv_c[g][:, 0:TQ]
        for hh in range(1, HPG):
            p_sum = p_sum + e_c[g][:, hh * TQ:(hh + 1) * TQ] * inv_c[g][:, hh * TQ:(hh + 1) * TQ]
        imp = jnp.dot(ovl_ref[...], p_sum, preferred_element_type=F32, precision=HIGHEST)
        score.append(jnp.where(forced, jnp.inf, jnp.where(valid, imp, -jnp.inf)))

    def ranked(n_rows):
        def fn(*score):
            n_grp = n_rows // 8
            outs = []
            grp = [[s[8 * a:8 * a + 8] for a in range(n_grp)] for s in score]
            rank = [[jnp.zeros((8, TQ), jnp.int32) for _ in range(n_grp)] for _ in score]
            for j in range(n_rows):
                for g in groups:
                    r = score[g][j:j + 1, :]
                    for a in range(n_grp):
                        if a > j // 8:
                            ahead = (r >= grp[g][a]).astype(jnp.int32)
                        elif a < j // 8:
                            ahead = (r > grp[g][a]).astype(jnp.int32)
                        else:
                            ahead = jnp.where(sub > j % 8, (r >= grp[g][a]).astype(jnp.int32),
                                              (r > grp[g][a]).astype(jnp.int32))
                        rank[g][a] = rank[g][a] + ahead
            for g in groups:
                top = jnp.where(jnp.concatenate(rank[g], axis=0) < SEL_TOPN, 1.0, 0.0)
                outs.append(jnp.concatenate([top, jnp.zeros((n_sb - n_rows, TQ), F32)], axis=0) if n_rows < n_sb else top)
            return tuple(outs)
        return fn

    sizes = [n for n in range(RANK_STEP, n_sb + 1, RANK_STEP)]
    rows_needed = (q0 + TQ + SEL_LEN - 1) // SEL_LEN
    in_top = lax.switch((rows_needed + RANK_STEP - 1) // RANK_STEP - 1, [ranked(n) for n in sizes], *score)
    picked = [valid & (in_top[g] > 0.5) for g in groups]
    qs_t = [q_aug(g, jnp.where(picked[g], 0.0, NEG_BIG)) for g in groups]

    filler = 1 - kt_d % 2
    n_seq = kt_d + 1 + filler

    def sel_tile(i):
        return jnp.where(i == 0, kt_d, jnp.maximum(i - 1 - filler, 0))

    def a_stage(slot, kt, mask_add=None):
        return [stage_scores(ss_s.at[g, slot], ksa_s.at[g], qs_t[g], kt, mask_add) for g in groups]

    def b_stage(slot, tmax, m):
        out = [stage_probs(ss_s.at[g, slot], ps_s.at[g, slot], tmax[g], m[g]) for g in groups]
        return [o[0] for o in out], [o[1] for o in out]

    def c_stage(slot, kt, alpha, acc):
        return [stage_values(ps_s.at[g, slot], vsa_s.at[g], kt, alpha[g], acc[g]) for g in groups]

    tmax0 = a_stage(0, kt_d, tile_heads(wmask_ref[0]))
    m, alpha = b_stage(0, tmax0, [m0] * G)
    tmax1 = a_stage(1, sel_tile(1), jnp.where(filler == 1, NEG_BIG, 0.0))

    m_w = [functools.reduce(jnp.maximum, [max_win[back][g] for back in range(n_win)]) for g in groups]
    accw = [None] * G
    for back in range(n_win):
        for g in groups:
            pw_s[g, back] = jnp.exp2(sw_s[g, back] - m_w[g]).astype(BF16)
        for g in groups:
            pv = _dot(vwa_s[g, :, pl.ds(pl.multiple_of(kt_win[back] * TK, TK), TK)], pw_s[g, back])
            accw[g] = pv if accw[g] is None else accw[g] + pv
    o_w = [normalised(accw[g]) for g in groups]

    def sel_body(k, carry):
        m, alpha, acc, tmax1 = carry
        i = 2 * k
        tmax0 = a_stage(0, sel_tile(i + 2))
        m, alpha1 = b_stage(1, tmax1, m)
        tmax1 = a_stage(1, sel_tile(i + 3))
        acc = c_stage(0, sel_tile(i), alpha, acc)
        m, alpha2 = b_stage(0, tmax0, m)
        acc = c_stage(1, sel_tile(i + 1), alpha1, acc)
        return m, alpha2, acc, tmax1

    m, alpha, acc, tmax1 = lax.fori_loop(0, (n_seq - 2) // 2, sel_body, (m, alpha, [acc0] * G, tmax1))
    m, alpha1 = b_stage(1, tmax1, m)
    acc = c_stage(0, sel_tile(n_seq - 2), alpha, acc)
    acc = c_stage(1, sel_tile(n_seq - 1), alpha1, acc)
    o_s = [normalised(acc[g]) for g in groups]

    gates = _sigmoid(gt_ref[...] + gb_ref[...]).T
    outs = []
    for g in groups:
        for hh in range(HPG):
            cols = slice(hh * TQ, (hh + 1) * TQ)
            row = NSA_GATE_LANE0 + 3 * (g * HPG + hh)
            outs.append(gates[row:row + 1] * o_c[g][:, cols] + gates[row + 1:row + 2] * o_s[g][:, cols]
                        + gates[row + 2:row + 3] * o_w[g][:, cols])
    o_ref[...] = jnp.concatenate(outs, axis=0).T.astype(o_ref.dtype)


def nsa_block(zt, zb, zf, gate_bias, kcmp, vcmp_t, q_col_block, k_col_blocks, v_row_blocks):
    B, S, _ = zb.shape
    G, d = NSA_KV_HEADS, HEAD_DIM
    TQ = NSA_TQ
    n_cb = kcmp.shape[2]
    n_cmp = (S - CMP_LEN) // CMP_STRIDE + 1
    n_sb = S // SEL_LEN
    n_terms = len(LOG2E_TERMS)
    kaug = -(-(d + n_sb + 2 * n_terms) // LANES) * LANES
    R = NSA_HPG * TQ
    n_win = (WINDOW - 1 + NSA_TK - 1) // NSA_TK + 1
    assert S % NSA_TK == 0 and NSA_TK == TQ and TQ % SEL_LEN == 0

    cidx = np.arange(n_cb)[None, :] * CMP_STRIDE
    sstart = np.arange(n_sb)[:, None] * SEL_LEN
    ovl = ((cidx < sstart + SEL_LEN) & (cidx + CMP_LEN - 1 >= sstart) & (np.arange(n_cb)[None, :] < n_cmp))
    ovl = jnp.asarray(ovl.astype(np.float32))
    pos = np.arange(S)
    aug = np.zeros((S, kaug - d), np.float32)
    aug[pos, pos // SEL_LEN] = 1.0
    tail = np.zeros((kaug - d - n_sb, TQ), np.float32)
    caug = np.zeros((n_cb, d), np.float32)
    ctail = np.zeros((d, TQ), np.float32)
    for i, term in enumerate(LOG2E_TERMS):
        aug[:, n_sb + i] = pos // SEL_LEN
        aug[:, n_sb + n_terms + i] = pos % SEL_LEN
        tail[i] = term * SEL_LEN
        tail[n_terms + i] = term
        caug[:, i] = np.arange(n_cb)
        ctail[i] = term * CMP_STRIDE
    aug, caug, tail, ctail = jnp.asarray(aug, BF16), jnp.asarray(caug, BF16), jnp.asarray(tail), jnp.asarray(ctail)
    dist = np.arange(n_win)[:, None, None] * NSA_TK + np.arange(TQ)[None, None, :] - np.arange(NSA_TK)[None, :, None]
    wmask = jnp.asarray(np.where((dist >= 0) & (dist < WINDOW), 0.0, NEG_BIG).astype(np.float32))

    def kspec(j):
        return pl.BlockSpec((None, S, G * d), lambda b, i: (b, 0, k_col_blocks[j]))

    def vspec(j):
        return pl.BlockSpec((None, G * d, S), lambda b, i: (b, v_row_blocks[j], 0))

    def const(a):
        return pl.BlockSpec(a.shape, lambda b, i: (0,) * a.ndim)

    W = G * NSA_HPG * d
    return pl.pallas_call(
        functools.partial(_nsa_kernel, n_cmp=n_cmp), grid=(B, S // TQ),
        in_specs=[pl.BlockSpec((None, TQ, W), lambda b, i: (b, i, q_col_block)),
                  pl.BlockSpec((None, TQ, LANES), lambda b, i: (b, i, GATE_COL_BLOCK)),
                  const(gate_bias),
                  pl.BlockSpec((None, G, n_cb, d), lambda b, i: (b, 0, 0, 0)),
                  pl.BlockSpec((None, G, d, n_cb), lambda b, i: (b, 0, 0, 0)),
                  kspec(0), vspec(0), kspec(1), vspec(1), const(ovl), const(aug), const(caug), const(tail),
                  const(ctail), const(wmask)],
        out_specs=pl.BlockSpec((None, TQ, W), lambda b, i: (b, i, 0)),
        out_shape=jax.ShapeDtypeStruct((B, S, NSA_W), BF16),
        scratch_shapes=[pltpu.VMEM((G, S, kaug), BF16), pltpu.VMEM((G, S, kaug), BF16),
                        pltpu.VMEM((G, n_cb, 2 * d), BF16),
                        pltpu.VMEM((G, d + NSA_ONES, S), BF16), pltpu.VMEM((G, d + NSA_ONES, S), BF16),
                        pltpu.VMEM((G, 2, NSA_TK, R), F32), pltpu.VMEM((G, 2, NSA_TK, R), BF16),
                        pltpu.VMEM((G, n_win, NSA_TK, R), F32), pltpu.VMEM((G, n_win, NSA_TK, R), BF16)],
        compiler_params=_cparams(("parallel", "arbitrary")), name="nsa",
    )(zb, zf, gate_bias, kcmp, vcmp_t, zb, zt, zb, zt, ovl, aug, caug, tail, ctail, wmask)


def _rglru_kernel(h_ref, hh_ref, g_ref, wg_ref, wi_ref, cw_ref, cb_ref, wa_ref, wx_ref, ba_ref, bx_ref, lam_ref,
                  o_ref, h_s, au_s):
    NB, T, W = o_ref.shape
    first = pl.program_id(1) == 0

    @pl.when(first)
    def _():
        h_s[...] = jnp.zeros_like(h_s)

    def project(nb):
        xn = _rms(h_ref[nb], g_ref[...]).astype(BF16)
        x_halo = _dot(_rms(hh_ref[nb], g_ref[...]).astype(BF16), wi_ref[...])
        xe = jnp.concatenate([jnp.where(first, 0.0, x_halo), _dot(xn, wi_ref[...])], axis=0)
        return _dot(xn, wg_ref[...]), xe

    def recur(nb, gate, xe):
        xc = cb_ref[...] + cw_ref[3:4, :] * xe
        for k in range(1, LRU_CONV):
            xc = xc + cw_ref[3 - k:4 - k, :] * pltpu.roll(xe, k, 0)
        xc = xc[LRU_HALO:, :]
        xcb = xc.astype(BF16)
        half = W // 2

        def blockdiag(w_ref):
            return jnp.concatenate([_dot(xcb[:, :half], w_ref[0]), _dot(xcb[:, half:], w_ref[1])], axis=1)

        r = _sigmoid(blockdiag(wa_ref) + ba_ref[...])
        i = _sigmoid(blockdiag(wx_ref) + bx_ref[...])
        nl = -lam_ref[...]
        softplus = jnp.maximum(nl, 0.0) + jnp.log1p(jnp.exp(-jnp.abs(nl)))
        log_a = -LRU_C * r * softplus
        a = jnp.exp(log_a)
        one_m_a2 = -jnp.tanh(log_a) * (a * a + 1.0)
        u = jnp.where(one_m_a2 > 0.0, one_m_a2 * lax.rsqrt(one_m_a2), 0.0) * (i * xc)

        n_grp, n_slab = T // 8, W // LANES

        def phases(x, slab):
            for c in range(n_slab):
                au_s[nb, slab, c] = x[:, c * LANES:(c + 1) * LANES]
            return [jnp.concatenate([au_s[nb, slab, c, pl.ds(j, n_grp, stride=8), :] for c in range(n_slab)], axis=1)
                    for j in range(8)]

        a_ph, u_ph = phases(a, 0), phases(u, 1)
        prod, part = [a_ph[0]], [u_ph[0]]
        for j in range(1, 8):
            part.append(a_ph[j] * part[-1] + u_ph[j])
            prod.append(a_ph[j] * prod[-1])
        grp = lax.broadcasted_iota(jnp.int32, (n_grp, W), 0)
        ag, ug = prod[7], part[7]
        sft = 1
        while sft < n_grp:
            keep = grp >= sft
            ug = ag * jnp.where(keep, pltpu.roll(ug, sft, 0), 0.0) + ug
            ag = ag * jnp.where(keep, pltpu.roll(ag, sft, 0), 1.0)
            sft *= 2
        h_prev = h_s[nb, 0:1, :]
        hg = ug + ag * h_prev
        carry_in = jnp.where(grp >= 1, pltpu.roll(hg, 1, 0), h_prev)
        for j in range(8):
            hj = part[j] + prod[j] * carry_in
            for c in range(n_slab):
                au_s[nb, 0, c, pl.ds(j, n_grp, stride=8), :] = hj[:, c * LANES:(c + 1) * LANES]
        h = jnp.concatenate([au_s[nb, 0, c] for c in range(n_slab)], axis=1)
        h_s[nb] = jnp.broadcast_to(hg[n_grp - 1:n_grp, :], h_s.shape[1:])
        o_ref[nb] = (_gelu(gate) * h).astype(o_ref.dtype)

    nxt = project(0)
    for nb in range(NB):
        cur = nxt
        if nb + 1 < NB:
            nxt = project(nb + 1)
        recur(nb, *cur)


def rglru_block(h3, pre_g, w_gate, w_in, conv_w, conv_b, wa, ba, wx, bx, lam, T=LRU_T, NB=LRU_NB):
    B, S, D = h3.shape
    W = LRU_W
    half = W // 2
    hb = T // LRU_HALO

    def bd(w):
        blocks = [jax.scipy.linalg.block_diag(*[w[h] for h in range(4 * j, 4 * j + 4)]) for j in range(2)]
        return jnp.stack(blocks).astype(BF16)

    vec = lambda a: a.reshape(1, W)
    assert B % NB == 0 and S % T == 0
    vspec = pl.BlockSpec((1, W), lambda b, t: (0, 0))
    wspec = pl.BlockSpec((2, half, half), lambda b, t: (0, 0, 0))
    return pl.pallas_call(
        _rglru_kernel, grid=(B // NB, S // T),
        in_specs=[pl.BlockSpec((NB, T, D), lambda b, t: (b, t, 0)),
                  pl.BlockSpec((NB, LRU_HALO, D), lambda b, t: (b, jnp.maximum(t * hb - 1, 0), 0)),
                  pl.BlockSpec((1, D), lambda b, t: (0, 0)),
                  pl.BlockSpec((D, W), lambda b, t: (0, 0)), pl.BlockSpec((D, W), lambda b, t: (0, 0)),
                  pl.BlockSpec((LRU_CONV, W), lambda b, t: (0, 0)), vspec, wspec, wspec, vspec, vspec, vspec],
        out_specs=pl.BlockSpec((NB, T, W), lambda b, t: (b, t, 0)),
        out_shape=jax.ShapeDtypeStruct((B, S, W), BF16),
        scratch_shapes=[pltpu.VMEM((NB, 8, W), F32), pltpu.VMEM((NB, 2, W // LANES, T, LANES), F32)],
        compiler_params=_cparams(("parallel", "arbitrary"), 12, (3, 4)), name="rglru",
    )(h3, h3, pre_g.reshape(1, D), w_gate, w_in, conv_w, vec(conv_b), bd(wa), bd(wx), vec(ba), vec(bx), vec(lam))


def _sgu_kernel(h_ref, gpre_ref, wu_ref, wv_ref, g_ref, b_ref, w_ref, bias_ref, o_ref):
    C, W = SG_CHUNK, o_ref.shape[1]
    dg = W // SG_GROUPS
    n_sub = o_ref.shape[0] // SG_SUB
    row = lax.broadcasted_iota(jnp.int32, (C, C), 0)
    col = lax.broadcasted_iota(jnp.int32, (C, C), 1)
    wc = [jnp.where(col <= row, w_ref[gi], 0.0).astype(BF16) for gi in range(SG_GROUPS)]

    def project(j):
        xn = _rms(h_ref[j * SG_SUB:(j + 1) * SG_SUB, :], gpre_ref[...]).astype(BF16)
        return _dot(xn, wu_ref[...]), _dot(xn, wv_ref[...])

    def gate(j, u, v):
        u, v = _gelu(u), _gelu(v)
        mu = jnp.mean(v, axis=-1, keepdims=True)
        vc = v - mu
        vn = (vc * lax.rsqrt(jnp.mean(vc * vc, axis=-1, keepdims=True) + EPS) * g_ref[...] + b_ref[...]).astype(BF16)
        for gi in range(SG_GROUPS):
            sl = slice(gi * dg, (gi + 1) * dg)
            for c in range(SG_SUB // C):
                rows = slice(c * C, (c + 1) * C)
                mixed = _dot(wc[gi], vn[rows, sl]) + bias_ref[:, sl]
                o_ref[j * SG_SUB + c * C:j * SG_SUB + (c + 1) * C, sl] = (u[rows, sl] * mixed).astype(o_ref.dtype)

    nxt = project(0)
    for j in range(n_sub):
        cur = nxt
        if j + 1 < n_sub:
            nxt = project(j + 1)
        gate(j, *cur)


def sgu_block(h3, pre_g, w_u, w_v, ln_g, ln_b, w, b):
    B, S, D = h3.shape
    W, C, T = SG_W, SG_CHUNK, SG_ROWS
    assert S % T == 0 and T % SG_SUB == 0 and SG_SUB % C == 0
    bias = jnp.repeat(b.T, W // SG_GROUPS, axis=1)
    vspec = pl.BlockSpec((1, W), lambda bb, c: (0, 0))
    return pl.pallas_call(
        _sgu_kernel, grid=(B, S // T),
        in_specs=[pl.BlockSpec((None, T, D), lambda bb, c: (bb, c, 0)),
                  pl.BlockSpec((1, D), lambda bb, c: (0, 0)),
                  pl.BlockSpec((D, W), lambda bb, c: (0, 0)), pl.BlockSpec((D, W), lambda bb, c: (0, 0)),
                  vspec, vspec,
                  pl.BlockSpec((SG_GROUPS, C, C), lambda bb, c: (0, 0, 0)),
                  pl.BlockSpec((C, W), lambda bb, c: (0, 0))],
        out_specs=pl.BlockSpec((None, T, W), lambda bb, c: (bb, c, 0)),
        out_shape=jax.ShapeDtypeStruct((B, S, W), BF16),
        compiler_params=_cparams(("parallel", "parallel"), 8, (2, 3)), name="sgu",
    )(h3, pre_g.reshape(1, D), w_u, w_v, ln_g.reshape(1, W), ln_b.reshape(1, W), w, bias)


def _mixer_ab(h2, B, S, pre_g, w_in, ml_gate_b, ml_norm_g, nsa_gate_b, k_pe, k_w1, k_w2, v_pe, v_w1, v_w2):
    D = h2.shape[1]
    G, d = NSA_KV_HEADS, HEAD_DIM
    offs = np.cumsum([0, ML_W, ML_W, ML_W, ML_W, 2 * ML_HEADS, NSA_W] + [G * d] * 6 + [3 * NSA_HEADS])
    mq, mk, mv, mo, mif, nq, kc, vc, ks, vs, kw, vw, ng = [w_in[:, offs[i]:offs[i + 1]] for i in range(13)]
    w_b = jnp.concatenate([mk, nq, ks, kw], axis=1).astype(BF16)
    w_t = jnp.concatenate([mq, mv, vs, vw], axis=1).T.astype(BF16)
    gpad = LANES - 2 * ML_HEADS - 3 * NSA_HEADS
    w_f = jnp.concatenate([mo, kc, vc, mif, ng, jnp.zeros((D, gpad), w_in.dtype)], axis=1).astype(BF16)
    gate_bias = jnp.concatenate([ml_gate_b, nsa_gate_b, jnp.zeros((gpad,), F32)]).reshape(1, LANES)
    zb, zf, zt = norm_proj(h2, pre_g, [w_b, w_f], [BF16, F32], wts=[w_t], batch=B)
    zb = zb.reshape(B, S, -1)
    zf = zf.reshape(B, S, -1)
    h_ml = mlstm_block(zt, zb, zf, gate_bias, ml_norm_g)
    kcmp, vcmp_t = compress_block(zf, ML_W // (G * d), ML_W // (G * d) + 1, k_pe, k_w1, k_w2, v_pe, v_w1, v_w2)
    h_nsa = nsa_block(zt, zb, zf, gate_bias, kcmp, vcmp_t, q_col_block=ML_W // NSA_W,
                      k_col_blocks=((ML_W + NSA_W) // (G * d), (ML_W + NSA_W) // (G * d) + 1),
                      v_row_blocks=(2 * ML_W // (G * d), 2 * ML_W // (G * d) + 1))
    return h_ml.reshape(B * S, ML_W), h_nsa.reshape(B * S, NSA_W)


def _mixer_cd(h2, B, S, pre_g, w_in, conv_w, conv_b, wa, ba, wx, bx, lam, sg_g, sg_bn, sg_w, sg_b):
    h3 = h2.reshape(B, S, -1)
    w_gate, w_x, w_u, w_v = (w_in[:, j * LRU_W:(j + 1) * LRU_W].astype(BF16) for j in range(4))
    y_lru = rglru_block(h3, pre_g, w_gate, w_x, conv_w, conv_b, wa, ba, wx, bx, lam)
    y_sg = sgu_block(h3, pre_g, w_u, w_v, sg_g, sg_bn, sg_w, sg_b)
    return y_lru.reshape(B * S, LRU_W), y_sg.reshape(B * S, SG_W)


def kernel(x, pre_mix_g, post_mix_g, pre_ffn_g, post_ffn_g, ab_w_in, ab_w_out, ml_gate_b, ml_norm_g, nsa_gate_b, cmp_k_pe, cmp_k_w1, cmp_k_w2, cmp_v_pe, cmp_v_w1, cmp_v_w2, cd_w_in, cd_w_out, lru_conv_w, lru_conv_b, lru_wa, lru_ba, lru_wx, lru_bx, lru_lambda, sg_norm_g, sg_norm_b, sg_w, sg_b, ffn_w_up, ffn_conv_w, ffn_conv_b, ffn_w_down):
    B, S, D = x.shape
    depth = pre_mix_g.shape[0]
    h2 = x.reshape(B * S, D)
    for layer in range(depth):
        if layer % 2 == 0:
            e = layer // 2
            a1, a2 = _mixer_ab(h2, B, S, pre_mix_g[layer], ab_w_in[e], ml_gate_b[e], ml_norm_g[e], nsa_gate_b[e],
                               cmp_k_pe[e], cmp_k_w1[e], cmp_k_w2[e], cmp_v_pe[e], cmp_v_w1[e], cmp_v_w2[e])
            w_out = ab_w_out[e]
        else:
            o = layer // 2
            a1, a2 = _mixer_cd(h2, B, S, pre_mix_g[layer], cd_w_in[o], lru_conv_w[o], lru_conv_b[o], lru_wa[o],
                               lru_ba[o], lru_wx[o], lru_bx[o], lru_lambda[o], sg_norm_g[o], sg_norm_b[o], sg_w[o],
                               sg_b[o])
            w_out = cd_w_out[o]
        h2 = mix_ffn_block(h2, a1, a2, S, w_out, post_mix_g[layer], pre_ffn_g[layer], ffn_w_up[layer],
                           ffn_conv_w[layer], ffn_conv_b[layer], ffn_w_down[layer], post_ffn_g[layer])
    return h2.reshape(B, S, D)
```

```python
import functools

import numpy as np
import jax
import jax.numpy as jnp
from jax import lax
from jax.experimental import pallas as pl
from jax.experimental.pallas import tpu as pltpu

F32 = jnp.float32
BF16 = jnp.bfloat16

EPS = 1e-6
HEAD_DIM = 64
ML_HEADS = 8
ML_W = 512
GATE_SOFTCAP = 15.0
NSA_HEADS = 8
NSA_KV_HEADS = 2
NSA_HPG = NSA_HEADS // NSA_KV_HEADS
NSA_W = 512
CMP_LEN = 32
CMP_STRIDE = 16
CMP_HID = 128
SEL_LEN = 64
SEL_TOPN = 16
WINDOW = 512
LRU_W = 512
LRU_C = 8.0
LRU_CONV = 4
SG_GROUPS = 8
SG_W = 512
SG_CHUNK = 128
FFN_CONV = 3
GATE_COL_BLOCK = (ML_W + 2 * NSA_KV_HEADS * HEAD_DIM) // 128
NSA_GATE_LANE0 = 2 * ML_HEADS

LANES = 128
VMEM_LIMIT = 56 * 1024 * 1024
NEG_BIG = -1e30
HIGHEST = lax.Precision.HIGHEST
LOG2E = 1.4426950408889634


def _bf16_terms(x, n):
    terms = []
    for _ in range(n):
        bits = int(np.array(x, np.float32).view(np.uint32))
        t = float(np.array((bits + 0x7FFF + ((bits >> 16) & 1)) & 0xFFFF0000, np.uint32).view(np.float32))
        terms.append(t)
        x -= t
    return tuple(terms)


LOG2E_TERMS = _bf16_terms(LOG2E, 3)

ML_CHUNK = 128
ML_NB = 8
NSA_TQ = 256
NSA_TK = 256
RANK_STEP = 16
NSA_ONES = 16
ROW_TILE = 512
FFN_TM = 512
FFN_SUB = 256
FFN_CK = 256
FFN_HALO = 16
SG_ROWS = 1024
SG_SUB = 256
LRU_T = 512
LRU_HALO = 8
LRU_NB = 4


def _cparams(sem, n_in=0, fuse=()):
    fusion = [i in fuse for i in range(n_in)] if fuse else None
    return pltpu.CompilerParams(dimension_semantics=sem, vmem_limit_bytes=VMEM_LIMIT, allow_input_fusion=fusion)


def _rms(x, g):
    return x * lax.rsqrt(jnp.mean(x * x, axis=-1, keepdims=True) + EPS) * g


def _gelu(x):
    return 0.5 * x * (1.0 + jnp.tanh(0.7978845608028654 * (x + 0.044715 * (x * x * x))))


def _sigmoid(x):
    return 1.0 / (1.0 + jnp.exp(-x))


def _dot(a, b):
    return jnp.dot(a, b, preferred_element_type=F32)


def _dot_nt(a, b, precision=None):
    return lax.dot_general(a, b, (((1,), (1,)), ((), ())), preferred_element_type=F32, precision=precision)


def _norm_proj_kernel(h_ref, g_ref, *refs, n_row, n_t, cn):
    w_refs, wt_refs = refs[:n_row], refs[n_row:n_row + n_t]
    o_refs, ot_refs = refs[n_row + n_t:2 * n_row + n_t], refs[2 * n_row + n_t:]
    xn = _rms(h_ref[...], g_ref[...]).astype(BF16)
    for w_ref, o_ref in zip(w_refs, o_refs):
        n = w_ref.shape[1]
        for c in range(0, n, cn):
            ce = min(c + cn, n)
            o_ref[:, c:ce] = _dot(xn, w_ref[:, c:ce]).astype(o_ref.dtype)
    for wt_ref, ot_ref in zip(wt_refs, ot_refs):
        n = wt_ref.shape[0]
        for c in range(0, n, cn):
            ce = min(c + cn, n)
            ot_ref[c:ce, :] = _dot_nt(wt_ref[c:ce, :], xn).astype(ot_ref.dtype)


def norm_proj(h2, g, ws, dtypes, wts=(), batch=1, tm=ROW_TILE):
    M, D = h2.shape
    tps = M // batch // tm
    in_specs = [pl.BlockSpec((tm, D), lambda i: (i, 0)), pl.BlockSpec((1, D), lambda i: (0, 0))]
    in_specs += [pl.BlockSpec(w.shape, lambda i: (0, 0)) for w in (*ws, *wts)]
    out_specs = [pl.BlockSpec((tm, w.shape[1]), lambda i: (i, 0)) for w in ws]
    out_specs += [pl.BlockSpec((None, w.shape[0], tm), lambda i: (i // tps, 0, i % tps)) for w in wts]
    out_shape = [jax.ShapeDtypeStruct((M, w.shape[1]), dt) for w, dt in zip(ws, dtypes)]
    out_shape += [jax.ShapeDtypeStruct((batch, w.shape[0], M // batch), BF16) for w in wts]
    return pl.pallas_call(
        functools.partial(_norm_proj_kernel, n_row=len(ws), n_t=len(wts), cn=512),
        grid=(M // tm,), in_specs=in_specs, out_specs=out_specs, out_shape=out_shape,
        compiler_params=_cparams(("parallel",), 2 + len(ws) + len(wts), range(2, 2 + len(ws) + len(wts))),
        name="norm_proj",
    )(h2, g.reshape(1, D), *ws, *wts)


def _mix_ffn_kernel(h_ref, a1_ref, a2_ref, wo1_ref, wo2_ref, gmix_ref, gpre_ref, wu_ref, cw_ref, cb_ref, wd_ref,
                    gpost_ref, o_ref, xn_s, acc_s, ut_s, *, tiles_per_seq, ck):
    F = wd_ref.shape[0]
    n_chunks = F // ck
    n_sub = h_ref.shape[0] // FFN_SUB
    g = gpre_ref[...]

    @pl.when(pl.program_id(0) % tiles_per_seq == 0)
    def _():
        ut_s[...] = jnp.zeros_like(ut_s)

    def mixed(h, a1, a2):
        return h + _rms(_dot(a1, wo1_ref[...]) + _dot(a2, wo2_ref[...]), gmix_ref[...])

    def conv(u, cols):
        y = (cw_ref[2:3, cols] * u + cw_ref[1:2, cols] * pltpu.roll(u, 1, 0) + cw_ref[0:1, cols] * pltpu.roll(u, 2, 0)
             + cb_ref[:, cols])
        return y[FFN_HALO:, :]

    def cols_of(c, half):
        return slice(half * F + c * ck, half * F + (c + 1) * ck)

    x = {}

    def head(j):
        rows = slice(j * FFN_SUB, (j + 1) * FFN_SUB)
        x[j] = mixed(h_ref[rows, :], a1_ref[rows, :], a2_ref[rows, :])
        xn_s[rows, :] = _rms(x[j], g).astype(BF16)
        acc_s[rows, :] = jnp.zeros((FFN_SUB, acc_s.shape[1]), F32)

    def up(j, c):
        xn = xn_s[j * FFN_SUB:(j + 1) * FFN_SUB, :]
        u = [jnp.concatenate([ut_s[c, half], _dot(xn, wu_ref[:, cols_of(c, half)])], axis=0) for half in range(2)]
        for half in range(2):
            ut_s[c, half] = u[half][FFN_SUB:, :]
        return u

    def tail(j):
        rows = slice(j * FFN_SUB, (j + 1) * FFN_SUB)
        o_ref[rows, :] = x[j] + _rms(acc_s[rows, :], gpost_ref[...])

    head(0)
    for j in range(n_sub):
        u = up(j, 0)
        for c in range(n_chunks):
            u_next = up(j, c + 1) if c + 1 < n_chunks else None
            if c == 0 and j + 1 < n_sub:
                head(j + 1)
            if c == n_chunks // 3 and j >= 1:
                tail(j - 1)
            act = (_gelu(conv(u[0], cols_of(c, 0))) * conv(u[1], cols_of(c, 1))).astype(BF16)
            acc_s[j * FFN_SUB:(j + 1) * FFN_SUB, :] += _dot(act, wd_ref[c * ck:(c + 1) * ck, :])
            u = u_next
    tail(n_sub - 1)


def mix_ffn_block(h2, a1, a2, seq_len, w_out, g_mix, g_pre, w_up, conv_w, conv_b, w_down, g_post, tm=FFN_TM, ck=FFN_CK):
    M, D = h2.shape
    F = w_down.shape[0]
    K1, K2 = a1.shape[1], a2.shape[1]
    assert F % ck == 0 and seq_len % tm == 0 and tm % FFN_SUB == 0 and FFN_SUB % FFN_HALO == 0
    assert conv_w.shape[0] == FFN_CONV and FFN_CONV - 1 <= FFN_HALO
    consts = (w_out[:K1].astype(BF16), w_out[K1:].astype(BF16), g_mix.reshape(1, D), g_pre.reshape(1, D),
              w_up.astype(BF16), conv_w, conv_b.reshape(1, 2 * F), w_down.astype(BF16), g_post.reshape(1, D))

    def tile(width):
        return pl.BlockSpec((tm, width), lambda i: (i, 0))

    def full(a):
        return pl.BlockSpec(a.shape, lambda i: (0,) * a.ndim)

    return pl.pallas_call(
        functools.partial(_mix_ffn_kernel, tiles_per_seq=seq_len // tm, ck=ck),
        grid=(M // tm,),
        in_specs=[tile(D), tile(K1), tile(K2)] + [full(a) for a in consts],
        out_specs=tile(D),
        out_shape=jax.ShapeDtypeStruct((M, D), F32),
        scratch_shapes=[pltpu.VMEM((tm, D), BF16), pltpu.VMEM((tm, D), F32),
                        pltpu.VMEM((F // ck, 2, FFN_HALO, ck), F32)],
        compiler_params=_cparams(("arbitrary",), 3 + len(consts), (3, 4, 7, 10)), name="mix_ffn",
    )(h2, a1, a2, *consts)


def _mlstm_kernel(qt_ref, vt_ref, k_ref, mo_ref, gt_ref, gb_ref, ng_ref, o_ref, cn_s, m_s):
    NB, _, L = qt_ref.shape
    d = HEAD_DIM

    @pl.when(pl.program_id(1) == 0)
    def _():
        cn_s[...] = jnp.zeros_like(cn_s)
        m_s[...] = jnp.zeros_like(m_s)

    src = lax.broadcasted_iota(jnp.int32, (L, L), 0)
    tgt = lax.broadcasted_iota(jnp.int32, (L, L), 1)
    causal = src <= tgt
    tri = (tgt <= src).astype(F32)

    gcap = [GATE_SOFTCAP * jnp.tanh((gt_ref[nb] + gb_ref[...]) * (1.0 / GATE_SOFTCAP)) for nb in range(NB)]
    lf = [jnp.minimum(x, 0.0) - jnp.log1p(jnp.exp(-jnp.abs(x))) for x in gcap]
    b_col = [jnp.dot(tri, x, preferred_element_type=F32, precision=HIGHEST) for x in lf]
    b_row = [x.T for x in b_col]
    i_row = [x.T for x in gcap]
    c_col = [b_col[nb] - pltpu.roll(gcap[nb], ML_HEADS, 1) for nb in range(NB)]

    chains = [(nb, h) for nb in range(NB) for h in range(ML_HEADS)]
    n = range(len(chains))

    def rows(h):
        return slice(h * d, (h + 1) * d)

    q_t = [qt_ref[nb, rows(h), :] for nb, h in chains]
    v_t = [vt_ref[nb, rows(h), :] for nb, h in chains]
    k = [k_ref[nb, :, rows(h)] * 0.125 for nb, h in chains]
    br = [b_row[nb][ML_HEADS + h:ML_HEADS + h + 1, :] for nb, h in chains]
    ir = [i_row[nb][h:h + 1, :] for nb, h in chains]
    g = [x[:, L - 1:L] for x in br]
    m_prev = [m_s[nb, h:h + 1, 0:1] for nb, h in chains]
    cn_prev = [cn_s[nb, h] for nb, h in chains]
    kq = [_dot(k[c], q_t[c]) for c in n]
    carry = [_dot(cn_prev[c].astype(BF16), q_t[c]) for c in n]
    dlog = [jnp.where(causal, br[c] - c_col[nb][:, ML_HEADS + h:ML_HEADS + h + 1], -jnp.inf)
            for c, (nb, h) in enumerate(chains)]
    inter = [br[c] + m_prev[c] for c in n]
    m_row = [jnp.maximum(inter[c], jnp.max(dlog[c], axis=0, keepdims=True)) for c in n]
    s = [kq[c] * jnp.exp(dlog[c] - m_row[c]) for c in n]
    w_inter = [jnp.exp(inter[c] - m_row[c]) for c in n]
    num = [_dot(v_t[c], s[c].astype(BF16)) + w_inter[c] * carry[c][0:d] for c in n]
    den = [jnp.sum(s[c], axis=0, keepdims=True) + w_inter[c] * carry[c][d:d + 1] for c in n]
    hh = [num[c] * (1.0 / jnp.maximum(jnp.abs(den[c]), jnp.exp(-m_row[c]))) for c in n]
    wlog = [g[c] - br[c] + ir[c] for c in n]
    m_new = [jnp.maximum(g[c] + m_prev[c], jnp.max(wlog[c], axis=-1, keepdims=True)) for c in n]
    w_row = [jnp.exp(wlog[c] - m_new[c]) for c in n]
    decay = [jnp.exp(g[c] + m_prev[c] - m_new[c]) for c in n]
    for c, (nb, h) in enumerate(chains):
        vw = jnp.concatenate([v_t[c].astype(F32) * w_row[c], jnp.broadcast_to(w_row[c], (8, L))], axis=0).astype(BF16)
        cn_s[nb, h] = decay[c] * cn_prev[c] + _dot(vw, k[c])
        m_s[nb, h:h + 1, :] = jnp.broadcast_to(m_new[c], (1, m_s.shape[2]))
    outs = [hh[c] * lax.rsqrt(jnp.mean(hh[c] * hh[c], axis=0, keepdims=True) + EPS) * ng_ref[rows(h), :]
            for c, (nb, h) in enumerate(chains)]
    for nb in range(NB):
        out_t = jnp.concatenate(outs[nb * ML_HEADS:(nb + 1) * ML_HEADS], axis=0)
        o_ref[nb] = (out_t.T * _sigmoid(mo_ref[nb])).astype(o_ref.dtype)


def mlstm_block(zt, zb, zf, gate_bias, norm_g, L=ML_CHUNK, NB=ML_NB):
    B, S, _ = zb.shape
    W, H, d = ML_W, ML_HEADS, HEAD_DIM
    assert B % NB == 0 and S % L == 0
    ng = jnp.broadcast_to(norm_g.reshape(W, 1), (W, L))
    return pl.pallas_call(
        _mlstm_kernel, grid=(B // NB, S // L),
        in_specs=[pl.BlockSpec((NB, W, L), lambda b, c: (b, 0, c)),
                  pl.BlockSpec((NB, W, L), lambda b, c: (b, 1, c)),
                  pl.BlockSpec((NB, L, W), lambda b, c: (b, c, 0)),
                  pl.BlockSpec((NB, L, W), lambda b, c: (b, c, 0)),
                  pl.BlockSpec((NB, L, LANES), lambda b, c: (b, c, GATE_COL_BLOCK)),
                  pl.BlockSpec((1, LANES), lambda b, c: (0, 0)),
                  pl.BlockSpec((W, L), lambda b, c: (0, 0))],
        out_specs=pl.BlockSpec((NB, L, W), lambda b, c: (b, c, 0)),
        out_shape=jax.ShapeDtypeStruct((B, S, W), BF16),
        scratch_shapes=[pltpu.VMEM((NB, H, d + 8, d), F32), pltpu.VMEM((NB, H, LANES), F32)],
        compiler_params=_cparams(("parallel", "arbitrary")), name="mlstm",
    )(zt, zt, zb, zf, zf, gate_bias, ng)


def _compress_kernel(kc_ref, vc_ref, pek_ref, pev_ref, w1k_ref, w1v_ref, w2k_ref, w2v_ref, ok_ref, ov_ref):
    G = NSA_KV_HEADS
    nh = kc_ref.shape[0] // CMP_STRIDE

    def one(x_ref, pe_ref, w1_ref, w2_ref):
        hid = None
        for l in range(CMP_STRIDE):
            y = x_ref[pl.ds(l, nh, stride=CMP_STRIDE), :]
            ya = (y + pe_ref[l:l + 1, :]).astype(BF16)
            yb = (pltpu.roll(y, nh - 1, 0) + pe_ref[CMP_STRIDE + l:CMP_STRIDE + l + 1, :]).astype(BF16)
            t = _dot(ya, w1_ref[l]) + _dot(yb, w1_ref[CMP_STRIDE + l])
            hid = t if hid is None else hid + t
        hid = _gelu(hid).astype(BF16)
        return [_dot(hid[:, gi * CMP_HID:(gi + 1) * CMP_HID], w2_ref[...]) for gi in range(G)]

    for gi, (ko, vo) in enumerate(zip(one(kc_ref, pek_ref, w1k_ref, w2k_ref), one(vc_ref, pev_ref, w1v_ref, w2v_ref))):
        ok_ref[gi] = ko.astype(ok_ref.dtype)
        ov_ref[gi] = vo.T.astype(ov_ref.dtype)


def compress_block(zf, kc_block, vc_block, k_pe, k_w1, k_w2, v_pe, v_w1, v_w2):
    B, S, _ = zf.shape
    G, d = NSA_KV_HEADS, HEAD_DIM
    nh = S // CMP_STRIDE

    def prep(pe, w1):
        w1bd = jnp.zeros((CMP_LEN, G * d, G * CMP_HID), w1.dtype)
        for gi in range(G):
            w1bd = w1bd.at[:, gi * d:(gi + 1) * d, gi * CMP_HID:(gi + 1) * CMP_HID].set(w1)
        return jnp.tile(pe, (1, G)), w1bd.astype(BF16)

    pek, w1k = prep(k_pe, k_w1)
    pev, w1v = prep(v_pe, v_w1)

    def full(a):
        return pl.BlockSpec(a.shape, lambda b: (0,) * a.ndim)

    w2k, w2v = k_w2.astype(BF16), v_w2.astype(BF16)
    return pl.pallas_call(
        _compress_kernel, grid=(B,),
        in_specs=[pl.BlockSpec((None, S, G * d), lambda b: (b, 0, kc_block)),
                  pl.BlockSpec((None, S, G * d), lambda b: (b, 0, vc_block)),
                  full(pek), full(pev), full(w1k), full(w1v), full(w2k), full(w2v)],
        out_specs=[pl.BlockSpec((None, G, nh, d), lambda b: (b, 0, 0, 0)),
                   pl.BlockSpec((None, G, d, nh), lambda b: (b, 0, 0, 0))],
        out_shape=[jax.ShapeDtypeStruct((B, G, nh, d), BF16), jax.ShapeDtypeStruct((B, G, d, nh), BF16)],
        compiler_params=_cparams(("parallel",)), name="nsa_compress",
    )(zf, zf, pek, pev, w1k, w1v, w2k, w2v)


def _nsa_kernel(q_ref, gt_ref, gb_ref, kc_ref, vct_ref, ks_ref, vst_ref, kw_ref, vwt_ref, ovl_ref, aug_ref, caug_ref,
                tail_ref, ctail_ref, wmask_ref, o_ref, ksa_s, kwa_s, kca_s, vsa_s, vwa_s, ss_s, ps_s, sw_s, pw_s, *, n_cmp):
    TQ, TK, d, HPG, G = NSA_TQ, NSA_TK, HEAD_DIM, NSA_HPG, NSA_KV_HEADS
    R = HPG * TQ
    groups = range(G)
    qi = pl.program_id(1)
    q0 = qi * TQ
    kt_d = q0 // TK
    n_cb = kc_ref.shape[1]
    n_sb = ovl_ref.shape[0]

    @pl.when(qi == 0)
    def _():
        for g in groups:
            ksa_s[g, :, 0:d] = ks_ref[:, g * d:(g + 1) * d]
            ksa_s[g, :, d:] = aug_ref[...]
            kwa_s[g, :, 0:d] = kw_ref[:, g * d:(g + 1) * d]
            kwa_s[g, :, d:] = aug_ref[...]
            kca_s[g, :, 0:d] = kc_ref[g]
            kca_s[g, :, d:] = caug_ref[...]
            vsa_s[g, 0:d, :] = vst_ref[g * d:(g + 1) * d, :]
            vsa_s[g, d:, :] = jnp.ones((vsa_s.shape[1] - d, vsa_s.shape[2]), BF16)
            vwa_s[g, 0:d, :] = vwt_ref[g * d:(g + 1) * d, :]
            vwa_s[g, d:, :] = jnp.ones((vwa_s.shape[1] - d, vwa_s.shape[2]), BF16)

    def slope(g, hh):
        return 2.0 ** (-(g * HPG + hh + 1))

    def per_head(fn):
        return jnp.concatenate([fn(hh) for hh in range(HPG)], axis=1)

    def tile_heads(x):
        return jnp.concatenate([x] * HPG, axis=1)

    q_t = [(q_ref[:, g * HPG * d:(g + 1) * HPG * d].astype(F32) * (LOG2E * 0.125)).T for g in groups]

    def q_head(g, hh):
        return q_t[g][hh * d:(hh + 1) * d]

    def q_aug(g, block_rows):
        return per_head(lambda hh: jnp.concatenate(
            [q_head(g, hh), block_rows, tail_ref[...] * slope(g, hh)], axis=0)).astype(BF16)

    def scores(ka, qa_t, kt):
        return _dot(ka[pl.ds(pl.multiple_of(kt * TK, TK), TK), :], qa_t)

    def stage_scores(buf, ka, qa_t, kt, mask_add=None):
        s = scores(ka, qa_t, kt)
        if mask_add is not None:
            s = s + mask_add
        buf[...] = s
        return jnp.max(s, axis=0, keepdims=True)

    def stage_probs(sbuf, pbuf, tile_max, m):
        m_new = jnp.maximum(m, tile_max)
        pbuf[...] = jnp.exp2(sbuf[...] - m_new).astype(BF16)
        return m_new, jnp.exp2(m - m_new)

    def stage_values(pbuf, va, kt, alpha, acc):
        return alpha * acc + _dot(va[:, pl.ds(pl.multiple_of(kt * TK, TK), TK)], pbuf[...])

    def normalised(acc):
        return acc[0:d] * (1.0 / acc[d:d + 1])

    m0, acc0 = jnp.full((1, R), NEG_BIG, F32), jnp.zeros((vsa_s.shape[1], R), F32)

    qc_t = [per_head(lambda hh: jnp.concatenate([q_head(g, hh), ctail_ref[...] * slope(g, hh)], axis=0)).astype(BF16)
            for g in groups]
    qw_t = [q_aug(g, jnp.zeros((n_sb, TQ), F32)) for g in groups]
    n_r = lax.broadcasted_iota(jnp.int32, (n_cb, TQ), 0)
    t_c = q0 + lax.broadcasted_iota(jnp.int32, (n_cb, TQ), 1)
    ok_c = (n_r * CMP_STRIDE + (CMP_LEN - 1) <= t_c) & (n_r < n_cmp)
    add_c = tile_heads(jnp.where(ok_c, 0.0, NEG_BIG))
    s_c = [_dot(kca_s[g], qc_t[g]) + add_c for g in groups]
    n_win = (WINDOW - 1 + TK - 1) // TK + 1
    kt_win, max_win = [], []
    for back in range(n_win):
        kt_raw = kt_d - back
        kt_win.append(jnp.maximum(kt_raw, 0))
        if back == 0:
            mask_add = tile_heads(wmask_ref[0])
        elif (back + 1) * TK <= WINDOW:
            mask_add = jnp.where(kt_raw >= 0, 0.0, NEG_BIG)
        else:
            mask_add = tile_heads(wmask_ref[back] + jnp.where(kt_raw >= 0, 0.0, NEG_BIG))
        max_win.append([stage_scores(sw_s.at[g, back], kwa_s.at[g], qw_t[g], kt_win[back], mask_add) for g in groups])

    e_c = [jnp.exp2(s_c[g] - jnp.maximum(jnp.max(s_c[g], axis=0, keepdims=True), 0.1 * NEG_BIG)) for g in groups]
    inv_c = [1.0 / jnp.maximum(jnp.sum(e_c[g], axis=0, keepdims=True), 1.0) for g in groups]
    o_c = [_dot(vct_ref[g], e_c[g].astype(BF16)) * inv_c[g] for g in groups]

    jb = lax.broadcasted_iota(jnp.int32, (n_sb, TQ), 0)
    cur = (q0 + lax.broadcasted_iota(jnp.int32, (n_sb, TQ), 1)) // SEL_LEN
    valid = jb <= cur
    forced = (jb == 0) | (jb == cur) | (jb == cur - 1)
    sub = lax.broadcasted_iota(jnp.int32, (8, TQ), 0)
    score = []
    for g in groups:
        p_sum = e_c[g][:, 0:TQ] * inv_c[g][:, 0:TQ]
        for hh in range(1, HPG):
            p_sum = p_sum + e_c[g][:, hh * TQ:(hh + 1) * TQ] * inv_c[g][:, hh * TQ:(hh + 1) * TQ]
        imp = jnp.dot(ovl_ref[...], p_sum, preferred_element_type=F32, precision=HIGHEST)
        score.append(jnp.where(forced, jnp.inf, jnp.where(valid, imp, -jnp.inf)))

    def ranked(n_rows):
        def fn(*score):
            n_grp = n_rows // 8
            outs = []
            grp = [[s[8 * a:8 * a + 8] for a in range(n_grp)] for s in score]
            rank = [[jnp.zeros((8, TQ), jnp.int32) for _ in range(n_grp)] for _ in score]
            for j in range(n_rows):
                for g in groups:
                    r = score[g][j:j + 1, :]
                    for a in range(n_grp):
                        if a > j // 8:
                            ahead = (r >= grp[g][a]).astype(jnp.int32)
                        elif a < j // 8:
                            ahead = (r > grp[g][a]).astype(jnp.int32)
                        else:
                            ahead = jnp.where(sub > j % 8, (r >= grp[g][a]).astype(jnp.int32),
                                              (r > grp[g][a]).astype(jnp.int32))
                        rank[g][a] = rank[g][a] + ahead
            for g in groups:
                top = jnp.where(jnp.concatenate(rank[g], axis=0) < SEL_TOPN, 1.0, 0.0)
                outs.append(jnp.concatenate([top, jnp.zeros((n_sb - n_rows, TQ), F32)], axis=0) if n_rows < n_sb else top)
            return tuple(outs)
        return fn

    sizes = [n for n in range(RANK_STEP, n_sb + 1, RANK_STEP)]
    rows_needed = (q0 + TQ + SEL_LEN - 1) // SEL_LEN
    in_top = lax.switch((rows_needed + RANK_STEP - 1) // RANK_STEP - 1, [ranked(n) for n in sizes], *score)
    picked = [valid & (in_top[g] > 0.5) for g in groups]
    qs_t = [q_aug(g, jnp.where(picked[g], 0.0, NEG_BIG)) for g in groups]

    filler = 1 - kt_d % 2
    n_seq = kt_d + 1 + filler

    def sel_tile(i):
        return jnp.where(i == 0, kt_d, jnp.maximum(i - 1 - filler, 0))

    def a_stage(slot, kt, mask_add=None):
        return [stage_scores(ss_s.at[g, slot], ksa_s.at[g], qs_t[g], kt, mask_add) for g in groups]

    def b_stage(slot, tmax, m):
        out = [stage_probs(ss_s.at[g, slot], ps_s.at[g, slot], tmax[g], m[g]) for g in groups]
        return [o[0] for o in out], [o[1] for o in out]

    def c_stage(slot, kt, alpha, acc):
        return [stage_values(ps_s.at[g, slot], vsa_s.at[g], kt, alpha[g], acc[g]) for g in groups]

    tmax0 = a_stage(0, kt_d, tile_heads(wmask_ref[0]))
    m, alpha = b_stage(0, tmax0, [m0] * G)
    tmax1 = a_stage(1, sel_tile(1), jnp.where(filler == 1, NEG_BIG, 0.0))

    m_w = [functools.reduce(jnp.maximum, [max_win[back][g] for back in range(n_win)]) for g in groups]
    accw = [None] * G
    for back in range(n_win):
        for g in groups:
            pw_s[g, back] = jnp.exp2(sw_s[g, back] - m_w[g]).astype(BF16)
        for g in groups:
            pv = _dot(vwa_s[g, :, pl.ds(pl.multiple_of(kt_win[back] * TK, TK), TK)], pw_s[g, back])
            accw[g] = pv if accw[g] is None else accw[g] + pv
    o_w = [normalised(accw[g]) for g in groups]

    def sel_body(k, carry):
        m, alpha, acc, tmax1 = carry
        i = 2 * k
        tmax0 = a_stage(0, sel_tile(i + 2))
        m, alpha1 = b_stage(1, tmax1, m)
        tmax1 = a_stage(1, sel_tile(i + 3))
        acc = c_stage(0, sel_tile(i), alpha, acc)
        m, alpha2 = b_stage(0, tmax0, m)
        acc = c_stage(1, sel_tile(i + 1), alpha1, acc)
        return m, alpha2, acc, tmax1

    m, alpha, acc, tmax1 = lax.fori_loop(0, (n_seq - 2) // 2, sel_body, (m, alpha, [acc0] * G, tmax1))
    m, alpha1 = b_stage(1, tmax1, m)
    acc = c_stage(0, sel_tile(n_seq - 2), alpha, acc)
    acc = c_stage(1, sel_tile(n_seq - 1), alpha1, acc)
    o_s = [normalised(acc[g]) for g in groups]

    gates = _sigmoid(gt_ref[...] + gb_ref[...]).T
    outs = []
    for g in groups:
        for hh in range(HPG):
            cols = slice(hh * TQ, (hh + 1) * TQ)
            row = NSA_GATE_LANE0 + 3 * (g * HPG + hh)
            outs.append(gates[row:row + 1] * o_c[g][:, cols] + gates[row + 1:row + 2] * o_s[g][:, cols]
                        + gates[row + 2:row + 3] * o_w[g][:, cols])
    o_ref[...] = jnp.concatenate(outs, axis=0).T.astype(o_ref.dtype)


def nsa_block(zt, zb, zf, gate_bias, kcmp, vcmp_t, q_col_block, k_col_blocks, v_row_blocks):
    B, S, _ = zb.shape
    G, d = NSA_KV_HEADS, HEAD_DIM
    TQ = NSA_TQ
    n_cb = kcmp.shape[2]
    n_cmp = (S - CMP_LEN) // CMP_STRIDE + 1
    n_sb = S // SEL_LEN
    n_terms = len(LOG2E_TERMS)
    kaug = -(-(d + n_sb + 2 * n_terms) // LANES) * LANES
    R = NSA_HPG * TQ
    n_win = (WINDOW - 1 + NSA_TK - 1) // NSA_TK + 1
    assert S % NSA_TK == 0 and NSA_TK == TQ and TQ % SEL_LEN == 0

    cidx = np.arange(n_cb)[None, :] * CMP_STRIDE
    sstart = np.arange(n_sb)[:, None] * SEL_LEN
    ovl = ((cidx < sstart + SEL_LEN) & (cidx + CMP_LEN - 1 >= sstart) & (np.arange(n_cb)[None, :] < n_cmp))
    ovl = jnp.asarray(ovl.astype(np.float32))
    pos = np.arange(S)
    aug = np.zeros((S, kaug - d), np.float32)
    aug[pos, pos // SEL_LEN] = 1.0
    tail = np.zeros((kaug - d - n_sb, TQ), np.float32)
    caug = np.zeros((n_cb, d), np.float32)
    ctail = np.zeros((d, TQ), np.float32)
    for i, term in enumerate(LOG2E_TERMS):
        aug[:, n_sb + i] = pos // SEL_LEN
        aug[:, n_sb + n_terms + i] = pos % SEL_LEN
        tail[i] = term * SEL_LEN
        tail[n_terms + i] = term
        caug[:, i] = np.arange(n_cb)
        ctail[i] = term * CMP_STRIDE
    aug, caug, tail, ctail = jnp.asarray(aug, BF16), jnp.asarray(caug, BF16), jnp.asarray(tail), jnp.asarray(ctail)
    dist = np.arange(n_win)[:, None, None] * NSA_TK + np.arange(TQ)[None, None, :] - np.arange(NSA_TK)[None, :, None]
    wmask = jnp.asarray(np.where((dist >= 0) & (dist < WINDOW), 0.0, NEG_BIG).astype(np.float32))

    def kspec(j):
        return pl.BlockSpec((None, S, G * d), lambda b, i: (b, 0, k_col_blocks[j]))

    def vspec(j):
        return pl.BlockSpec((None, G * d, S), lambda b, i: (b, v_row_blocks[j], 0))

    def const(a):
        return pl.BlockSpec(a.shape, lambda b, i: (0,) * a.ndim)

    W = G * NSA_HPG * d
    return pl.pallas_call(
        functools.partial(_nsa_kernel, n_cmp=n_cmp), grid=(B, S // TQ),
        in_specs=[pl.BlockSpec((None, TQ, W), lambda b, i: (b, i, q_col_block)),
                  pl.BlockSpec((None, TQ, LANES), lambda b, i: (b, i, GATE_COL_BLOCK)),
                  const(gate_bias),
                  pl.BlockSpec((None, G, n_cb, d), lambda b, i: (b, 0, 0, 0)),
                  pl.BlockSpec((None, G, d, n_cb), lambda b, i: (b, 0, 0, 0)),
                  kspec(0), vspec(0), kspec(1), vspec(1), const(ovl), const(aug), const(caug), const(tail),
                  const(ctail), const(wmask)],
        out_specs=pl.BlockSpec((None, TQ, W), lambda b, i: (b, i, 0)),
        out_shape=jax.ShapeDtypeStruct((B, S, NSA_W), BF16),
        scratch_shapes=[pltpu.VMEM((G, S, kaug), BF16), pltpu.VMEM((G, S, kaug), BF16),
                        pltpu.VMEM((G, n_cb, 2 * d), BF16),
                        pltpu.VMEM((G, d + NSA_ONES, S), BF16), pltpu.VMEM((G, d + NSA_ONES, S), BF16),
                        pltpu.VMEM((G, 2, NSA_TK, R), F32), pltpu.VMEM((G, 2, NSA_TK, R), BF16),
                        pltpu.VMEM((G, n_win, NSA_TK, R), F32), pltpu.VMEM((G, n_win, NSA_TK, R), BF16)],
        compiler_params=_cparams(("parallel", "arbitrary")), name="nsa",
    )(zb, zf, gate_bias, kcmp, vcmp_t, zb, zt, zb, zt, ovl, aug, caug, tail, ctail, wmask)


def _rglru_kernel(h_ref, hh_ref, g_ref, wg_ref, wi_ref, cw_ref, cb_ref, wa_ref, wx_ref, ba_ref, bx_ref, lam_ref,
                  o_ref, h_s, au_s):
    NB, T, W = o_ref.shape
    first = pl.program_id(1) == 0

    @pl.when(first)
    def _():
        h_s[...] = jnp.zeros_like(h_s)

    def project(nb):
        xn = _rms(h_ref[nb], g_ref[...]).astype(BF16)
        x_halo = _dot(_rms(hh_ref[nb], g_ref[...]).astype(BF16), wi_ref[...])
        xe = jnp.concatenate([jnp.where(first, 0.0, x_halo), _dot(xn, wi_ref[...])], axis=0)
        return _dot(xn, wg_ref[...]), xe

    def recur(nb, gate, xe):
        xc = cb_ref[...] + cw_ref[3:4, :] * xe
        for k in range(1, LRU_CONV):
            xc = xc + cw_ref[3 - k:4 - k, :] * pltpu.roll(xe, k, 0)
        xc = xc[LRU_HALO:, :]
        xcb = xc.astype(BF16)
        half = W // 2

        def blockdiag(w_ref):
            return jnp.concatenate([_dot(xcb[:, :half], w_ref[0]), _dot(xcb[:, half:], w_ref[1])], axis=1)

        r = _sigmoid(blockdiag(wa_ref) + ba_ref[...])
        i = _sigmoid(blockdiag(wx_ref) + bx_ref[...])
        nl = -lam_ref[...]
        softplus = jnp.maximum(nl, 0.0) + jnp.log1p(jnp.exp(-jnp.abs(nl)))
        log_a = -LRU_C * r * softplus
        a = jnp.exp(log_a)
        one_m_a2 = -jnp.tanh(log_a) * (a * a + 1.0)
        u = jnp.where(one_m_a2 > 0.0, one_m_a2 * lax.rsqrt(one_m_a2), 0.0) * (i * xc)

        n_grp, n_slab = T // 8, W // LANES

        def phases(x, slab):
            for c in range(n_slab):
                au_s[nb, slab, c] = x[:, c * LANES:(c + 1) * LANES]
            return [jnp.concatenate([au_s[nb, slab, c, pl.ds(j, n_grp, stride=8), :] for c in range(n_slab)], axis=1)
                    for j in range(8)]

        a_ph, u_ph = phases(a, 0), phases(u, 1)
        prod, part = [a_ph[0]], [u_ph[0]]
        for j in range(1, 8):
            part.append(a_ph[j] * part[-1] + u_ph[j])
            prod.append(a_ph[j] * prod[-1])
        grp = lax.broadcasted_iota(jnp.int32, (n_grp, W), 0)
        ag, ug = prod[7], part[7]
        sft = 1
        while sft < n_grp:
            keep = grp >= sft
            ug = ag * jnp.where(keep, pltpu.roll(ug, sft, 0), 0.0) + ug
            ag = ag * jnp.where(keep, pltpu.roll(ag, sft, 0), 1.0)
            sft *= 2
        h_prev = h_s[nb, 0:1, :]
        hg = ug + ag * h_prev
        carry_in = jnp.where(grp >= 1, pltpu.roll(hg, 1, 0), h_prev)
        for j in range(8):
            hj = part[j] + prod[j] * carry_in
            for c in range(n_slab):
                au_s[nb, 0, c, pl.ds(j, n_grp, stride=8), :] = hj[:, c * LANES:(c + 1) * LANES]
        h = jnp.concatenate([au_s[nb, 0, c] for c in range(n_slab)], axis=1)
        h_s[nb] = jnp.broadcast_to(hg[n_grp - 1:n_grp, :], h_s.shape[1:])
        o_ref[nb] = (_gelu(gate) * h).astype(o_ref.dtype)

    nxt = project(0)
    for nb in range(NB):
        cur = nxt
        if nb + 1 < NB:
            nxt = project(nb + 1)
        recur(nb, *cur)


def rglru_block(h3, pre_g, w_gate, w_in, conv_w, conv_b, wa, ba, wx, bx, lam, T=LRU_T, NB=LRU_NB):
    B, S, D = h3.shape
    W = LRU_W
    half = W // 2
    hb = T // LRU_HALO

    def bd(w):
        blocks = [jax.scipy.linalg.block_diag(*[w[h] for h in range(4 * j, 4 * j + 4)]) for j in range(2)]
        return jnp.stack(blocks).astype(BF16)

    vec = lambda a: a.reshape(1, W)
    assert B % NB == 0 and S % T == 0
    vspec = pl.BlockSpec((1, W), lambda b, t: (0, 0))
    wspec = pl.BlockSpec((2, half, half), lambda b, t: (0, 0, 0))
    return pl.pallas_call(
        _rglru_kernel, grid=(B // NB, S // T),
        in_specs=[pl.BlockSpec((NB, T, D), lambda b, t: (b, t, 0)),
                  pl.BlockSpec((NB, LRU_HALO, D), lambda b, t: (b, jnp.maximum(t * hb - 1, 0), 0)),
                  pl.BlockSpec((1, D), lambda b, t: (0, 0)),
                  pl.BlockSpec((D, W), lambda b, t: (0, 0)), pl.BlockSpec((D, W), lambda b, t: (0, 0)),
                  pl.BlockSpec((LRU_CONV, W), lambda b, t: (0, 0)), vspec, wspec, wspec, vspec, vspec, vspec],
        out_specs=pl.BlockSpec((NB, T, W), lambda b, t: (b, t, 0)),
        out_shape=jax.ShapeDtypeStruct((B, S, W), BF16),
        scratch_shapes=[pltpu.VMEM((NB, 8, W), F32), pltpu.VMEM((NB, 2, W // LANES, T, LANES), F32)],
        compiler_params=_cparams(("parallel", "arbitrary"), 12, (3, 4)), name="rglru",
    )(h3, h3, pre_g.reshape(1, D), w_gate, w_in, conv_w, vec(conv_b), bd(wa), bd(wx), vec(ba), vec(bx), vec(lam))


def _sgu_kernel(h_ref, gpre_ref, wu_ref, wv_ref, g_ref, b_ref, w_ref, bias_ref, o_ref):
    C, W = SG_CHUNK, o_ref.shape[1]
    dg = W // SG_GROUPS
    n_sub = o_ref.shape[0] // SG_SUB
    row = lax.broadcasted_iota(jnp.int32, (C, C), 0)
    col = lax.broadcasted_iota(jnp.int32, (C, C), 1)
    wc = [jnp.where(col <= row, w_ref[gi], 0.0).astype(BF16) for gi in range(SG_GROUPS)]

    def project(j):
        xn = _rms(h_ref[j * SG_SUB:(j + 1) * SG_SUB, :], gpre_ref[...]).astype(BF16)
        return _dot(xn, wu_ref[...]), _dot(xn, wv_ref[...])

    def gate(j, u, v):
        u, v = _gelu(u), _gelu(v)
        mu = jnp.mean(v, axis=-1, keepdims=True)
        vc = v - mu
        vn = (vc * lax.rsqrt(jnp.mean(vc * vc, axis=-1, keepdims=True) + EPS) * g_ref[...] + b_ref[...]).astype(BF16)
        for gi in range(SG_GROUPS):
            sl = slice(gi * dg, (gi + 1) * dg)
            for c in range(SG_SUB // C):
                rows = slice(c * C, (c + 1) * C)
                mixed = _dot(wc[gi], vn[rows, sl]) + bias_ref[:, sl]
                o_ref[j * SG_SUB + c * C:j * SG_SUB + (c + 1) * C, sl] = (u[rows, sl] * mixed).astype(o_ref.dtype)

    nxt = project(0)
    for j in range(n_sub):
        cur = nxt
        if j + 1 < n_sub:
            nxt = project(j + 1)
        gate(j, *cur)


def sgu_block(h3, pre_g, w_u, w_v, ln_g, ln_b, w, b):
    B, S, D = h3.shape
    W, C, T = SG_W, SG_CHUNK, SG_ROWS
    assert S % T == 0 and T % SG_SUB == 0 and SG_SUB % C == 0
    bias = jnp.repeat(b.T, W // SG_GROUPS, axis=1)
    vspec = pl.BlockSpec((1, W), lambda bb, c: (0, 0))
    return pl.pallas_call(
        _sgu_kernel, grid=(B, S // T),
        in_specs=[pl.BlockSpec((None, T, D), lambda bb, c: (bb, c, 0)),
                  pl.BlockSpec((1, D), lambda bb, c: (0, 0)),
                  pl.BlockSpec((D, W), lambda bb, c: (0, 0)), pl.BlockSpec((D, W), lambda bb, c: (0, 0)),
                  vspec, vspec,
                  pl.BlockSpec((SG_GROUPS, C, C), lambda bb, c: (0, 0, 0)),
                  pl.BlockSpec((C, W), lambda bb, c: (0, 0))],
        out_specs=pl.BlockSpec((None, T, W), lambda bb, c: (bb, c, 0)),
        out_shape=jax.ShapeDtypeStruct((B, S, W), BF16),
        compiler_params=_cparams(("parallel", "parallel"), 8, (2, 3)), name="sgu",
    )(h3, pre_g.reshape(1, D), w_u, w_v, ln_g.reshape(1, W), ln_b.reshape(1, W), w, bias)


def _mixer_ab(h2, B, S, pre_g, w_in, ml_gate_b, ml_norm_g, nsa_gate_b, k_pe, k_w1, k_w2, v_pe, v_w1, v_w2):
    D = h2.shape[1]
    G, d = NSA_KV_HEADS, HEAD_DIM
    offs = np.cumsum([0, ML_W, ML_W, ML_W, ML_W, 2 * ML_HEADS, NSA_W] + [G * d] * 6 + [3 * NSA_HEADS])
    mq, mk, mv, mo, mif, nq, kc, vc, ks, vs, kw, vw, ng = [w_in[:, offs[i]:offs[i + 1]] for i in range(13)]
    w_b = jnp.concatenate([mk, nq, ks, kw], axis=1).astype(BF16)
    w_t = jnp.concatenate([mq, mv, vs, vw], axis=1).T.astype(BF16)
    gpad = LANES - 2 * ML_HEADS - 3 * NSA_HEADS
    w_f = jnp.concatenate([mo, kc, vc, mif, ng, jnp.zeros((D, gpad), w_in.dtype)], axis=1).astype(BF16)
    gate_bias = jnp.concatenate([ml_gate_b, nsa_gate_b, jnp.zeros((gpad,), F32)]).reshape(1, LANES)
    zb, zf, zt = norm_proj(h2, pre_g, [w_b, w_f], [BF16, F32], wts=[w_t], batch=B)
    zb = zb.reshape(B, S, -1)
    zf = zf.reshape(B, S, -1)
    h_ml = mlstm_block(zt, zb, zf, gate_bias, ml_norm_g)
    kcmp, vcmp_t = compress_block(zf, ML_W // (G * d), ML_W // (G * d) + 1, k_pe, k_w1, k_w2, v_pe, v_w1, v_w2)
    h_nsa = nsa_block(zt, zb, zf, gate_bias, kcmp, vcmp_t, q_col_block=ML_W // NSA_W,
                      k_col_blocks=((ML_W + NSA_W) // (G * d), (ML_W + NSA_W) // (G * d) + 1),
                      v_row_blocks=(2 * ML_W // (G * d), 2 * ML_W // (G * d) + 1))
    return h_ml.reshape(B * S, ML_W), h_nsa.reshape(B * S, NSA_W)


def _mixer_cd(h2, B, S, pre_g, w_in, conv_w, conv_b, wa, ba, wx, bx, lam, sg_g, sg_bn, sg_w, sg_b):
    h3 = h2.reshape(B, S, -1)
    w_gate, w_x, w_u, w_v = (w_in[:, j * LRU_W:(j + 1) * LRU_W].astype(BF16) for j in range(4))
    y_lru = rglru_block(h3, pre_g, w_gate, w_x, conv_w, conv_b, wa, ba, wx, bx, lam)
    y_sg = sgu_block(h3, pre_g, w_u, w_v, sg_g, sg_bn, sg_w, sg_b)
    return y_lru.reshape(B * S, LRU_W), y_sg.reshape(B * S, SG_W)


def kernel(x, pre_mix_g, post_mix_g, pre_ffn_g, post_ffn_g, ab_w_in, ab_w_out, ml_gate_b, ml_norm_g, nsa_gate_b, cmp_k_pe, cmp_k_w1, cmp_k_w2, cmp_v_pe, cmp_v_w1, cmp_v_w2, cd_w_in, cd_w_out, lru_conv_w, lru_conv_b, lru_wa, lru_ba, lru_wx, lru_bx, lru_lambda, sg_norm_g, sg_norm_b, sg_w, sg_b, ffn_w_up, ffn_conv_w, ffn_conv_b, ffn_w_down):
    B, S, D = x.shape
    depth = pre_mix_g.shape[0]
    h2 = x.reshape(B * S, D)
    for layer in range(depth):
        if layer % 2 == 0:
            e = layer // 2
            a1, a2 = _mixer_ab(h2, B, S, pre_mix_g[layer], ab_w_in[e], ml_gate_b[e], ml_norm_g[e], nsa_gate_b[e],
                               cmp_k_pe[e], cmp_k_w1[e], cmp_k_w2[e], cmp_v_pe[e], cmp_v_w1[e], cmp_v_w2[e])
            w_out = ab_w_out[e]
        else:
            o = layer // 2
            a1, a2 = _mixer_cd(h2, B, S, pre_mix_g[layer], cd_w_in[o], lru_conv_w[o], lru_conv_b[o], lru_wa[o],
                               lru_ba[o], lru_wx[o], lru_bx[o], lru_lambda[o], sg_norm_g[o], sg_norm_b[o], sg_w[o],
                               sg_b[o])
            w_out = cd_w_out[o]
        h2 = mix_ffn_block(h2, a1, a2, S, w_out, post_mix_g[layer], pre_ffn_g[layer], ffn_w_up[layer],
                           ffn_conv_w[layer], ffn_conv_b[layer], ffn_w_down[layer], post_ffn_g[layer])
    return h2.reshape(B, S, D)
```
